```python
import math
import jax, jax.numpy as jnp
from jax import lax
import numpy as np

D_MODEL = 1024
BATCH = 8
SEQ = 2048
DEPTH = 1

D_FF = 2816
HEAD_DIM = 64
NSA_HEADS = 8
NSA_KV_GROUPS = 2
NSA_HPG = NSA_HEADS // NSA_KV_GROUPS
CMP_BLOCK = 32
CMP_STRIDE = 16
CMP_HIDDEN = 256
SEL_BLOCK = 64
SEL_TOP_N = 16
SEL_Q_BLOCK = 64
WINDOW = 512
WIN_Q_BLOCK = 128
FOX_HEADS = 8
FOX_Q_BLOCK = 128
MEM_LEN = 256
MEM_HEADS = 4
MEM_HEAD_DIM = D_MODEL // MEM_HEADS
NUM_BUCKETS = 32
MAX_DISTANCE = 128
RMS_EPS = 1e-6
NEG_INF = -1e30
FORCE_BONUS = 1e4

NSA_Q = NSA_HEADS * HEAD_DIM
NSA_KV = NSA_KV_GROUPS * HEAD_DIM
FOX_W = FOX_HEADS * HEAD_DIM
IN_SPLITS = (NSA_Q, NSA_KV, NSA_KV, NSA_KV, NSA_KV, NSA_KV, NSA_KV, 3 * NSA_HEADS,
             FOX_W, FOX_W, FOX_W, FOX_HEADS, D_MODEL, D_MODEL)
IN_WIDTH = NSA_Q + 6 * NSA_KV + 3 * NSA_HEADS + 3 * FOX_W + FOX_HEADS + 2 * D_MODEL

kernel_name = 'hybrid_nsa_fox_macaron'


def rmsnorm(x, g):
    x32 = x.astype(jnp.float32)
    y = x32 * lax.rsqrt(jnp.mean(x32 * x32, axis=-1, keepdims=True) + RMS_EPS)
    return (y * g.astype(jnp.float32)).astype(x.dtype)


def swiglu(h, w_gate, w_up, w_down):
    return (jax.nn.silu(h @ w_gate) * (h @ w_up)) @ w_down


def split_cols(z, sizes):
    return jnp.split(z, np.cumsum(sizes)[:-1].tolist(), axis=-1)


def t5_bucket(dist):
    n = jnp.maximum(dist, 0)
    exact = NUM_BUCKETS // 2
    nf = jnp.maximum(n, 1).astype(jnp.float32)
    large = exact + (jnp.log(nf / exact) / math.log(MAX_DISTANCE / exact)
                     * (NUM_BUCKETS - exact)).astype(jnp.int32)
    large = jnp.minimum(large, NUM_BUCKETS - 1)
    return jnp.where(n < exact, n, large)


def masked_softmax(s, mask):
    s = jnp.where(mask, s.astype(jnp.float32), NEG_INF)
    p = jax.nn.softmax(s, axis=-1)
    return jnp.where(mask, p, 0.0)


def cmp_sel_overlap(n_cmp, n_sel):
    c0 = np.arange(n_cmp)[:, None] * CMP_STRIDE
    s0 = np.arange(n_sel)[None, :] * SEL_BLOCK
    ov = np.clip(np.minimum(c0 + CMP_BLOCK, s0 + SEL_BLOCK) - np.maximum(c0, s0), 0, None)
    return jnp.asarray(ov / CMP_STRIDE, dtype=jnp.float32)


def compress(kv, pos, w1, w2):
    b, s, g, d = kv.shape
    n_chunks = s // CMP_STRIDE
    r = CMP_BLOCK // CMP_STRIDE
    nc = n_chunks - r + 1
    chunks = kv.reshape(b, n_chunks, CMP_STRIDE, g, d)
    blocks = jnp.concatenate([chunks[:, i:i + nc] for i in range(r)], axis=2)
    blocks = blocks + pos[None, None, :, None, :]
    flat = blocks.transpose(0, 1, 3, 2, 4).reshape(b, nc, g, CMP_BLOCK * d)
    return jax.nn.gelu(flat @ w1) @ w2


def nsa_attention(q, k_c, v_c, k_s, v_s, k_w, v_w, g_nsa, rel_bias_table,
                  cmp_pos_k, cmp_pos_v, cmp_k_w1, cmp_k_w2, cmp_v_w1, cmp_v_w2):
    b, s = q.shape[:2]
    G, J, D = NSA_KV_GROUPS, NSA_HPG, HEAD_DIM
    qg = q.reshape(b, s, G, J, D) * (D ** -0.5)
    t = jnp.arange(s)
    tbl = rel_bias_table.reshape(NUM_BUCKETS, G, J)

    kc = compress(k_c, cmp_pos_k, cmp_k_w1, cmp_k_w2)
    vc = compress(v_c, cmp_pos_v, cmp_v_w1, cmp_v_w2)
    nc = kc.shape[1]
    c_end = jnp.arange(nc) * CMP_STRIDE + CMP_BLOCK - 1
    dist_c = t[:, None] - c_end[None, :]
    bias_c = tbl[t5_bucket(dist_c)].transpose(2, 3, 0, 1)
    s_c = jnp.einsum('bsgjd,bcgd->bgjsc', qg, kc) + bias_c
    p_c = masked_softmax(s_c, dist_c >= 0)
    o_c = jnp.einsum('bgjsc,bcgd->bsgjd', p_c.astype(vc.dtype), vc)

    n_sel = s // SEL_BLOCK
    top_n = min(SEL_TOP_N, n_sel)
    imp = jnp.einsum('bgjsc,cn->bgsn', p_c, cmp_sel_overlap(nc, n_sel))
    blk = jnp.arange(n_sel)[None, :]
    cur = (t // SEL_BLOCK)[:, None]
    forced = (blk == 0) | (blk == cur) | (blk == cur - 1)
    imp = jnp.where(blk * SEL_BLOCK <= t[:, None],
                    imp + jnp.where(forced, FORCE_BONUS, 0.0), NEG_INF)
    _, sel_idx = lax.top_k(imp, top_n)

    kb = k_s.reshape(b, n_sel, SEL_BLOCK, G, D).transpose(0, 3, 1, 2, 4)
    vb = v_s.reshape(b, n_sel, SEL_BLOCK, G, D).transpose(0, 3, 1, 2, 4)
    nq = s // SEL_Q_BLOCK
    q_blocks = qg.reshape(b, nq, SEL_Q_BLOCK, G, J, D).transpose(1, 0, 3, 4, 2, 5)
    idx_blocks = sel_idx.reshape(b, G, nq, SEL_Q_BLOCK, top_n).transpose(2, 0, 1, 3, 4)
    bi = jnp.arange(b)[:, None, None, None]
    gi = jnp.arange(G)[None, :, None, None]
    gi5 = jnp.arange(G)[None, :, None, None, None]
    r = jnp.arange(SEL_BLOCK)

    def sel_block(args):
        qb, ib, start = args
        kg = kb[bi, gi, ib]
        vg = vb[bi, gi, ib]
        tq = start + jnp.arange(SEL_Q_BLOCK)
        pos = ib[..., None] * SEL_BLOCK + r
        dist = tq[None, None, :, None, None] - pos
        bias = tbl[t5_bucket(dist), gi5].transpose(0, 1, 5, 2, 3, 4)
        sc = jnp.einsum('bgjqd,bgqnrd->bgjqnr', qb, kg) + bias
        flat = top_n * SEL_BLOCK
        p = masked_softmax(sc.reshape(b, G, J, SEL_Q_BLOCK, flat),
                           (dist >= 0).reshape(b, G, 1, SEL_Q_BLOCK, flat))
        p = p.reshape(b, G, J, SEL_Q_BLOCK, top_n, SEL_BLOCK).astype(vg.dtype)
        return jnp.einsum('bgjqnr,bgqnrd->bqgjd', p, vg)

    o_s = lax.map(sel_block, (q_blocks, idx_blocks, jnp.arange(nq) * SEL_Q_BLOCK))
    o_s = o_s.transpose(1, 0, 2, 3, 4, 5).reshape(b, s, G, J, D)

    nw = s // WIN_Q_BLOCK
    span = WINDOW + WIN_Q_BLOCK
    kpad = jnp.pad(k_w, ((0, 0), (WINDOW, 0), (0, 0), (0, 0)))
    vpad = jnp.pad(v_w, ((0, 0), (WINDOW, 0), (0, 0), (0, 0)))
    win_idx = (jnp.arange(nw) * WIN_Q_BLOCK)[:, None] + jnp.arange(span)[None, :]
    kwin = kpad[:, win_idx]
    vwin = vpad[:, win_idx]
    qw = qg.reshape(b, nw, WIN_Q_BLOCK, G, J, D)
    dist_w = jnp.arange(WIN_Q_BLOCK)[:, None] + WINDOW - jnp.arange(span)[None, :]
    bias_w = tbl[t5_bucket(dist_w)].transpose(2, 3, 0, 1)[:, :, None]
    mask_w = ((dist_w >= 0) & (dist_w < WINDOW))[None] & (win_idx >= WINDOW)[:, None, :]
    s_w = jnp.einsum('bnqgjd,bnkgd->bgjnqk', qw, kwin) + bias_w
    p_w = masked_softmax(s_w, mask_w)
    o_w = jnp.einsum('bgjnqk,bnkgd->bnqgjd', p_w.astype(vwin.dtype), vwin).reshape(b, s, G, J, D)

    g = jax.nn.sigmoid(g_nsa).reshape(b, s, 3, G, J)[..., None]
    o = g[:, :, 0] * o_c + g[:, :, 1] * o_s + g[:, :, 2] * o_w
    return o.reshape(b, s, NSA_Q)


def forgetting_attention(q, k, v, f_logit):
    b, s, h, d = q.shape
    c = jnp.cumsum(jax.nn.log_sigmoid(f_logit.astype(jnp.float32)), axis=1).transpose(0, 2, 1)
    nb = s // FOX_Q_BLOCK
    qb = (q * (d ** -0.5)).reshape(b, nb, FOX_Q_BLOCK, h, d).transpose(1, 0, 3, 2, 4)
    cb = c.reshape(b, h, nb, FOX_Q_BLOCK).transpose(2, 0, 1, 3)
    kpos = jnp.arange(s)

    def block(args):
        qi, ci, start = args
        sc = jnp.einsum('bhqd,bkhd->bhqk', qi, k).astype(jnp.float32) + ci[..., None] - c[:, :, None, :]
        mask = (start + jnp.arange(FOX_Q_BLOCK))[:, None] >= kpos[None, :]
        p = masked_softmax(sc, mask)
        return jnp.einsum('bhqk,bkhd->bqhd', p.astype(v.dtype), v)

    o = lax.map(block, (qb, cb, jnp.arange(nb) * FOX_Q_BLOCK))
    return o.transpose(1, 0, 2, 3, 4).reshape(b, s, h * d)


def hybrid_mixer(h, w_in, b_forget, rel_bias_table, cmp_pos_k, cmp_pos_v, cmp_k_w1, cmp_k_w2,
                 cmp_v_w1, cmp_v_w2, w_up_nsa, w_up_fox, w_out):
    b, s, _ = h.shape
    (q_n, k_c, v_c, k_s, v_s, k_w, v_w, g_nsa, q_f, k_f, v_f, f_logit,
     gate_a, gate_b) = split_cols(h @ w_in, IN_SPLITS)
    kvh = lambda z: z.reshape(b, s, NSA_KV_GROUPS, HEAD_DIM)
    o_nsa = nsa_attention(q_n, kvh(k_c), kvh(v_c), kvh(k_s), kvh(v_s), kvh(k_w), kvh(v_w), g_nsa,
                          rel_bias_table, cmp_pos_k, cmp_pos_v, cmp_k_w1, cmp_k_w2, cmp_v_w1, cmp_v_w2)
    fh = lambda z: z.reshape(b, s, FOX_HEADS, HEAD_DIM)
    o_fox = forgetting_attention(fh(q_f), fh(k_f), fh(v_f), f_logit + b_forget)
    y = jax.nn.sigmoid(gate_a) * (o_nsa @ w_up_nsa) + jax.nn.sigmoid(gate_b) * (o_fox @ w_up_fox)
    return y @ w_out


def memory_cross_attention(h, mem_n, w_q, w_kv, w_o):
    b, s, _ = h.shape
    q = (h @ w_q).reshape(b, s, MEM_HEADS, MEM_HEAD_DIM) * (MEM_HEAD_DIM ** -0.5)
    k, v = jnp.split(mem_n @ w_kv, 2, axis=-1)
    k = k.reshape(b, -1, MEM_HEADS, MEM_HEAD_DIM)
    v = v.reshape(b, -1, MEM_HEADS, MEM_HEAD_DIM)
    p = jax.nn.softmax(jnp.einsum('bshd,bmhd->bhsm', q, k).astype(jnp.float32), axis=-1)
    o = jnp.einsum('bhsm,bmhd->bshd', p.astype(v.dtype), v).reshape(b, s, D_MODEL)
    return o @ w_o


def setup_inputs(seed: int = 0) -> dict:
    key = jax.random.key(seed)
    ks = iter(jax.random.split(key, 40))
    nrm = lambda shape, scale: jax.random.normal(next(ks), shape, jnp.float32) * scale
    gain = lambda shape: 1.0 + nrm(shape, 0.02)
    L = DEPTH
    return {
        'x': nrm((BATCH, SEQ, D_MODEL), 1.0),
        'mem': nrm((BATCH, MEM_LEN, D_MODEL), 1.0),
        'rel_bias_table': nrm((NUM_BUCKETS, NSA_HEADS), 0.5),
        'ffn1_norm': gain((L, D_MODEL)),
        'ffn1_w_gate': nrm((L, D_MODEL, D_FF), D_MODEL ** -0.5),
        'ffn1_w_up': nrm((L, D_MODEL, D_FF), D_MODEL ** -0.5),
        'ffn1_w_down': nrm((L, D_FF, D_MODEL), D_FF ** -0.5),
        'mix_norm': gain((L, D_MODEL)),
        'mix_w_in': nrm((L, D_MODEL, IN_WIDTH), D_MODEL ** -0.5),
        'mix_b_forget': 2.0 + nrm((L, FOX_HEADS), 0.5),
        'cmp_pos_k': nrm((L, CMP_BLOCK, HEAD_DIM), 0.1),
        'cmp_pos_v': nrm((L, CMP_BLOCK, HEAD_DIM), 0.1),
        'cmp_k_w1': nrm((L, CMP_BLOCK * HEAD_DIM, CMP_HIDDEN), (CMP_BLOCK * HEAD_DIM) ** -0.5),
        'cmp_k_w2': nrm((L, CMP_HIDDEN, HEAD_DIM), 2.0 * CMP_HIDDEN ** -0.5),
        'cmp_v_w1': nrm((L, CMP_BLOCK * HEAD_DIM, CMP_HIDDEN), (CMP_BLOCK * HEAD_DIM) ** -0.5),
        'cmp_v_w2': nrm((L, CMP_HIDDEN, HEAD_DIM), 2.0 * CMP_HIDDEN ** -0.5),
        'w_up_nsa': nrm((L, NSA_Q, D_MODEL), NSA_Q ** -0.5),
        'w_up_fox': nrm((L, FOX_W, D_MODEL), FOX_W ** -0.5),
        'mix_w_out': nrm((L, D_MODEL, D_MODEL), D_MODEL ** -0.5),
        'mem_q_norm': gain((L, D_MODEL)),
        'mem_kv_norm': gain((L, D_MODEL)),
        'mem_w_q': nrm((L, D_MODEL, D_MODEL), D_MODEL ** -0.5),
        'mem_w_kv': nrm((L, D_MODEL, 2 * D_MODEL), D_MODEL ** -0.5),
        'mem_w_o': nrm((L, D_MODEL, D_MODEL), D_MODEL ** -0.5),
        'ffn2_norm': gain((L, D_MODEL)),
        'ffn2_w_gate': nrm((L, D_MODEL, D_FF), D_MODEL ** -0.5),
        'ffn2_w_up': nrm((L, D_MODEL, D_FF), D_MODEL ** -0.5),
        'ffn2_w_down': nrm((L, D_FF, D_MODEL), D_FF ** -0.5),
        'final_norm': gain((D_MODEL,)),
    }


def reference(x, mem, rel_bias_table, ffn1_norm, ffn1_w_gate, ffn1_w_up, ffn1_w_down,
              mix_norm, mix_w_in, mix_b_forget, cmp_pos_k, cmp_pos_v, cmp_k_w1, cmp_k_w2,
              cmp_v_w1, cmp_v_w2, w_up_nsa, w_up_fox, mix_w_out, mem_q_norm, mem_kv_norm,
              mem_w_q, mem_w_kv, mem_w_o, ffn2_norm, ffn2_w_gate, ffn2_w_up, ffn2_w_down,
              final_norm):
    for l in range(DEPTH):
        x = x + 0.5 * swiglu(rmsnorm(x, ffn1_norm[l]), ffn1_w_gate[l], ffn1_w_up[l], ffn1_w_down[l])
        x = x + hybrid_mixer(rmsnorm(x, mix_norm[l]), mix_w_in[l], mix_b_forget[l], rel_bias_table,
                             cmp_pos_k[l], cmp_pos_v[l], cmp_k_w1[l], cmp_k_w2[l], cmp_v_w1[l],
                             cmp_v_w2[l], w_up_nsa[l], w_up_fox[l], mix_w_out[l])
        x = x + memory_cross_attention(rmsnorm(x, mem_q_norm[l]), rmsnorm(mem, mem_kv_norm[l]),
                                       mem_w_q[l], mem_w_kv[l], mem_w_o[l])
        x = x + 0.5 * swiglu(rmsnorm(x, ffn2_norm[l]), ffn2_w_gate[l], ffn2_w_up[l], ffn2_w_down[l])
    return rmsnorm(x, final_norm)
```

```python
import functools
import math

import numpy as np
import jax
import jax.numpy as jnp
from jax import lax
from jax.experimental import pallas as pl
from jax.experimental.pallas import tpu as pltpu

D_MODEL = 1024
D_FF = 2816
HEAD_DIM = 64
NSA_HEADS = 8
NSA_KV_GROUPS = 2
NSA_HPG = NSA_HEADS // NSA_KV_GROUPS
CMP_BLOCK = 32
CMP_STRIDE = 16
CMP_HIDDEN = 256
SEL_BLOCK = 64
SEL_TOP_N = 16
WINDOW = 512
FOX_HEADS = 8
MEM_HEADS = 4
MEM_HEAD_DIM = D_MODEL // MEM_HEADS
NUM_BUCKETS = 32
MAX_DISTANCE = 128
RMS_EPS = 1e-6
NEG_INF = -1e30
FORCE_BONUS = 1e4

NSA_Q = NSA_HEADS * HEAD_DIM
NSA_KV = NSA_KV_GROUPS * HEAD_DIM
FOX_W = FOX_HEADS * HEAD_DIM

LANES = 128
VMEM_LIMIT = 56 * 1024 * 1024

BF16 = jnp.bfloat16
F32 = jnp.float32

ATT_TILE = 256
TOK_TILE = 512
FF_CHUNK = 1408


def _bucket_thresholds():
    n = np.arange(0, 4 * MAX_DISTANCE)
    exact = NUM_BUCKETS // 2
    large = exact + (np.log(np.maximum(n, 1) / exact) / math.log(MAX_DISTANCE / exact)
                     * (NUM_BUCKETS - exact)).astype(np.int64)
    bucket = np.where(n < exact, n, np.minimum(large, NUM_BUCKETS - 1))
    assert np.all(np.diff(bucket) >= 0)
    return [int(np.argmax(bucket >= k)) for k in range(1, NUM_BUCKETS)]


BUCKET_THRESHOLDS = _bucket_thresholds()


def _dot(a, b):
    return jnp.dot(a, b, preferred_element_type=F32)


def _dot_nt(a, b):
    return lax.dot_general(a, b, (((1,), (1,)), ((), ())), preferred_element_type=F32)


def _rms(x, g):
    return x * lax.rsqrt(jnp.mean(x * x, axis=-1, keepdims=True) + RMS_EPS) * g


def _resident(shape):
    nd = len(shape)
    return pl.BlockSpec(shape, lambda *_: (0,) * nd, pipeline_mode=pl.Buffered(1))


def _params(sem):
    return pltpu.CompilerParams(dimension_semantics=sem, vmem_limit_bytes=VMEM_LIMIT)


def _ffn_body(x_ref, g_ref, wg_ref, wu_ref, wd_ref, *rest, final):
    if final:
        fg_ref, o_ref = rest
    else:
        (o_ref,) = rest
    x = x_ref[...]
    h = _rms(x, g_ref[...]).astype(BF16)
    acc = jnp.zeros(x.shape, F32)
    for c in range(D_FF // FF_CHUNK):
        sl = slice(c * FF_CHUNK, (c + 1) * FF_CHUNK)
        a = _dot(h, wg_ref[:, sl])
        b = _dot(h, wu_ref[:, sl])
        t = (a * jax.nn.sigmoid(a)) * b
        acc = acc + _dot(t.astype(BF16), wd_ref[sl, :])
    y = x + 0.5 * acc
    if final:
        y = _rms(y, fg_ref[...])
    o_ref[...] = y


def _ffn(x2d, g, wg, wu, wd, final_g=None):
    n = x2d.shape[0]
    final = final_g is not None
    tok = pl.BlockSpec((TOK_TILE, D_MODEL), lambda i: (i, 0))
    in_specs = [tok, _resident((1, D_MODEL)), _resident((D_MODEL, D_FF)),
                _resident((D_MODEL, D_FF)), _resident((D_FF, D_MODEL))]
    args = [x2d, g.reshape(1, D_MODEL), wg, wu, wd]
    if final:
        in_specs.append(_resident((1, D_MODEL)))
        args.append(final_g.reshape(1, D_MODEL))
    return pl.pallas_call(
        functools.partial(_ffn_body, final=final),
        grid=(n // TOK_TILE,),
        in_specs=in_specs,
        out_specs=tok,
        out_shape=jax.ShapeDtypeStruct((n, D_MODEL), F32),
        compiler_params=_params(("parallel",)),
        name="ffn_final" if final else "ffn",
    )(*args)


_C_QN = 0
_C_CMP = _C_QN + NSA_Q
_C_KV = _C_CMP + 2 * NSA_KV
_C_FOX = _C_KV + 4 * NSA_KV
_C_SMALL = _C_FOX + 3 * FOX_W
_C_END = _C_SMALL + LANES
GATE_PAD = 16


def _inproj_body(x_ref, g_ref, w_ref, qn_ref, kc_ref, vc_ref, ks_ref, vs_ref, kw_ref, vw_ref,
                 qf_ref, kf_ref, vf_ref, gate_ref, fl_ref):
    h = _rms(x_ref[0], g_ref[...]).astype(BF16)
    z = _dot(h, w_ref[...])
    qn_ref[0] = (z[:, _C_QN:_C_CMP] * (HEAD_DIM ** -0.5)).astype(BF16)
    kc_ref[0] = z[:, _C_CMP:_C_CMP + NSA_KV].astype(BF16)
    vc_ref[0] = z[:, _C_CMP + NSA_KV:_C_KV].astype(BF16)
    for i, ref in enumerate((ks_ref, vs_ref, kw_ref, vw_ref)):
        for g in range(NSA_KV_GROUPS):
            c0 = _C_KV + i * NSA_KV + g * HEAD_DIM
            ref[0, g] = z[:, c0:c0 + HEAD_DIM].astype(BF16)
    for i, ref in enumerate((qf_ref, kf_ref, vf_ref)):
        scale = HEAD_DIM ** -0.5 if i == 0 else 1.0
        for hd in range(FOX_HEADS):
            c0 = _C_FOX + i * FOX_W + hd * HEAD_DIM
            ref[0, hd] = (z[:, c0:c0 + HEAD_DIM] * scale).astype(BF16)
    for g in range(NSA_KV_GROUPS):
        c0 = _C_SMALL + g * GATE_PAD
        gate_ref[0, g] = z[:, c0:c0 + GATE_PAD]
    c0 = _C_SMALL + NSA_KV_GROUPS * GATE_PAD
    fl_ref[0] = z[:, c0:c0 + FOX_HEADS]


def _pack_w_in(w_in):
    cols = np.cumsum((0, NSA_Q, NSA_KV, NSA_KV, NSA_KV, NSA_KV, NSA_KV, NSA_KV, 3 * NSA_HEADS,
                      FOX_W, FOX_W, FOX_W, FOX_HEADS, D_MODEL, D_MODEL))
    g0 = int(cols[7])
    gate_cols = []
    for g in range(NSA_KV_GROUPS):
        idx = [g0 + br * NSA_HEADS + g * NSA_HPG + j for br in range(3) for j in range(NSA_HPG)]
        gate_cols.append(w_in[:, np.array(idx)])
        gate_cols.append(jnp.zeros((D_MODEL, GATE_PAD - len(idx)), w_in.dtype))
    small = jnp.concatenate(gate_cols + [w_in[:, int(cols[11]):int(cols[12])]], axis=1)
    small = jnp.pad(small, ((0, 0), (0, LANES - small.shape[1])))
    main = jnp.concatenate([w_in[:, :g0], w_in[:, int(cols[8]):int(cols[11])], small], axis=1)
    w_a = w_in[:, int(cols[12]):int(cols[13])]
    w_b = w_in[:, int(cols[13]):int(cols[14])]
    return main.astype(BF16), w_a.astype(BF16), w_b.astype(BF16)


def _inproj(x, g, w_main):
    b, s, _ = x.shape
    grid = (b, s // TOK_TILE)
    tok = lambda c: pl.BlockSpec((1, TOK_TILE, c), lambda i, j: (i, j, 0))
    heads = lambda nh: pl.BlockSpec((1, nh, TOK_TILE, HEAD_DIM), lambda i, j: (i, 0, j, 0))
    sds = jax.ShapeDtypeStruct
    kv = sds((b, NSA_KV_GROUPS, s, HEAD_DIM), BF16)
    fx = sds((b, FOX_HEADS, s, HEAD_DIM), BF16)
    out_shape = (sds((b, s, NSA_Q), BF16), sds((b, s, NSA_KV), BF16), sds((b, s, NSA_KV), BF16),
                 kv, kv, kv, kv, fx, fx, fx,
                 sds((b, NSA_KV_GROUPS, s, GATE_PAD), F32), sds((b, s, FOX_HEADS), F32))
    out_specs = (tok(NSA_Q), tok(NSA_KV), tok(NSA_KV),
                 heads(NSA_KV_GROUPS), heads(NSA_KV_GROUPS), heads(NSA_KV_GROUPS), heads(NSA_KV_GROUPS),
                 heads(FOX_HEADS), heads(FOX_HEADS), heads(FOX_HEADS),
                 pl.BlockSpec((1, NSA_KV_GROUPS, TOK_TILE, GATE_PAD), lambda i, j: (i, 0, j, 0)),
                 tok(FOX_HEADS))
    return pl.pallas_call(
        _inproj_body,
        grid=grid,
        in_specs=[tok(D_MODEL), _resident((1, D_MODEL)), _resident((D_MODEL, _C_END))],
        out_specs=out_specs,
        out_shape=out_shape,
        compiler_params=_params(("parallel", "parallel")),
        name="inproj",
    )(x, g.reshape(1, D_MODEL), w_main)


N_CHUNKS_PAD = 128
CHUNK_W = CMP_STRIDE * NSA_KV


def _compress_body(xk_ref, xv_ref, pk_ref, pv_ref, wk1_ref, wv1_ref, wk2_ref, wv2_ref, ok_ref, ov_ref):
    for x_ref, p_ref, w1_ref, w2_ref, o_ref in ((xk_ref, pk_ref, wk1_ref, wk2_ref, ok_ref),
                                                (xv_ref, pv_ref, wv1_ref, wv2_ref, ov_ref)):
        x = x_ref[0]
        a0 = _dot(x, w1_ref[0])
        a1 = _dot(x, w1_ref[1])
        c = _dot(p_ref[0], w1_ref[0]) + _dot(p_ref[1], w1_ref[1])
        pre = a0 + pltpu.roll(a1, N_CHUNKS_PAD - 1, axis=0) + c[0:1, :]
        hid = jax.nn.gelu(pre).astype(BF16)
        out = _dot(hid, w2_ref[...])
        for g in range(NSA_KV_GROUPS):
            o_ref[0, g] = out[:, g * HEAD_DIM:(g + 1) * HEAD_DIM].astype(BF16)


def _pack_compress(pos, w1, w2):
    r = CMP_BLOCK // CMP_STRIDE
    w1r = w1.reshape(r, CMP_STRIDE, HEAD_DIM, CMP_HIDDEN)
    eye = jnp.eye(NSA_KV_GROUPS, dtype=w1.dtype)
    w1big = jnp.einsum('irdk,gh->irgdhk', w1r, eye).reshape(r, CHUNK_W, NSA_KV_GROUPS * CMP_HIDDEN)
    w2big = jnp.einsum('kd,gh->gkhd', w2, eye).reshape(NSA_KV_GROUPS * CMP_HIDDEN, NSA_KV)
    p = pos.reshape(r, CMP_STRIDE, 1, HEAD_DIM)
    p = jnp.broadcast_to(p, (r, CMP_STRIDE, NSA_KV_GROUPS, HEAD_DIM)).reshape(r, 1, CHUNK_W)
    p = jnp.broadcast_to(p, (r, 16, CHUNK_W))
    return p.astype(BF16), w1big.astype(BF16), w2big.astype(BF16)


def _compress(kc, vc, pos_k, pos_v, k_w1, k_w2, v_w1, v_w2):
    b, s, _ = kc.shape
    xk = kc.reshape(b, N_CHUNKS_PAD, CHUNK_W)
    xv = vc.reshape(b, N_CHUNKS_PAD, CHUNK_W)
    pk, wk1, wk2 = _pack_compress(pos_k, k_w1, k_w2)
    pv, wv1, wv2 = _pack_compress(pos_v, v_w1, v_w2)
    xs = pl.BlockSpec((1, N_CHUNKS_PAD, CHUNK_W), lambda i: (i, 0, 0))
    os_ = pl.BlockSpec((1, NSA_KV_GROUPS, N_CHUNKS_PAD, HEAD_DIM), lambda i: (i, 0, 0, 0))
    osd = jax.ShapeDtypeStruct((b, NSA_KV_GROUPS, N_CHUNKS_PAD, HEAD_DIM), BF16)
    return pl.pallas_call(
        _compress_body,
        grid=(b,),
        in_specs=[xs, xs, _resident(pk.shape), _resident(pv.shape), _resident(wk1.shape),
                  _resident(wv1.shape), _resident(wk2.shape), _resident(wv2.shape)],
        out_specs=(os_, os_),
        out_shape=(osd, osd),
        compiler_params=_params(("parallel",)),
        name="compress",
    )(xk, xv, pk, pv, wk1, wv1, wk2, wv2)


def _bias_lookup(tbl_ref, head, dist):
    acc = jnp.full(dist.shape, tbl_ref[0, head], F32)
    for k, thr in enumerate(BUCKET_THRESHOLDS):
        acc = jnp.where(dist >= thr, tbl_ref[k + 1, head], acc)
    return acc - tbl_ref[NUM_BUCKETS - 1, head]


def _relbias_body(tbl_ref, toep_ref, bc_ref):
    head = pl.program_id(0)
    t = ATT_TILE
    i = lax.broadcasted_iota(jnp.int32, (t, 2 * t), 0)
    u = lax.broadcasted_iota(jnp.int32, (t, 2 * t), 1)
    d = i - u + t
    toep_ref[0] = jnp.where(d >= 0, _bias_lookup(tbl_ref, head, d), NEG_INF)
    s = bc_ref.shape[1]
    tq = lax.broadcasted_iota(jnp.int32, (s, N_CHUNKS_PAD), 0)
    c = lax.broadcasted_iota(jnp.int32, (s, N_CHUNKS_PAD), 1)
    dc = tq - (c * CMP_STRIDE + CMP_BLOCK - 1)
    bc_ref[0] = jnp.where(dc >= 0, _bias_lookup(tbl_ref, head, dc), NEG_INF)


def _relbias(tbl, s):
    return pl.pallas_call(
        _relbias_body,
        grid=(NSA_HEADS,),
        in_specs=[pl.BlockSpec(memory_space=pltpu.SMEM)],
        out_specs=(pl.BlockSpec((1, ATT_TILE, 2 * ATT_TILE), lambda h: (h, 0, 0)),
                   pl.BlockSpec((1, s, N_CHUNKS_PAD), lambda h: (h, 0, 0))),
        out_shape=(jax.ShapeDtypeStruct((NSA_HEADS, ATT_TILE, 2 * ATT_TILE), F32),
                   jax.ShapeDtypeStruct((NSA_HEADS, s, N_CHUNKS_PAD), F32)),
        compiler_params=_params(("parallel",)),
        name="relbias",
    )(tbl)


def _softmax_start(s, v):
    mx = jnp.max(s, axis=-1, keepdims=True)
    p = jnp.exp(s - mx)
    return mx, jnp.sum(p, axis=-1, keepdims=True), _dot(p.astype(BF16), v)


def _softmax_step(carry, s, v):
    mx, l, acc = carry
    mn = jnp.maximum(mx, jnp.max(s, axis=-1, keepdims=True))
    a = jnp.exp(mx - mn)
    p = jnp.exp(s - mn)
    return mn, a * l + jnp.sum(p, axis=-1, keepdims=True), a * acc + _dot(p.astype(BF16), v)


def _split3(x):
    hi = x.astype(BF16)
    r = x - hi.astype(F32)
    mid = r.astype(BF16)
    lo = (r - mid.astype(F32)).astype(BF16)
    return hi, mid, lo


def _nsa_body(q_ref, kc_ref, vc_ref, ks_ref, vs_ref, kw_ref, vw_ref, gate_ref, toep_ref, bc_ref,
              ov_ref, ex_ref, o_ref):
    t = ATT_TILE
    j4 = NSA_HPG
    m = pl.program_id(2)
    qt = q_ref[0]
    q = jnp.concatenate([qt[:, j * HEAD_DIM:(j + 1) * HEAD_DIM] for j in range(j4)], axis=0)

    def heads(x):
        return x.reshape(j4, t, x.shape[-1])

    def flat(x):
        return x.reshape(j4 * t, x.shape[-1])

    bc = bc_ref[...]
    sc = heads(_dot_nt(q, kc_ref[0, 0])) + bc
    valid = bc > 0.5 * NEG_INF
    e = jnp.where(valid, jnp.exp(sc - jnp.max(sc, axis=-1, keepdims=True)), 0.0)
    l = jnp.sum(e, axis=-1, keepdims=True)
    pc = jnp.where(l > 0.0, e / l, 0.0)
    o_c = _dot(flat(pc).astype(BF16), vc_ref[0, 0])

    psum = pc[0] + pc[1] + pc[2] + pc[3]
    ov = ov_ref[...]
    imp = sum(_dot(part, ov) for part in _split3(psum))
    blk = lax.broadcasted_iota(jnp.int32, (t, LANES), 1)
    tpos = m * t + lax.broadcasted_iota(jnp.int32, (t, LANES), 0)
    cur = tpos // SEL_BLOCK
    bonus = jnp.where(blk == 0, FORCE_BONUS,
                      jnp.where(blk == cur, FORCE_BONUS, jnp.where(blk == cur - 1, FORCE_BONUS, 0.0)))
    imp = jnp.where(blk * SEL_BLOCK <= tpos, imp + bonus, NEG_INF)
    n_sel = ex_ref.shape[0] * (t // SEL_BLOCK)
    rank = jnp.zeros((t, LANES), F32)
    for i in range(n_sel):
        col = imp[:, i:i + 1]
        before = jnp.where(blk > i, 1.0, 0.0)
        rank = rank + jnp.where(col > imp, 1.0, jnp.where(col == imp, before, 0.0))
    selb = jnp.where(rank < float(SEL_TOP_N), 0.0, NEG_INF)
    selb = jnp.where(blk < n_sel, selb, 0.0).astype(BF16)

    def sel_bias(n, ok):
        sb = _dot(selb, ex_ref[n])
        return sb + jnp.where(ok, 0.0, NEG_INF)

    def key_tile(ref, n):
        return ref[0, 0, pl.ds(pl.multiple_of(n * t, t), t), :]

    toep = toep_ref[...]
    diag_bias = toep[:, :, t:]
    prev_bias = toep[:, :, :t]
    m1 = jnp.maximum(m - 1, 0)
    m2 = jnp.maximum(m - 2, 0)

    s0 = heads(_dot_nt(q, key_tile(ks_ref, m))) + diag_bias + sel_bias(m, True)[None]
    carry = _softmax_start(flat(s0), key_tile(vs_ref, m))
    s1 = heads(_dot_nt(q, key_tile(ks_ref, m1))) + prev_bias + sel_bias(m1, m >= 1)[None]
    carry = _softmax_step(carry, flat(s1), key_tile(vs_ref, m1))

    def far(n, carry):
        s = heads(_dot_nt(q, key_tile(ks_ref, n))) + sel_bias(n, True)[None]
        return _softmax_step(carry, flat(s), key_tile(vs_ref, n))

    mx, l, acc = lax.fori_loop(0, m1, far, carry)
    o_s = acc / l

    s0 = heads(_dot_nt(q, key_tile(kw_ref, m))) + diag_bias
    carry = _softmax_start(flat(s0), key_tile(vw_ref, m))
    s1 = heads(_dot_nt(q, key_tile(kw_ref, m1))) + prev_bias + jnp.where(m >= 1, 0.0, NEG_INF)
    carry = _softmax_step(carry, flat(s1), key_tile(vw_ref, m1))
    ii = lax.broadcasted_iota(jnp.int32, (t, t), 0)
    jj = lax.broadcasted_iota(jnp.int32, (t, t), 1)
    tri = jnp.where(jj > ii, 0.0, NEG_INF) + jnp.where(m >= 2, 0.0, NEG_INF)
    s2 = heads(_dot_nt(q, key_tile(kw_ref, m2))) + tri[None]
    mx, l, acc = _softmax_step(carry, flat(s2), key_tile(vw_ref, m2))
    o_w = acc / l

    gate = jax.nn.sigmoid(gate_ref[0, 0])
    outs = []
    for j in range(j4):
        rows = slice(j * t, (j + 1) * t)
        outs.append(gate[:, j:j + 1] * o_c[rows]
                    + gate[:, j4 + j:j4 + j + 1] * o_s[rows]
                    + gate[:, 2 * j4 + j:2 * j4 + j + 1] * o_w[rows])
    o_ref[0] = jnp.concatenate(outs, axis=1).astype(BF16)


def _nsa_tables(s):
    n_sel = s // SEL_BLOCK
    nc = N_CHUNKS_PAD
    c0 = np.arange(nc)[:, None] * CMP_STRIDE
    s0 = np.arange(n_sel)[None, :] * SEL_BLOCK
    ov = np.clip(np.minimum(c0 + CMP_BLOCK, s0 + SEL_BLOCK) - np.maximum(c0, s0), 0, None) / CMP_STRIDE
    ov[nc - 1] = 0.0
    ov = np.pad(ov, ((0, 0), (0, LANES - n_sel)))
    key_blk = np.arange(s) // SEL_BLOCK
    ex = (np.arange(LANES)[:, None] == key_blk[None, :]).astype(np.float32)
    ex = ex.reshape(LANES, s // ATT_TILE, ATT_TILE).transpose(1, 0, 2)
    return jnp.asarray(ov, BF16), jnp.asarray(ex, BF16)


def _nsa(qn, kc, vc, ks, vs, kw, vw, gates, toep, bc):
    b, s, _ = qn.shape
    t = ATT_TILE
    ov, ex = _nsa_tables(s)
    grp = NSA_HPG * HEAD_DIM
    seq = lambda n: pl.BlockSpec((1, 1, n, HEAD_DIM), lambda i, g, j: (i, g, 0, 0))
    return pl.pallas_call(
        _nsa_body,
        grid=(b, NSA_KV_GROUPS, s // t),
        in_specs=[pl.BlockSpec((1, t, grp), lambda i, g, j: (i, j, g)),
                  seq(N_CHUNKS_PAD), seq(N_CHUNKS_PAD), seq(s), seq(s), seq(s), seq(s),
                  pl.BlockSpec((1, 1, t, GATE_PAD), lambda i, g, j: (i, g, j, 0)),
                  pl.BlockSpec((NSA_HPG, t, 2 * t), lambda i, g, j: (g, 0, 0)),
                  pl.BlockSpec((NSA_HPG, t, N_CHUNKS_PAD), lambda i, g, j: (g, j, 0)),
                  _resident(ov.shape), _resident(ex.shape)],
        out_specs=pl.BlockSpec((1, t, grp), lambda i, g, j: (i, j, g)),
        out_shape=jax.ShapeDtypeStruct((b, s, NSA_Q), BF16),
        compiler_params=_params(("parallel", "parallel", "parallel")),
        name="nsa",
    )(qn, kc, vc, ks, vs, kw, vw, gates, toep, bc, ov, ex)


def _fcum_body(f_ref, b_ref, o_ref):
    z = f_ref[0] + b_ref[...]
    x = -(jnp.maximum(-z, 0.0) + jnp.log1p(jnp.exp(-jnp.abs(z))))
    n = x.shape[-1]
    lane = lax.broadcasted_iota(jnp.int32, x.shape, 1)
    sh = 1
    while sh < n:
        x = x + jnp.where(lane >= sh, pltpu.roll(x, sh, axis=1), 0.0)
        sh *= 2
    o_ref[0] = x


def _fcum(f_t, b_forget):
    b, h, s = f_t.shape
    blk = pl.BlockSpec((1, h, s), lambda i: (i, 0, 0))
    return pl.pallas_call(
        _fcum_body,
        grid=(b,),
        in_specs=[blk, _resident((h, 1))],
        out_specs=blk,
        out_shape=jax.ShapeDtypeStruct((b, h, s), F32),
        compiler_params=_params(("parallel",)),
        name="fcum",
    )(f_t, b_forget.reshape(h, 1))


FOX_PAIR = 2


def _fox_body(q_ref, k_ref, v_ref, c_ref, o_ref):
    t = ATT_TILE
    m = pl.program_id(2)
    ii = lax.broadcasted_iota(jnp.int32, (t, t), 0)
    jj = lax.broadcasted_iota(jnp.int32, (t, t), 1)
    causal = jnp.where(jj <= ii, 0.0, NEG_INF)
    outs = []
    for hh in range(FOX_PAIR):
        q = q_ref[0, hh]

        def tile(n, hh=hh, q=q):
            rows = pl.ds(pl.multiple_of(n * t, t), t)
            s = _dot_nt(q, k_ref[0, hh, rows, :]) - c_ref[0, hh, :, pl.ds(pl.multiple_of(n * t, t), t)]
            return s, v_ref[0, hh, rows, :]

        s, v = tile(m)
        carry = _softmax_start(s + causal, v)

        def far(n, carry, tile=tile):
            s, v = tile(n)
            return _softmax_step(carry, s, v)

        mx, l, acc = lax.fori_loop(0, m, far, carry)
        outs.append(acc / l)
    o_ref[0] = jnp.concatenate(outs, axis=1).astype(BF16)


def _fox(qf, kf, vf, c):
    b, h, s, _ = qf.shape
    t = ATT_TILE
    seq = pl.BlockSpec((1, FOX_PAIR, s, HEAD_DIM), lambda i, p, j: (i, p, 0, 0))
    return pl.pallas_call(
        _fox_body,
        grid=(b, h // FOX_PAIR, s // t),
        in_specs=[pl.BlockSpec((1, FOX_PAIR, t, HEAD_DIM), lambda i, p, j: (i, p, j, 0)), seq, seq,
                  pl.BlockSpec((1, FOX_PAIR, 1, s), lambda i, p, j: (i, p, 0, 0))],
        out_specs=pl.BlockSpec((1, t, FOX_PAIR * HEAD_DIM), lambda i, p, j: (i, j, p)),
        out_shape=jax.ShapeDtypeStruct((b, s, h * HEAD_DIM), BF16),
        compiler_params=_params(("parallel", "parallel", "parallel")),
        name="fox",
    )(qf, kf, vf, c.reshape(b, h, 1, s))


def _mixout_body(x_ref, g_ref, on_ref, of_ref, wa_ref, wb_ref, wun_ref, wuf_ref, wo_ref, o_ref):
    x = x_ref[...]
    h = _rms(x, g_ref[...]).astype(BF16)
    y = (jax.nn.sigmoid(_dot(h, wa_ref[...])) * _dot(on_ref[...], wun_ref[...])
         + jax.nn.sigmoid(_dot(h, wb_ref[...])) * _dot(of_ref[...], wuf_ref[...]))
    o_ref[...] = x + _dot(y.astype(BF16), wo_ref[...])


def _mixout(x2d, g, o_nsa, o_fox, w_a, w_b, w_un, w_uf, w_o):
    n = x2d.shape[0]
    tok = lambda c: pl.BlockSpec((TOK_TILE, c), lambda i: (i, 0))
    return pl.pallas_call(
        _mixout_body,
        grid=(n // TOK_TILE,),
        in_specs=[tok(D_MODEL), _resident((1, D_MODEL)), tok(NSA_Q), tok(FOX_W),
                  _resident(w_a.shape), _resident(w_b.shape), _resident(w_un.shape),
                  _resident(w_uf.shape), _resident(w_o.shape)],
        out_specs=tok(D_MODEL),
        out_shape=jax.ShapeDtypeStruct((n, D_MODEL), F32),
        compiler_params=_params(("parallel",)),
        name="mixout",
    )(x2d, g.reshape(1, D_MODEL), o_nsa, o_fox, w_a, w_b, w_un, w_uf, w_o)


def _memkv_body(m_ref, g_ref, w_ref, k_ref, v_ref):
    h = _rms(m_ref[0], g_ref[...]).astype(BF16)
    z = _dot(h, w_ref[...])
    k_ref[0] = z[:, :D_MODEL].astype(BF16)
    v_ref[0] = z[:, D_MODEL:].astype(BF16)


def _memkv(mem, g, w_kv):
    b, ml, _ = mem.shape
    blk = pl.BlockSpec((1, ml, D_MODEL), lambda i: (i, 0, 0))
    sd = jax.ShapeDtypeStruct((b, ml, D_MODEL), BF16)
    return pl.pallas_call(
        _memkv_body,
        grid=(b,),
        in_specs=[blk, _resident((1, D_MODEL)), _resident(w_kv.shape)],
        out_specs=(blk, blk),
        out_shape=(sd, sd),
        compiler_params=_params(("parallel",)),
        name="memkv",
    )(mem, g.reshape(1, D_MODEL), w_kv)


def _memattn_body(x_ref, g_ref, k_ref, v_ref, wq_ref, wo_ref, o_ref):
    x = x_ref[0]
    h = _rms(x, g_ref[...]).astype(BF16)
    q = (_dot(h, wq_ref[...]) * (MEM_HEAD_DIM ** -0.5)).astype(BF16)
    outs = []
    for hd in range(MEM_HEADS):
        cols = slice(hd * MEM_HEAD_DIM, (hd + 1) * MEM_HEAD_DIM)
        s = _dot_nt(q[:, cols], k_ref[0, :, cols])
        e = jnp.exp(s - jnp.max(s, axis=-1, keepdims=True))
        p = e / jnp.sum(e, axis=-1, keepdims=True)
        outs.append(_dot(p.astype(BF16), v_ref[0, :, cols]))
    o = jnp.concatenate(outs, axis=1).astype(BF16)
    o_ref[0] = x + _dot(o, wo_ref[...])


def _memattn(x, g, k, v, w_q, w_o):
    b, s, _ = x.shape
    ml = k.shape[1]
    tok = pl.BlockSpec((1, TOK_TILE, D_MODEL), lambda i, j: (i, j, 0))
    kvb = pl.BlockSpec((1, ml, D_MODEL), lambda i, j: (i, 0, 0))
    return pl.pallas_call(
        _memattn_body,
        grid=(b, s // TOK_TILE),
        in_specs=[tok, _resident((1, D_MODEL)), kvb, kvb, _resident(w_q.shape), _resident(w_o.shape)],
        out_specs=tok,
        out_shape=jax.ShapeDtypeStruct((b, s, D_MODEL), F32),
        compiler_params=_params(("parallel", "parallel")),
        name="memattn",
    )(x, g.reshape(1, D_MODEL), k, v, w_q, w_o)


def kernel(x, mem, rel_bias_table, ffn1_norm, ffn1_w_gate, ffn1_w_up, ffn1_w_down, mix_norm, mix_w_in, mix_b_forget, cmp_pos_k, cmp_pos_v, cmp_k_w1, cmp_k_w2, cmp_v_w1, cmp_v_w2, w_up_nsa, w_up_fox, mix_w_out, mem_q_norm, mem_kv_norm, mem_w_q, mem_w_kv, mem_w_o, ffn2_norm, ffn2_w_gate, ffn2_w_up, ffn2_w_down, final_norm):
    b, s, d = x.shape
    depth = ffn1_norm.shape[0]
    bf = lambda w: w.astype(BF16)
    toep, bias_c = _relbias(rel_bias_table, s)
    x = x.reshape(b * s, d)
    for l in range(depth):
        last = l == depth - 1
        x = _ffn(x, ffn1_norm[l], bf(ffn1_w_gate[l]), bf(ffn1_w_up[l]), bf(ffn1_w_down[l]))

        w_main, w_a, w_b = _pack_w_in(mix_w_in[l])
        qn, kc, vc, ks, vs, kw, vw, qf, kf, vf, gates, flog = _inproj(x.reshape(b, s, d), mix_norm[l], w_main)
        kcc, vcc = _compress(kc, vc, cmp_pos_k[l], cmp_pos_v[l], cmp_k_w1[l], cmp_k_w2[l],
                             cmp_v_w1[l], cmp_v_w2[l])
        o_nsa = _nsa(qn, kcc, vcc, ks, vs, kw, vw, gates, toep, bias_c)
        c = _fcum(flog.transpose(0, 2, 1), mix_b_forget[l])
        o_fox = _fox(qf, kf, vf, c)
        x = _mixout(x, mix_norm[l], o_nsa.reshape(b * s, NSA_Q), o_fox.reshape(b * s, FOX_W),
                    w_a, w_b, bf(w_up_nsa[l]), bf(w_up_fox[l]), bf(mix_w_out[l]))

        mk, mv = _memkv(mem, mem_kv_norm[l], bf(mem_w_kv[l]))
        x = _memattn(x.reshape(b, s, d), mem_q_norm[l], mk, mv, bf(mem_w_q[l]), bf(mem_w_o[l]))
        x = _ffn(x.reshape(b * s, d), ffn2_norm[l], bf(ffn2_w_gate[l]), bf(ffn2_w_up[l]),
                 bf(ffn2_w_down[l]), final_g=final_norm if last else None)
    return x.reshape(b, s, d)
```

```python
import functools
import math

import numpy as np
import jax
import jax.numpy as jnp
from jax import lax
from jax.experimental import pallas as pl
from jax.experimental.pallas import tpu as pltpu

D_MODEL = 1024
D_FF = 2816
HEAD_DIM = 64
NSA_HEADS = 8
NSA_KV_GROUPS = 2
NSA_HPG = NSA_HEADS // NSA_KV_GROUPS
CMP_BLOCK = 32
CMP_STRIDE = 16
CMP_HIDDEN = 256
SEL_BLOCK = 64
SEL_TOP_N = 16
WINDOW = 512
FOX_HEADS = 8
MEM_HEADS = 4
MEM_HEAD_DIM = D_MODEL // MEM_HEADS
NUM_BUCKETS = 32
MAX_DISTANCE = 128
RMS_EPS = 1e-6
NEG_INF = -1e30
FORCE_BONUS = 1e4

NSA_Q = NSA_HEADS * HEAD_DIM
NSA_KV = NSA_KV_GROUPS * HEAD_DIM
FOX_W = FOX_HEADS * HEAD_DIM

LANES = 128
VMEM_LIMIT = 56 * 1024 * 1024

BF16 = jnp.bfloat16
F32 = jnp.float32

ATT_TILE = 256
TOK_TILE = 512
FF_CHUNK = 1408


def _bucket_thresholds():
    n = np.arange(0, 4 * MAX_DISTANCE)
    exact = NUM_BUCKETS // 2
    large = exact + (np.log(np.maximum(n, 1) / exact) / math.log(MAX_DISTANCE / exact)
                     * (NUM_BUCKETS - exact)).astype(np.int64)
    bucket = np.where(n < exact, n, np.minimum(large, NUM_BUCKETS - 1))
    assert np.all(np.diff(bucket) >= 0)
    return [int(np.argmax(bucket >= k)) for k in range(1, NUM_BUCKETS)]


BUCKET_THRESHOLDS = _bucket_thresholds()


def _dot(a, b):
    return jnp.dot(a, b, preferred_element_type=F32)


def _dot_nt(a, b):
    return lax.dot_general(a, b, (((1,), (1,)), ((), ())), preferred_element_type=F32)


def _rms(x, g):
    return x * lax.rsqrt(jnp.mean(x * x, axis=-1, keepdims=True) + RMS_EPS) * g


def _resident(shape):
    nd = len(shape)
    return pl.BlockSpec(shape, lambda *_: (0,) * nd, pipeline_mode=pl.Buffered(1))


def _params(sem):
    return pltpu.CompilerParams(dimension_semantics=sem, vmem_limit_bytes=VMEM_LIMIT)


def _ffn_body(x_ref, g_ref, wg_ref, wu_ref, wd_ref, *rest, final):
    if final:
        fg_ref, o_ref = rest
    else:
        (o_ref,) = rest
    x = x_ref[...]
    h = _rms(x, g_ref[...]).astype(BF16)
    acc = jnp.zeros(x.shape, F32)
    for c in range(D_FF // FF_CHUNK):
        sl = slice(c * FF_CHUNK, (c + 1) * FF_CHUNK)
        a = _dot(h, wg_ref[:, sl])
        b = _dot(h, wu_ref[:, sl])
        t = (a * jax.nn.sigmoid(a)) * b
        acc = acc + _dot(t.astype(BF16), wd_ref[sl, :])
    y = x + 0.5 * acc
    if final:
        y = _rms(y, fg_ref[...])
    o_ref[...] = y


def _ffn(x2d, g, wg, wu, wd, final_g=None):
    n = x2d.shape[0]
    final = final_g is not None
    tok = pl.BlockSpec((TOK_TILE, D_MODEL), lambda i: (i, 0))
    in_specs = [tok, _resident((1, D_MODEL)), _resident((D_MODEL, D_FF)),
                _resident((D_MODEL, D_FF)), _resident((D_FF, D_MODEL))]
    args = [x2d, g.reshape(1, D_MODEL), wg, wu, wd]
    if final:
        in_specs.append(_resident((1, D_MODEL)))
        args.append(final_g.reshape(1, D_MODEL))
    return pl.pallas_call(
        functools.partial(_ffn_body, final=final),
        grid=(n // TOK_TILE,),
        in_specs=in_specs,
        out_specs=tok,
        out_shape=jax.ShapeDtypeStruct((n, D_MODEL), F32),
        compiler_params=_params(("parallel",)),
        name="ffn_final" if final else "ffn",
    )(*args)


_C_QN = 0
_C_CMP = _C_QN + NSA_Q
_C_KV = _C_CMP + 2 * NSA_KV
_C_FOX = _C_KV + 4 * NSA_KV
_C_SMALL = _C_FOX + 3 * FOX_W
_C_END = _C_SMALL + LANES
GATE_PAD = 16


def _inproj_body(x_ref, g_ref, w_ref, blk_ref, qn_ref, kc_ref, vc_ref, ks_ref, vs_ref, kw_ref, vw_ref,
                 qf_ref, kf_ref, vf_ref, gate_ref, fl_ref):
    h = _rms(x_ref[0], g_ref[...]).astype(BF16)
    z = _dot(h, w_ref[...])
    qn_ref[0] = (z[:, _C_QN:_C_CMP] * (HEAD_DIM ** -0.5)).astype(BF16)
    kc_ref[0] = z[:, _C_CMP:_C_CMP + NSA_KV].astype(BF16)
    vc_ref[0] = z[:, _C_CMP + NSA_KV:_C_KV].astype(BF16)
    ones = jnp.ones((z.shape[0], HEAD_DIM), BF16)
    for i, (ref, extra) in enumerate(((ks_ref, blk_ref[...]), (vs_ref, ones), (kw_ref, None),
                                      (vw_ref, ones))):
        for g in range(NSA_KV_GROUPS):
            c0 = _C_KV + i * NSA_KV + g * HEAD_DIM
            val = z[:, c0:c0 + HEAD_DIM].astype(BF16)
            ref[0, g] = val if extra is None else jnp.concatenate([val, extra], axis=1)
    for hd in range(FOX_HEADS):
        c0 = _C_FOX + hd * HEAD_DIM
        qf_ref[0, hd] = (z[:, c0:c0 + HEAD_DIM] * (HEAD_DIM ** -0.5)).astype(BF16)
        kf_ref[0, hd] = z[:, c0 + FOX_W:c0 + FOX_W + HEAD_DIM].astype(BF16)
        v = z[:, c0 + 2 * FOX_W:c0 + 2 * FOX_W + HEAD_DIM].astype(BF16)
        vf_ref[0, hd] = jnp.concatenate([v, ones], axis=1)
    for g in range(NSA_KV_GROUPS):
        c0 = _C_SMALL + g * GATE_PAD
        gate_ref[0, g] = z[:, c0:c0 + GATE_PAD]
    c0 = _C_SMALL + NSA_KV_GROUPS * GATE_PAD
    fl_ref[0] = z[:, c0:c0 + FOX_HEADS]


def _pack_w_in(w_in):
    cols = np.cumsum((0, NSA_Q, NSA_KV, NSA_KV, NSA_KV, NSA_KV, NSA_KV, NSA_KV, 3 * NSA_HEADS,
                      FOX_W, FOX_W, FOX_W, FOX_HEADS, D_MODEL, D_MODEL))
    g0 = int(cols[7])
    gate_cols = []
    for g in range(NSA_KV_GROUPS):
        idx = [g0 + br * NSA_HEADS + g * NSA_HPG + j for br in range(3) for j in range(NSA_HPG)]
        gate_cols.append(w_in[:, np.array(idx)])
        gate_cols.append(jnp.zeros((D_MODEL, GATE_PAD - len(idx)), w_in.dtype))
    small = jnp.concatenate(gate_cols + [w_in[:, int(cols[11]):int(cols[12])]], axis=1)
    small = jnp.pad(small, ((0, 0), (0, LANES - small.shape[1])))
    main = jnp.concatenate([w_in[:, :g0], w_in[:, int(cols[8]):int(cols[11])], small], axis=1)
    w_a = w_in[:, int(cols[12]):int(cols[13])]
    w_b = w_in[:, int(cols[13]):int(cols[14])]
    return main.astype(BF16), w_a.astype(BF16), w_b.astype(BF16)


def _inproj(x, g, w_main):
    b, s, _ = x.shape
    grid = (b, s // TOK_TILE)
    tok = lambda c: pl.BlockSpec((1, TOK_TILE, c), lambda i, j: (i, j, 0))
    heads = lambda nh, w=HEAD_DIM: pl.BlockSpec((1, nh, TOK_TILE, w), lambda i, j: (i, 0, j, 0))
    sds = jax.ShapeDtypeStruct
    kv = sds((b, NSA_KV_GROUPS, s, HEAD_DIM), BF16)
    kv2 = sds((b, NSA_KV_GROUPS, s, 2 * HEAD_DIM), BF16)
    fx = sds((b, FOX_HEADS, s, HEAD_DIM), BF16)
    out_shape = (sds((b, s, NSA_Q), BF16), sds((b, s, NSA_KV), BF16), sds((b, s, NSA_KV), BF16),
                 kv2, kv2, kv, kv2, fx, fx, sds((b, FOX_HEADS, s, 2 * HEAD_DIM), BF16),
                 sds((b, NSA_KV_GROUPS, s, GATE_PAD), F32), sds((b, s, FOX_HEADS), F32))
    out_specs = (tok(NSA_Q), tok(NSA_KV), tok(NSA_KV),
                 heads(NSA_KV_GROUPS, 2 * HEAD_DIM), heads(NSA_KV_GROUPS, 2 * HEAD_DIM),
                 heads(NSA_KV_GROUPS), heads(NSA_KV_GROUPS, 2 * HEAD_DIM),
                 heads(FOX_HEADS), heads(FOX_HEADS), heads(FOX_HEADS, 2 * HEAD_DIM),
                 pl.BlockSpec((1, NSA_KV_GROUPS, TOK_TILE, GATE_PAD), lambda i, j: (i, 0, j, 0)),
                 tok(FOX_HEADS))
    key_blk = np.arange(s) // SEL_BLOCK
    blk_onehot = jnp.asarray(key_blk[:, None] == np.arange(HEAD_DIM)[None, :], BF16)
    return pl.pallas_call(
        _inproj_body,
        grid=grid,
        in_specs=[tok(D_MODEL), _resident((1, D_MODEL)), _resident((D_MODEL, _C_END)),
                  pl.BlockSpec((TOK_TILE, HEAD_DIM), lambda i, j: (j, 0))],
        out_specs=out_specs,
        out_shape=out_shape,
        compiler_params=_params(("parallel", "parallel")),
        name="inproj",
    )(x, g.reshape(1, D_MODEL), w_main, blk_onehot)


N_CHUNKS_PAD = 128
CHUNK_W = CMP_STRIDE * NSA_KV


def _compress_body(xk_ref, xv_ref, pk_ref, pv_ref, wk1_ref, wv1_ref, wk2_ref, wv2_ref, ok_ref, ov_ref):
    for x_ref, p_ref, w1_ref, w2_ref, o_ref in ((xk_ref, pk_ref, wk1_ref, wk2_ref, ok_ref),
                                                (xv_ref, pv_ref, wv1_ref, wv2_ref, ov_ref)):
        x = x_ref[0]
        a0 = _dot(x, w1_ref[0])
        a1 = _dot(x, w1_ref[1])
        c = _dot(p_ref[0], w1_ref[0]) + _dot(p_ref[1], w1_ref[1])
        pre = a0 + pltpu.roll(a1, N_CHUNKS_PAD - 1, axis=0) + c[0:1, :]
        hid = jax.nn.gelu(pre).astype(BF16)
        out = _dot(hid, w2_ref[...])
        for g in range(NSA_KV_GROUPS):
            o_ref[0, g] = out[:, g * HEAD_DIM:(g + 1) * HEAD_DIM].astype(BF16)


def _pack_compress(pos, w1, w2):
    r = CMP_BLOCK // CMP_STRIDE
    w1r = w1.reshape(r, CMP_STRIDE, HEAD_DIM, CMP_HIDDEN)
    eye = jnp.eye(NSA_KV_GROUPS, dtype=w1.dtype)
    w1big = jnp.einsum('irdk,gh->irgdhk', w1r, eye).reshape(r, CHUNK_W, NSA_KV_GROUPS * CMP_HIDDEN)
    w2big = jnp.einsum('kd,gh->gkhd', w2, eye).reshape(NSA_KV_GROUPS * CMP_HIDDEN, NSA_KV)
    p = pos.reshape(r, CMP_STRIDE, 1, HEAD_DIM)
    p = jnp.broadcast_to(p, (r, CMP_STRIDE, NSA_KV_GROUPS, HEAD_DIM)).reshape(r, 1, CHUNK_W)
    p = jnp.broadcast_to(p, (r, 16, CHUNK_W))
    return p.astype(BF16), w1big.astype(BF16), w2big.astype(BF16)


def _compress(kc, vc, pos_k, pos_v, k_w1, k_w2, v_w1, v_w2):
    b, s, _ = kc.shape
    xk = kc.reshape(b, N_CHUNKS_PAD, CHUNK_W)
    xv = vc.reshape(b, N_CHUNKS_PAD, CHUNK_W)
    pk, wk1, wk2 = _pack_compress(pos_k, k_w1, k_w2)
    pv, wv1, wv2 = _pack_compress(pos_v, v_w1, v_w2)
    xs = pl.BlockSpec((1, N_CHUNKS_PAD, CHUNK_W), lambda i: (i, 0, 0))
    os_ = pl.BlockSpec((1, NSA_KV_GROUPS, N_CHUNKS_PAD, HEAD_DIM), lambda i: (i, 0, 0, 0))
    osd = jax.ShapeDtypeStruct((b, NSA_KV_GROUPS, N_CHUNKS_PAD, HEAD_DIM), BF16)
    return pl.pallas_call(
        _compress_body,
        grid=(b,),
        in_specs=[xs, xs, _resident(pk.shape), _resident(pv.shape), _resident(wk1.shape),
                  _resident(wv1.shape), _resident(wk2.shape), _resident(wv2.shape)],
        out_specs=(os_, os_),
        out_shape=(osd, osd),
        compiler_params=_params(("parallel",)),
        name="compress",
    )(xk, xv, pk, pv, wk1, wv1, wk2, wv2)


def _bias_lookup(tbl_ref, head, dist):
    acc = jnp.full(dist.shape, tbl_ref[0, head], F32)
    for k, thr in enumerate(BUCKET_THRESHOLDS):
        acc = jnp.where(dist >= thr, tbl_ref[k + 1, head], acc)
    return acc - tbl_ref[NUM_BUCKETS - 1, head]


def _relbias_body(tbl_ref, toep_ref, bc_ref):
    head = pl.program_id(0)
    t = ATT_TILE
    i = lax.broadcasted_iota(jnp.int32, (t, 2 * t), 0)
    u = lax.broadcasted_iota(jnp.int32, (t, 2 * t), 1)
    d = i - u + t
    toep_ref[0] = jnp.where(d >= 0, _bias_lookup(tbl_ref, head, d), NEG_INF)
    s = bc_ref.shape[1]
    tq = lax.broadcasted_iota(jnp.int32, (s, N_CHUNKS_PAD), 0)
    c = lax.broadcasted_iota(jnp.int32, (s, N_CHUNKS_PAD), 1)
    dc = tq - (c * CMP_STRIDE + CMP_BLOCK - 1)
    bc_ref[0] = jnp.where(dc >= 0, _bias_lookup(tbl_ref, head, dc), NEG_INF)


def _relbias(tbl, s):
    return pl.pallas_call(
        _relbias_body,
        grid=(NSA_HEADS,),
        in_specs=[pl.BlockSpec(memory_space=pltpu.SMEM)],
        out_specs=(pl.BlockSpec((1, ATT_TILE, 2 * ATT_TILE), lambda h: (h, 0, 0)),
                   pl.BlockSpec((1, s, N_CHUNKS_PAD), lambda h: (h, 0, 0))),
        out_shape=(jax.ShapeDtypeStruct((NSA_HEADS, ATT_TILE, 2 * ATT_TILE), F32),
                   jax.ShapeDtypeStruct((NSA_HEADS, s, N_CHUNKS_PAD), F32)),
        compiler_params=_params(("parallel",)),
        name="relbias",
    )(tbl)


def _lane_fold(x, op):
    return functools.reduce(op, [x[:, i * LANES:(i + 1) * LANES] for i in range(x.shape[1] // LANES)])


def _lane_tile(x, width):
    return jnp.concatenate([x] * (width // LANES), axis=1)


def _row_max_tile(rmax):
    return jnp.broadcast_to(jnp.max(rmax, axis=-1, keepdims=True), rmax.shape)


def _normalize(acc):
    return (acc / pltpu.roll(acc, HEAD_DIM, axis=1))[:, :HEAD_DIM]


def _split3(x):
    hi = x.astype(BF16)
    r = x - hi.astype(F32)
    mid = r.astype(BF16)
    lo = (r - mid.astype(F32)).astype(BF16)
    return hi, mid, lo


def _nsa_body(q_ref, kc_ref, vc_ref, ks_ref, vs_ref, kw_ref, vw_ref, gate_ref, toep_ref, bc_ref,
              ov_ref, o_ref, s_ref, near_ref, mb_ref):
    t = ATT_TILE
    j4 = NSA_HPG
    rows = j4 * t
    m = pl.program_id(2)
    qt = q_ref[0]
    q = jnp.concatenate([qt[:, j * HEAD_DIM:(j + 1) * HEAD_DIM] for j in range(j4)], axis=0)

    def heads(x):
        return x.reshape(j4, t, x.shape[-1])

    def flat(x):
        return x.reshape(rows, x.shape[-1])

    bc = bc_ref[...]
    sc = heads(_dot_nt(q, kc_ref[0, 0])) + bc
    valid = bc > 0.5 * NEG_INF
    e = jnp.where(valid, jnp.exp(sc - jnp.max(sc, axis=-1, keepdims=True)), 0.0)
    l = jnp.sum(e, axis=-1, keepdims=True)
    pc = jnp.where(l > 0.0, e / l, 0.0)
    o_c = _dot(flat(pc).astype(BF16), vc_ref[0, 0])

    psum = pc[0] + pc[1] + pc[2] + pc[3]
    ov = ov_ref[...]
    imp = sum(_dot(part, ov) for part in _split3(psum))
    n_sel = s_ref.shape[0] * (t // SEL_BLOCK)
    imp = imp.T[:n_sel]
    blk = lax.broadcasted_iota(jnp.int32, (n_sel, t), 0)
    tpos = m * t + lax.broadcasted_iota(jnp.int32, (n_sel, t), 1)
    cur = tpos // SEL_BLOCK
    bonus = jnp.where(blk == 0, FORCE_BONUS,
                      jnp.where(blk == cur, FORCE_BONUS, jnp.where(blk == cur - 1, FORCE_BONUS, 0.0)))
    imp = jnp.where(blk * SEL_BLOCK <= tpos, imp + bonus, NEG_INF)
    rank = jnp.zeros((n_sel, t), F32)
    for i in range(n_sel):
        row = imp[i:i + 1, :]
        before = jnp.where(blk > i, 1.0, 0.0)
        rank = rank + jnp.where(row > imp, 1.0, jnp.where(row == imp, before, 0.0))
    selb = jnp.where(rank < float(SEL_TOP_N), 0.0, NEG_INF)
    selb = jnp.concatenate([selb, jnp.zeros((LANES - n_sel, t), F32)], axis=0).T
    selb = selb[:, :HEAD_DIM].astype(BF16)

    def key_tile(ref, n):
        return ref[0, 0, pl.ds(pl.multiple_of(n * t, t), t), :]

    toep = toep_ref[...]
    diag_bias = toep[:, :, t:]
    prev_bias = toep[:, :, :t]
    m1 = jnp.maximum(m - 1, 0)
    m2 = jnp.maximum(m - 2, 0)
    no_prev = jnp.where(m >= 1, 0.0, NEG_INF)

    q_aug = jnp.concatenate([q, jnp.concatenate([selb] * j4, axis=0)], axis=1)
    s0 = flat(heads(_dot_nt(q_aug, key_tile(ks_ref, m))) + diag_bias)
    s1 = flat(heads(_dot_nt(q_aug, key_tile(ks_ref, m1))) + (prev_bias + no_prev))
    near_ref[0] = s0
    near_ref[1] = s1
    rmax = jnp.maximum(_lane_fold(s0, jnp.maximum), _lane_fold(s1, jnp.maximum))

    def sel_a(n, rmax):
        s = _dot_nt(q_aug, key_tile(ks_ref, n))
        s_ref[n] = s
        return jnp.maximum(rmax, _lane_fold(s, jnp.maximum))

    mb_ref[...] = _row_max_tile(lax.fori_loop(0, m1, sel_a, rmax))

    def probs(s):
        return jnp.exp(s - _lane_tile(mb_ref[...], t)).astype(BF16)

    acc = _dot(probs(near_ref[0]), key_tile(vs_ref, m)) + _dot(probs(near_ref[1]), key_tile(vs_ref, m1))

    def sel_b(n, acc):
        return acc + _dot(probs(s_ref[n]), key_tile(vs_ref, n))

    o_s = _normalize(lax.fori_loop(0, m1, sel_b, acc))

    ii = lax.broadcasted_iota(jnp.int32, (t, t), 0)
    jj = lax.broadcasted_iota(jnp.int32, (t, t), 1)
    tri = jnp.where(jj > ii, 0.0, NEG_INF) + jnp.where(m >= 2, 0.0, NEG_INF)
    w0 = flat(heads(_dot_nt(q, key_tile(kw_ref, m))) + diag_bias)
    w1 = flat(heads(_dot_nt(q, key_tile(kw_ref, m1))) + (prev_bias + no_prev))
    w2 = flat(heads(_dot_nt(q, key_tile(kw_ref, m2))) + tri[None])
    wmax = _lane_tile(_row_max_tile(functools.reduce(
        jnp.maximum, [_lane_fold(w, jnp.maximum) for w in (w0, w1, w2)])), t)
    acc = sum(_dot(jnp.exp(w - wmax).astype(BF16), key_tile(vw_ref, n))
              for w, n in ((w0, m), (w1, m1), (w2, m2)))
    o_w = _normalize(acc)

    gate = jax.nn.sigmoid(gate_ref[0, 0])
    outs = []
    for j in range(j4):
        rj = slice(j * t, (j + 1) * t)
        outs.append(gate[:, j:j + 1] * o_c[rj]
                    + gate[:, j4 + j:j4 + j + 1] * o_s[rj]
                    + gate[:, 2 * j4 + j:2 * j4 + j + 1] * o_w[rj])
    o_ref[0] = jnp.concatenate(outs, axis=1).astype(BF16)


def _overlap_table(s):
    n_sel = s // SEL_BLOCK
    nc = N_CHUNKS_PAD
    c0 = np.arange(nc)[:, None] * CMP_STRIDE
    s0 = np.arange(n_sel)[None, :] * SEL_BLOCK
    ov = np.clip(np.minimum(c0 + CMP_BLOCK, s0 + SEL_BLOCK) - np.maximum(c0, s0), 0, None) / CMP_STRIDE
    ov[nc - 1] = 0.0
    return jnp.asarray(np.pad(ov, ((0, 0), (0, LANES - n_sel))), BF16)


def _nsa(qn, kc, vc, ks, vs, kw, vw, gates, toep, bc):
    b, s, _ = qn.shape
    t = ATT_TILE
    assert s // SEL_BLOCK <= HEAD_DIM
    ov = _overlap_table(s)
    grp = NSA_HPG * HEAD_DIM
    rows = NSA_HPG * t
    seq = lambda n, w=HEAD_DIM: pl.BlockSpec((1, 1, n, w), lambda i, g, j: (i, g, 0, 0))
    return pl.pallas_call(
        _nsa_body,
        grid=(b, NSA_KV_GROUPS, s // t),
        in_specs=[pl.BlockSpec((1, t, grp), lambda i, g, j: (i, j, g)),
                  seq(N_CHUNKS_PAD), seq(N_CHUNKS_PAD),
                  seq(s, 2 * HEAD_DIM), seq(s, 2 * HEAD_DIM), seq(s), seq(s, 2 * HEAD_DIM),
                  pl.BlockSpec((1, 1, t, GATE_PAD), lambda i, g, j: (i, g, j, 0)),
                  pl.BlockSpec((NSA_HPG, t, 2 * t), lambda i, g, j: (g, 0, 0)),
                  pl.BlockSpec((NSA_HPG, t, N_CHUNKS_PAD), lambda i, g, j: (g, j, 0)),
                  _resident(ov.shape)],
        out_specs=pl.BlockSpec((1, t, grp), lambda i, g, j: (i, j, g)),
        out_shape=jax.ShapeDtypeStruct((b, s, NSA_Q), BF16),
        scratch_shapes=[pltpu.VMEM((s // t, rows, t), F32), pltpu.VMEM((2, rows, t), F32),
                        pltpu.VMEM((rows, LANES), F32)],
        compiler_params=_params(("parallel", "parallel", "parallel")),
        name="nsa",
    )(qn, kc, vc, ks, vs, kw, vw, gates, toep, bc, ov)


def _fcum_body(f_ref, b_ref, o_ref):
    z = f_ref[0] + b_ref[...]
    x = -(jnp.maximum(-z, 0.0) + jnp.log1p(jnp.exp(-jnp.abs(z))))
    n = x.shape[-1]
    lane = lax.broadcasted_iota(jnp.int32, x.shape, 1)
    sh = 1
    while sh < n:
        x = x + jnp.where(lane >= sh, pltpu.roll(x, sh, axis=1), 0.0)
        sh *= 2
    o_ref[0] = x


def _fcum(f_t, b_forget):
    b, h, s = f_t.shape
    blk = pl.BlockSpec((1, h, s), lambda i: (i, 0, 0))
    return pl.pallas_call(
        _fcum_body,
        grid=(b,),
        in_specs=[blk, _resident((h, 1))],
        out_specs=blk,
        out_shape=jax.ShapeDtypeStruct((b, h, s), F32),
        compiler_params=_params(("parallel",)),
        name="fcum",
    )(f_t, b_forget.reshape(h, 1))


FOX_PAIR = 2
FOX_TILE = 512


def _causal_bias(t):
    i = np.arange(t)
    return jnp.asarray(np.where(i[None, :] <= i[:, None], 0.0, NEG_INF), F32)


def _fox_body(q_ref, k_ref, v_ref, c_ref, cm_ref, o_ref, s_ref, mb_ref):
    t = FOX_TILE
    m = pl.program_id(2)
    heads = range(FOX_PAIR)
    qs = [q_ref[0, hh] for hh in heads]

    def keys(n):
        return pl.ds(pl.multiple_of(n * t, t), t)

    def scores(hh, n):
        return _dot_nt(qs[hh], k_ref[0, hh, keys(n), :]) - c_ref[0, hh, :, keys(n)]

    rmax = []
    for hh in heads:
        s = scores(hh, m) + cm_ref[...]
        s_ref[hh, m] = s
        rmax.append(_lane_fold(s, jnp.maximum))

    def pass_a(n, rmax):
        out = []
        for hh in heads:
            s = scores(hh, n)
            s_ref[hh, n] = s
            out.append(jnp.maximum(rmax[hh], _lane_fold(s, jnp.maximum)))
        return tuple(out)

    rmax = lax.fori_loop(0, m, pass_a, tuple(rmax))
    for hh in heads:
        mb_ref[hh] = _row_max_tile(rmax[hh])

    def pass_b(n, acc):
        out = []
        for hh in heads:
            p = jnp.exp(s_ref[hh, n] - _lane_tile(mb_ref[hh], t)).astype(BF16)
            out.append(acc[hh] + _dot(p, v_ref[0, hh, keys(n), :]))
        return tuple(out)

    res = lax.fori_loop(0, m + 1, pass_b, (jnp.zeros((t, 2 * HEAD_DIM), F32),) * FOX_PAIR)
    o_ref[0] = jnp.concatenate([_normalize(r) for r in res], axis=1).astype(BF16)


def _fox(qf, kf, vf, c):
    b, h, s, _ = qf.shape
    t = FOX_TILE
    seq = lambda w: pl.BlockSpec((1, FOX_PAIR, s, w), lambda i, p, j: (i, p, 0, 0))
    return pl.pallas_call(
        _fox_body,
        grid=(b, h // FOX_PAIR, s // t),
        in_specs=[pl.BlockSpec((1, FOX_PAIR, t, HEAD_DIM), lambda i, p, j: (i, p, j, 0)),
                  seq(HEAD_DIM), seq(2 * HEAD_DIM),
                  pl.BlockSpec((1, FOX_PAIR, 1, s), lambda i, p, j: (i, p, 0, 0)),
                  _resident((t, t))],
        out_specs=pl.BlockSpec((1, t, FOX_PAIR * HEAD_DIM), lambda i, p, j: (i, j, p)),
        out_shape=jax.ShapeDtypeStruct((b, s, h * HEAD_DIM), BF16),
        scratch_shapes=[pltpu.VMEM((FOX_PAIR, s // t, t, t), F32),
                        pltpu.VMEM((FOX_PAIR, t, LANES), F32)],
        compiler_params=_params(("parallel", "parallel", "parallel")),
        name="fox",
    )(qf, kf, vf, c.reshape(b, h, 1, s), _causal_bias(t))


def _mixout_body(x_ref, g_ref, on_ref, of_ref, wa_ref, wb_ref, wun_ref, wuf_ref, wo_ref, o_ref):
    x = x_ref[...]
    h = _rms(x, g_ref[...]).astype(BF16)
    y = (jax.nn.sigmoid(_dot(h, wa_ref[...])) * _dot(on_ref[...], wun_ref[...])
         + jax.nn.sigmoid(_dot(h, wb_ref[...])) * _dot(of_ref[...], wuf_ref[...]))
    o_ref[...] = x + _dot(y.astype(BF16), wo_ref[...])


def _mixout(x2d, g, o_nsa, o_fox, w_a, w_b, w_un, w_uf, w_o):
    n = x2d.shape[0]
    tok = lambda c: pl.BlockSpec((TOK_TILE, c), lambda i: (i, 0))
    return pl.pallas_call(
        _mixout_body,
        grid=(n // TOK_TILE,),
        in_specs=[tok(D_MODEL), _resident((1, D_MODEL)), tok(NSA_Q), tok(FOX_W),
                  _resident(w_a.shape), _resident(w_b.shape), _resident(w_un.shape),
                  _resident(w_uf.shape), _resident(w_o.shape)],
        out_specs=tok(D_MODEL),
        out_shape=jax.ShapeDtypeStruct((n, D_MODEL), F32),
        compiler_params=_params(("parallel",)),
        name="mixout",
    )(x2d, g.reshape(1, D_MODEL), o_nsa, o_fox, w_a, w_b, w_un, w_uf, w_o)


def _memkv_body(m_ref, g_ref, w_ref, k_ref, v_ref):
    h = _rms(m_ref[0], g_ref[...]).astype(BF16)
    z = _dot(h, w_ref[...])
    k_ref[0] = z[:, :D_MODEL].astype(BF16)
    v_ref[0] = z[:, D_MODEL:].astype(BF16)


def _memkv(mem, g, w_kv):
    b, ml, _ = mem.shape
    blk = pl.BlockSpec((1, ml, D_MODEL), lambda i: (i, 0, 0))
    sd = jax.ShapeDtypeStruct((b, ml, D_MODEL), BF16)
    return pl.pallas_call(
        _memkv_body,
        grid=(b,),
        in_specs=[blk, _resident((1, D_MODEL)), _resident(w_kv.shape)],
        out_specs=(blk, blk),
        out_shape=(sd, sd),
        compiler_params=_params(("parallel",)),
        name="memkv",
    )(mem, g.reshape(1, D_MODEL), w_kv)


def _memattn_body(x_ref, g_ref, k_ref, v_ref, wq_ref, wo_ref, o_ref):
    x = x_ref[0]
    h = _rms(x, g_ref[...]).astype(BF16)
    q = (_dot(h, wq_ref[...]) * (MEM_HEAD_DIM ** -0.5)).astype(BF16)
    outs = []
    for hd in range(MEM_HEADS):
        cols = slice(hd * MEM_HEAD_DIM, (hd + 1) * MEM_HEAD_DIM)
        s = _dot_nt(q[:, cols], k_ref[0, :, cols])
        e = jnp.exp(s - jnp.max(s, axis=-1, keepdims=True))
        p = e / jnp.sum(e, axis=-1, keepdims=True)
        outs.append(_dot(p.astype(BF16), v_ref[0, :, cols]))
    o = jnp.concatenate(outs, axis=1).astype(BF16)
    o_ref[0] = x + _dot(o, wo_ref[...])


def _memattn(x, g, k, v, w_q, w_o):
    b, s, _ = x.shape
    ml = k.shape[1]
    tok = pl.BlockSpec((1, TOK_TILE, D_MODEL), lambda i, j: (i, j, 0))
    kvb = pl.BlockSpec((1, ml, D_MODEL), lambda i, j: (i, 0, 0))
    return pl.pallas_call(
        _memattn_body,
        grid=(b, s // TOK_TILE),
        in_specs=[tok, _resident((1, D_MODEL)), kvb, kvb, _resident(w_q.shape), _resident(w_o.shape)],
        out_specs=tok,
        out_shape=jax.ShapeDtypeStruct((b, s, D_MODEL), F32),
        compiler_params=_params(("parallel", "parallel")),
        name="memattn",
    )(x, g.reshape(1, D_MODEL), k, v, w_q, w_o)


def kernel(x, mem, rel_bias_table, ffn1_norm, ffn1_w_gate, ffn1_w_up, ffn1_w_down, mix_norm, mix_w_in, mix_b_forget, cmp_pos_k, cmp_pos_v, cmp_k_w1, cmp_k_w2, cmp_v_w1, cmp_v_w2, w_up_nsa, w_up_fox, mix_w_out, mem_q_norm, mem_kv_norm, mem_w_q, mem_w_kv, mem_w_o, ffn2_norm, ffn2_w_gate, ffn2_w_up, ffn2_w_down, final_norm):
    b, s, d = x.shape
    depth = ffn1_norm.shape[0]
    bf = lambda w: w.astype(BF16)
    toep, bias_c = _relbias(rel_bias_table, s)
    x = x.reshape(b * s, d)
    for l in range(depth):
        last = l == depth - 1
        x = _ffn(x, ffn1_norm[l], bf(ffn1_w_gate[l]), bf(ffn1_w_up[l]), bf(ffn1_w_down[l]))

        w_main, w_a, w_b = _pack_w_in(mix_w_in[l])
        qn, kc, vc, ks, vs, kw, vw, qf, kf, vf, gates, flog = _inproj(x.reshape(b, s, d), mix_norm[l], w_main)
        kcc, vcc = _compress(kc, vc, cmp_pos_k[l], cmp_pos_v[l], cmp_k_w1[l], cmp_k_w2[l],
                             cmp_v_w1[l], cmp_v_w2[l])
        o_nsa = _nsa(qn, kcc, vcc, ks, vs, kw, vw, gates, toep, bias_c)
        c = _fcum(flog.transpose(0, 2, 1), mix_b_forget[l])
        o_fox = _fox(qf, kf, vf, c)
        x = _mixout(x, mix_norm[l], o_nsa.reshape(b * s, NSA_Q), o_fox.reshape(b * s, FOX_W),
                    w_a, w_b, bf(w_up_nsa[l]), bf(w_up_fox[l]), bf(mix_w_out[l]))

        mk, mv = _memkv(mem, mem_kv_norm[l], bf(mem_w_kv[l]))
        x = _memattn(x.reshape(b, s, d), mem_q_norm[l], mk, mv, bf(mem_w_q[l]), bf(mem_w_o[l]))
        x = _ffn(x.reshape(b * s, d), ffn2_norm[l], bf(ffn2_w_gate[l]), bf(ffn2_w_up[l]),
                 bf(ffn2_w_down[l]), final_g=final_norm if last else None)
    return x.reshape(b, s, d)
```

```python
import functools
import math

import numpy as np
import jax
import jax.numpy as jnp
from jax import lax
from jax.experimental import pallas as pl
from jax.experimental.pallas import tpu as pltpu

D_MODEL = 1024
D_FF = 2816
HEAD_DIM = 64
NSA_HEADS = 8
NSA_KV_GROUPS = 2
NSA_HPG = NSA_HEADS // NSA_KV_GROUPS
CMP_BLOCK = 32
CMP_STRIDE = 16
CMP_HIDDEN = 256
SEL_BLOCK = 64
SEL_TOP_N = 16
WINDOW = 512
FOX_HEADS = 8
MEM_HEADS = 4
MEM_HEAD_DIM = D_MODEL // MEM_HEADS
NUM_BUCKETS = 32
MAX_DISTANCE = 128
RMS_EPS = 1e-6
NEG_INF = -1e30
FORCE_BONUS = 1e4

NSA_Q = NSA_HEADS * HEAD_DIM
NSA_KV = NSA_KV_GROUPS * HEAD_DIM
FOX_W = FOX_HEADS * HEAD_DIM

LANES = 128
VMEM_LIMIT = 56 * 1024 * 1024

BF16 = jnp.bfloat16
F32 = jnp.float32

ATT_TILE = 256
TOK_TILE = 512
FF_CHUNK = 1408


def _bucket_thresholds():
    n = np.arange(0, 4 * MAX_DISTANCE)
    exact = NUM_BUCKETS // 2
    large = exact + (np.log(np.maximum(n, 1) / exact) / math.log(MAX_DISTANCE / exact)
                     * (NUM_BUCKETS - exact)).astype(np.int64)
    bucket = np.where(n < exact, n, np.minimum(large, NUM_BUCKETS - 1))
    assert np.all(np.diff(bucket) >= 0)
    return [int(np.argmax(bucket >= k)) for k in range(1, NUM_BUCKETS)]


BUCKET_THRESHOLDS = _bucket_thresholds()


def _dot(a, b):
    return jnp.dot(a, b, preferred_element_type=F32)


def _dot_nt(a, b):
    return lax.dot_general(a, b, (((1,), (1,)), ((), ())), preferred_element_type=F32)


def _rms(x, g):
    return x * lax.rsqrt(jnp.mean(x * x, axis=-1, keepdims=True) + RMS_EPS) * g


def _resident(shape):
    nd = len(shape)
    return pl.BlockSpec(shape, lambda *_: (0,) * nd, pipeline_mode=pl.Buffered(1))


def _params(sem):
    return pltpu.CompilerParams(dimension_semantics=sem, vmem_limit_bytes=VMEM_LIMIT)


def _ffn_body(x_ref, g_ref, wg_ref, wu_ref, wd_ref, *rest, final):
    if final:
        fg_ref, o_ref = rest
    else:
        (o_ref,) = rest
    x = x_ref[...]
    h = _rms(x, g_ref[...]).astype(BF16)
    acc = jnp.zeros(x.shape, F32)
    for c in range(D_FF // FF_CHUNK):
        sl = slice(c * FF_CHUNK, (c + 1) * FF_CHUNK)
        a = _dot(h, wg_ref[:, sl])
        b = _dot(h, wu_ref[:, sl])
        t = (a * jax.nn.sigmoid(a)) * b
        acc = acc + _dot(t.astype(BF16), wd_ref[sl, :])
    y = x + 0.5 * acc
    if final:
        y = _rms(y, fg_ref[...])
    o_ref[...] = y


def _ffn(x2d, g, wg, wu, wd, final_g=None):
    n = x2d.shape[0]
    final = final_g is not None
    tok = pl.BlockSpec((TOK_TILE, D_MODEL), lambda i: (i, 0))
    in_specs = [tok, _resident((1, D_MODEL)), _resident((D_MODEL, D_FF)),
                _resident((D_MODEL, D_FF)), _resident((D_FF, D_MODEL))]
    args = [x2d, g.reshape(1, D_MODEL), wg, wu, wd]
    if final:
        in_specs.append(_resident((1, D_MODEL)))
        args.append(final_g.reshape(1, D_MODEL))
    return pl.pallas_call(
        functools.partial(_ffn_body, final=final),
        grid=(n // TOK_TILE,),
        in_specs=in_specs,
        out_specs=tok,
        out_shape=jax.ShapeDtypeStruct((n, D_MODEL), F32),
        compiler_params=_params(("parallel",)),
        name="ffn_final" if final else "ffn",
    )(*args)


_C_QN = 0
_C_CMP = _C_QN + NSA_Q
_C_KV = _C_CMP + 2 * NSA_KV
_C_FOX = _C_KV + 4 * NSA_KV
_C_SMALL = _C_FOX + 3 * FOX_W
_C_END = _C_SMALL + LANES
GATE_PAD = 16


def _inproj_body(x_ref, g_ref, w_ref, blk_ref, qn_ref, kc_ref, vc_ref, ks_ref, vs_ref, kw_ref, vw_ref,
                 qf_ref, kf_ref, vf_ref, gate_ref, fl_ref):
    h = _rms(x_ref[0], g_ref[...]).astype(BF16)
    z = _dot(h, w_ref[...])
    qn_ref[0] = (z[:, _C_QN:_C_CMP] * (HEAD_DIM ** -0.5)).astype(BF16)
    kc_ref[0] = z[:, _C_CMP:_C_CMP + NSA_KV].astype(BF16)
    vc_ref[0] = z[:, _C_CMP + NSA_KV:_C_KV].astype(BF16)
    ones = jnp.ones((z.shape[0], HEAD_DIM), BF16)
    for i, (ref, extra) in enumerate(((ks_ref, blk_ref[...]), (vs_ref, ones), (kw_ref, None),
                                      (vw_ref, ones))):
        for g in range(NSA_KV_GROUPS):
            c0 = _C_KV + i * NSA_KV + g * HEAD_DIM
            val = z[:, c0:c0 + HEAD_DIM].astype(BF16)
            ref[0, g] = val if extra is None else jnp.concatenate([val, extra], axis=1)
    for hd in range(FOX_HEADS):
        c0 = _C_FOX + hd * HEAD_DIM
        qf_ref[0, hd] = (z[:, c0:c0 + HEAD_DIM] * (HEAD_DIM ** -0.5)).astype(BF16)
        kf_ref[0, hd] = z[:, c0 + FOX_W:c0 + FOX_W + HEAD_DIM].astype(BF16)
        v = z[:, c0 + 2 * FOX_W:c0 + 2 * FOX_W + HEAD_DIM].astype(BF16)
        vf_ref[0, hd] = jnp.concatenate([v, ones], axis=1)
    for g in range(NSA_KV_GROUPS):
        c0 = _C_SMALL + g * GATE_PAD
        gate_ref[0, g] = z[:, c0:c0 + GATE_PAD]
    c0 = _C_SMALL + NSA_KV_GROUPS * GATE_PAD
    fl_ref[0] = z[:, c0:c0 + FOX_HEADS]


def _pack_w_in(w_in):
    cols = np.cumsum((0, NSA_Q, NSA_KV, NSA_KV, NSA_KV, NSA_KV, NSA_KV, NSA_KV, 3 * NSA_HEADS,
                      FOX_W, FOX_W, FOX_W, FOX_HEADS, D_MODEL, D_MODEL))
    g0 = int(cols[7])
    gate_cols = []
    for g in range(NSA_KV_GROUPS):
        idx = [g0 + br * NSA_HEADS + g * NSA_HPG + j for br in range(3) for j in range(NSA_HPG)]
        gate_cols.append(w_in[:, np.array(idx)])
        gate_cols.append(jnp.zeros((D_MODEL, GATE_PAD - len(idx)), w_in.dtype))
    small = jnp.concatenate(gate_cols + [w_in[:, int(cols[11]):int(cols[12])]], axis=1)
    small = jnp.pad(small, ((0, 0), (0, LANES - small.shape[1])))
    main = jnp.concatenate([w_in[:, :g0], w_in[:, int(cols[8]):int(cols[11])], small], axis=1)
    w_a = w_in[:, int(cols[12]):int(cols[13])]
    w_b = w_in[:, int(cols[13]):int(cols[14])]
    return main.astype(BF16), w_a.astype(BF16), w_b.astype(BF16)


def _inproj(x, g, w_main):
    b, s, _ = x.shape
    grid = (b, s // TOK_TILE)
    tok = lambda c: pl.BlockSpec((1, TOK_TILE, c), lambda i, j: (i, j, 0))
    heads = lambda nh, w=HEAD_DIM: pl.BlockSpec((1, nh, TOK_TILE, w), lambda i, j: (i, 0, j, 0))
    sds = jax.ShapeDtypeStruct
    kv = sds((b, NSA_KV_GROUPS, s, HEAD_DIM), BF16)
    kv2 = sds((b, NSA_KV_GROUPS, s, 2 * HEAD_DIM), BF16)
    fx = sds((b, FOX_HEADS, s, HEAD_DIM), BF16)
    out_shape = (sds((b, s, NSA_Q), BF16), sds((b, s, NSA_KV), BF16), sds((b, s, NSA_KV), BF16),
                 kv2, kv2, kv, kv2, fx, fx, sds((b, FOX_HEADS, s, 2 * HEAD_DIM), BF16),
                 sds((b, NSA_KV_GROUPS, s, GATE_PAD), F32), sds((b, s, FOX_HEADS), F32))
    out_specs = (tok(NSA_Q), tok(NSA_KV), tok(NSA_KV),
                 heads(NSA_KV_GROUPS, 2 * HEAD_DIM), heads(NSA_KV_GROUPS, 2 * HEAD_DIM),
                 heads(NSA_KV_GROUPS), heads(NSA_KV_GROUPS, 2 * HEAD_DIM),
                 heads(FOX_HEADS), heads(FOX_HEADS), heads(FOX_HEADS, 2 * HEAD_DIM),
                 pl.BlockSpec((1, NSA_KV_GROUPS, TOK_TILE, GATE_PAD), lambda i, j: (i, 0, j, 0)),
                 tok(FOX_HEADS))
    key_blk = np.arange(s) // SEL_BLOCK
    blk_onehot = jnp.asarray(key_blk[:, None] == np.arange(HEAD_DIM)[None, :], BF16)
    return pl.pallas_call(
        _inproj_body,
        grid=grid,
        in_specs=[tok(D_MODEL), _resident((1, D_MODEL)), _resident((D_MODEL, _C_END)),
                  pl.BlockSpec((TOK_TILE, HEAD_DIM), lambda i, j: (j, 0))],
        out_specs=out_specs,
        out_shape=out_shape,
        compiler_params=_params(("parallel", "parallel")),
        name="inproj",
    )(x, g.reshape(1, D_MODEL), w_main, blk_onehot)


N_CHUNKS_PAD = 128
CHUNK_W = CMP_STRIDE * NSA_KV


def _compress_body(xk_ref, xv_ref, pk_ref, pv_ref, wk1_ref, wv1_ref, wk2_ref, wv2_ref, ok_ref, ov_ref):
    for x_ref, p_ref, w1_ref, w2_ref, o_ref in ((xk_ref, pk_ref, wk1_ref, wk2_ref, ok_ref),
                                                (xv_ref, pv_ref, wv1_ref, wv2_ref, ov_ref)):
        x = x_ref[0]
        a0 = _dot(x, w1_ref[0])
        a1 = _dot(x, w1_ref[1])
        c = _dot(p_ref[0], w1_ref[0]) + _dot(p_ref[1], w1_ref[1])
        pre = a0 + pltpu.roll(a1, N_CHUNKS_PAD - 1, axis=0) + c[0:1, :]
        hid = jax.nn.gelu(pre).astype(BF16)
        out = _dot(hid, w2_ref[...])
        for g in range(NSA_KV_GROUPS):
            o_ref[0, g] = out[:, g * HEAD_DIM:(g + 1) * HEAD_DIM].astype(BF16)


def _pack_compress(pos, w1, w2):
    r = CMP_BLOCK // CMP_STRIDE
    w1r = w1.reshape(r, CMP_STRIDE, HEAD_DIM, CMP_HIDDEN)
    eye = jnp.eye(NSA_KV_GROUPS, dtype=w1.dtype)
    w1big = jnp.einsum('irdk,gh->irgdhk', w1r, eye).reshape(r, CHUNK_W, NSA_KV_GROUPS * CMP_HIDDEN)
    w2big = jnp.einsum('kd,gh->gkhd', w2, eye).reshape(NSA_KV_GROUPS * CMP_HIDDEN, NSA_KV)
    p = pos.reshape(r, CMP_STRIDE, 1, HEAD_DIM)
    p = jnp.broadcast_to(p, (r, CMP_STRIDE, NSA_KV_GROUPS, HEAD_DIM)).reshape(r, 1, CHUNK_W)
    p = jnp.broadcast_to(p, (r, 16, CHUNK_W))
    return p.astype(BF16), w1big.astype(BF16), w2big.astype(BF16)


def _compress(kc, vc, pos_k, pos_v, k_w1, k_w2, v_w1, v_w2):
    b, s, _ = kc.shape
    xk = kc.reshape(b, N_CHUNKS_PAD, CHUNK_W)
    xv = vc.reshape(b, N_CHUNKS_PAD, CHUNK_W)
    pk, wk1, wk2 = _pack_compress(pos_k, k_w1, k_w2)
    pv, wv1, wv2 = _pack_compress(pos_v, v_w1, v_w2)
    xs = pl.BlockSpec((1, N_CHUNKS_PAD, CHUNK_W), lambda i: (i, 0, 0))
    os_ = pl.BlockSpec((1, NSA_KV_GROUPS, N_CHUNKS_PAD, HEAD_DIM), lambda i: (i, 0, 0, 0))
    osd = jax.ShapeDtypeStruct((b, NSA_KV_GROUPS, N_CHUNKS_PAD, HEAD_DIM), BF16)
    return pl.pallas_call(
        _compress_body,
        grid=(b,),
        in_specs=[xs, xs, _resident(pk.shape), _resident(pv.shape), _resident(wk1.shape),
                  _resident(wv1.shape), _resident(wk2.shape), _resident(wv2.shape)],
        out_specs=(os_, os_),
        out_shape=(osd, osd),
        compiler_params=_params(("parallel",)),
        name="compress",
    )(xk, xv, pk, pv, wk1, wv1, wk2, wv2)


def _bias_lookup(tbl_ref, head, dist):
    acc = jnp.full(dist.shape, tbl_ref[0, head], F32)
    for k, thr in enumerate(BUCKET_THRESHOLDS):
        acc = jnp.where(dist >= thr, tbl_ref[k + 1, head], acc)
    return acc - tbl_ref[NUM_BUCKETS - 1, head]


def _relbias_body(tbl_ref, toep_ref, bc_ref):
    head = pl.program_id(0)
    t = ATT_TILE
    i = lax.broadcasted_iota(jnp.int32, (t, 2 * t), 0)
    u = lax.broadcasted_iota(jnp.int32, (t, 2 * t), 1)
    d = i - u + t
    toep_ref[0] = jnp.where(d >= 0, _bias_lookup(tbl_ref, head, d), NEG_INF)
    s = bc_ref.shape[1]
    tq = lax.broadcasted_iota(jnp.int32, (s, N_CHUNKS_PAD), 0)
    c = lax.broadcasted_iota(jnp.int32, (s, N_CHUNKS_PAD), 1)
    dc = tq - (c * CMP_STRIDE + CMP_BLOCK - 1)
    bc_ref[0] = jnp.where(dc >= 0, _bias_lookup(tbl_ref, head, dc), NEG_INF)


def _relbias(tbl, s):
    return pl.pallas_call(
        _relbias_body,
        grid=(NSA_HEADS,),
        in_specs=[pl.BlockSpec(memory_space=pltpu.SMEM)],
        out_specs=(pl.BlockSpec((1, ATT_TILE, 2 * ATT_TILE), lambda h: (h, 0, 0)),
                   pl.BlockSpec((1, s, N_CHUNKS_PAD), lambda h: (h, 0, 0))),
        out_shape=(jax.ShapeDtypeStruct((NSA_HEADS, ATT_TILE, 2 * ATT_TILE), F32),
                   jax.ShapeDtypeStruct((NSA_HEADS, s, N_CHUNKS_PAD), F32)),
        compiler_params=_params(("parallel",)),
        name="relbias",
    )(tbl)


def _lane_fold(x, op):
    return functools.reduce(op, [x[:, i * LANES:(i + 1) * LANES] for i in range(x.shape[1] // LANES)])


def _lane_tile(x, width):
    return jnp.concatenate([x] * (width // LANES), axis=1)


def _row_max_tile(rmax):
    return jnp.broadcast_to(jnp.max(rmax, axis=-1, keepdims=True), rmax.shape)


def _normalize(acc):
    return (acc / pltpu.roll(acc, HEAD_DIM, axis=1))[:, :HEAD_DIM]


def _split3(x):
    hi = x.astype(BF16)
    r = x - hi.astype(F32)
    mid = r.astype(BF16)
    lo = (r - mid.astype(F32)).astype(BF16)
    return hi, mid, lo


def _nsa_body(q_ref, kc_ref, vc_ref, ks_ref, vs_ref, kw_ref, vw_ref, gate_ref, toep_ref, bc_ref,
              ov_ref, o_ref, s_ref, near_ref, mb_ref, acc_ref, part_ref):
    t = ATT_TILE
    j4 = NSA_HPG
    rows = j4 * t
    groups = range(NSA_KV_GROUPS)
    m = pl.program_id(1)
    m1 = jnp.maximum(m - 1, 0)
    m2 = jnp.maximum(m - 2, 0)
    no_prev = jnp.where(m >= 1, 0.0, NEG_INF)
    n_sel = s_ref.shape[1] * (t // SEL_BLOCK)

    def heads(x):
        return x.reshape(j4, t, x.shape[-1])

    def flat(x):
        return x.reshape(rows, x.shape[-1])

    def key_tile(ref, g, n):
        return ref[0, g, pl.ds(pl.multiple_of(n * t, t), t), :]

    def biases(g):
        toep = toep_ref[g * j4:(g + 1) * j4]
        return toep[:, :, t:], toep[:, :, :t] + no_prev

    qs = []
    for g in groups:
        qt = q_ref[0, :, g * j4 * HEAD_DIM:(g + 1) * j4 * HEAD_DIM]
        qs.append(jnp.concatenate([qt[:, j * HEAD_DIM:(j + 1) * HEAD_DIM] for j in range(j4)], axis=0))

    def compressed(g):
        bc = bc_ref[g * j4:(g + 1) * j4]
        sc = heads(_dot_nt(qs[g], kc_ref[0, g])) + bc
        valid = bc > 0.5 * NEG_INF
        e = jnp.where(valid, jnp.exp(sc - jnp.max(sc, axis=-1, keepdims=True)), 0.0)
        l = jnp.sum(e, axis=-1, keepdims=True)
        pc = jnp.where(l > 0.0, e / l, 0.0)
        o_c = _dot(flat(pc).astype(BF16), vc_ref[0, g])
        psum = pc[0] + pc[1] + pc[2] + pc[3]
        ov = ov_ref[...]
        imp = sum(_dot(part, ov) for part in _split3(psum)).T[:n_sel]
        blk = lax.broadcasted_iota(jnp.int32, (n_sel, t), 0)
        tpos = m * t + lax.broadcasted_iota(jnp.int32, (n_sel, t), 1)
        cur = tpos // SEL_BLOCK
        bonus = jnp.where(blk == 0, FORCE_BONUS,
                          jnp.where(blk == cur, FORCE_BONUS, jnp.where(blk == cur - 1, FORCE_BONUS, 0.0)))
        imp = jnp.where(blk * SEL_BLOCK <= tpos, imp + bonus, NEG_INF)
        rank = jnp.zeros((n_sel, t), F32)
        for i in range(n_sel):
            row = imp[i:i + 1, :]
            before = jnp.where(blk > i, 1.0, 0.0)
            rank = rank + jnp.where(row > imp, 1.0, jnp.where(row == imp, before, 0.0))
        selb = jnp.where(rank < float(SEL_TOP_N), 0.0, NEG_INF)
        selb = jnp.concatenate([selb, jnp.zeros((LANES - n_sel, t), F32)], axis=0).T
        return o_c, selb[:, :HEAD_DIM].astype(BF16)

    def window(g):
        diag_bias, prev_bias = biases(g)
        ii = lax.broadcasted_iota(jnp.int32, (t, t), 0)
        jj = lax.broadcasted_iota(jnp.int32, (t, t), 1)
        tri = jnp.where(jj > ii, 0.0, NEG_INF) + jnp.where(m >= 2, 0.0, NEG_INF)
        w0 = flat(heads(_dot_nt(qs[g], key_tile(kw_ref, g, m))) + diag_bias)
        w1 = flat(heads(_dot_nt(qs[g], key_tile(kw_ref, g, m1))) + prev_bias)
        w2 = flat(heads(_dot_nt(qs[g], key_tile(kw_ref, g, m2))) + tri[None])
        wmax = _lane_tile(_row_max_tile(functools.reduce(
            jnp.maximum, [_lane_fold(w, jnp.maximum) for w in (w0, w1, w2)])), t)
        acc = sum(_dot(jnp.exp(w - wmax).astype(BF16), key_tile(vw_ref, g, n))
                  for w, n in ((w0, m), (w1, m1), (w2, m2)))
        return _normalize(acc)

    def gated(g, branch, o):
        gate = jax.nn.sigmoid(gate_ref[0, g])
        return [gate[:, branch * j4 + j:branch * j4 + j + 1] * o[j * t:(j + 1) * t] for j in range(j4)]

    cmp_out = [compressed(g) for g in groups]
    part = []
    for g in groups:
        part += [a + b for a, b in zip(gated(g, 0, cmp_out[g][0]), gated(g, 2, window(g)))]
    part_ref[...] = jnp.concatenate(part, axis=1)

    q_aug = []
    for g in groups:
        diag_bias, prev_bias = biases(g)
        q_aug.append(jnp.concatenate([qs[g], jnp.concatenate([cmp_out[g][1]] * j4, axis=0)], axis=1))
        s0 = flat(heads(_dot_nt(q_aug[g], key_tile(ks_ref, g, m))) + diag_bias)
        s1 = flat(heads(_dot_nt(q_aug[g], key_tile(ks_ref, g, m1))) + prev_bias)
        near_ref[g, 0] = s0
        near_ref[g, 1] = s1
        mb_ref[g] = jnp.maximum(_lane_fold(s0, jnp.maximum), _lane_fold(s1, jnp.maximum))

    @pl.loop(0, m1)
    def _(n):
        for g in groups:
            s = _dot_nt(q_aug[g], key_tile(ks_ref, g, n))
            s_ref[g, n] = s
            mb_ref[g] = jnp.maximum(mb_ref[g], _lane_fold(s, jnp.maximum))

    for g in groups:
        mb_ref[g] = _row_max_tile(mb_ref[g])

    def probs(g, s):
        return jnp.exp(s - _lane_tile(mb_ref[g], t)).astype(BF16)

    for g in groups:
        acc_ref[g] = (_dot(probs(g, near_ref[g, 0]), key_tile(vs_ref, g, m))
                      + _dot(probs(g, near_ref[g, 1]), key_tile(vs_ref, g, m1)))

    @pl.loop(0, m1)
    def _(n):
        for g in groups:
            acc_ref[g] += _dot(probs(g, s_ref[g, n]), key_tile(vs_ref, g, n))

    outs = []
    for g in groups:
        outs += gated(g, 1, _normalize(acc_ref[g]))
    o_ref[0] = (part_ref[...] + jnp.concatenate(outs, axis=1)).astype(BF16)


def _overlap_table(s):
    n_sel = s // SEL_BLOCK
    nc = N_CHUNKS_PAD
    c0 = np.arange(nc)[:, None] * CMP_STRIDE
    s0 = np.arange(n_sel)[None, :] * SEL_BLOCK
    ov = np.clip(np.minimum(c0 + CMP_BLOCK, s0 + SEL_BLOCK) - np.maximum(c0, s0), 0, None) / CMP_STRIDE
    ov[nc - 1] = 0.0
    return jnp.asarray(np.pad(ov, ((0, 0), (0, LANES - n_sel))), BF16)


def _nsa(qn, kc, vc, ks, vs, kw, vw, gates, toep, bc):
    b, s, _ = qn.shape
    t = ATT_TILE
    assert s // SEL_BLOCK <= HEAD_DIM
    ov = _overlap_table(s)
    rows = NSA_HPG * t
    ng = NSA_KV_GROUPS
    seq = lambda n, w=HEAD_DIM: pl.BlockSpec((1, ng, n, w), lambda i, j: (i, 0, 0, 0))
    return pl.pallas_call(
        _nsa_body,
        grid=(b, s // t),
        in_specs=[pl.BlockSpec((1, t, NSA_Q), lambda i, j: (i, j, 0)),
                  seq(N_CHUNKS_PAD), seq(N_CHUNKS_PAD),
                  seq(s, 2 * HEAD_DIM), seq(s, 2 * HEAD_DIM), seq(s), seq(s, 2 * HEAD_DIM),
                  pl.BlockSpec((1, ng, t, GATE_PAD), lambda i, j: (i, 0, j, 0)),
                  _resident((NSA_HEADS, t, 2 * t)),
                  pl.BlockSpec((NSA_HEADS, t, N_CHUNKS_PAD), lambda i, j: (0, j, 0)),
                  _resident(ov.shape)],
        out_specs=pl.BlockSpec((1, t, NSA_Q), lambda i, j: (i, j, 0)),
        out_shape=jax.ShapeDtypeStruct((b, s, NSA_Q), BF16),
        scratch_shapes=[pltpu.VMEM((ng, s // t, rows, t), F32), pltpu.VMEM((ng, 2, rows, t), F32),
                        pltpu.VMEM((ng, rows, LANES), F32), pltpu.VMEM((ng, rows, 2 * HEAD_DIM), F32),
                        pltpu.VMEM((t, NSA_Q), F32)],
        compiler_params=_params(("parallel", "parallel")),
        name="nsa",
    )(qn, kc, vc, ks, vs, kw, vw, gates, toep, bc, ov)


def _fcum_body(f_ref, b_ref, o_ref):
    z = f_ref[0] + b_ref[...]
    x = -(jnp.maximum(-z, 0.0) + jnp.log1p(jnp.exp(-jnp.abs(z))))
    n = x.shape[-1]
    lane = lax.broadcasted_iota(jnp.int32, x.shape, 1)
    sh = 1
    while sh < n:
        x = x + jnp.where(lane >= sh, pltpu.roll(x, sh, axis=1), 0.0)
        sh *= 2
    o_ref[0] = x


def _fcum(f_t, b_forget):
    b, h, s = f_t.shape
    blk = pl.BlockSpec((1, h, s), lambda i: (i, 0, 0))
    return pl.pallas_call(
        _fcum_body,
        grid=(b,),
        in_specs=[blk, _resident((h, 1))],
        out_specs=blk,
        out_shape=jax.ShapeDtypeStruct((b, h, s), F32),
        compiler_params=_params(("parallel",)),
        name="fcum",
    )(f_t, b_forget.reshape(h, 1))


FOX_PAIR = 2
FOX_TILE = 512


def _causal_bias(t):
    i = np.arange(t)
    return jnp.asarray(np.where(i[None, :] <= i[:, None], 0.0, NEG_INF), F32)


def _fox_body(q_ref, k_ref, v_ref, c_ref, cm_ref, o_ref, s_ref, mb_ref, acc_ref):
    t = FOX_TILE
    m = pl.program_id(2)
    heads = range(FOX_PAIR)
    qs = [q_ref[0, hh] for hh in heads]

    def keys(n):
        return pl.ds(pl.multiple_of(n * t, t), t)

    def scores(hh, n):
        return _dot_nt(qs[hh], k_ref[0, hh, keys(n), :]) - c_ref[0, hh, :, keys(n)]

    for hh in heads:
        s = scores(hh, m) + cm_ref[...]
        s_ref[hh, m] = s
        mb_ref[hh] = _lane_fold(s, jnp.maximum)

    @pl.loop(0, m)
    def _(n):
        for hh in heads:
            s = scores(hh, n)
            s_ref[hh, n] = s
            mb_ref[hh] = jnp.maximum(mb_ref[hh], _lane_fold(s, jnp.maximum))

    for hh in heads:
        mb_ref[hh] = _row_max_tile(mb_ref[hh])
        acc_ref[hh] = jnp.zeros((t, 2 * HEAD_DIM), F32)

    @pl.loop(0, m + 1)
    def _(n):
        for hh in heads:
            p = jnp.exp(s_ref[hh, n] - _lane_tile(mb_ref[hh], t)).astype(BF16)
            acc_ref[hh] += _dot(p, v_ref[0, hh, keys(n), :])

    o_ref[0] = jnp.concatenate([_normalize(acc_ref[hh]) for hh in heads], axis=1).astype(BF16)


def _fox(qf, kf, vf, c):
    b, h, s, _ = qf.shape
    t = FOX_TILE
    seq = lambda w: pl.BlockSpec((1, FOX_PAIR, s, w), lambda i, p, j: (i, p, 0, 0))
    return pl.pallas_call(
        _fox_body,
        grid=(b, h // FOX_PAIR, s // t),
        in_specs=[pl.BlockSpec((1, FOX_PAIR, t, HEAD_DIM), lambda i, p, j: (i, p, j, 0)),
                  seq(HEAD_DIM), seq(2 * HEAD_DIM),
                  pl.BlockSpec((1, FOX_PAIR, 1, s), lambda i, p, j: (i, p, 0, 0)),
                  _resident((t, t))],
        out_specs=pl.BlockSpec((1, t, FOX_PAIR * HEAD_DIM), lambda i, p, j: (i, j, p)),
        out_shape=jax.ShapeDtypeStruct((b, s, h * HEAD_DIM), BF16),
        scratch_shapes=[pltpu.VMEM((FOX_PAIR, s // t, t, t), F32),
                        pltpu.VMEM((FOX_PAIR, t, LANES), F32),
                        pltpu.VMEM((FOX_PAIR, t, 2 * HEAD_DIM), F32)],
        compiler_params=_params(("parallel", "parallel", "parallel")),
        name="fox",
    )(qf, kf, vf, c.reshape(b, h, 1, s), _causal_bias(t))


def _mixout_body(x_ref, g_ref, on_ref, of_ref, wa_ref, wb_ref, wun_ref, wuf_ref, wo_ref, o_ref):
    x = x_ref[...]
    h = _rms(x, g_ref[...]).astype(BF16)
    y = (jax.nn.sigmoid(_dot(h, wa_ref[...])) * _dot(on_ref[...], wun_ref[...])
         + jax.nn.sigmoid(_dot(h, wb_ref[...])) * _dot(of_ref[...], wuf_ref[...]))
    o_ref[...] = x + _dot(y.astype(BF16), wo_ref[...])


def _mixout(x2d, g, o_nsa, o_fox, w_a, w_b, w_un, w_uf, w_o):
    n = x2d.shape[0]
    tok = lambda c: pl.BlockSpec((TOK_TILE, c), lambda i: (i, 0))
    return pl.pallas_call(
        _mixout_body,
        grid=(n // TOK_TILE,),
        in_specs=[tok(D_MODEL), _resident((1, D_MODEL)), tok(NSA_Q), tok(FOX_W),
                  _resident(w_a.shape), _resident(w_b.shape), _resident(w_un.shape),
                  _resident(w_uf.shape), _resident(w_o.shape)],
        out_specs=tok(D_MODEL),
        out_shape=jax.ShapeDtypeStruct((n, D_MODEL), F32),
        compiler_params=_params(("parallel",)),
        name="mixout",
    )(x2d, g.reshape(1, D_MODEL), o_nsa, o_fox, w_a, w_b, w_un, w_uf, w_o)


def _memkv_body(m_ref, g_ref, w_ref, k_ref, v_ref):
    h = _rms(m_ref[0], g_ref[...]).astype(BF16)
    z = _dot(h, w_ref[...])
    k_ref[0] = z[:, :D_MODEL].astype(BF16)
    v_ref[0] = z[:, D_MODEL:].astype(BF16)


def _memkv(mem, g, w_kv):
    b, ml, _ = mem.shape
    blk = pl.BlockSpec((1, ml, D_MODEL), lambda i: (i, 0, 0))
    sd = jax.ShapeDtypeStruct((b, ml, D_MODEL), BF16)
    return pl.pallas_call(
        _memkv_body,
        grid=(b,),
        in_specs=[blk, _resident((1, D_MODEL)), _resident(w_kv.shape)],
        out_specs=(blk, blk),
        out_shape=(sd, sd),
        compiler_params=_params(("parallel",)),
        name="memkv",
    )(mem, g.reshape(1, D_MODEL), w_kv)


def _memattn_body(x_ref, g_ref, k_ref, v_ref, wq_ref, wo_ref, o_ref):
    x = x_ref[0]
    h = _rms(x, g_ref[...]).astype(BF16)
    q = (_dot(h, wq_ref[...]) * (MEM_HEAD_DIM ** -0.5)).astype(BF16)
    outs = []
    for hd in range(MEM_HEADS):
        cols = slice(hd * MEM_HEAD_DIM, (hd + 1) * MEM_HEAD_DIM)
        s = _dot_nt(q[:, cols], k_ref[0, :, cols])
        e = jnp.exp(s - jnp.max(s, axis=-1, keepdims=True))
        p = e / jnp.sum(e, axis=-1, keepdims=True)
        outs.append(_dot(p.astype(BF16), v_ref[0, :, cols]))
    o = jnp.concatenate(outs, axis=1).astype(BF16)
    o_ref[0] = x + _dot(o, wo_ref[...])


def _memattn(x, g, k, v, w_q, w_o):
    b, s, _ = x.shape
    ml = k.shape[1]
    tok = pl.BlockSpec((1, TOK_TILE, D_MODEL), lambda i, j: (i, j, 0))
    kvb = pl.BlockSpec((1, ml, D_MODEL), lambda i, j: (i, 0, 0))
    return pl.pallas_call(
        _memattn_body,
        grid=(b, s // TOK_TILE),
        in_specs=[tok, _resident((1, D_MODEL)), kvb, kvb, _resident(w_q.shape), _resident(w_o.shape)],
        out_specs=tok,
        out_shape=jax.ShapeDtypeStruct((b, s, D_MODEL), F32),
        compiler_params=_params(("parallel", "parallel")),
        name="memattn",
    )(x, g.reshape(1, D_MODEL), k, v, w_q, w_o)


def kernel(x, mem, rel_bias_table, ffn1_norm, ffn1_w_gate, ffn1_w_up, ffn1_w_down, mix_norm, mix_w_in, mix_b_forget, cmp_pos_k, cmp_pos_v, cmp_k_w1, cmp_k_w2, cmp_v_w1, cmp_v_w2, w_up_nsa, w_up_fox, mix_w_out, mem_q_norm, mem_kv_norm, mem_w_q, mem_w_kv, mem_w_o, ffn2_norm, ffn2_w_gate, ffn2_w_up, ffn2_w_down, final_norm):
    b, s, d = x.shape
    depth = ffn1_norm.shape[0]
    bf = lambda w: w.astype(BF16)
    toep, bias_c = _relbias(rel_bias_table, s)
    x = x.reshape(b * s, d)
    for l in range(depth):
        last = l == depth - 1
        x = _ffn(x, ffn1_norm[l], bf(ffn1_w_gate[l]), bf(ffn1_w_up[l]), bf(ffn1_w_down[l]))

        w_main, w_a, w_b = _pack_w_in(mix_w_in[l])
        qn, kc, vc, ks, vs, kw, vw, qf, kf, vf, gates, flog = _inproj(x.reshape(b, s, d), mix_norm[l], w_main)
        kcc, vcc = _compress(kc, vc, cmp_pos_k[l], cmp_pos_v[l], cmp_k_w1[l], cmp_k_w2[l],
                             cmp_v_w1[l], cmp_v_w2[l])
        o_nsa = _nsa(qn, kcc, vcc, ks, vs, kw, vw, gates, toep, bias_c)
        c = _fcum(flog.transpose(0, 2, 1), mix_b_forget[l])
        o_fox = _fox(qf, kf, vf, c)
        x = _mixout(x, mix_norm[l], o_nsa.reshape(b * s, NSA_Q), o_fox.reshape(b * s, FOX_W),
                    w_a, w_b, bf(w_up_nsa[l]), bf(w_up_fox[l]), bf(mix_w_out[l]))

        mk, mv = _memkv(mem, mem_kv_norm[l], bf(mem_w_kv[l]))
        x = _memattn(x.reshape(b, s, d), mem_q_norm[l], mk, mv, bf(mem_w_q[l]), bf(mem_w_o[l]))
        x = _ffn(x.reshape(b * s, d), ffn2_norm[l], bf(ffn2_w_gate[l]), bf(ffn2_w_up[l]),
                 bf(ffn2_w_down[l]), final_g=final_norm if last else None)
    return x.reshape(b, s, d)
```

```python
import functools
import math

import numpy as np
import jax
import jax.numpy as jnp
from jax import lax
from jax.experimental import pallas as pl
from jax.experimental.pallas import tpu as pltpu

D_MODEL = 1024
D_FF = 2816
HEAD_DIM = 64
NSA_HEADS = 8
NSA_KV_GROUPS = 2
NSA_HPG = NSA_HEADS // NSA_KV_GROUPS
CMP_BLOCK = 32
CMP_STRIDE = 16
CMP_HIDDEN = 256
SEL_BLOCK = 64
SEL_TOP_N = 16
WINDOW = 512
FOX_HEADS = 8
MEM_HEADS = 4
MEM_HEAD_DIM = D_MODEL // MEM_HEADS
NUM_BUCKETS = 32
MAX_DISTANCE = 128
RMS_EPS = 1e-6
NEG_INF = -1e30
FORCE_BONUS = 1e4

NSA_Q = NSA_HEADS * HEAD_DIM
NSA_KV = NSA_KV_GROUPS * HEAD_DIM
FOX_W = FOX_HEADS * HEAD_DIM

LANES = 128
VMEM_LIMIT = 56 * 1024 * 1024

BF16 = jnp.bfloat16
F32 = jnp.float32

ATT_TILE = 256
TOK_TILE = 512
FF_CHUNK = 1408


def _bucket_thresholds():
    n = np.arange(0, 4 * MAX_DISTANCE)
    exact = NUM_BUCKETS // 2
    large = exact + (np.log(np.maximum(n, 1) / exact) / math.log(MAX_DISTANCE / exact)
                     * (NUM_BUCKETS - exact)).astype(np.int64)
    bucket = np.where(n < exact, n, np.minimum(large, NUM_BUCKETS - 1))
    assert np.all(np.diff(bucket) >= 0)
    return [int(np.argmax(bucket >= k)) for k in range(1, NUM_BUCKETS)]


BUCKET_THRESHOLDS = _bucket_thresholds()


def _dot(a, b):
    return jnp.dot(a, b, preferred_element_type=F32)


def _dot_nt(a, b):
    return lax.dot_general(a, b, (((1,), (1,)), ((), ())), preferred_element_type=F32)


def _rms(x, g):
    return x * lax.rsqrt(jnp.mean(x * x, axis=-1, keepdims=True) + RMS_EPS) * g


def _resident(shape):
    nd = len(shape)
    return pl.BlockSpec(shape, lambda *_: (0,) * nd, pipeline_mode=pl.Buffered(1))


def _params(sem):
    return pltpu.CompilerParams(dimension_semantics=sem, vmem_limit_bytes=VMEM_LIMIT)


def _ffn_body(x_ref, g_ref, wg_ref, wu_ref, wd_ref, *rest, final):
    if final:
        fg_ref, o_ref = rest
    else:
        (o_ref,) = rest
    x = x_ref[...]
    h = _rms(x, g_ref[...]).astype(BF16)
    acc = jnp.zeros(x.shape, F32)
    for c in range(D_FF // FF_CHUNK):
        sl = slice(c * FF_CHUNK, (c + 1) * FF_CHUNK)
        a = _dot(h, wg_ref[:, sl])
        b = _dot(h, wu_ref[:, sl])
        t = (a * jax.nn.sigmoid(a)) * b
        acc = acc + _dot(t.astype(BF16), wd_ref[sl, :])
    y = x + 0.5 * acc
    if final:
        y = _rms(y, fg_ref[...])
    o_ref[...] = y


def _ffn(x2d, g, wg, wu, wd, final_g=None):
    n = x2d.shape[0]
    final = final_g is not None
    tok = pl.BlockSpec((TOK_TILE, D_MODEL), lambda i: (i, 0))
    in_specs = [tok, _resident((1, D_MODEL)), _resident((D_MODEL, D_FF)),
                _resident((D_MODEL, D_FF)), _resident((D_FF, D_MODEL))]
    args = [x2d, g.reshape(1, D_MODEL), wg, wu, wd]
    if final:
        in_specs.append(_resident((1, D_MODEL)))
        args.append(final_g.reshape(1, D_MODEL))
    return pl.pallas_call(
        functools.partial(_ffn_body, final=final),
        grid=(n // TOK_TILE,),
        in_specs=in_specs,
        out_specs=tok,
        out_shape=jax.ShapeDtypeStruct((n, D_MODEL), F32),
        compiler_params=_params(("parallel",)),
        name="ffn_final" if final else "ffn",
    )(*args)


_C_QN = 0
_C_CMP = _C_QN + NSA_Q
_C_KV = _C_CMP + 2 * NSA_KV
_C_FOX = _C_KV + 4 * NSA_KV
_C_SMALL = _C_FOX + 3 * FOX_W
_C_END = _C_SMALL + LANES
GATE_PAD = 16


def _inproj_body(x_ref, g_ref, w_ref, blk_ref, qn_ref, kc_ref, vc_ref, ks_ref, vs_ref, kw_ref, vw_ref,
                 qf_ref, kf_ref, vf_ref, gate_ref, fl_ref):
    h = _rms(x_ref[0], g_ref[...]).astype(BF16)
    z = _dot(h, w_ref[...])
    qn_ref[0] = (z[:, _C_QN:_C_CMP] * (HEAD_DIM ** -0.5)).astype(BF16)
    kc_ref[0] = z[:, _C_CMP:_C_CMP + NSA_KV].astype(BF16)
    vc_ref[0] = z[:, _C_CMP + NSA_KV:_C_KV].astype(BF16)
    ones = jnp.ones((z.shape[0], HEAD_DIM), BF16)
    for i, (ref, extra) in enumerate(((ks_ref, blk_ref[...]), (vs_ref, ones), (kw_ref, None),
                                      (vw_ref, ones))):
        for g in range(NSA_KV_GROUPS):
            c0 = _C_KV + i * NSA_KV + g * HEAD_DIM
            val = z[:, c0:c0 + HEAD_DIM].astype(BF16)
            ref[0, g] = val if extra is None else jnp.concatenate([val, extra], axis=1)
    for hd in range(FOX_HEADS):
        c0 = _C_FOX + hd * HEAD_DIM
        qf_ref[0, hd] = (z[:, c0:c0 + HEAD_DIM] * (HEAD_DIM ** -0.5)).astype(BF16)
        kf_ref[0, hd] = z[:, c0 + FOX_W:c0 + FOX_W + HEAD_DIM].astype(BF16)
        v = z[:, c0 + 2 * FOX_W:c0 + 2 * FOX_W + HEAD_DIM].astype(BF16)
        vf_ref[0, hd] = jnp.concatenate([v, ones], axis=1)
    for g in range(NSA_KV_GROUPS):
        c0 = _C_SMALL + g * GATE_PAD
        gate_ref[0, g] = z[:, c0:c0 + GATE_PAD]
    c0 = _C_SMALL + NSA_KV_GROUPS * GATE_PAD
    fl_ref[0] = z[:, c0:c0 + FOX_HEADS]


def _pack_w_in(w_in):
    cols = np.cumsum((0, NSA_Q, NSA_KV, NSA_KV, NSA_KV, NSA_KV, NSA_KV, NSA_KV, 3 * NSA_HEADS,
                      FOX_W, FOX_W, FOX_W, FOX_HEADS, D_MODEL, D_MODEL))
    g0 = int(cols[7])
    gate_cols = []
    for g in range(NSA_KV_GROUPS):
        idx = [g0 + br * NSA_HEADS + g * NSA_HPG + j for br in range(3) for j in range(NSA_HPG)]
        gate_cols.append(w_in[:, np.array(idx)])
        gate_cols.append(jnp.zeros((D_MODEL, GATE_PAD - len(idx)), w_in.dtype))
    small = jnp.concatenate(gate_cols + [w_in[:, int(cols[11]):int(cols[12])]], axis=1)
    small = jnp.pad(small, ((0, 0), (0, LANES - small.shape[1])))
    main = jnp.concatenate([w_in[:, :g0], w_in[:, int(cols[8]):int(cols[11])], small], axis=1)
    w_a = w_in[:, int(cols[12]):int(cols[13])]
    w_b = w_in[:, int(cols[13]):int(cols[14])]
    return main.astype(BF16), w_a.astype(BF16), w_b.astype(BF16)


def _inproj(x, g, w_main):
    b, s, _ = x.shape
    grid = (b, s // TOK_TILE)
    tok = lambda c: pl.BlockSpec((1, TOK_TILE, c), lambda i, j: (i, j, 0))
    heads = lambda nh, w=HEAD_DIM: pl.BlockSpec((1, nh, TOK_TILE, w), lambda i, j: (i, 0, j, 0))
    sds = jax.ShapeDtypeStruct
    kv = sds((b, NSA_KV_GROUPS, s, HEAD_DIM), BF16)
    kv2 = sds((b, NSA_KV_GROUPS, s, 2 * HEAD_DIM), BF16)
    fx = sds((b, FOX_HEADS, s, HEAD_DIM), BF16)
    out_shape = (sds((b, s, NSA_Q), BF16), sds((b, s, NSA_KV), BF16), sds((b, s, NSA_KV), BF16),
                 kv2, kv2, kv, kv2, fx, fx, sds((b, FOX_HEADS, s, 2 * HEAD_DIM), BF16),
                 sds((b, NSA_KV_GROUPS, s, GATE_PAD), F32), sds((b, s, FOX_HEADS), F32))
    out_specs = (tok(NSA_Q), tok(NSA_KV), tok(NSA_KV),
                 heads(NSA_KV_GROUPS, 2 * HEAD_DIM), heads(NSA_KV_GROUPS, 2 * HEAD_DIM),
                 heads(NSA_KV_GROUPS), heads(NSA_KV_GROUPS, 2 * HEAD_DIM),
                 heads(FOX_HEADS), heads(FOX_HEADS), heads(FOX_HEADS, 2 * HEAD_DIM),
                 pl.BlockSpec((1, NSA_KV_GROUPS, TOK_TILE, GATE_PAD), lambda i, j: (i, 0, j, 0)),
                 tok(FOX_HEADS))
    key_blk = np.arange(s) // SEL_BLOCK
    blk_onehot = jnp.asarray(key_blk[:, None] == np.arange(HEAD_DIM)[None, :], BF16)
    return pl.pallas_call(
        _inproj_body,
        grid=grid,
        in_specs=[tok(D_MODEL), _resident((1, D_MODEL)), _resident((D_MODEL, _C_END)),
                  pl.BlockSpec((TOK_TILE, HEAD_DIM), lambda i, j: (j, 0))],
        out_specs=out_specs,
        out_shape=out_shape,
        compiler_params=_params(("parallel", "parallel")),
        name="inproj",
    )(x, g.reshape(1, D_MODEL), w_main, blk_onehot)


N_CHUNKS_PAD = 128
CHUNK_W = CMP_STRIDE * NSA_KV


def _compress_body(xk_ref, xv_ref, pk_ref, pv_ref, wk1_ref, wv1_ref, wk2_ref, wv2_ref, ok_ref, ov_ref):
    for x_ref, p_ref, w1_ref, w2_ref, o_ref in ((xk_ref, pk_ref, wk1_ref, wk2_ref, ok_ref),
                                                (xv_ref, pv_ref, wv1_ref, wv2_ref, ov_ref)):
        x = x_ref[0]
        a0 = _dot(x, w1_ref[0])
        a1 = _dot(x, w1_ref[1])
        c = _dot(p_ref[0], w1_ref[0]) + _dot(p_ref[1], w1_ref[1])
        pre = a0 + pltpu.roll(a1, N_CHUNKS_PAD - 1, axis=0) + c[0:1, :]
        hid = jax.nn.gelu(pre).astype(BF16)
        out = _dot(hid, w2_ref[...])
        for g in range(NSA_KV_GROUPS):
            o_ref[0, g] = out[:, g * HEAD_DIM:(g + 1) * HEAD_DIM].astype(BF16)


def _pack_compress(pos, w1, w2):
    r = CMP_BLOCK // CMP_STRIDE
    w1r = w1.reshape(r, CMP_STRIDE, HEAD_DIM, CMP_HIDDEN)
    eye = jnp.eye(NSA_KV_GROUPS, dtype=w1.dtype)
    w1big = jnp.einsum('irdk,gh->irgdhk', w1r, eye).reshape(r, CHUNK_W, NSA_KV_GROUPS * CMP_HIDDEN)
    w2big = jnp.einsum('kd,gh->gkhd', w2, eye).reshape(NSA_KV_GROUPS * CMP_HIDDEN, NSA_KV)
    p = pos.reshape(r, CMP_STRIDE, 1, HEAD_DIM)
    p = jnp.broadcast_to(p, (r, CMP_STRIDE, NSA_KV_GROUPS, HEAD_DIM)).reshape(r, 1, CHUNK_W)
    p = jnp.broadcast_to(p, (r, 16, CHUNK_W))
    return p.astype(BF16), w1big.astype(BF16), w2big.astype(BF16)


def _compress(kc, vc, pos_k, pos_v, k_w1, k_w2, v_w1, v_w2):
    b, s, _ = kc.shape
    xk = kc.reshape(b, N_CHUNKS_PAD, CHUNK_W)
    xv = vc.reshape(b, N_CHUNKS_PAD, CHUNK_W)
    pk, wk1, wk2 = _pack_compress(pos_k, k_w1, k_w2)
    pv, wv1, wv2 = _pack_compress(pos_v, v_w1, v_w2)
    xs = pl.BlockSpec((1, N_CHUNKS_PAD, CHUNK_W), lambda i: (i, 0, 0))
    os_ = pl.BlockSpec((1, NSA_KV_GROUPS, N_CHUNKS_PAD, HEAD_DIM), lambda i: (i, 0, 0, 0))
    osd = jax.ShapeDtypeStruct((b, NSA_KV_GROUPS, N_CHUNKS_PAD, HEAD_DIM), BF16)
    return pl.pallas_call(
        _compress_body,
        grid=(b,),
        in_specs=[xs, xs, _resident(pk.shape), _resident(pv.shape), _resident(wk1.shape),
                  _resident(wv1.shape), _resident(wk2.shape), _resident(wv2.shape)],
        out_specs=(os_, os_),
        out_shape=(osd, osd),
        compiler_params=_params(("parallel",)),
        name="compress",
    )(xk, xv, pk, pv, wk1, wv1, wk2, wv2)


def _bias_lookup(tbl_ref, head, dist):
    acc = jnp.full(dist.shape, tbl_ref[0, head], F32)
    for k, thr in enumerate(BUCKET_THRESHOLDS):
        acc = jnp.where(dist >= thr, tbl_ref[k + 1, head], acc)
    return acc - tbl_ref[NUM_BUCKETS - 1, head]


def _relbias_body(tbl_ref, toep_ref, bc_ref):
    head = pl.program_id(0)
    t = ATT_TILE
    i = lax.broadcasted_iota(jnp.int32, (t, 2 * t), 0)
    u = lax.broadcasted_iota(jnp.int32, (t, 2 * t), 1)
    d = i - u + t
    toep_ref[0] = jnp.where(d >= 0, _bias_lookup(tbl_ref, head, d), NEG_INF)
    s = bc_ref.shape[1]
    tq = lax.broadcasted_iota(jnp.int32, (s, N_CHUNKS_PAD), 0)
    c = lax.broadcasted_iota(jnp.int32, (s, N_CHUNKS_PAD), 1)
    dc = tq - (c * CMP_STRIDE + CMP_BLOCK - 1)
    bc_ref[0] = jnp.where(dc >= 0, _bias_lookup(tbl_ref, head, dc), NEG_INF)


def _relbias(tbl, s):
    return pl.pallas_call(
        _relbias_body,
        grid=(NSA_HEADS,),
        in_specs=[pl.BlockSpec(memory_space=pltpu.SMEM)],
        out_specs=(pl.BlockSpec((1, ATT_TILE, 2 * ATT_TILE), lambda h: (h, 0, 0)),
                   pl.BlockSpec((1, s, N_CHUNKS_PAD), lambda h: (h, 0, 0))),
        out_shape=(jax.ShapeDtypeStruct((NSA_HEADS, ATT_TILE, 2 * ATT_TILE), F32),
                   jax.ShapeDtypeStruct((NSA_HEADS, s, N_CHUNKS_PAD), F32)),
        compiler_params=_params(("parallel",)),
        name="relbias",
    )(tbl)


def _lane_fold(x, op):
    return functools.reduce(op, [x[:, i * LANES:(i + 1) * LANES] for i in range(x.shape[1] // LANES)])


def _lane_tile(x, width):
    return jnp.concatenate([x] * (width // LANES), axis=1)


def _row_max_tile(rmax):
    return jnp.broadcast_to(jnp.max(rmax, axis=-1, keepdims=True), rmax.shape)


def _normalize(acc):
    return (acc / pltpu.roll(acc, HEAD_DIM, axis=1))[:, :HEAD_DIM]


def _split3(x):
    hi = x.astype(BF16)
    r = x - hi.astype(F32)
    mid = r.astype(BF16)
    lo = (r - mid.astype(F32)).astype(BF16)
    return hi, mid, lo


def _nsa_body(q_ref, kc_ref, vc_ref, ks_ref, vs_ref, kw_ref, vw_ref, gate_ref, toep_ref, bc_ref,
              ov_ref, o_ref, s_ref, near_ref, mb_ref, acc_ref, part_ref):
    t = ATT_TILE
    j4 = NSA_HPG
    rows = j4 * t
    groups = range(NSA_KV_GROUPS)
    m = pl.program_id(1)
    m1 = jnp.maximum(m - 1, 0)
    m2 = jnp.maximum(m - 2, 0)
    no_prev = jnp.where(m >= 1, 0.0, NEG_INF)
    n_sel = s_ref.shape[1] * (t // SEL_BLOCK)

    def heads(x):
        return x.reshape(j4, t, x.shape[-1])

    def flat(x):
        return x.reshape(rows, x.shape[-1])

    def key_tile(ref, g, n):
        return ref[0, g, pl.ds(pl.multiple_of(n * t, t), t), :]

    def biases(g):
        toep = toep_ref[g * j4:(g + 1) * j4]
        return toep[:, :, t:], toep[:, :, :t] + no_prev

    qs = []
    for g in groups:
        qt = q_ref[0, :, g * j4 * HEAD_DIM:(g + 1) * j4 * HEAD_DIM]
        qs.append(jnp.concatenate([qt[:, j * HEAD_DIM:(j + 1) * HEAD_DIM] for j in range(j4)], axis=0))

    def compressed(g):
        bc = bc_ref[g * j4:(g + 1) * j4]
        sc = heads(_dot_nt(qs[g], kc_ref[0, g])) + bc
        valid = bc > 0.5 * NEG_INF
        e = jnp.where(valid, jnp.exp(sc - jnp.max(sc, axis=-1, keepdims=True)), 0.0)
        l = jnp.sum(e, axis=-1, keepdims=True)
        pc = jnp.where(l > 0.0, e / l, 0.0)
        o_c = _dot(flat(pc).astype(BF16), vc_ref[0, g])
        psum = pc[0] + pc[1] + pc[2] + pc[3]
        ov = ov_ref[...]
        imp = sum(_dot(part, ov) for part in _split3(psum)).T[:n_sel]
        blk = lax.broadcasted_iota(jnp.int32, (n_sel, t), 0)
        tpos = m * t + lax.broadcasted_iota(jnp.int32, (n_sel, t), 1)
        cur = tpos // SEL_BLOCK
        bonus = jnp.where(blk == 0, FORCE_BONUS,
                          jnp.where(blk == cur, FORCE_BONUS, jnp.where(blk == cur - 1, FORCE_BONUS, 0.0)))
        imp = jnp.where(blk * SEL_BLOCK <= tpos, imp + bonus, NEG_INF)
        rank = jnp.zeros((n_sel, t), F32)
        for i in range(n_sel):
            row = imp[i:i + 1, :]
            before = jnp.where(blk > i, 1.0, 0.0)
            rank = rank + jnp.where(row > imp, 1.0, jnp.where(row == imp, before, 0.0))
        selb = jnp.where(rank < float(SEL_TOP_N), 0.0, NEG_INF)
        selb = jnp.concatenate([selb, jnp.zeros((LANES - n_sel, t), F32)], axis=0).T
        return o_c, selb[:, :HEAD_DIM].astype(BF16)

    def window(g):
        diag_bias, prev_bias = biases(g)
        ii = lax.broadcasted_iota(jnp.int32, (t, t), 0)
        jj = lax.broadcasted_iota(jnp.int32, (t, t), 1)
        tri = jnp.where(jj > ii, 0.0, NEG_INF) + jnp.where(m >= 2, 0.0, NEG_INF)
        w0 = flat(heads(_dot_nt(qs[g], key_tile(kw_ref, g, m))) + diag_bias)
        w1 = flat(heads(_dot_nt(qs[g], key_tile(kw_ref, g, m1))) + prev_bias)
        w2 = flat(heads(_dot_nt(qs[g], key_tile(kw_ref, g, m2))) + tri[None])
        wmax = _lane_tile(_row_max_tile(functools.reduce(
            jnp.maximum, [_lane_fold(w, jnp.maximum) for w in (w0, w1, w2)])), t)
        acc = sum(_dot(jnp.exp(w - wmax).astype(BF16), key_tile(vw_ref, g, n))
                  for w, n in ((w0, m), (w1, m1), (w2, m2)))
        return _normalize(acc)

    def gated(g, branch, o):
        gate = jax.nn.sigmoid(gate_ref[0, g])
        return [gate[:, branch * j4 + j:branch * j4 + j + 1] * o[j * t:(j + 1) * t] for j in range(j4)]

    cmp_out = [compressed(g) for g in groups]
    part = []
    for g in groups:
        part += [a + b for a, b in zip(gated(g, 0, cmp_out[g][0]), gated(g, 2, window(g)))]
    part_ref[...] = jnp.concatenate(part, axis=1)

    q_aug = []
    for g in groups:
        diag_bias, prev_bias = biases(g)
        q_aug.append(jnp.concatenate([qs[g], jnp.concatenate([cmp_out[g][1]] * j4, axis=0)], axis=1))
        s0 = flat(heads(_dot_nt(q_aug[g], key_tile(ks_ref, g, m))) + diag_bias)
        s1 = flat(heads(_dot_nt(q_aug[g], key_tile(ks_ref, g, m1))) + prev_bias)
        near_ref[g, 0] = s0
        near_ref[g, 1] = s1
        mb_ref[g] = jnp.maximum(_lane_fold(s0, jnp.maximum), _lane_fold(s1, jnp.maximum))

    @pl.loop(0, m1)
    def _(n):
        for g in groups:
            s = _dot_nt(q_aug[g], key_tile(ks_ref, g, n))
            s_ref[g, n] = s
            mb_ref[g] = jnp.maximum(mb_ref[g], _lane_fold(s, jnp.maximum))

    for g in groups:
        mb_ref[g] = _row_max_tile(mb_ref[g])

    def probs(g, s):
        return jnp.exp(s - _lane_tile(mb_ref[g], t)).astype(BF16)

    for g in groups:
        acc_ref[g] = (_dot(probs(g, near_ref[g, 0]), key_tile(vs_ref, g, m))
                      + _dot(probs(g, near_ref[g, 1]), key_tile(vs_ref, g, m1)))

    @pl.loop(0, m1)
    def _(n):
        for g in groups:
            acc_ref[g] += _dot(probs(g, s_ref[g, n]), key_tile(vs_ref, g, n))

    outs = []
    for g in groups:
        outs += gated(g, 1, _normalize(acc_ref[g]))
    o_ref[0] = (part_ref[...] + jnp.concatenate(outs, axis=1)).astype(BF16)


def _overlap_table(s):
    n_sel = s // SEL_BLOCK
    nc = N_CHUNKS_PAD
    c0 = np.arange(nc)[:, None] * CMP_STRIDE
    s0 = np.arange(n_sel)[None, :] * SEL_BLOCK
    ov = np.clip(np.minimum(c0 + CMP_BLOCK, s0 + SEL_BLOCK) - np.maximum(c0, s0), 0, None) / CMP_STRIDE
    ov[nc - 1] = 0.0
    return jnp.asarray(np.pad(ov, ((0, 0), (0, LANES - n_sel))), BF16)


def _nsa(qn, kc, vc, ks, vs, kw, vw, gates, toep, bc):
    b, s, _ = qn.shape
    t = ATT_TILE
    assert s // SEL_BLOCK <= HEAD_DIM
    ov = _overlap_table(s)
    rows = NSA_HPG * t
    ng = NSA_KV_GROUPS
    seq = lambda n, w=HEAD_DIM: pl.BlockSpec((1, ng, n, w), lambda i, j: (i, 0, 0, 0))
    return pl.pallas_call(
        _nsa_body,
        grid=(b, s // t),
        in_specs=[pl.BlockSpec((1, t, NSA_Q), lambda i, j: (i, j, 0)),
                  seq(N_CHUNKS_PAD), seq(N_CHUNKS_PAD),
                  seq(s, 2 * HEAD_DIM), seq(s, 2 * HEAD_DIM), seq(s), seq(s, 2 * HEAD_DIM),
                  pl.BlockSpec((1, ng, t, GATE_PAD), lambda i, j: (i, 0, j, 0)),
                  _resident((NSA_HEADS, t, 2 * t)),
                  pl.BlockSpec((NSA_HEADS, t, N_CHUNKS_PAD), lambda i, j: (0, j, 0)),
                  _resident(ov.shape)],
        out_specs=pl.BlockSpec((1, t, NSA_Q), lambda i, j: (i, j, 0)),
        out_shape=jax.ShapeDtypeStruct((b, s, NSA_Q), BF16),
        scratch_shapes=[pltpu.VMEM((ng, s // t, rows, t), F32), pltpu.VMEM((ng, 2, rows, t), F32),
                        pltpu.VMEM((ng, rows, LANES), F32), pltpu.VMEM((ng, rows, 2 * HEAD_DIM), F32),
                        pltpu.VMEM((t, NSA_Q), F32)],
        compiler_params=_params(("parallel", "parallel")),
        name="nsa",
    )(qn, kc, vc, ks, vs, kw, vw, gates, toep, bc, ov)


def _fcum_body(f_ref, b_ref, o_ref):
    z = f_ref[0] + b_ref[...]
    x = -(jnp.maximum(-z, 0.0) + jnp.log1p(jnp.exp(-jnp.abs(z))))
    n = x.shape[-1]
    lane = lax.broadcasted_iota(jnp.int32, x.shape, 1)
    sh = 1
    while sh < n:
        x = x + jnp.where(lane >= sh, pltpu.roll(x, sh, axis=1), 0.0)
        sh *= 2
    o_ref[0] = x


def _fcum(f_t, b_forget):
    b, h, s = f_t.shape
    blk = pl.BlockSpec((1, h, s), lambda i: (i, 0, 0))
    return pl.pallas_call(
        _fcum_body,
        grid=(b,),
        in_specs=[blk, _resident((h, 1))],
        out_specs=blk,
        out_shape=jax.ShapeDtypeStruct((b, h, s), F32),
        compiler_params=_params(("parallel",)),
        name="fcum",
    )(f_t, b_forget.reshape(h, 1))


FOX_PAIR = 2
FOX_TILE = 512


def _causal_bias(t):
    i = np.arange(t)
    return jnp.asarray(np.where(i[None, :] <= i[:, None], 0.0, NEG_INF), F32)


def _fox_body(q_ref, k_ref, v_ref, c_ref, cm_ref, o_ref, s_ref, mb_ref, acc_ref):
    t = FOX_TILE
    heads = range(FOX_PAIR)
    nq = q_ref.shape[2] // t

    def keys(n):
        return pl.ds(pl.multiple_of(n * t, t), t)

    def scores(m, hh, n):
        q = q_ref[0, hh, m * t:(m + 1) * t, :]
        return _dot_nt(q, k_ref[0, hh, keys(n), :]) - c_ref[0, hh, :, keys(n)]

    def scores_diag(m):
        for hh in heads:
            s = scores(m, hh, m) + cm_ref[...]
            s_ref[m % 2, hh, m] = s
            mb_ref[m % 2, hh] = _lane_fold(s, jnp.maximum)

    def scores_far(m, n):
        for hh in heads:
            s = scores(m, hh, n)
            s_ref[m % 2, hh, n] = s
            mb_ref[m % 2, hh] = jnp.maximum(mb_ref[m % 2, hh], _lane_fold(s, jnp.maximum))

    def finish_max(m):
        for hh in heads:
            mb_ref[m % 2, hh] = _row_max_tile(mb_ref[m % 2, hh])

    def weigh(m, n):
        for hh in heads:
            p = jnp.exp(s_ref[m % 2, hh, n] - _lane_tile(mb_ref[m % 2, hh], t)).astype(BF16)
            acc_ref[hh] += _dot(p, v_ref[0, hh, keys(n), :])

    scores_diag(0)
    finish_max(0)
    for m in range(nq):
        ahead = m + 1 < nq
        if ahead:
            scores_diag(m + 1)
        for hh in heads:
            acc_ref[hh] = jnp.zeros((t, 2 * HEAD_DIM), F32)

        @pl.loop(0, m + 1)
        def _(n, m=m, ahead=ahead):
            weigh(m, n)
            if ahead:
                scores_far(m + 1, n)

        if ahead:
            finish_max(m + 1)
        o_ref[0, m * t:(m + 1) * t, :] = jnp.concatenate(
            [_normalize(acc_ref[hh]) for hh in heads], axis=1).astype(BF16)


def _fox(qf, kf, vf, c):
    b, h, s, _ = qf.shape
    t = FOX_TILE
    seq = lambda w: pl.BlockSpec((1, FOX_PAIR, s, w), lambda i, p: (i, p, 0, 0))
    return pl.pallas_call(
        _fox_body,
        grid=(b, h // FOX_PAIR),
        in_specs=[seq(HEAD_DIM), seq(HEAD_DIM), seq(2 * HEAD_DIM),
                  pl.BlockSpec((1, FOX_PAIR, 1, s), lambda i, p: (i, p, 0, 0)),
                  _resident((t, t))],
        out_specs=pl.BlockSpec((1, s, FOX_PAIR * HEAD_DIM), lambda i, p: (i, 0, p)),
        out_shape=jax.ShapeDtypeStruct((b, s, h * HEAD_DIM), BF16),
        scratch_shapes=[pltpu.VMEM((2, FOX_PAIR, s // t, t, t), F32),
                        pltpu.VMEM((2, FOX_PAIR, t, LANES), F32),
                        pltpu.VMEM((FOX_PAIR, t, 2 * HEAD_DIM), F32)],
        compiler_params=_params(("parallel", "parallel")),
        name="fox",
    )(qf, kf, vf, c.reshape(b, h, 1, s), _causal_bias(t))


def _mixout_body(x_ref, g_ref, on_ref, of_ref, wa_ref, wb_ref, wun_ref, wuf_ref, wo_ref, o_ref):
    x = x_ref[...]
    h = _rms(x, g_ref[...]).astype(BF16)
    y = (jax.nn.sigmoid(_dot(h, wa_ref[...])) * _dot(on_ref[...], wun_ref[...])
         + jax.nn.sigmoid(_dot(h, wb_ref[...])) * _dot(of_ref[...], wuf_ref[...]))
    o_ref[...] = x + _dot(y.astype(BF16), wo_ref[...])


def _mixout(x2d, g, o_nsa, o_fox, w_a, w_b, w_un, w_uf, w_o):
    n = x2d.shape[0]
    tok = lambda c: pl.BlockSpec((TOK_TILE, c), lambda i: (i, 0))
    return pl.pallas_call(
        _mixout_body,
        grid=(n // TOK_TILE,),
        in_specs=[tok(D_MODEL), _resident((1, D_MODEL)), tok(NSA_Q), tok(FOX_W),
                  _resident(w_a.shape), _resident(w_b.shape), _resident(w_un.shape),
                  _resident(w_uf.shape), _resident(w_o.shape)],
        out_specs=tok(D_MODEL),
        out_shape=jax.ShapeDtypeStruct((n, D_MODEL), F32),
        compiler_params=_params(("parallel",)),
        name="mixout",
    )(x2d, g.reshape(1, D_MODEL), o_nsa, o_fox, w_a, w_b, w_un, w_uf, w_o)


def _memkv_body(m_ref, g_ref, w_ref, k_ref, v_ref):
    h = _rms(m_ref[0], g_ref[...]).astype(BF16)
    z = _dot(h, w_ref[...])
    k_ref[0] = z[:, :D_MODEL].astype(BF16)
    v_ref[0] = z[:, D_MODEL:].astype(BF16)


def _memkv(mem, g, w_kv):
    b, ml, _ = mem.shape
    blk = pl.BlockSpec((1, ml, D_MODEL), lambda i: (i, 0, 0))
    sd = jax.ShapeDtypeStruct((b, ml, D_MODEL), BF16)
    return pl.pallas_call(
        _memkv_body,
        grid=(b,),
        in_specs=[blk, _resident((1, D_MODEL)), _resident(w_kv.shape)],
        out_specs=(blk, blk),
        out_shape=(sd, sd),
        compiler_params=_params(("parallel",)),
        name="memkv",
    )(mem, g.reshape(1, D_MODEL), w_kv)


def _memattn_body(x_ref, g_ref, k_ref, v_ref, wq_ref, wo_ref, o_ref):
    x = x_ref[0]
    h = _rms(x, g_ref[...]).astype(BF16)
    q = (_dot(h, wq_ref[...]) * (MEM_HEAD_DIM ** -0.5)).astype(BF16)
    outs = []
    for hd in range(MEM_HEADS):
        cols = slice(hd * MEM_HEAD_DIM, (hd + 1) * MEM_HEAD_DIM)
        s = _dot_nt(q[:, cols], k_ref[0, :, cols])
        e = jnp.exp(s - jnp.max(s, axis=-1, keepdims=True))
        p = e / jnp.sum(e, axis=-1, keepdims=True)
        outs.append(_dot(p.astype(BF16), v_ref[0, :, cols]))
    o = jnp.concatenate(outs, axis=1).astype(BF16)
    o_ref[0] = x + _dot(o, wo_ref[...])


def _memattn(x, g, k, v, w_q, w_o):
    b, s, _ = x.shape
    ml = k.shape[1]
    tok = pl.BlockSpec((1, TOK_TILE, D_MODEL), lambda i, j: (i, j, 0))
    kvb = pl.BlockSpec((1, ml, D_MODEL), lambda i, j: (i, 0, 0))
    return pl.pallas_call(
        _memattn_body,
        grid=(b, s // TOK_TILE),
        in_specs=[tok, _resident((1, D_MODEL)), kvb, kvb, _resident(w_q.shape), _resident(w_o.shape)],
        out_specs=tok,
        out_shape=jax.ShapeDtypeStruct((b, s, D_MODEL), F32),
        compiler_params=_params(("parallel", "parallel")),
        name="memattn",
    )(x, g.reshape(1, D_MODEL), k, v, w_q, w_o)


def kernel(x, mem, rel_bias_table, ffn1_norm, ffn1_w_gate, ffn1_w_up, ffn1_w_down, mix_norm, mix_w_in, mix_b_forget, cmp_pos_k, cmp_pos_v, cmp_k_w1, cmp_k_w2, cmp_v_w1, cmp_v_w2, w_up_nsa, w_up_fox, mix_w_out, mem_q_norm, mem_kv_norm, mem_w_q, mem_w_kv, mem_w_o, ffn2_norm, ffn2_w_gate, ffn2_w_up, ffn2_w_down, final_norm):
    b, s, d = x.shape
    depth = ffn1_norm.shape[0]
    bf = lambda w: w.astype(BF16)
    toep, bias_c = _relbias(rel_bias_table, s)
    x = x.reshape(b * s, d)
    for l in range(depth):
        last = l == depth - 1
        x = _ffn(x, ffn1_norm[l], bf(ffn1_w_gate[l]), bf(ffn1_w_up[l]), bf(ffn1_w_down[l]))

        w_main, w_a, w_b = _pack_w_in(mix_w_in[l])
        qn, kc, vc, ks, vs, kw, vw, qf, kf, vf, gates, flog = _inproj(x.reshape(b, s, d), mix_norm[l], w_main)
        kcc, vcc = _compress(kc, vc, cmp_pos_k[l], cmp_pos_v[l], cmp_k_w1[l], cmp_k_w2[l],
                             cmp_v_w1[l], cmp_v_w2[l])
        o_nsa = _nsa(qn, kcc, vcc, ks, vs, kw, vw, gates, toep, bias_c)
        c = _fcum(flog.transpose(0, 2, 1), mix_b_forget[l])
        o_fox = _fox(qf, kf, vf, c)
        x = _mixout(x, mix_norm[l], o_nsa.reshape(b * s, NSA_Q), o_fox.reshape(b * s, FOX_W),
                    w_a, w_b, bf(w_up_nsa[l]), bf(w_up_fox[l]), bf(mix_w_out[l]))

        mk, mv = _memkv(mem, mem_kv_norm[l], bf(mem_w_kv[l]))
        x = _memattn(x.reshape(b, s, d), mem_q_norm[l], mk, mv, bf(mem_w_q[l]), bf(mem_w_o[l]))
        x = _ffn(x.reshape(b * s, d), ffn2_norm[l], bf(ffn2_w_gate[l]), bf(ffn2_w_up[l]),
                 bf(ffn2_w_down[l]), final_g=final_norm if last else None)
    return x.reshape(b, s, d)
```

```python
import functools
import math

import numpy as np
import jax
import jax.numpy as jnp
from jax import lax
from jax.experimental import pallas as pl
from jax.experimental.pallas import tpu as pltpu

D_MODEL = 1024
D_FF = 2816
HEAD_DIM = 64
NSA_HEADS = 8
NSA_KV_GROUPS = 2
NSA_HPG = NSA_HEADS // NSA_KV_GROUPS
CMP_BLOCK = 32
CMP_STRIDE = 16
CMP_HIDDEN = 256
SEL_BLOCK = 64
SEL_TOP_N = 16
WINDOW = 512
FOX_HEADS = 8
MEM_HEADS = 4
MEM_HEAD_DIM = D_MODEL // MEM_HEADS
NUM_BUCKETS = 32
MAX_DISTANCE = 128
RMS_EPS = 1e-6
NEG_INF = -1e30
FORCE_BONUS = 1e4

NSA_Q = NSA_HEADS * HEAD_DIM
NSA_KV = NSA_KV_GROUPS * HEAD_DIM
FOX_W = FOX_HEADS * HEAD_DIM

LANES = 128
VMEM_LIMIT = 56 * 1024 * 1024

BF16 = jnp.bfloat16
F32 = jnp.float32

ATT_TILE = 256
TOK_TILE = 1024
FFN_TILE = 1024
FF_CHUNK = 256


def _bucket_thresholds():
    n = np.arange(0, 4 * MAX_DISTANCE)
    exact = NUM_BUCKETS // 2
    large = exact + (np.log(np.maximum(n, 1) / exact) / math.log(MAX_DISTANCE / exact)
                     * (NUM_BUCKETS - exact)).astype(np.int64)
    bucket = np.where(n < exact, n, np.minimum(large, NUM_BUCKETS - 1))
    assert np.all(np.diff(bucket) >= 0)
    return [int(np.argmax(bucket >= k)) for k in range(1, NUM_BUCKETS)]


BUCKET_THRESHOLDS = _bucket_thresholds()


def _dot(a, b):
    return jnp.dot(a, b, preferred_element_type=F32)


def _dot_nt(a, b):
    return lax.dot_general(a, b, (((1,), (1,)), ((), ())), preferred_element_type=F32)


def _rms(x, g):
    return x * lax.rsqrt(jnp.mean(x * x, axis=-1, keepdims=True) + RMS_EPS) * g


def _resident(shape):
    nd = len(shape)
    return pl.BlockSpec(shape, lambda *_: (0,) * nd, pipeline_mode=pl.Buffered(1))


def _params(sem):
    return pltpu.CompilerParams(dimension_semantics=sem, vmem_limit_bytes=VMEM_LIMIT)


def _ffn_body(x_ref, g_ref, wg_ref, wu_ref, wd_ref, *rest, final):
    if final:
        fg_ref, o_ref = rest
    else:
        (o_ref,) = rest
    x = x_ref[...]
    h = _rms(x, g_ref[...]).astype(BF16)
    acc = jnp.zeros(x.shape, F32)
    for c in range(D_FF // FF_CHUNK):
        sl = slice(c * FF_CHUNK, (c + 1) * FF_CHUNK)
        a = _dot(h, wg_ref[:, sl])
        b = _dot(h, wu_ref[:, sl])
        t = (a * jax.nn.sigmoid(a)) * b
        acc = acc + _dot(t.astype(BF16), wd_ref[sl, :])
    y = x + 0.5 * acc
    if final:
        y = _rms(y, fg_ref[...])
    o_ref[...] = y


def _ffn(x2d, g, wg, wu, wd, final_g=None):
    n = x2d.shape[0]
    final = final_g is not None
    tok = pl.BlockSpec((FFN_TILE, D_MODEL), lambda i: (i, 0))
    in_specs = [tok, _resident((1, D_MODEL)), _resident((D_MODEL, D_FF)),
                _resident((D_MODEL, D_FF)), _resident((D_FF, D_MODEL))]
    args = [x2d, g.reshape(1, D_MODEL), wg, wu, wd]
    if final:
        in_specs.append(_resident((1, D_MODEL)))
        args.append(final_g.reshape(1, D_MODEL))
    return pl.pallas_call(
        functools.partial(_ffn_body, final=final),
        grid=(n // FFN_TILE,),
        in_specs=in_specs,
        out_specs=tok,
        out_shape=jax.ShapeDtypeStruct((n, D_MODEL), F32),
        compiler_params=_params(("parallel",)),
        name="ffn_final" if final else "ffn",
    )(*args)


_C_QN = 0
_C_CMP = _C_QN + NSA_Q
_C_KV = _C_CMP + 2 * NSA_KV
_C_FOX = _C_KV + 4 * NSA_KV
_C_SMALL = _C_FOX + 3 * FOX_W
_C_END = _C_SMALL + LANES
GATE_PAD = 16


def _inproj_body(x_ref, g_ref, w_ref, blk_ref, qn_ref, kc_ref, vc_ref, ks_ref, vs_ref, kw_ref, vw_ref,
                 qf_ref, kf_ref, vf_ref, gate_ref, fl_ref):
    h = _rms(x_ref[0], g_ref[...]).astype(BF16)
    z = _dot(h, w_ref[...])
    qn_ref[0] = (z[:, _C_QN:_C_CMP] * (HEAD_DIM ** -0.5)).astype(BF16)
    kc_ref[0] = z[:, _C_CMP:_C_CMP + NSA_KV].astype(BF16)
    vc_ref[0] = z[:, _C_CMP + NSA_KV:_C_KV].astype(BF16)
    ones = jnp.ones((z.shape[0], HEAD_DIM), BF16)
    for i, (ref, extra) in enumerate(((ks_ref, blk_ref[...]), (vs_ref, ones), (kw_ref, None),
                                      (vw_ref, ones))):
        for g in range(NSA_KV_GROUPS):
            c0 = _C_KV + i * NSA_KV + g * HEAD_DIM
            val = z[:, c0:c0 + HEAD_DIM].astype(BF16)
            ref[0, g] = val if extra is None else jnp.concatenate([val, extra], axis=1)
    for hd in range(FOX_HEADS):
        c0 = _C_FOX + hd * HEAD_DIM
        qf_ref[0, hd] = (z[:, c0:c0 + HEAD_DIM] * (HEAD_DIM ** -0.5)).astype(BF16)
        kf_ref[0, hd] = z[:, c0 + FOX_W:c0 + FOX_W + HEAD_DIM].astype(BF16)
        v = z[:, c0 + 2 * FOX_W:c0 + 2 * FOX_W + HEAD_DIM].astype(BF16)
        vf_ref[0, hd] = jnp.concatenate([v, ones], axis=1)
    for g in range(NSA_KV_GROUPS):
        c0 = _C_SMALL + g * GATE_PAD
        gate_ref[0, g] = z[:, c0:c0 + GATE_PAD]
    c0 = _C_SMALL + NSA_KV_GROUPS * GATE_PAD
    fl_ref[0] = z[:, c0:c0 + FOX_HEADS]


def _pack_w_in(w_in):
    cols = np.cumsum((0, NSA_Q, NSA_KV, NSA_KV, NSA_KV, NSA_KV, NSA_KV, NSA_KV, 3 * NSA_HEADS,
                      FOX_W, FOX_W, FOX_W, FOX_HEADS, D_MODEL, D_MODEL))
    g0 = int(cols[7])
    gate_cols = []
    for g in range(NSA_KV_GROUPS):
        for br in range(3):
            c0 = g0 + br * NSA_HEADS + g * NSA_HPG
            gate_cols.append(w_in[:, c0:c0 + NSA_HPG])
        gate_cols.append(jnp.zeros((D_MODEL, GATE_PAD - 3 * NSA_HPG), w_in.dtype))
    small = jnp.concatenate(gate_cols + [w_in[:, int(cols[11]):int(cols[12])]], axis=1)
    small = jnp.pad(small, ((0, 0), (0, LANES - small.shape[1])))
    main = jnp.concatenate([w_in[:, :g0], w_in[:, int(cols[8]):int(cols[11])], small], axis=1)
    w_a = w_in[:, int(cols[12]):int(cols[13])]
    w_b = w_in[:, int(cols[13]):int(cols[14])]
    return main.astype(BF16), w_a.astype(BF16), w_b.astype(BF16)


def _inproj(x, g, w_main):
    b, s, _ = x.shape
    grid = (b, s // TOK_TILE)
    tok = lambda c: pl.BlockSpec((1, TOK_TILE, c), lambda i, j: (i, j, 0))
    heads = lambda nh, w=HEAD_DIM: pl.BlockSpec((1, nh, TOK_TILE, w), lambda i, j: (i, 0, j, 0))
    sds = jax.ShapeDtypeStruct
    kv = sds((b, NSA_KV_GROUPS, s, HEAD_DIM), BF16)
    kv2 = sds((b, NSA_KV_GROUPS, s, 2 * HEAD_DIM), BF16)
    fx = sds((b, FOX_HEADS, s, HEAD_DIM), BF16)
    out_shape = (sds((b, s, NSA_Q), BF16), sds((b, s, NSA_KV), BF16), sds((b, s, NSA_KV), BF16),
                 kv2, kv2, kv, kv2, fx, fx, sds((b, FOX_HEADS, s, 2 * HEAD_DIM), BF16),
                 sds((b, NSA_KV_GROUPS, s, GATE_PAD), F32), sds((b, s, FOX_HEADS), F32))
    out_specs = (tok(NSA_Q), tok(NSA_KV), tok(NSA_KV),
                 heads(NSA_KV_GROUPS, 2 * HEAD_DIM), heads(NSA_KV_GROUPS, 2 * HEAD_DIM),
                 heads(NSA_KV_GROUPS), heads(NSA_KV_GROUPS, 2 * HEAD_DIM),
                 heads(FOX_HEADS), heads(FOX_HEADS), heads(FOX_HEADS, 2 * HEAD_DIM),
                 pl.BlockSpec((1, NSA_KV_GROUPS, TOK_TILE, GATE_PAD), lambda i, j: (i, 0, j, 0)),
                 tok(FOX_HEADS))
    key_blk = np.arange(s) // SEL_BLOCK
    blk_onehot = jnp.asarray(key_blk[:, None] == np.arange(HEAD_DIM)[None, :], BF16)
    return pl.pallas_call(
        _inproj_body,
        grid=grid,
        in_specs=[tok(D_MODEL), _resident((1, D_MODEL)), _resident((D_MODEL, _C_END)),
                  pl.BlockSpec((TOK_TILE, HEAD_DIM), lambda i, j: (j, 0))],
        out_specs=out_specs,
        out_shape=out_shape,
        compiler_params=_params(("parallel", "parallel")),
        name="inproj",
    )(x, g.reshape(1, D_MODEL), w_main, blk_onehot)


N_CHUNKS_PAD = 128
CHUNK_W = CMP_STRIDE * NSA_KV


def _compress_body(xk_ref, xv_ref, pk_ref, pv_ref, wk1_ref, wv1_ref, wk2_ref, wv2_ref, ok_ref, ov_ref):
    for x_ref, p_ref, w1_ref, w2_ref, o_ref in ((xk_ref, pk_ref, wk1_ref, wk2_ref, ok_ref),
                                                (xv_ref, pv_ref, wv1_ref, wv2_ref, ov_ref)):
        x = x_ref[0]
        a0 = _dot(x, w1_ref[0])
        a1 = _dot(x, w1_ref[1])
        c = _dot(p_ref[0], w1_ref[0]) + _dot(p_ref[1], w1_ref[1])
        pre = a0 + pltpu.roll(a1, N_CHUNKS_PAD - 1, axis=0) + c[0:1, :]
        hid = jax.nn.gelu(pre).astype(BF16)
        out = _dot(hid, w2_ref[...])
        for g in range(NSA_KV_GROUPS):
            o_ref[0, g] = out[:, g * HEAD_DIM:(g + 1) * HEAD_DIM].astype(BF16)


def _pack_compress(pos, w1, w2):
    r = CMP_BLOCK // CMP_STRIDE
    assert NSA_KV_GROUPS == 2
    w1r = w1.astype(BF16).reshape(r, CMP_STRIDE, HEAD_DIM, CMP_HIDDEN)
    z1 = jnp.zeros_like(w1r)
    w1big = jnp.stack([jnp.concatenate([w1r, z1], axis=-1), jnp.concatenate([z1, w1r], axis=-1)], axis=2)
    w1big = w1big.reshape(r, CHUNK_W, NSA_KV_GROUPS * CMP_HIDDEN)
    z2 = jnp.zeros_like(w2)
    w2big = jnp.concatenate([jnp.concatenate([w2, z2], axis=1), jnp.concatenate([z2, w2], axis=1)], axis=0)
    p = pos.reshape(r, CMP_STRIDE, 1, HEAD_DIM)
    p = jnp.broadcast_to(p, (r, CMP_STRIDE, NSA_KV_GROUPS, HEAD_DIM)).reshape(r, 1, CHUNK_W)
    p = jnp.broadcast_to(p, (r, 16, CHUNK_W))
    return p.astype(BF16), w1big.astype(BF16), w2big.astype(BF16)


def _compress(kc, vc, pos_k, pos_v, k_w1, k_w2, v_w1, v_w2):
    b, s, _ = kc.shape
    xk = kc.reshape(b, N_CHUNKS_PAD, CHUNK_W)
    xv = vc.reshape(b, N_CHUNKS_PAD, CHUNK_W)
    pk, wk1, wk2 = _pack_compress(pos_k, k_w1, k_w2)
    pv, wv1, wv2 = _pack_compress(pos_v, v_w1, v_w2)
    xs = pl.BlockSpec((1, N_CHUNKS_PAD, CHUNK_W), lambda i: (i, 0, 0))
    os_ = pl.BlockSpec((1, NSA_KV_GROUPS, N_CHUNKS_PAD, HEAD_DIM), lambda i: (i, 0, 0, 0))
    osd = jax.ShapeDtypeStruct((b, NSA_KV_GROUPS, N_CHUNKS_PAD, HEAD_DIM), BF16)
    return pl.pallas_call(
        _compress_body,
        grid=(b,),
        in_specs=[xs, xs, _resident(pk.shape), _resident(pv.shape), _resident(wk1.shape),
                  _resident(wv1.shape), _resident(wk2.shape), _resident(wv2.shape)],
        out_specs=(os_, os_),
        out_shape=(osd, osd),
        compiler_params=_params(("parallel",)),
        name="compress",
    )(xk, xv, pk, pv, wk1, wv1, wk2, wv2)


def _bias_lookup(tbl_ref, head, dist):
    acc = jnp.full(dist.shape, tbl_ref[0, head], F32)
    for k, thr in enumerate(BUCKET_THRESHOLDS):
        acc = jnp.where(dist >= thr, tbl_ref[k + 1, head], acc)
    return acc - tbl_ref[NUM_BUCKETS - 1, head]


def _relbias_body(tbl_ref, toep_ref, bc_ref):
    head = pl.program_id(0)
    t = ATT_TILE
    i = lax.broadcasted_iota(jnp.int32, (t, 2 * t), 0)
    u = lax.broadcasted_iota(jnp.int32, (t, 2 * t), 1)
    d = i - u + t
    toep_ref[0] = jnp.where(d >= 0, _bias_lookup(tbl_ref, head, d), NEG_INF)
    n_tiles = bc_ref.shape[1] // t
    per_tile = t // CMP_STRIDE
    off = per_tile * (n_tiles - 1)
    width = 2 * N_CHUNKS_PAD
    assert off + N_CHUNKS_PAD <= width
    i = lax.broadcasted_iota(jnp.int32, (t, width), 0)
    u = lax.broadcasted_iota(jnp.int32, (t, width), 1)
    dc = i - ((u - off) * CMP_STRIDE + CMP_BLOCK - 1)
    strip = jnp.where(dc >= 0, _bias_lookup(tbl_ref, head, dc), NEG_INF)
    for q in range(n_tiles):
        lo = off - q * per_tile
        bc_ref[0, q * t:(q + 1) * t, :] = strip[:, lo:lo + N_CHUNKS_PAD]


def _relbias(tbl, s):
    return pl.pallas_call(
        _relbias_body,
        grid=(NSA_HEADS,),
        in_specs=[pl.BlockSpec(memory_space=pltpu.SMEM)],
        out_specs=(pl.BlockSpec((1, ATT_TILE, 2 * ATT_TILE), lambda h: (h, 0, 0)),
                   pl.BlockSpec((1, s, N_CHUNKS_PAD), lambda h: (h, 0, 0))),
        out_shape=(jax.ShapeDtypeStruct((NSA_HEADS, ATT_TILE, 2 * ATT_TILE), F32),
                   jax.ShapeDtypeStruct((NSA_HEADS, s, N_CHUNKS_PAD), F32)),
        compiler_params=_params(("parallel",)),
        name="relbias",
    )(tbl)


def _lane_fold(x, op):
    return functools.reduce(op, [x[:, i * LANES:(i + 1) * LANES] for i in range(x.shape[1] // LANES)])


def _lane_tile(x, width):
    return jnp.concatenate([x] * (width // LANES), axis=1)


def _row_max_tile(rmax):
    return jnp.broadcast_to(jnp.max(rmax, axis=-1, keepdims=True), rmax.shape)


def _normalize(acc):
    return (acc / pltpu.roll(acc, HEAD_DIM, axis=1))[:, :HEAD_DIM]


def _split3(x):
    hi = x.astype(BF16)
    r = x - hi.astype(F32)
    mid = r.astype(BF16)
    lo = (r - mid.astype(F32)).astype(BF16)
    return hi, mid, lo


def _nsa_body(q_ref, kc_ref, vc_ref, ks_ref, vs_ref, kw_ref, vw_ref, gate_ref, toep_ref, bc_ref,
              ov_ref, o_ref, s_ref, near_ref, mb_ref, acc_ref, part_ref):
    t = ATT_TILE
    j4 = NSA_HPG
    rows = j4 * t
    groups = range(NSA_KV_GROUPS)
    m = pl.program_id(1)
    m1 = jnp.maximum(m - 1, 0)
    m2 = jnp.maximum(m - 2, 0)
    no_prev = jnp.where(m >= 1, 0.0, NEG_INF)
    n_sel = s_ref.shape[1] * (t // SEL_BLOCK)

    def heads(x):
        return x.reshape(j4, t, x.shape[-1])

    def flat(x):
        return x.reshape(rows, x.shape[-1])

    def key_tile(ref, g, n):
        return ref[0, g, pl.ds(pl.multiple_of(n * t, t), t), :]

    def biases(g):
        toep = toep_ref[g * j4:(g + 1) * j4]
        return toep[:, :, t:], toep[:, :, :t] + no_prev

    qs = []
    for g in groups:
        qt = q_ref[0, :, g * j4 * HEAD_DIM:(g + 1) * j4 * HEAD_DIM]
        qs.append(jnp.concatenate([qt[:, j * HEAD_DIM:(j + 1) * HEAD_DIM] for j in range(j4)], axis=0))

    def compressed(g):
        bc = bc_ref[g * j4:(g + 1) * j4]
        sc = heads(_dot_nt(qs[g], kc_ref[0, g])) + bc
        valid = bc > 0.5 * NEG_INF
        e = jnp.where(valid, jnp.exp(sc - jnp.max(sc, axis=-1, keepdims=True)), 0.0)
        l = jnp.sum(e, axis=-1, keepdims=True)
        pc = jnp.where(l > 0.0, e / l, 0.0)
        o_c = _dot(flat(pc).astype(BF16), vc_ref[0, g])
        psum = pc[0] + pc[1] + pc[2] + pc[3]
        ov = ov_ref[...]
        imp = sum(_dot(part, ov) for part in _split3(psum)).T[:n_sel]
        blk = lax.broadcasted_iota(jnp.int32, (n_sel, t), 0)
        tpos = m * t + lax.broadcasted_iota(jnp.int32, (n_sel, t), 1)
        cur = tpos // SEL_BLOCK
        bonus = jnp.where(blk == 0, FORCE_BONUS,
                          jnp.where(blk == cur, FORCE_BONUS, jnp.where(blk == cur - 1, FORCE_BONUS, 0.0)))
        imp = jnp.where(blk * SEL_BLOCK <= tpos, imp + bonus, NEG_INF)
        rank = jnp.zeros((n_sel, t), F32)
        for i in range(n_sel):
            row = imp[i:i + 1, :]
            before = jnp.where(blk > i, 1.0, 0.0)
            rank = rank + jnp.where(row > imp, 1.0, jnp.where(row == imp, before, 0.0))
        selb = jnp.where(rank < float(SEL_TOP_N), 0.0, NEG_INF)
        selb = jnp.concatenate([selb, jnp.zeros((LANES - n_sel, t), F32)], axis=0).T
        return o_c, selb[:, :HEAD_DIM].astype(BF16)

    def window(g):
        diag_bias, prev_bias = biases(g)
        ii = lax.broadcasted_iota(jnp.int32, (t, t), 0)
        jj = lax.broadcasted_iota(jnp.int32, (t, t), 1)
        tri = jnp.where(jj > ii, 0.0, NEG_INF) + jnp.where(m >= 2, 0.0, NEG_INF)
        w0 = flat(heads(_dot_nt(qs[g], key_tile(kw_ref, g, m))) + diag_bias)
        w1 = flat(heads(_dot_nt(qs[g], key_tile(kw_ref, g, m1))) + prev_bias)
        w2 = flat(heads(_dot_nt(qs[g], key_tile(kw_ref, g, m2))) + tri[None])
        wmax = _lane_tile(_row_max_tile(functools.reduce(
            jnp.maximum, [_lane_fold(w, jnp.maximum) for w in (w0, w1, w2)])), t)
        acc = sum(_dot(jnp.exp(w - wmax).astype(BF16), key_tile(vw_ref, g, n))
                  for w, n in ((w0, m), (w1, m1), (w2, m2)))
        return _normalize(acc)

    def gated(g, branch, o):
        gate = jax.nn.sigmoid(gate_ref[0, g])
        return [gate[:, branch * j4 + j:branch * j4 + j + 1] * o[j * t:(j + 1) * t] for j in range(j4)]

    cmp_out = [compressed(g) for g in groups]
    part = []
    for g in groups:
        part += [a + b for a, b in zip(gated(g, 0, cmp_out[g][0]), gated(g, 2, window(g)))]
    part_ref[...] = jnp.concatenate(part, axis=1)

    q_aug = []
    for g in groups:
        diag_bias, prev_bias = biases(g)
        q_aug.append(jnp.concatenate([qs[g], jnp.concatenate([cmp_out[g][1]] * j4, axis=0)], axis=1))
        s0 = flat(heads(_dot_nt(q_aug[g], key_tile(ks_ref, g, m))) + diag_bias)
        s1 = flat(heads(_dot_nt(q_aug[g], key_tile(ks_ref, g, m1))) + prev_bias)
        near_ref[g, 0] = s0
        near_ref[g, 1] = s1
        mb_ref[g] = jnp.maximum(_lane_fold(s0, jnp.maximum), _lane_fold(s1, jnp.maximum))

    @pl.loop(0, m1)
    def _(n):
        for g in groups:
            s = _dot_nt(q_aug[g], key_tile(ks_ref, g, n))
            s_ref[g, n] = s
            mb_ref[g] = jnp.maximum(mb_ref[g], _lane_fold(s, jnp.maximum))

    for g in groups:
        mb_ref[g] = _row_max_tile(mb_ref[g])

    def probs(g, s):
        return jnp.exp(s - _lane_tile(mb_ref[g], t)).astype(BF16)

    for g in groups:
        acc_ref[g] = (_dot(probs(g, near_ref[g, 0]), key_tile(vs_ref, g, m))
                      + _dot(probs(g, near_ref[g, 1]), key_tile(vs_ref, g, m1)))

    @pl.loop(0, m1)
    def _(n):
        for g in groups:
            acc_ref[g] += _dot(probs(g, s_ref[g, n]), key_tile(vs_ref, g, n))

    outs = []
    for g in groups:
        outs += gated(g, 1, _normalize(acc_ref[g]))
    o_ref[0] = (part_ref[...] + jnp.concatenate(outs, axis=1)).astype(BF16)


def _overlap_table(s):
    n_sel = s // SEL_BLOCK
    nc = N_CHUNKS_PAD
    c0 = np.arange(nc)[:, None] * CMP_STRIDE
    s0 = np.arange(n_sel)[None, :] * SEL_BLOCK
    ov = np.clip(np.minimum(c0 + CMP_BLOCK, s0 + SEL_BLOCK) - np.maximum(c0, s0), 0, None) / CMP_STRIDE
    ov[nc - 1] = 0.0
    return jnp.asarray(np.pad(ov, ((0, 0), (0, LANES - n_sel))), BF16)


def _nsa(qn, kc, vc, ks, vs, kw, vw, gates, toep, bc):
    b, s, _ = qn.shape
    t = ATT_TILE
    assert s // SEL_BLOCK <= HEAD_DIM
    ov = _overlap_table(s)
    rows = NSA_HPG * t
    ng = NSA_KV_GROUPS
    seq = lambda n, w=HEAD_DIM: pl.BlockSpec((1, ng, n, w), lambda i, j: (i, 0, 0, 0))
    return pl.pallas_call(
        _nsa_body,
        grid=(b, s // t),
        in_specs=[pl.BlockSpec((1, t, NSA_Q), lambda i, j: (i, j, 0)),
                  seq(N_CHUNKS_PAD), seq(N_CHUNKS_PAD),
                  seq(s, 2 * HEAD_DIM), seq(s, 2 * HEAD_DIM), seq(s), seq(s, 2 * HEAD_DIM),
                  pl.BlockSpec((1, ng, t, GATE_PAD), lambda i, j: (i, 0, j, 0)),
                  _resident((NSA_HEADS, t, 2 * t)),
                  pl.BlockSpec((NSA_HEADS, t, N_CHUNKS_PAD), lambda i, j: (0, j, 0)),
                  _resident(ov.shape)],
        out_specs=pl.BlockSpec((1, t, NSA_Q), lambda i, j: (i, j, 0)),
        out_shape=jax.ShapeDtypeStruct((b, s, NSA_Q), BF16),
        scratch_shapes=[pltpu.VMEM((ng, s // t, rows, t), F32), pltpu.VMEM((ng, 2, rows, t), F32),
                        pltpu.VMEM((ng, rows, LANES), F32), pltpu.VMEM((ng, rows, 2 * HEAD_DIM), F32),
                        pltpu.VMEM((t, NSA_Q), F32)],
        compiler_params=_params(("parallel", "parallel")),
        name="nsa",
    )(qn, kc, vc, ks, vs, kw, vw, gates, toep, bc, ov)


def _fcum_body(f_ref, b_ref, o_ref):
    z = f_ref[0] + b_ref[...]
    x = -(jnp.maximum(-z, 0.0) + jnp.log1p(jnp.exp(-jnp.abs(z))))
    n = x.shape[-1]
    lane = lax.broadcasted_iota(jnp.int32, x.shape, 1)
    sh = 1
    while sh < n:
        x = x + jnp.where(lane >= sh, pltpu.roll(x, sh, axis=1), 0.0)
        sh *= 2
    o_ref[0] = x


def _fcum(f_t, b_forget):
    b, h, s = f_t.shape
    blk = pl.BlockSpec((1, h, s), lambda i: (i, 0, 0))
    return pl.pallas_call(
        _fcum_body,
        grid=(b,),
        in_specs=[blk, _resident((h, 1))],
        out_specs=blk,
        out_shape=jax.ShapeDtypeStruct((b, h, s), F32),
        compiler_params=_params(("parallel",)),
        name="fcum",
    )(f_t, b_forget.reshape(h, 1))


FOX_PAIR = 2
FOX_TILE = 512


def _causal_bias(t):
    i = np.arange(t)
    return jnp.asarray(np.where(i[None, :] <= i[:, None], 0.0, NEG_INF), F32)


def _fox_body(q_ref, k_ref, v_ref, c_ref, cm_ref, o_ref, s_ref, mb_ref, acc_ref):
    t = FOX_TILE
    heads = range(FOX_PAIR)
    nq = q_ref.shape[2] // t

    def keys(n):
        return pl.ds(pl.multiple_of(n * t, t), t)

    def scores(m, hh, n):
        q = q_ref[0, hh, m * t:(m + 1) * t, :]
        return _dot_nt(q, k_ref[0, hh, keys(n), :]) - c_ref[0, hh, :, keys(n)]

    def scores_diag(m):
        for hh in heads:
            s = scores(m, hh, m) + cm_ref[...]
            s_ref[m % 2, hh, m] = s
            mb_ref[m % 2, hh] = _lane_fold(s, jnp.maximum)

    def scores_far(m, n):
        for hh in heads:
            s = scores(m, hh, n)
            s_ref[m % 2, hh, n] = s
            mb_ref[m % 2, hh] = jnp.maximum(mb_ref[m % 2, hh], _lane_fold(s, jnp.maximum))

    def finish_max(m):
        for hh in heads:
            mb_ref[m % 2, hh] = _row_max_tile(mb_ref[m % 2, hh])

    def weigh(m, n):
        for hh in heads:
            p = jnp.exp(s_ref[m % 2, hh, n] - _lane_tile(mb_ref[m % 2, hh], t)).astype(BF16)
            acc_ref[hh] += _dot(p, v_ref[0, hh, keys(n), :])

    scores_diag(0)
    finish_max(0)
    for m in range(nq):
        ahead = m + 1 < nq
        if ahead:
            scores_diag(m + 1)
        for hh in heads:
            acc_ref[hh] = jnp.zeros((t, 2 * HEAD_DIM), F32)

        @pl.loop(0, m + 1)
        def _(n, m=m, ahead=ahead):
            weigh(m, n)
            if ahead:
                scores_far(m + 1, n)

        if ahead:
            finish_max(m + 1)
        o_ref[0, m * t:(m + 1) * t, :] = jnp.concatenate(
            [_normalize(acc_ref[hh]) for hh in heads], axis=1).astype(BF16)


def _fox(qf, kf, vf, c):
    b, h, s, _ = qf.shape
    t = FOX_TILE
    seq = lambda w: pl.BlockSpec((1, FOX_PAIR, s, w), lambda i, p: (i, p, 0, 0))
    return pl.pallas_call(
        _fox_body,
        grid=(b, h // FOX_PAIR),
        in_specs=[seq(HEAD_DIM), seq(HEAD_DIM), seq(2 * HEAD_DIM),
                  pl.BlockSpec((1, FOX_PAIR, 1, s), lambda i, p: (i, p, 0, 0)),
                  _resident((t, t))],
        out_specs=pl.BlockSpec((1, s, FOX_PAIR * HEAD_DIM), lambda i, p: (i, 0, p)),
        out_shape=jax.ShapeDtypeStruct((b, s, h * HEAD_DIM), BF16),
        scratch_shapes=[pltpu.VMEM((2, FOX_PAIR, s // t, t, t), F32),
                        pltpu.VMEM((2, FOX_PAIR, t, LANES), F32),
                        pltpu.VMEM((FOX_PAIR, t, 2 * HEAD_DIM), F32)],
        compiler_params=_params(("parallel", "parallel")),
        name="fox",
    )(qf, kf, vf, c.reshape(b, h, 1, s), _causal_bias(t))


def _mixout_body(x_ref, g_ref, on_ref, of_ref, wa_ref, wb_ref, wun_ref, wuf_ref, wo_ref, o_ref):
    x = x_ref[...]
    h = _rms(x, g_ref[...]).astype(BF16)
    y = (jax.nn.sigmoid(_dot(h, wa_ref[...])) * _dot(on_ref[...], wun_ref[...])
         + jax.nn.sigmoid(_dot(h, wb_ref[...])) * _dot(of_ref[...], wuf_ref[...]))
    o_ref[...] = x + _dot(y.astype(BF16), wo_ref[...])


def _mixout(x2d, g, o_nsa, o_fox, w_a, w_b, w_un, w_uf, w_o):
    n = x2d.shape[0]
    tok = lambda c: pl.BlockSpec((TOK_TILE, c), lambda i: (i, 0))
    return pl.pallas_call(
        _mixout_body,
        grid=(n // TOK_TILE,),
        in_specs=[tok(D_MODEL), _resident((1, D_MODEL)), tok(NSA_Q), tok(FOX_W),
                  _resident(w_a.shape), _resident(w_b.shape), _resident(w_un.shape),
                  _resident(w_uf.shape), _resident(w_o.shape)],
        out_specs=tok(D_MODEL),
        out_shape=jax.ShapeDtypeStruct((n, D_MODEL), F32),
        compiler_params=_params(("parallel",)),
        name="mixout",
    )(x2d, g.reshape(1, D_MODEL), o_nsa, o_fox, w_a, w_b, w_un, w_uf, w_o)


def _memkv_body(m_ref, g_ref, w_ref, k_ref, v_ref):
    h = _rms(m_ref[0], g_ref[...]).astype(BF16)
    z = _dot(h, w_ref[...])
    k_ref[0] = z[:, :D_MODEL].astype(BF16)
    v_ref[0] = z[:, D_MODEL:].astype(BF16)


def _memkv(mem, g, w_kv):
    b, ml, _ = mem.shape
    blk = pl.BlockSpec((1, ml, D_MODEL), lambda i: (i, 0, 0))
    sd = jax.ShapeDtypeStruct((b, ml, D_MODEL), BF16)
    return pl.pallas_call(
        _memkv_body,
        grid=(b,),
        in_specs=[blk, _resident((1, D_MODEL)), _resident(w_kv.shape)],
        out_specs=(blk, blk),
        out_shape=(sd, sd),
        compiler_params=_params(("parallel",)),
        name="memkv",
    )(mem, g.reshape(1, D_MODEL), w_kv)


def _memattn_body(x_ref, g_ref, k_ref, v_ref, wq_ref, wo_ref, o_ref):
    x = x_ref[0]
    h = _rms(x, g_ref[...]).astype(BF16)
    q = (_dot(h, wq_ref[...]) * (MEM_HEAD_DIM ** -0.5)).astype(BF16)
    outs = []
    for hd in range(MEM_HEADS):
        cols = slice(hd * MEM_HEAD_DIM, (hd + 1) * MEM_HEAD_DIM)
        s = _dot_nt(q[:, cols], k_ref[0, :, cols])
        e = jnp.exp(s - jnp.max(s, axis=-1, keepdims=True))
        p = e / jnp.sum(e, axis=-1, keepdims=True)
        outs.append(_dot(p.astype(BF16), v_ref[0, :, cols]))
    o = jnp.concatenate(outs, axis=1).astype(BF16)
    o_ref[0] = x + _dot(o, wo_ref[...])


def _memattn(x, g, k, v, w_q, w_o):
    b, s, _ = x.shape
    ml = k.shape[1]
    tok = pl.BlockSpec((1, TOK_TILE, D_MODEL), lambda i, j: (i, j, 0))
    kvb = pl.BlockSpec((1, ml, D_MODEL), lambda i, j: (i, 0, 0))
    return pl.pallas_call(
        _memattn_body,
        grid=(b, s // TOK_TILE),
        in_specs=[tok, _resident((1, D_MODEL)), kvb, kvb, _resident(w_q.shape), _resident(w_o.shape)],
        out_specs=tok,
        out_shape=jax.ShapeDtypeStruct((b, s, D_MODEL), F32),
        compiler_params=_params(("parallel", "parallel")),
        name="memattn",
    )(x, g.reshape(1, D_MODEL), k, v, w_q, w_o)


def kernel(x, mem, rel_bias_table, ffn1_norm, ffn1_w_gate, ffn1_w_up, ffn1_w_down, mix_norm, mix_w_in, mix_b_forget, cmp_pos_k, cmp_pos_v, cmp_k_w1, cmp_k_w2, cmp_v_w1, cmp_v_w2, w_up_nsa, w_up_fox, mix_w_out, mem_q_norm, mem_kv_norm, mem_w_q, mem_w_kv, mem_w_o, ffn2_norm, ffn2_w_gate, ffn2_w_up, ffn2_w_down, final_norm):
    b, s, d = x.shape
    depth = ffn1_norm.shape[0]
    bf = lambda w: w.astype(BF16)
    toep, bias_c = _relbias(rel_bias_table, s)
    x = x.reshape(b * s, d)
    for l in range(depth):
        last = l == depth - 1
        x = _ffn(x, ffn1_norm[l], bf(ffn1_w_gate[l]), bf(ffn1_w_up[l]), bf(ffn1_w_down[l]))

        w_main, w_a, w_b = _pack_w_in(mix_w_in[l])
        qn, kc, vc, ks, vs, kw, vw, qf, kf, vf, gates, flog = _inproj(x.reshape(b, s, d), mix_norm[l], w_main)
        kcc, vcc = _compress(kc, vc, cmp_pos_k[l], cmp_pos_v[l], cmp_k_w1[l], cmp_k_w2[l],
                             cmp_v_w1[l], cmp_v_w2[l])
        o_nsa = _nsa(qn, kcc, vcc, ks, vs, kw, vw, gates, toep, bias_c)
        c = _fcum(flog.transpose(0, 2, 1), mix_b_forget[l])
        o_fox = _fox(qf, kf, vf, c)
        x = _mixout(x, mix_norm[l], o_nsa.reshape(b * s, NSA_Q), o_fox.reshape(b * s, FOX_W),
                    w_a, w_b, bf(w_up_nsa[l]), bf(w_up_fox[l]), bf(mix_w_out[l]))

        mk, mv = _memkv(mem, mem_kv_norm[l], bf(mem_w_kv[l]))
        x = _memattn(x.reshape(b, s, d), mem_q_norm[l], mk, mv, bf(mem_w_q[l]), bf(mem_w_o[l]))
        x = _ffn(x.reshape(b * s, d), ffn2_norm[l], bf(ffn2_w_gate[l]), bf(ffn2_w_up[l]),
                 bf(ffn2_w_down[l]), final_g=final_norm if last else None)
    return x.reshape(b, s, d)
```

```python
import functools
import math

import numpy as np
import jax
import jax.numpy as jnp
from jax import lax
from jax.experimental import pallas as pl
from jax.experimental.pallas import tpu as pltpu

D_MODEL = 1024
D_FF = 2816
HEAD_DIM = 64
NSA_HEADS = 8
NSA_KV_GROUPS = 2
NSA_HPG = NSA_HEADS // NSA_KV_GROUPS
CMP_BLOCK = 32
CMP_STRIDE = 16
CMP_HIDDEN = 256
SEL_BLOCK = 64
SEL_TOP_N = 16
WINDOW = 512
FOX_HEADS = 8
MEM_HEADS = 4
MEM_HEAD_DIM = D_MODEL // MEM_HEADS
NUM_BUCKETS = 32
MAX_DISTANCE = 128
RMS_EPS = 1e-6
NEG_INF = -1e30
FORCE_BONUS = 1e4

NSA_Q = NSA_HEADS * HEAD_DIM
NSA_KV = NSA_KV_GROUPS * HEAD_DIM
FOX_W = FOX_HEADS * HEAD_DIM

LANES = 128
VMEM_LIMIT = 56 * 1024 * 1024

BF16 = jnp.bfloat16
F32 = jnp.float32

ATT_TILE = 256
TOK_TILE = 1024
FFN_TILE = 1024
FF_CHUNK = 256


def _bucket_thresholds():
    n = np.arange(0, 4 * MAX_DISTANCE)
    exact = NUM_BUCKETS // 2
    large = exact + (np.log(np.maximum(n, 1) / exact) / math.log(MAX_DISTANCE / exact)
                     * (NUM_BUCKETS - exact)).astype(np.int64)
    bucket = np.where(n < exact, n, np.minimum(large, NUM_BUCKETS - 1))
    assert np.all(np.diff(bucket) >= 0)
    return [int(np.argmax(bucket >= k)) for k in range(1, NUM_BUCKETS)]


BUCKET_THRESHOLDS = _bucket_thresholds()


def _dot(a, b):
    return jnp.dot(a, b, preferred_element_type=F32)


def _dot_nt(a, b):
    return lax.dot_general(a, b, (((1,), (1,)), ((), ())), preferred_element_type=F32)


def _row_halves(fn, a, b):
    h = a.shape[0] // 2
    return jnp.concatenate([fn(a[:h], b), fn(a[h:], b)], axis=0)


def _dot_pair(a, b):
    return _row_halves(_dot, a, b)


def _dot_nt_pair(a, b):
    return _row_halves(_dot_nt, a, b)


def _rms(x, g):
    return x * lax.rsqrt(jnp.mean(x * x, axis=-1, keepdims=True) + RMS_EPS) * g


def _resident(shape):
    nd = len(shape)
    return pl.BlockSpec(shape, lambda *_: (0,) * nd, pipeline_mode=pl.Buffered(1))


def _params(sem):
    return pltpu.CompilerParams(dimension_semantics=sem, vmem_limit_bytes=VMEM_LIMIT)


def _ffn_body(x_ref, g_ref, wg_ref, wu_ref, wd_ref, *rest, final):
    if final:
        fg_ref, o_ref = rest
    else:
        (o_ref,) = rest
    x = x_ref[...]
    h = _rms(x, g_ref[...]).astype(BF16)
    acc = jnp.zeros(x.shape, F32)
    for c in range(D_FF // FF_CHUNK):
        sl = slice(c * FF_CHUNK, (c + 1) * FF_CHUNK)
        a = _dot(h, wg_ref[:, sl])
        b = _dot(h, wu_ref[:, sl])
        t = (a * jax.nn.sigmoid(a)) * b
        acc = acc + _dot(t.astype(BF16), wd_ref[sl, :])
    y = x + 0.5 * acc
    if final:
        y = _rms(y, fg_ref[...])
    o_ref[...] = y


def _ffn(x2d, g, wg, wu, wd, final_g=None):
    n = x2d.shape[0]
    final = final_g is not None
    tok = pl.BlockSpec((FFN_TILE, D_MODEL), lambda i: (i, 0))
    in_specs = [tok, _resident((1, D_MODEL)), _resident((D_MODEL, D_FF)),
                _resident((D_MODEL, D_FF)), _resident((D_FF, D_MODEL))]
    args = [x2d, g.reshape(1, D_MODEL), wg, wu, wd]
    if final:
        in_specs.append(_resident((1, D_MODEL)))
        args.append(final_g.reshape(1, D_MODEL))
    return pl.pallas_call(
        functools.partial(_ffn_body, final=final),
        grid=(n // FFN_TILE,),
        in_specs=in_specs,
        out_specs=tok,
        out_shape=jax.ShapeDtypeStruct((n, D_MODEL), F32),
        compiler_params=_params(("parallel",)),
        name="ffn_final" if final else "ffn",
    )(*args)


_C_QN = 0
_C_CMP = _C_QN + NSA_Q
_C_KV = _C_CMP + 2 * NSA_KV
_C_FOX = _C_KV + 4 * NSA_KV
_C_SMALL = _C_FOX + 3 * FOX_W
_C_END = _C_SMALL + LANES
GATE_PAD = 16


def _inproj_body(x_ref, g_ref, w_ref, blk_ref, qn_ref, kc_ref, vc_ref, ks_ref, vs_ref, kw_ref, vw_ref,
                 qf_ref, kf_ref, vf_ref, gate_ref, fl_ref):
    h = _rms(x_ref[0], g_ref[...]).astype(BF16)
    z = _dot(h, w_ref[...])
    qn_ref[0] = (z[:, _C_QN:_C_CMP] * (HEAD_DIM ** -0.5)).astype(BF16)
    kc_ref[0] = z[:, _C_CMP:_C_CMP + NSA_KV].astype(BF16)
    vc_ref[0] = z[:, _C_CMP + NSA_KV:_C_KV].astype(BF16)
    ones = jnp.ones((z.shape[0], HEAD_DIM), BF16)
    for i, (ref, extra) in enumerate(((ks_ref, blk_ref[...]), (vs_ref, ones), (kw_ref, None),
                                      (vw_ref, ones))):
        for g in range(NSA_KV_GROUPS):
            c0 = _C_KV + i * NSA_KV + g * HEAD_DIM
            val = z[:, c0:c0 + HEAD_DIM].astype(BF16)
            ref[0, g] = val if extra is None else jnp.concatenate([val, extra], axis=1)
    for hd in range(FOX_HEADS):
        c0 = _C_FOX + hd * HEAD_DIM
        qf_ref[0, hd] = (z[:, c0:c0 + HEAD_DIM] * (HEAD_DIM ** -0.5)).astype(BF16)
        kf_ref[0, hd] = z[:, c0 + FOX_W:c0 + FOX_W + HEAD_DIM].astype(BF16)
        v = z[:, c0 + 2 * FOX_W:c0 + 2 * FOX_W + HEAD_DIM].astype(BF16)
        vf_ref[0, hd] = jnp.concatenate([v, ones], axis=1)
    for g in range(NSA_KV_GROUPS):
        c0 = _C_SMALL + g * GATE_PAD
        gate_ref[0, g] = z[:, c0:c0 + GATE_PAD]
    c0 = _C_SMALL + NSA_KV_GROUPS * GATE_PAD
    fl_ref[0] = z[:, c0:c0 + FOX_HEADS]


def _pack_w_in(w_in):
    cols = np.cumsum((0, NSA_Q, NSA_KV, NSA_KV, NSA_KV, NSA_KV, NSA_KV, NSA_KV, 3 * NSA_HEADS,
                      FOX_W, FOX_W, FOX_W, FOX_HEADS, D_MODEL, D_MODEL))
    g0 = int(cols[7])
    gate_cols = []
    for g in range(NSA_KV_GROUPS):
        for br in range(3):
            c0 = g0 + br * NSA_HEADS + g * NSA_HPG
            gate_cols.append(w_in[:, c0:c0 + NSA_HPG])
        gate_cols.append(jnp.zeros((D_MODEL, GATE_PAD - 3 * NSA_HPG), w_in.dtype))
    small = jnp.concatenate(gate_cols + [w_in[:, int(cols[11]):int(cols[12])]], axis=1)
    small = jnp.pad(small, ((0, 0), (0, LANES - small.shape[1])))
    main = jnp.concatenate([w_in[:, :g0], w_in[:, int(cols[8]):int(cols[11])], small], axis=1)
    w_a = w_in[:, int(cols[12]):int(cols[13])]
    w_b = w_in[:, int(cols[13]):int(cols[14])]
    return main.astype(BF16), w_a.astype(BF16), w_b.astype(BF16)


def _inproj(x, g, w_main):
    b, s, _ = x.shape
    grid = (b, s // TOK_TILE)
    tok = lambda c: pl.BlockSpec((1, TOK_TILE, c), lambda i, j: (i, j, 0))
    heads = lambda nh, w=HEAD_DIM: pl.BlockSpec((1, nh, TOK_TILE, w), lambda i, j: (i, 0, j, 0))
    sds = jax.ShapeDtypeStruct
    kv = sds((b, NSA_KV_GROUPS, s, HEAD_DIM), BF16)
    kv2 = sds((b, NSA_KV_GROUPS, s, 2 * HEAD_DIM), BF16)
    fx = sds((b, FOX_HEADS, s, HEAD_DIM), BF16)
    out_shape = (sds((b, s, NSA_Q), BF16), sds((b, s, NSA_KV), BF16), sds((b, s, NSA_KV), BF16),
                 kv2, kv2, kv, kv2, fx, fx, sds((b, FOX_HEADS, s, 2 * HEAD_DIM), BF16),
                 sds((b, NSA_KV_GROUPS, s, GATE_PAD), F32), sds((b, s, FOX_HEADS), F32))
    out_specs = (tok(NSA_Q), tok(NSA_KV), tok(NSA_KV),
                 heads(NSA_KV_GROUPS, 2 * HEAD_DIM), heads(NSA_KV_GROUPS, 2 * HEAD_DIM),
                 heads(NSA_KV_GROUPS), heads(NSA_KV_GROUPS, 2 * HEAD_DIM),
                 heads(FOX_HEADS), heads(FOX_HEADS), heads(FOX_HEADS, 2 * HEAD_DIM),
                 pl.BlockSpec((1, NSA_KV_GROUPS, TOK_TILE, GATE_PAD), lambda i, j: (i, 0, j, 0)),
                 tok(FOX_HEADS))
    key_blk = np.arange(s) // SEL_BLOCK
    blk_onehot = jnp.asarray(key_blk[:, None] == np.arange(HEAD_DIM)[None, :], BF16)
    return pl.pallas_call(
        _inproj_body,
        grid=grid,
        in_specs=[tok(D_MODEL), _resident((1, D_MODEL)), _resident((D_MODEL, _C_END)),
                  pl.BlockSpec((TOK_TILE, HEAD_DIM), lambda i, j: (j, 0))],
        out_specs=out_specs,
        out_shape=out_shape,
        compiler_params=_params(("parallel", "parallel")),
        name="inproj",
    )(x, g.reshape(1, D_MODEL), w_main, blk_onehot)


N_CHUNKS_PAD = 128
CHUNK_W = CMP_STRIDE * NSA_KV


def _compress_body(xk_ref, xv_ref, pk_ref, pv_ref, wk1_ref, wv1_ref, wk2_ref, wv2_ref, ok_ref, ov_ref):
    for x_ref, p_ref, w1_ref, w2_ref, o_ref in ((xk_ref, pk_ref, wk1_ref, wk2_ref, ok_ref),
                                                (xv_ref, pv_ref, wv1_ref, wv2_ref, ov_ref)):
        x = x_ref[0]
        a0 = _dot(x, w1_ref[0])
        a1 = _dot(x, w1_ref[1])
        c = _dot(p_ref[0], w1_ref[0]) + _dot(p_ref[1], w1_ref[1])
        pre = a0 + pltpu.roll(a1, N_CHUNKS_PAD - 1, axis=0) + c[0:1, :]
        hid = jax.nn.gelu(pre).astype(BF16)
        out = _dot(hid, w2_ref[...])
        for g in range(NSA_KV_GROUPS):
            o_ref[0, g] = out[:, g * HEAD_DIM:(g + 1) * HEAD_DIM].astype(BF16)


def _pack_compress(pos, w1, w2):
    r = CMP_BLOCK // CMP_STRIDE
    assert NSA_KV_GROUPS == 2
    w1r = w1.astype(BF16).reshape(r, CMP_STRIDE, HEAD_DIM, CMP_HIDDEN)
    z1 = jnp.zeros_like(w1r)
    w1big = jnp.stack([jnp.concatenate([w1r, z1], axis=-1), jnp.concatenate([z1, w1r], axis=-1)], axis=2)
    w1big = w1big.reshape(r, CHUNK_W, NSA_KV_GROUPS * CMP_HIDDEN)
    z2 = jnp.zeros_like(w2)
    w2big = jnp.concatenate([jnp.concatenate([w2, z2], axis=1), jnp.concatenate([z2, w2], axis=1)], axis=0)
    p = pos.reshape(r, CMP_STRIDE, 1, HEAD_DIM)
    p = jnp.broadcast_to(p, (r, CMP_STRIDE, NSA_KV_GROUPS, HEAD_DIM)).reshape(r, 1, CHUNK_W)
    p = jnp.broadcast_to(p, (r, 16, CHUNK_W))
    return p.astype(BF16), w1big.astype(BF16), w2big.astype(BF16)


def _compress(kc, vc, pos_k, pos_v, k_w1, k_w2, v_w1, v_w2):
    b, s, _ = kc.shape
    xk = kc.reshape(b, N_CHUNKS_PAD, CHUNK_W)
    xv = vc.reshape(b, N_CHUNKS_PAD, CHUNK_W)
    pk, wk1, wk2 = _pack_compress(pos_k, k_w1, k_w2)
    pv, wv1, wv2 = _pack_compress(pos_v, v_w1, v_w2)
    xs = pl.BlockSpec((1, N_CHUNKS_PAD, CHUNK_W), lambda i: (i, 0, 0))
    os_ = pl.BlockSpec((1, NSA_KV_GROUPS, N_CHUNKS_PAD, HEAD_DIM), lambda i: (i, 0, 0, 0))
    osd = jax.ShapeDtypeStruct((b, NSA_KV_GROUPS, N_CHUNKS_PAD, HEAD_DIM), BF16)
    return pl.pallas_call(
        _compress_body,
        grid=(b,),
        in_specs=[xs, xs, _resident(pk.shape), _resident(pv.shape), _resident(wk1.shape),
                  _resident(wv1.shape), _resident(wk2.shape), _resident(wv2.shape)],
        out_specs=(os_, os_),
        out_shape=(osd, osd),
        compiler_params=_params(("parallel",)),
        name="compress",
    )(xk, xv, pk, pv, wk1, wv1, wk2, wv2)


def _bias_lookup(tbl_ref, head, dist):
    acc = jnp.full(dist.shape, tbl_ref[0, head], F32)
    for k, thr in enumerate(BUCKET_THRESHOLDS):
        acc = jnp.where(dist >= thr, tbl_ref[k + 1, head], acc)
    return acc - tbl_ref[NUM_BUCKETS - 1, head]


def _relbias_body(tbl_ref, toep_ref, bc_ref):
    head = pl.program_id(0)
    t = ATT_TILE
    i = lax.broadcasted_iota(jnp.int32, (t, 2 * t), 0)
    u = lax.broadcasted_iota(jnp.int32, (t, 2 * t), 1)
    d = i - u + t
    toep_ref[0] = jnp.where(d >= 0, _bias_lookup(tbl_ref, head, d), NEG_INF)
    n_tiles = bc_ref.shape[1] // t
    per_tile = t // CMP_STRIDE
    off = per_tile * (n_tiles - 1)
    width = 2 * N_CHUNKS_PAD
    assert off + N_CHUNKS_PAD <= width
    i = lax.broadcasted_iota(jnp.int32, (t, width), 0)
    u = lax.broadcasted_iota(jnp.int32, (t, width), 1)
    dc = i - ((u - off) * CMP_STRIDE + CMP_BLOCK - 1)
    strip = jnp.where(dc >= 0, _bias_lookup(tbl_ref, head, dc), NEG_INF)
    for q in range(n_tiles):
        lo = off - q * per_tile
        bc_ref[0, q * t:(q + 1) * t, :] = strip[:, lo:lo + N_CHUNKS_PAD]


def _relbias(tbl, s):
    return pl.pallas_call(
        _relbias_body,
        grid=(NSA_HEADS,),
        in_specs=[pl.BlockSpec(memory_space=pltpu.SMEM)],
        out_specs=(pl.BlockSpec((1, ATT_TILE, 2 * ATT_TILE), lambda h: (h, 0, 0)),
                   pl.BlockSpec((1, s, N_CHUNKS_PAD), lambda h: (h, 0, 0))),
        out_shape=(jax.ShapeDtypeStruct((NSA_HEADS, ATT_TILE, 2 * ATT_TILE), F32),
                   jax.ShapeDtypeStruct((NSA_HEADS, s, N_CHUNKS_PAD), F32)),
        compiler_params=_params(("parallel",)),
        name="relbias",
    )(tbl)


def _lane_fold(x, op):
    return functools.reduce(op, [x[:, i * LANES:(i + 1) * LANES] for i in range(x.shape[1] // LANES)])


def _lane_tile(x, width):
    return jnp.concatenate([x] * (width // LANES), axis=1)


def _row_max_tile(rmax):
    return jnp.broadcast_to(jnp.max(rmax, axis=-1, keepdims=True), rmax.shape)


def _normalize(acc):
    return (acc / pltpu.roll(acc, HEAD_DIM, axis=1))[:, :HEAD_DIM]


def _split3(x):
    hi = x.astype(BF16)
    r = x - hi.astype(F32)
    mid = r.astype(BF16)
    lo = (r - mid.astype(F32)).astype(BF16)
    return hi, mid, lo


def _nsa_body(q_ref, kc_ref, vc_ref, ks_ref, vs_ref, kw_ref, vw_ref, gate_a_ref, gate_b_ref, toep_ref,
              bc_ref, ov_ref, o_ref, s_ref, near_ref, mb_ref, acc_ref, part_ref, *, n_tiles):
    t = ATT_TILE
    j4 = NSA_HPG
    rows = j4 * t
    groups = range(NSA_KV_GROUPS)
    step = pl.program_id(1)
    n_sel = n_tiles * (t // SEL_BLOCK)

    def heads(x):
        return x.reshape(j4, t, x.shape[-1])

    def flat(x):
        return x.reshape(rows, x.shape[-1])

    def key_tile(ref, g, n):
        return ref[0, g, pl.ds(pl.multiple_of(n * t, t), t), :]

    def gated(gate_ref, g, branch, o):
        gate = jax.nn.sigmoid(gate_ref[0, g])
        return [gate[:, branch * j4 + j:branch * j4 + j + 1] * o[j * t:(j + 1) * t] for j in range(j4)]

    def probs(par, g, s):
        return jnp.exp(s - _lane_tile(mb_ref[par, g], t)).astype(BF16)

    def open_tile(m, par):
        m1 = jnp.maximum(m - 1, 0)
        m2 = jnp.maximum(m - 2, 0)
        no_prev = jnp.where(m >= 1, 0.0, NEG_INF)
        qs = []
        for g in groups:
            qt = q_ref[0, :, g * j4 * HEAD_DIM:(g + 1) * j4 * HEAD_DIM]
            qs.append(jnp.concatenate([qt[:, j * HEAD_DIM:(j + 1) * HEAD_DIM] for j in range(j4)], axis=0))

        def biases(g):
            toep = toep_ref[g * j4:(g + 1) * j4]
            return toep[:, :, t:], toep[:, :, :t] + no_prev

        cmp_out = [_nsa_compressed(qs[g], kc_ref[0, g], vc_ref[0, g], bc_ref[g * j4:(g + 1) * j4],
                                   ov_ref[...], m, n_sel) for g in groups]
        part = []
        for g in groups:
            diag_bias, prev_bias = biases(g)
            ii = lax.broadcasted_iota(jnp.int32, (t, t), 0)
            jj = lax.broadcasted_iota(jnp.int32, (t, t), 1)
            tri = jnp.where(jj > ii, 0.0, NEG_INF) + jnp.where(m >= 2, 0.0, NEG_INF)
            w0 = flat(heads(_dot_nt_pair(qs[g], key_tile(kw_ref, g, m))) + diag_bias)
            w1 = flat(heads(_dot_nt_pair(qs[g], key_tile(kw_ref, g, m1))) + prev_bias)
            w2 = flat(heads(_dot_nt_pair(qs[g], key_tile(kw_ref, g, m2))) + tri[None])
            wmax = _lane_tile(_row_max_tile(functools.reduce(
                jnp.maximum, [_lane_fold(w, jnp.maximum) for w in (w0, w1, w2)])), t)
            acc = sum(_dot_pair(jnp.exp(w - wmax).astype(BF16), key_tile(vw_ref, g, n))
                      for w, n in ((w0, m), (w1, m1), (w2, m2)))
            part += [a + b for a, b in zip(gated(gate_a_ref, g, 0, cmp_out[g][0]),
                                           gated(gate_a_ref, g, 2, _normalize(acc)))]
        part_ref[par] = jnp.concatenate(part, axis=1)

        q_aug = []
        for g in groups:
            diag_bias, prev_bias = biases(g)
            q_aug.append(jnp.concatenate([qs[g], jnp.concatenate([cmp_out[g][1]] * j4, axis=0)], axis=1))
            s0 = flat(heads(_dot_nt_pair(q_aug[g], key_tile(ks_ref, g, m))) + diag_bias)
            s1 = flat(heads(_dot_nt_pair(q_aug[g], key_tile(ks_ref, g, m1))) + prev_bias)
            s2 = _dot_nt_pair(q_aug[g], key_tile(ks_ref, g, m2))
            near_ref[par, g, 0] = s0
            near_ref[par, g, 1] = s1
            s_ref[g, m2] = s2
            far = jnp.where(m >= 2, _lane_fold(s2, jnp.maximum), NEG_INF)
            mb_ref[par, g] = jnp.maximum(jnp.maximum(_lane_fold(s0, jnp.maximum),
                                                     _lane_fold(s1, jnp.maximum)), far)
        return q_aug

    def open_far(par, q_aug, n):
        for g in groups:
            s = _dot_nt_pair(q_aug[g], key_tile(ks_ref, g, n))
            s_ref[g, n] = s
            mb_ref[par, g] = jnp.maximum(mb_ref[par, g], _lane_fold(s, jnp.maximum))

    def open_finish(par):
        for g in groups:
            mb_ref[par, g] = _row_max_tile(mb_ref[par, g])

    def close_tile(m, par):
        m1 = jnp.maximum(m - 1, 0)
        for g in groups:
            acc_ref[g] = (_dot_pair(probs(par, g, near_ref[par, g, 0]), key_tile(vs_ref, g, m))
                          + _dot_pair(probs(par, g, near_ref[par, g, 1]), key_tile(vs_ref, g, m1)))

    def close_far(par, n):
        for g in groups:
            acc_ref[g] += _dot_pair(probs(par, g, s_ref[g, n]), key_tile(vs_ref, g, n))

    def close_finish(par):
        outs = []
        for g in groups:
            outs += gated(gate_b_ref, g, 1, _normalize(acc_ref[g]))
        o_ref[0] = (part_ref[par] + jnp.concatenate(outs, axis=1)).astype(BF16)

    @pl.when(step == 0)
    def _():
        open_tile(step, 0)
        open_finish(0)

    for par in (0, 1):
        @pl.when(jnp.logical_and(jnp.logical_and(step >= 1, step < n_tiles), step % 2 == par))
        def _(par=par):
            q_aug = open_tile(step, par)
            close_tile(step - 1, 1 - par)

            @pl.loop(0, jnp.maximum(step - 2, 0))
            def _(n):
                close_far(1 - par, n)
                open_far(par, q_aug, n)

            open_finish(par)
            close_finish(1 - par)

    @pl.when(step == n_tiles)
    def _():
        par = (n_tiles - 1) % 2
        close_tile(step - 1, par)

        @pl.loop(0, step - 2)
        def _(n):
            close_far(par, n)

        close_finish(par)


def _nsa_compressed(q, kc, vc, bc, ov, m, n_sel):
    t = ATT_TILE
    j4 = NSA_HPG

    def heads(x):
        return x.reshape(j4, t, x.shape[-1])

    def flat(x):
        return x.reshape(j4 * t, x.shape[-1])

    sc = heads(_dot_nt_pair(q, kc)) + bc
    valid = bc > 0.5 * NEG_INF
    e = jnp.where(valid, jnp.exp(sc - jnp.max(sc, axis=-1, keepdims=True)), 0.0)
    l = jnp.sum(e, axis=-1, keepdims=True)
    pc = jnp.where(l > 0.0, e / l, 0.0)
    o_c = _dot_pair(flat(pc).astype(BF16), vc)
    psum = pc[0] + pc[1] + pc[2] + pc[3]
    imp = sum(_dot(part, ov) for part in _split3(psum)).T[:n_sel]
    blk = lax.broadcasted_iota(jnp.int32, (n_sel, t), 0)
    tpos = m * t + lax.broadcasted_iota(jnp.int32, (n_sel, t), 1)
    cur = tpos // SEL_BLOCK
    bonus = jnp.where(blk == 0, FORCE_BONUS,
                      jnp.where(blk == cur, FORCE_BONUS, jnp.where(blk == cur - 1, FORCE_BONUS, 0.0)))
    imp = jnp.where(blk * SEL_BLOCK <= tpos, imp + bonus, NEG_INF)
    rank = jnp.zeros((n_sel, t), F32)
    for i in range(n_sel):
        row = imp[i:i + 1, :]
        before = jnp.where(blk > i, 1.0, 0.0)
        rank = rank + jnp.where(row > imp, 1.0, jnp.where(row == imp, before, 0.0))
    selb = jnp.where(rank < float(SEL_TOP_N), 0.0, NEG_INF)
    selb = jnp.concatenate([selb, jnp.zeros((LANES - n_sel, t), F32)], axis=0).T
    return o_c, selb[:, :HEAD_DIM].astype(BF16)


def _overlap_table(s):
    n_sel = s // SEL_BLOCK
    nc = N_CHUNKS_PAD
    c0 = np.arange(nc)[:, None] * CMP_STRIDE
    s0 = np.arange(n_sel)[None, :] * SEL_BLOCK
    ov = np.clip(np.minimum(c0 + CMP_BLOCK, s0 + SEL_BLOCK) - np.maximum(c0, s0), 0, None) / CMP_STRIDE
    ov[nc - 1] = 0.0
    return jnp.asarray(np.pad(ov, ((0, 0), (0, LANES - n_sel))), BF16)


def _nsa(qn, kc, vc, ks, vs, kw, vw, gates, toep, bc):
    b, s, _ = qn.shape
    t = ATT_TILE
    assert s // SEL_BLOCK <= HEAD_DIM
    ov = _overlap_table(s)
    rows = NSA_HPG * t
    ng = NSA_KV_GROUPS
    nq = s // t
    assert nq >= 2
    seq = lambda n, w=HEAD_DIM: pl.BlockSpec((1, ng, n, w), lambda i, j: (i, 0, 0, 0))
    opened = lambda j: jnp.minimum(j, nq - 1)
    closed = lambda j: jnp.maximum(j - 1, 0)
    return pl.pallas_call(
        functools.partial(_nsa_body, n_tiles=nq),
        grid=(b, nq + 1),
        in_specs=[pl.BlockSpec((1, t, NSA_Q), lambda i, j: (i, opened(j), 0)),
                  seq(N_CHUNKS_PAD), seq(N_CHUNKS_PAD),
                  seq(s, 2 * HEAD_DIM), seq(s, 2 * HEAD_DIM), seq(s), seq(s, 2 * HEAD_DIM),
                  pl.BlockSpec((1, ng, t, GATE_PAD), lambda i, j: (i, 0, opened(j), 0)),
                  pl.BlockSpec((1, ng, t, GATE_PAD), lambda i, j: (i, 0, closed(j), 0)),
                  _resident((NSA_HEADS, t, 2 * t)),
                  pl.BlockSpec((NSA_HEADS, t, N_CHUNKS_PAD), lambda i, j: (0, opened(j), 0)),
                  _resident(ov.shape)],
        out_specs=pl.BlockSpec((1, t, NSA_Q), lambda i, j: (i, closed(j), 0)),
        out_shape=jax.ShapeDtypeStruct((b, s, NSA_Q), BF16),
        scratch_shapes=[pltpu.VMEM((ng, nq - 2, rows, t), F32),
                        pltpu.VMEM((2, ng, 2, rows, t), F32),
                        pltpu.VMEM((2, ng, rows, LANES), F32),
                        pltpu.VMEM((ng, rows, 2 * HEAD_DIM), F32),
                        pltpu.VMEM((2, t, NSA_Q), F32)],
        compiler_params=_params(("parallel", "arbitrary")),
        name="nsa",
    )(qn, kc, vc, ks, vs, kw, vw, gates, gates, toep, bc, ov)


def _fcum_body(f_ref, b_ref, o_ref):
    z = f_ref[0] + b_ref[...]
    x = -(jnp.maximum(-z, 0.0) + jnp.log1p(jnp.exp(-jnp.abs(z))))
    n = x.shape[-1]
    lane = lax.broadcasted_iota(jnp.int32, x.shape, 1)
    sh = 1
    while sh < n:
        x = x + jnp.where(lane >= sh, pltpu.roll(x, sh, axis=1), 0.0)
        sh *= 2
    o_ref[0] = x


def _fcum(f_t, b_forget):
    b, h, s = f_t.shape
    blk = pl.BlockSpec((1, h, s), lambda i: (i, 0, 0))
    return pl.pallas_call(
        _fcum_body,
        grid=(b,),
        in_specs=[blk, _resident((h, 1))],
        out_specs=blk,
        out_shape=jax.ShapeDtypeStruct((b, h, s), F32),
        compiler_params=_params(("parallel",)),
        name="fcum",
    )(f_t, b_forget.reshape(h, 1))


FOX_PAIR = 2
FOX_TILE = 512


def _causal_bias(t):
    i = np.arange(t)
    return jnp.asarray(np.where(i[None, :] <= i[:, None], 0.0, NEG_INF), F32)


def _fox_body(q_ref, k_ref, v_ref, c_ref, cm_ref, o_ref, s_ref, mb_ref, acc_ref):
    t = FOX_TILE
    heads = range(FOX_PAIR)
    nq = q_ref.shape[2] // t

    def keys(n):
        return pl.ds(pl.multiple_of(n * t, t), t)

    def scores(m, hh, n):
        q = q_ref[0, hh, m * t:(m + 1) * t, :]
        return _dot_nt(q, k_ref[0, hh, keys(n), :]) - c_ref[0, hh, :, keys(n)]

    def scores_diag(m):
        for hh in heads:
            s = scores(m, hh, m) + cm_ref[...]
            s_ref[m % 2, hh, m] = s
            mb_ref[m % 2, hh] = _lane_fold(s, jnp.maximum)

    def scores_far(m, n):
        for hh in heads:
            s = scores(m, hh, n)
            s_ref[m % 2, hh, n] = s
            mb_ref[m % 2, hh] = jnp.maximum(mb_ref[m % 2, hh], _lane_fold(s, jnp.maximum))

    def finish_max(m):
        for hh in heads:
            mb_ref[m % 2, hh] = _row_max_tile(mb_ref[m % 2, hh])

    def weigh(m, n):
        for hh in heads:
            p = jnp.exp(s_ref[m % 2, hh, n] - _lane_tile(mb_ref[m % 2, hh], t)).astype(BF16)
            acc_ref[hh] += _dot(p, v_ref[0, hh, keys(n), :])

    scores_diag(0)
    finish_max(0)
    for m in range(nq):
        ahead = m + 1 < nq
        if ahead:
            scores_diag(m + 1)
        for hh in heads:
            acc_ref[hh] = jnp.zeros((t, 2 * HEAD_DIM), F32)

        @pl.loop(0, m + 1)
        def _(n, m=m, ahead=ahead):
            weigh(m, n)
            if ahead:
                scores_far(m + 1, n)

        if ahead:
            finish_max(m + 1)
        o_ref[0, m * t:(m + 1) * t, :] = jnp.concatenate(
            [_normalize(acc_ref[hh]) for hh in heads], axis=1).astype(BF16)


def _fox(qf, kf, vf, c):
    b, h, s, _ = qf.shape
    t = FOX_TILE
    seq = lambda w: pl.BlockSpec((1, FOX_PAIR, s, w), lambda i, p: (i, p, 0, 0))
    return pl.pallas_call(
        _fox_body,
        grid=(b, h // FOX_PAIR),
        in_specs=[seq(HEAD_DIM), seq(HEAD_DIM), seq(2 * HEAD_DIM),
                  pl.BlockSpec((1, FOX_PAIR, 1, s), lambda i, p: (i, p, 0, 0)),
                  _resident((t, t))],
        out_specs=pl.BlockSpec((1, s, FOX_PAIR * HEAD_DIM), lambda i, p: (i, 0, p)),
        out_shape=jax.ShapeDtypeStruct((b, s, h * HEAD_DIM), BF16),
        scratch_shapes=[pltpu.VMEM((2, FOX_PAIR, s // t, t, t), F32),
                        pltpu.VMEM((2, FOX_PAIR, t, LANES), F32),
                        pltpu.VMEM((FOX_PAIR, t, 2 * HEAD_DIM), F32)],
        compiler_params=_params(("parallel", "parallel")),
        name="fox",
    )(qf, kf, vf, c.reshape(b, h, 1, s), _causal_bias(t))


def _mixout_body(x_ref, g_ref, on_ref, of_ref, wa_ref, wb_ref, wun_ref, wuf_ref, wo_ref, o_ref):
    x = x_ref[...]
    h = _rms(x, g_ref[...]).astype(BF16)
    y = (jax.nn.sigmoid(_dot(h, wa_ref[...])) * _dot(on_ref[...], wun_ref[...])
         + jax.nn.sigmoid(_dot(h, wb_ref[...])) * _dot(of_ref[...], wuf_ref[...]))
    o_ref[...] = x + _dot(y.astype(BF16), wo_ref[...])


def _mixout(x2d, g, o_nsa, o_fox, w_a, w_b, w_un, w_uf, w_o):
    n = x2d.shape[0]
    tok = lambda c: pl.BlockSpec((TOK_TILE, c), lambda i: (i, 0))
    return pl.pallas_call(
        _mixout_body,
        grid=(n // TOK_TILE,),
        in_specs=[tok(D_MODEL), _resident((1, D_MODEL)), tok(NSA_Q), tok(FOX_W),
                  _resident(w_a.shape), _resident(w_b.shape), _resident(w_un.shape),
                  _resident(w_uf.shape), _resident(w_o.shape)],
        out_specs=tok(D_MODEL),
        out_shape=jax.ShapeDtypeStruct((n, D_MODEL), F32),
        compiler_params=_params(("parallel",)),
        name="mixout",
    )(x2d, g.reshape(1, D_MODEL), o_nsa, o_fox, w_a, w_b, w_un, w_uf, w_o)


def _memkv_body(m_ref, g_ref, w_ref, k_ref, v_ref):
    h = _rms(m_ref[0], g_ref[...]).astype(BF16)
    z = _dot(h, w_ref[...])
    k_ref[0] = z[:, :D_MODEL].astype(BF16)
    v_ref[0] = z[:, D_MODEL:].astype(BF16)


def _memkv(mem, g, w_kv):
    b, ml, _ = mem.shape
    blk = pl.BlockSpec((1, ml, D_MODEL), lambda i: (i, 0, 0))
    sd = jax.ShapeDtypeStruct((b, ml, D_MODEL), BF16)
    return pl.pallas_call(
        _memkv_body,
        grid=(b,),
        in_specs=[blk, _resident((1, D_MODEL)), _resident(w_kv.shape)],
        out_specs=(blk, blk),
        out_shape=(sd, sd),
        compiler_params=_params(("parallel",)),
        name="memkv",
    )(mem, g.reshape(1, D_MODEL), w_kv)


def _memattn_body(x_ref, g_ref, k_ref, v_ref, wq_ref, wo_ref, o_ref):
    x = x_ref[0]
    h = _rms(x, g_ref[...]).astype(BF16)
    q = (_dot(h, wq_ref[...]) * (MEM_HEAD_DIM ** -0.5)).astype(BF16)
    outs = []
    for hd in range(MEM_HEADS):
        cols = slice(hd * MEM_HEAD_DIM, (hd + 1) * MEM_HEAD_DIM)
        s = _dot_nt(q[:, cols], k_ref[0, :, cols])
        e = jnp.exp(s - jnp.max(s, axis=-1, keepdims=True))
        p = e / jnp.sum(e, axis=-1, keepdims=True)
        outs.append(_dot(p.astype(BF16), v_ref[0, :, cols]))
    o = jnp.concatenate(outs, axis=1).astype(BF16)
    o_ref[0] = x + _dot(o, wo_ref[...])


def _memattn(x, g, k, v, w_q, w_o):
    b, s, _ = x.shape
    ml = k.shape[1]
    tok = pl.BlockSpec((1, TOK_TILE, D_MODEL), lambda i, j: (i, j, 0))
    kvb = pl.BlockSpec((1, ml, D_MODEL), lambda i, j: (i, 0, 0))
    return pl.pallas_call(
        _memattn_body,
        grid=(b, s // TOK_TILE),
        in_specs=[tok, _resident((1, D_MODEL)), kvb, kvb, _resident(w_q.shape), _resident(w_o.shape)],
        out_specs=tok,
        out_shape=jax.ShapeDtypeStruct((b, s, D_MODEL), F32),
        compiler_params=_params(("parallel", "parallel")),
        name="memattn",
    )(x, g.reshape(1, D_MODEL), k, v, w_q, w_o)


def kernel(x, mem, rel_bias_table, ffn1_norm, ffn1_w_gate, ffn1_w_up, ffn1_w_down, mix_norm, mix_w_in, mix_b_forget, cmp_pos_k, cmp_pos_v, cmp_k_w1, cmp_k_w2, cmp_v_w1, cmp_v_w2, w_up_nsa, w_up_fox, mix_w_out, mem_q_norm, mem_kv_norm, mem_w_q, mem_w_kv, mem_w_o, ffn2_norm, ffn2_w_gate, ffn2_w_up, ffn2_w_down, final_norm):
    b, s, d = x.shape
    depth = ffn1_norm.shape[0]
    bf = lambda w: w.astype(BF16)
    toep, bias_c = _relbias(rel_bias_table, s)
    x = x.reshape(b * s, d)
    for l in range(depth):
        last = l == depth - 1
        x = _ffn(x, ffn1_norm[l], bf(ffn1_w_gate[l]), bf(ffn1_w_up[l]), bf(ffn1_w_down[l]))

        w_main, w_a, w_b = _pack_w_in(mix_w_in[l])
        qn, kc, vc, ks, vs, kw, vw, qf, kf, vf, gates, flog = _inproj(x.reshape(b, s, d), mix_norm[l], w_main)
        kcc, vcc = _compress(kc, vc, cmp_pos_k[l], cmp_pos_v[l], cmp_k_w1[l], cmp_k_w2[l],
                             cmp_v_w1[l], cmp_v_w2[l])
        o_nsa = _nsa(qn, kcc, vcc, ks, vs, kw, vw, gates, toep, bias_c)
        c = _fcum(flog.transpose(0, 2, 1), mix_b_forget[l])
        o_fox = _fox(qf, kf, vf, c)
        x = _mixout(x, mix_norm[l], o_nsa.reshape(b * s, NSA_Q), o_fox.reshape(b * s, FOX_W),
                    w_a, w_b, bf(w_up_nsa[l]), bf(w_up_fox[l]), bf(mix_w_out[l]))

        mk, mv = _memkv(mem, mem_kv_norm[l], bf(mem_w_kv[l]))
        x = _memattn(x.reshape(b, s, d), mem_q_norm[l], mk, mv, bf(mem_w_q[l]), bf(mem_w_o[l]))
        x = _ffn(x.reshape(b * s, d), ffn2_norm[l], bf(ffn2_w_gate[l]), bf(ffn2_w_up[l]),
                 bf(ffn2_w_down[l]), final_g=final_norm if last else None)
    return x.reshape(b, s, d)
```

```python
import functools
import math

import numpy as np
import jax
import jax.numpy as jnp
from jax import lax
from jax.experimental import pallas as pl
from jax.experimental.pallas import tpu as pltpu

D_MODEL = 1024
D_FF = 2816
HEAD_DIM = 64
NSA_HEADS = 8
NSA_KV_GROUPS = 2
NSA_HPG = NSA_HEADS // NSA_KV_GROUPS
CMP_BLOCK = 32
CMP_STRIDE = 16
CMP_HIDDEN = 256
SEL_BLOCK = 64
SEL_TOP_N = 16
WINDOW = 512
FOX_HEADS = 8
MEM_HEADS = 4
MEM_HEAD_DIM = D_MODEL // MEM_HEADS
NUM_BUCKETS = 32
MAX_DISTANCE = 128
RMS_EPS = 1e-6
NEG_INF = -1e30
FORCE_BONUS = 1e4

NSA_Q = NSA_HEADS * HEAD_DIM
NSA_KV = NSA_KV_GROUPS * HEAD_DIM
FOX_W = FOX_HEADS * HEAD_DIM

LANES = 128
VMEM_LIMIT = 56 * 1024 * 1024

BF16 = jnp.bfloat16
F32 = jnp.float32

ATT_TILE = 256
TOK_TILE = 1024
FFN_TILE = 1024
FF_CHUNK = 256


def _bucket_thresholds():
    n = np.arange(0, 4 * MAX_DISTANCE)
    exact = NUM_BUCKETS // 2
    large = exact + (np.log(np.maximum(n, 1) / exact) / math.log(MAX_DISTANCE / exact)
                     * (NUM_BUCKETS - exact)).astype(np.int64)
    bucket = np.where(n < exact, n, np.minimum(large, NUM_BUCKETS - 1))
    assert np.all(np.diff(bucket) >= 0)
    return [int(np.argmax(bucket >= k)) for k in range(1, NUM_BUCKETS)]


BUCKET_THRESHOLDS = _bucket_thresholds()


def _dot(a, b):
    return jnp.dot(a, b, preferred_element_type=F32)


def _dot_nt(a, b):
    return lax.dot_general(a, b, (((1,), (1,)), ((), ())), preferred_element_type=F32)


def _row_halves(fn, a, b):
    h = a.shape[0] // 2
    return jnp.concatenate([fn(a[:h], b), fn(a[h:], b)], axis=0)


def _dot_pair(a, b):
    return _row_halves(_dot, a, b)


def _dot_nt_pair(a, b):
    return _row_halves(_dot_nt, a, b)


def _rms(x, g):
    return x * lax.rsqrt(jnp.mean(x * x, axis=-1, keepdims=True) + RMS_EPS) * g


def _resident(shape):
    nd = len(shape)
    return pl.BlockSpec(shape, lambda *_: (0,) * nd, pipeline_mode=pl.Buffered(1))


def _params(sem):
    return pltpu.CompilerParams(dimension_semantics=sem, vmem_limit_bytes=VMEM_LIMIT)


def _ffn_body(x_ref, g_ref, wg_ref, wu_ref, wd_ref, *rest, final):
    if final:
        fg_ref, o_ref = rest
    else:
        (o_ref,) = rest
    x = x_ref[...]
    h = _rms(x, g_ref[...]).astype(BF16)
    acc = jnp.zeros(x.shape, F32)
    for c in range(D_FF // FF_CHUNK):
        sl = slice(c * FF_CHUNK, (c + 1) * FF_CHUNK)
        a = _dot(h, wg_ref[:, sl])
        b = _dot(h, wu_ref[:, sl])
        t = (a * jax.nn.sigmoid(a)) * b
        acc = acc + _dot(t.astype(BF16), wd_ref[sl, :])
    y = x + 0.5 * acc
    if final:
        y = _rms(y, fg_ref[...])
    o_ref[...] = y


def _ffn(x2d, g, wg, wu, wd, final_g=None):
    n = x2d.shape[0]
    final = final_g is not None
    tok = pl.BlockSpec((FFN_TILE, D_MODEL), lambda i: (i, 0))
    in_specs = [tok, _resident((1, D_MODEL)), _resident((D_MODEL, D_FF)),
                _resident((D_MODEL, D_FF)), _resident((D_FF, D_MODEL))]
    args = [x2d, g.reshape(1, D_MODEL), wg, wu, wd]
    if final:
        in_specs.append(_resident((1, D_MODEL)))
        args.append(final_g.reshape(1, D_MODEL))
    return pl.pallas_call(
        functools.partial(_ffn_body, final=final),
        grid=(n // FFN_TILE,),
        in_specs=in_specs,
        out_specs=tok,
        out_shape=jax.ShapeDtypeStruct((n, D_MODEL), F32),
        compiler_params=_params(("parallel",)),
        name="ffn_final" if final else "ffn",
    )(*args)


_C_QN = 0
_C_CMP = _C_QN + NSA_Q
_C_KV = _C_CMP + 2 * NSA_KV
_C_FOX = _C_KV + 4 * NSA_KV
_C_SMALL = _C_FOX + 3 * FOX_W
_C_END = _C_SMALL + LANES
GATE_PAD = 16


def _inproj_body(x_ref, g_ref, w_ref, blk_ref, qn_ref, kc_ref, vc_ref, ks_ref, vs_ref, kw_ref, vw_ref,
                 qf_ref, kf_ref, vf_ref, gate_ref, fl_ref):
    h = _rms(x_ref[0], g_ref[...]).astype(BF16)
    z = _dot(h, w_ref[...])
    qn_ref[0] = (z[:, _C_QN:_C_CMP] * (HEAD_DIM ** -0.5)).astype(BF16)
    kc_ref[0] = z[:, _C_CMP:_C_CMP + NSA_KV].astype(BF16)
    vc_ref[0] = z[:, _C_CMP + NSA_KV:_C_KV].astype(BF16)
    ones = jnp.ones((z.shape[0], HEAD_DIM), BF16)
    for i, (ref, extra) in enumerate(((ks_ref, blk_ref[...]), (vs_ref, ones), (kw_ref, None),
                                      (vw_ref, ones))):
        for g in range(NSA_KV_GROUPS):
            c0 = _C_KV + i * NSA_KV + g * HEAD_DIM
            val = z[:, c0:c0 + HEAD_DIM].astype(BF16)
            ref[0, g] = val if extra is None else jnp.concatenate([val, extra], axis=1)
    for hd in range(FOX_HEADS):
        c0 = _C_FOX + hd * HEAD_DIM
        qf_ref[0, hd] = (z[:, c0:c0 + HEAD_DIM] * (HEAD_DIM ** -0.5)).astype(BF16)
        kf_ref[0, hd] = z[:, c0 + FOX_W:c0 + FOX_W + HEAD_DIM].astype(BF16)
        v = z[:, c0 + 2 * FOX_W:c0 + 2 * FOX_W + HEAD_DIM].astype(BF16)
        vf_ref[0, hd] = jnp.concatenate([v, ones], axis=1)
    for g in range(NSA_KV_GROUPS):
        c0 = _C_SMALL + g * GATE_PAD
        gate_ref[0, g] = z[:, c0:c0 + GATE_PAD]
    c0 = _C_SMALL + NSA_KV_GROUPS * GATE_PAD
    fl_ref[0] = z[:, c0:c0 + FOX_HEADS]


def _pack_w_in(w_in):
    cols = np.cumsum((0, NSA_Q, NSA_KV, NSA_KV, NSA_KV, NSA_KV, NSA_KV, NSA_KV, 3 * NSA_HEADS,
                      FOX_W, FOX_W, FOX_W, FOX_HEADS, D_MODEL, D_MODEL))
    g0 = int(cols[7])
    gate_cols = []
    for g in range(NSA_KV_GROUPS):
        for br in range(3):
            c0 = g0 + br * NSA_HEADS + g * NSA_HPG
            gate_cols.append(w_in[:, c0:c0 + NSA_HPG])
        gate_cols.append(jnp.zeros((D_MODEL, GATE_PAD - 3 * NSA_HPG), w_in.dtype))
    small = jnp.concatenate(gate_cols + [w_in[:, int(cols[11]):int(cols[12])]], axis=1)
    small = jnp.pad(small, ((0, 0), (0, LANES - small.shape[1])))
    main = jnp.concatenate([w_in[:, :g0], w_in[:, int(cols[8]):int(cols[11])], small], axis=1)
    w_a = w_in[:, int(cols[12]):int(cols[13])]
    w_b = w_in[:, int(cols[13]):int(cols[14])]
    return main.astype(BF16), w_a.astype(BF16), w_b.astype(BF16)


def _inproj(x, g, w_main):
    b, s, _ = x.shape
    grid = (b, s // TOK_TILE)
    tok = lambda c: pl.BlockSpec((1, TOK_TILE, c), lambda i, j: (i, j, 0))
    heads = lambda nh, w=HEAD_DIM: pl.BlockSpec((1, nh, TOK_TILE, w), lambda i, j: (i, 0, j, 0))
    sds = jax.ShapeDtypeStruct
    kv = sds((b, NSA_KV_GROUPS, s, HEAD_DIM), BF16)
    kv2 = sds((b, NSA_KV_GROUPS, s, 2 * HEAD_DIM), BF16)
    fx = sds((b, FOX_HEADS, s, HEAD_DIM), BF16)
    out_shape = (sds((b, s, NSA_Q), BF16), sds((b, s, NSA_KV), BF16), sds((b, s, NSA_KV), BF16),
                 kv2, kv2, kv, kv2, fx, fx, sds((b, FOX_HEADS, s, 2 * HEAD_DIM), BF16),
                 sds((b, NSA_KV_GROUPS, s, GATE_PAD), F32), sds((b, s, FOX_HEADS), F32))
    out_specs = (tok(NSA_Q), tok(NSA_KV), tok(NSA_KV),
                 heads(NSA_KV_GROUPS, 2 * HEAD_DIM), heads(NSA_KV_GROUPS, 2 * HEAD_DIM),
                 heads(NSA_KV_GROUPS), heads(NSA_KV_GROUPS, 2 * HEAD_DIM),
                 heads(FOX_HEADS), heads(FOX_HEADS), heads(FOX_HEADS, 2 * HEAD_DIM),
                 pl.BlockSpec((1, NSA_KV_GROUPS, TOK_TILE, GATE_PAD), lambda i, j: (i, 0, j, 0)),
                 tok(FOX_HEADS))
    key_blk = np.arange(s) // SEL_BLOCK
    blk_onehot = jnp.asarray(key_blk[:, None] == np.arange(HEAD_DIM)[None, :], BF16)
    return pl.pallas_call(
        _inproj_body,
        grid=grid,
        in_specs=[tok(D_MODEL), _resident((1, D_MODEL)), _resident((D_MODEL, _C_END)),
                  pl.BlockSpec((TOK_TILE, HEAD_DIM), lambda i, j: (j, 0))],
        out_specs=out_specs,
        out_shape=out_shape,
        compiler_params=_params(("parallel", "parallel")),
        name="inproj",
    )(x, g.reshape(1, D_MODEL), w_main, blk_onehot)


N_CHUNKS_PAD = 128
CHUNK_W = CMP_STRIDE * NSA_KV


def _compress_body(xk_ref, xv_ref, pk_ref, pv_ref, wk1_ref, wv1_ref, wk2_ref, wv2_ref, ok_ref, ov_ref):
    for x_ref, p_ref, w1_ref, w2_ref, o_ref in ((xk_ref, pk_ref, wk1_ref, wk2_ref, ok_ref),
                                                (xv_ref, pv_ref, wv1_ref, wv2_ref, ov_ref)):
        x = x_ref[0]
        a0 = _dot(x, w1_ref[0])
        a1 = _dot(x, w1_ref[1])
        c = _dot(p_ref[0], w1_ref[0]) + _dot(p_ref[1], w1_ref[1])
        pre = a0 + pltpu.roll(a1, N_CHUNKS_PAD - 1, axis=0) + c[0:1, :]
        hid = jax.nn.gelu(pre).astype(BF16)
        out = _dot(hid, w2_ref[...])
        for g in range(NSA_KV_GROUPS):
            o_ref[0, g] = out[:, g * HEAD_DIM:(g + 1) * HEAD_DIM].astype(BF16)


def _pack_compress(pos, w1, w2):
    r = CMP_BLOCK // CMP_STRIDE
    assert NSA_KV_GROUPS == 2
    w1r = w1.astype(BF16).reshape(r, CMP_STRIDE, HEAD_DIM, CMP_HIDDEN)
    z1 = jnp.zeros_like(w1r)
    w1big = jnp.stack([jnp.concatenate([w1r, z1], axis=-1), jnp.concatenate([z1, w1r], axis=-1)], axis=2)
    w1big = w1big.reshape(r, CHUNK_W, NSA_KV_GROUPS * CMP_HIDDEN)
    z2 = jnp.zeros_like(w2)
    w2big = jnp.concatenate([jnp.concatenate([w2, z2], axis=1), jnp.concatenate([z2, w2], axis=1)], axis=0)
    p = pos.reshape(r, CMP_STRIDE, 1, HEAD_DIM)
    p = jnp.broadcast_to(p, (r, CMP_STRIDE, NSA_KV_GROUPS, HEAD_DIM)).reshape(r, 1, CHUNK_W)
    p = jnp.broadcast_to(p, (r, 16, CHUNK_W))
    return p.astype(BF16), w1big.astype(BF16), w2big.astype(BF16)


def _compress(kc, vc, pos_k, pos_v, k_w1, k_w2, v_w1, v_w2):
    b, s, _ = kc.shape
    xk = kc.reshape(b, N_CHUNKS_PAD, CHUNK_W)
    xv = vc.reshape(b, N_CHUNKS_PAD, CHUNK_W)
    pk, wk1, wk2 = _pack_compress(pos_k, k_w1, k_w2)
    pv, wv1, wv2 = _pack_compress(pos_v, v_w1, v_w2)
    xs = pl.BlockSpec((1, N_CHUNKS_PAD, CHUNK_W), lambda i: (i, 0, 0))
    os_ = pl.BlockSpec((1, NSA_KV_GROUPS, N_CHUNKS_PAD, HEAD_DIM), lambda i: (i, 0, 0, 0))
    osd = jax.ShapeDtypeStruct((b, NSA_KV_GROUPS, N_CHUNKS_PAD, HEAD_DIM), BF16)
    return pl.pallas_call(
        _compress_body,
        grid=(b,),
        in_specs=[xs, xs, _resident(pk.shape), _resident(pv.shape), _resident(wk1.shape),
                  _resident(wv1.shape), _resident(wk2.shape), _resident(wv2.shape)],
        out_specs=(os_, os_),
        out_shape=(osd, osd),
        compiler_params=_params(("parallel",)),
        name="compress",
    )(xk, xv, pk, pv, wk1, wv1, wk2, wv2)


def _bias_lookup(tbl_ref, head, dist):
    acc = jnp.full(dist.shape, tbl_ref[0, head], F32)
    for k, thr in enumerate(BUCKET_THRESHOLDS):
        acc = jnp.where(dist >= thr, tbl_ref[k + 1, head], acc)
    return acc - tbl_ref[NUM_BUCKETS - 1, head]


def _relbias_body(tbl_ref, toep_ref, bc_ref):
    head = pl.program_id(0)
    t = ATT_TILE
    i = lax.broadcasted_iota(jnp.int32, (t, 2 * t), 0)
    u = lax.broadcasted_iota(jnp.int32, (t, 2 * t), 1)
    d = i - u + t
    toep_ref[0] = jnp.where(d >= 0, _bias_lookup(tbl_ref, head, d), NEG_INF)
    n_tiles = bc_ref.shape[1] // t
    per_tile = t // CMP_STRIDE
    off = per_tile * (n_tiles - 1)
    width = 2 * N_CHUNKS_PAD
    assert off + N_CHUNKS_PAD <= width
    i = lax.broadcasted_iota(jnp.int32, (t, width), 0)
    u = lax.broadcasted_iota(jnp.int32, (t, width), 1)
    dc = i - ((u - off) * CMP_STRIDE + CMP_BLOCK - 1)
    strip = jnp.where(dc >= 0, _bias_lookup(tbl_ref, head, dc), NEG_INF)
    for q in range(n_tiles):
        lo = off - q * per_tile
        bc_ref[0, q * t:(q + 1) * t, :] = strip[:, lo:lo + N_CHUNKS_PAD]


def _relbias(tbl, s):
    return pl.pallas_call(
        _relbias_body,
        grid=(NSA_HEADS,),
        in_specs=[pl.BlockSpec(memory_space=pltpu.SMEM)],
        out_specs=(pl.BlockSpec((1, ATT_TILE, 2 * ATT_TILE), lambda h: (h, 0, 0)),
                   pl.BlockSpec((1, s, N_CHUNKS_PAD), lambda h: (h, 0, 0))),
        out_shape=(jax.ShapeDtypeStruct((NSA_HEADS, ATT_TILE, 2 * ATT_TILE), F32),
                   jax.ShapeDtypeStruct((NSA_HEADS, s, N_CHUNKS_PAD), F32)),
        compiler_params=_params(("parallel",)),
        name="relbias",
    )(tbl)


def _lane_fold(x, op):
    return functools.reduce(op, [x[:, i * LANES:(i + 1) * LANES] for i in range(x.shape[1] // LANES)])


def _lane_tile(x, width):
    return jnp.concatenate([x] * (width // LANES), axis=1)


def _row_max_tile(rmax):
    return jnp.broadcast_to(jnp.max(rmax, axis=-1, keepdims=True), rmax.shape)


def _normalize(acc):
    return (acc / pltpu.roll(acc, HEAD_DIM, axis=1))[:, :HEAD_DIM]


def _split3(x):
    hi = x.astype(BF16)
    r = x - hi.astype(F32)
    mid = r.astype(BF16)
    lo = (r - mid.astype(F32)).astype(BF16)
    return hi, mid, lo


def _nsa_body(q_ref, kc_ref, vc_ref, ks_ref, vs_ref, kw_ref, vw_ref, gate_a_ref, gate_b_ref, toep_ref,
              bc_ref, ov_ref, o_ref, s_ref, near_ref, mb_ref, acc_ref, part_ref, *, n_tiles):
    t = ATT_TILE
    j4 = NSA_HPG
    rows = j4 * t
    groups = range(NSA_KV_GROUPS)
    step = pl.program_id(1)
    n_sel = n_tiles * (t // SEL_BLOCK)

    def heads(x):
        return x.reshape(j4, t, x.shape[-1])

    def flat(x):
        return x.reshape(rows, x.shape[-1])

    def key_tile(ref, g, n):
        return ref[0, g, pl.ds(pl.multiple_of(n * t, t), t), :]

    def gated(gate_ref, g, branch, o):
        gate = jax.nn.sigmoid(gate_ref[0, g])
        return [gate[:, branch * j4 + j:branch * j4 + j + 1] * o[j * t:(j + 1) * t] for j in range(j4)]

    def probs(par, g, s):
        return jnp.exp(s - _lane_tile(mb_ref[par, g], t)).astype(BF16)

    def open_tile(m, par):
        m1 = jnp.maximum(m - 1, 0)
        m2 = jnp.maximum(m - 2, 0)
        no_prev = jnp.where(m >= 1, 0.0, NEG_INF)
        qs = []
        for g in groups:
            qt = q_ref[0, :, g * j4 * HEAD_DIM:(g + 1) * j4 * HEAD_DIM]
            qs.append(jnp.concatenate([qt[:, j * HEAD_DIM:(j + 1) * HEAD_DIM] for j in range(j4)], axis=0))

        def biases(g):
            toep = toep_ref[g * j4:(g + 1) * j4]
            return toep[:, :, t:], toep[:, :, :t] + no_prev

        cmp_out = [_nsa_compressed(qs[g], kc_ref[0, g], vc_ref[0, g], bc_ref[g * j4:(g + 1) * j4],
                                   ov_ref[...], m, n_sel) for g in groups]
        part = []
        for g in groups:
            diag_bias, prev_bias = biases(g)
            ii = lax.broadcasted_iota(jnp.int32, (t, t), 0)
            jj = lax.broadcasted_iota(jnp.int32, (t, t), 1)
            tri = jnp.where(jj > ii, 0.0, NEG_INF) + jnp.where(m >= 2, 0.0, NEG_INF)
            w0 = flat(heads(_dot_nt_pair(qs[g], key_tile(kw_ref, g, m))) + diag_bias)
            w1 = flat(heads(_dot_nt_pair(qs[g], key_tile(kw_ref, g, m1))) + prev_bias)
            w2 = flat(heads(_dot_nt_pair(qs[g], key_tile(kw_ref, g, m2))) + tri[None])
            wmax = _lane_tile(_row_max_tile(functools.reduce(
                jnp.maximum, [_lane_fold(w, jnp.maximum) for w in (w0, w1, w2)])), t)
            acc = sum(_dot_pair(jnp.exp(w - wmax).astype(BF16), key_tile(vw_ref, g, n))
                      for w, n in ((w0, m), (w1, m1), (w2, m2)))
            part += [a + b for a, b in zip(gated(gate_a_ref, g, 0, cmp_out[g][0]),
                                           gated(gate_a_ref, g, 2, _normalize(acc)))]
        part_ref[par] = jnp.concatenate(part, axis=1)

        q_aug = []
        for g in groups:
            diag_bias, prev_bias = biases(g)
            q_aug.append(jnp.concatenate([qs[g], jnp.concatenate([cmp_out[g][1]] * j4, axis=0)], axis=1))
            s0 = flat(heads(_dot_nt_pair(q_aug[g], key_tile(ks_ref, g, m))) + diag_bias)
            s1 = flat(heads(_dot_nt_pair(q_aug[g], key_tile(ks_ref, g, m1))) + prev_bias)
            s2 = _dot_nt_pair(q_aug[g], key_tile(ks_ref, g, m2))
            near_ref[par, g, 0] = s0
            near_ref[par, g, 1] = s1
            s_ref[g, m2] = s2
            far = jnp.where(m >= 2, _lane_fold(s2, jnp.maximum), NEG_INF)
            mb_ref[par, g] = jnp.maximum(jnp.maximum(_lane_fold(s0, jnp.maximum),
                                                     _lane_fold(s1, jnp.maximum)), far)
        return q_aug

    def open_far(par, q_aug, n):
        for g in groups:
            s = _dot_nt_pair(q_aug[g], key_tile(ks_ref, g, n))
            s_ref[g, n] = s
            mb_ref[par, g] = jnp.maximum(mb_ref[par, g], _lane_fold(s, jnp.maximum))

    def close_tile(m, par):
        m1 = jnp.maximum(m - 1, 0)
        for g in groups:
            mb_ref[par, g] = _row_max_tile(mb_ref[par, g])
            acc_ref[par, g] = (_dot_pair(probs(par, g, near_ref[par, g, 0]), key_tile(vs_ref, g, m))
                               + _dot_pair(probs(par, g, near_ref[par, g, 1]), key_tile(vs_ref, g, m1)))

    def close_far(par, n):
        for g in groups:
            acc_ref[par, g] += _dot_pair(probs(par, g, s_ref[g, n]), key_tile(vs_ref, g, n))

    def finish(par):
        outs = []
        for g in groups:
            outs += gated(gate_b_ref, g, 1, _normalize(acc_ref[par, g]))
        o_ref[0] = (part_ref[par] + jnp.concatenate(outs, axis=1)).astype(BF16)

    @pl.when(step == 0)
    def _():
        open_tile(step, 0)

    @pl.when(step == 1)
    def _():
        open_tile(step, 1)
        close_tile(step - 1, 0)

    for par in (0, 1):
        @pl.when(jnp.logical_and(jnp.logical_and(step >= 2, step < n_tiles), step % 2 == par))
        def _(par=par):
            finish(par)
            q_aug = open_tile(step, par)
            close_tile(step - 1, 1 - par)

            @pl.loop(0, step - 2)
            def _(n):
                close_far(1 - par, n)
                open_far(par, q_aug, n)

    last = (n_tiles - 1) % 2

    @pl.when(step == n_tiles)
    def _():
        finish(1 - last)
        close_tile(step - 1, last)

        @pl.loop(0, step - 2)
        def _(n):
            close_far(last, n)

    @pl.when(step == n_tiles + 1)
    def _():
        finish(last)


def _nsa_compressed(q, kc, vc, bc, ov, m, n_sel):
    t = ATT_TILE
    j4 = NSA_HPG

    def heads(x):
        return x.reshape(j4, t, x.shape[-1])

    def flat(x):
        return x.reshape(j4 * t, x.shape[-1])

    sc = heads(_dot_nt_pair(q, kc)) + bc
    valid = bc > 0.5 * NEG_INF
    e = jnp.where(valid, jnp.exp(sc - jnp.max(sc, axis=-1, keepdims=True)), 0.0)
    l = jnp.sum(e, axis=-1, keepdims=True)
    pc = jnp.where(l > 0.0, e / l, 0.0)
    o_c = _dot_pair(flat(pc).astype(BF16), vc)
    psum = pc[0] + pc[1] + pc[2] + pc[3]
    imp = sum(_dot(part, ov) for part in _split3(psum)).T[:n_sel]
    blk = lax.broadcasted_iota(jnp.int32, (n_sel, t), 0)
    tpos = m * t + lax.broadcasted_iota(jnp.int32, (n_sel, t), 1)
    cur = tpos // SEL_BLOCK
    bonus = jnp.where(blk == 0, FORCE_BONUS,
                      jnp.where(blk == cur, FORCE_BONUS, jnp.where(blk == cur - 1, FORCE_BONUS, 0.0)))
    imp = jnp.where(blk * SEL_BLOCK <= tpos, imp + bonus, NEG_INF)
    rank = jnp.zeros((n_sel, t), F32)
    for i in range(n_sel):
        row = imp[i:i + 1, :]
        before = jnp.where(blk > i, 1.0, 0.0)
        rank = rank + jnp.where(row > imp, 1.0, jnp.where(row == imp, before, 0.0))
    selb = jnp.where(rank < float(SEL_TOP_N), 0.0, NEG_INF)
    selb = jnp.concatenate([selb, jnp.zeros((LANES - n_sel, t), F32)], axis=0).T
    return o_c, selb[:, :HEAD_DIM].astype(BF16)


def _overlap_table(s):
    n_sel = s // SEL_BLOCK
    nc = N_CHUNKS_PAD
    c0 = np.arange(nc)[:, None] * CMP_STRIDE
    s0 = np.arange(n_sel)[None, :] * SEL_BLOCK
    ov = np.clip(np.minimum(c0 + CMP_BLOCK, s0 + SEL_BLOCK) - np.maximum(c0, s0), 0, None) / CMP_STRIDE
    ov[nc - 1] = 0.0
    return jnp.asarray(np.pad(ov, ((0, 0), (0, LANES - n_sel))), BF16)


def _nsa(qn, kc, vc, ks, vs, kw, vw, gates, toep, bc):
    b, s, _ = qn.shape
    t = ATT_TILE
    assert s // SEL_BLOCK <= HEAD_DIM
    ov = _overlap_table(s)
    rows = NSA_HPG * t
    ng = NSA_KV_GROUPS
    nq = s // t
    assert nq >= 2
    seq = lambda n, w=HEAD_DIM: pl.BlockSpec((1, ng, n, w), lambda i, j: (i, 0, 0, 0))
    opened = lambda j: jnp.minimum(j, nq - 1)
    closed = lambda j: jnp.maximum(j - 2, 0)
    return pl.pallas_call(
        functools.partial(_nsa_body, n_tiles=nq),
        grid=(b, nq + 2),
        in_specs=[pl.BlockSpec((1, t, NSA_Q), lambda i, j: (i, opened(j), 0)),
                  seq(N_CHUNKS_PAD), seq(N_CHUNKS_PAD),
                  seq(s, 2 * HEAD_DIM), seq(s, 2 * HEAD_DIM), seq(s), seq(s, 2 * HEAD_DIM),
                  pl.BlockSpec((1, ng, t, GATE_PAD), lambda i, j: (i, 0, opened(j), 0)),
                  pl.BlockSpec((1, ng, t, GATE_PAD), lambda i, j: (i, 0, closed(j), 0)),
                  _resident((NSA_HEADS, t, 2 * t)),
                  pl.BlockSpec((NSA_HEADS, t, N_CHUNKS_PAD), lambda i, j: (0, opened(j), 0)),
                  _resident(ov.shape)],
        out_specs=pl.BlockSpec((1, t, NSA_Q), lambda i, j: (i, closed(j), 0)),
        out_shape=jax.ShapeDtypeStruct((b, s, NSA_Q), BF16),
        scratch_shapes=[pltpu.VMEM((ng, nq - 2, rows, t), F32),
                        pltpu.VMEM((2, ng, 2, rows, t), F32),
                        pltpu.VMEM((2, ng, rows, LANES), F32),
                        pltpu.VMEM((2, ng, rows, 2 * HEAD_DIM), F32),
                        pltpu.VMEM((2, t, NSA_Q), F32)],
        compiler_params=_params(("parallel", "arbitrary")),
        name="nsa",
    )(qn, kc, vc, ks, vs, kw, vw, gates, gates, toep, bc, ov)


def _fcum_body(f_ref, b_ref, o_ref):
    z = f_ref[0] + b_ref[...]
    x = -(jnp.maximum(-z, 0.0) + jnp.log1p(jnp.exp(-jnp.abs(z))))
    n = x.shape[-1]
    lane = lax.broadcasted_iota(jnp.int32, x.shape, 1)
    sh = 1
    while sh < n:
        x = x + jnp.where(lane >= sh, pltpu.roll(x, sh, axis=1), 0.0)
        sh *= 2
    o_ref[0] = x


def _fcum(f_t, b_forget):
    b, h, s = f_t.shape
    blk = pl.BlockSpec((1, h, s), lambda i: (i, 0, 0))
    return pl.pallas_call(
        _fcum_body,
        grid=(b,),
        in_specs=[blk, _resident((h, 1))],
        out_specs=blk,
        out_shape=jax.ShapeDtypeStruct((b, h, s), F32),
        compiler_params=_params(("parallel",)),
        name="fcum",
    )(f_t, b_forget.reshape(h, 1))


FOX_PAIR = 2
FOX_TILE = 512


def _causal_bias(t):
    i = np.arange(t)
    return jnp.asarray(np.where(i[None, :] <= i[:, None], 0.0, NEG_INF), F32)


def _fox_body(q_ref, k_ref, v_ref, c_ref, cm_ref, o_ref, s_ref, mb_ref, acc_ref):
    t = FOX_TILE
    heads = range(FOX_PAIR)
    nq = q_ref.shape[2] // t

    def keys(n):
        return pl.ds(pl.multiple_of(n * t, t), t)

    def scores(m, hh, n):
        q = q_ref[0, hh, m * t:(m + 1) * t, :]
        return _dot_nt(q, k_ref[0, hh, keys(n), :]) - c_ref[0, hh, :, keys(n)]

    def scores_diag(m):
        for hh in heads:
            s = scores(m, hh, m) + cm_ref[...]
            s_ref[m % 2, hh, m] = s
            mb_ref[m % 2, hh] = _lane_fold(s, jnp.maximum)

    def scores_far(m, n):
        for hh in heads:
            s = scores(m, hh, n)
            s_ref[m % 2, hh, n] = s
            mb_ref[m % 2, hh] = jnp.maximum(mb_ref[m % 2, hh], _lane_fold(s, jnp.maximum))

    def finish_max(m):
        for hh in heads:
            mb_ref[m % 2, hh] = _row_max_tile(mb_ref[m % 2, hh])

    def weigh(m, n):
        for hh in heads:
            p = jnp.exp(s_ref[m % 2, hh, n] - _lane_tile(mb_ref[m % 2, hh], t)).astype(BF16)
            acc_ref[hh] += _dot(p, v_ref[0, hh, keys(n), :])

    scores_diag(0)
    finish_max(0)
    for m in range(nq):
        ahead = m + 1 < nq
        if ahead:
            scores_diag(m + 1)
        for hh in heads:
            acc_ref[hh] = jnp.zeros((t, 2 * HEAD_DIM), F32)

        @pl.loop(0, m + 1)
        def _(n, m=m, ahead=ahead):
            weigh(m, n)
            if ahead:
                scores_far(m + 1, n)

        if ahead:
            finish_max(m + 1)
        o_ref[0, m * t:(m + 1) * t, :] = jnp.concatenate(
            [_normalize(acc_ref[hh]) for hh in heads], axis=1).astype(BF16)


def _fox(qf, kf, vf, c):
    b, h, s, _ = qf.shape
    t = FOX_TILE
    seq = lambda w: pl.BlockSpec((1, FOX_PAIR, s, w), lambda i, p: (i, p, 0, 0))
    return pl.pallas_call(
        _fox_body,
        grid=(b, h // FOX_PAIR),
        in_specs=[seq(HEAD_DIM), seq(HEAD_DIM), seq(2 * HEAD_DIM),
                  pl.BlockSpec((1, FOX_PAIR, 1, s), lambda i, p: (i, p, 0, 0)),
                  _resident((t, t))],
        out_specs=pl.BlockSpec((1, s, FOX_PAIR * HEAD_DIM), lambda i, p: (i, 0, p)),
        out_shape=jax.ShapeDtypeStruct((b, s, h * HEAD_DIM), BF16),
        scratch_shapes=[pltpu.VMEM((2, FOX_PAIR, s // t, t, t), F32),
                        pltpu.VMEM((2, FOX_PAIR, t, LANES), F32),
                        pltpu.VMEM((FOX_PAIR, t, 2 * HEAD_DIM), F32)],
        compiler_params=_params(("parallel", "parallel")),
        name="fox",
    )(qf, kf, vf, c.reshape(b, h, 1, s), _causal_bias(t))


def _mixout_body(x_ref, g_ref, on_ref, of_ref, wa_ref, wb_ref, wun_ref, wuf_ref, wo_ref, o_ref):
    x = x_ref[...]
    h = _rms(x, g_ref[...]).astype(BF16)
    y = (jax.nn.sigmoid(_dot(h, wa_ref[...])) * _dot(on_ref[...], wun_ref[...])
         + jax.nn.sigmoid(_dot(h, wb_ref[...])) * _dot(of_ref[...], wuf_ref[...]))
    o_ref[...] = x + _dot(y.astype(BF16), wo_ref[...])


def _mixout(x2d, g, o_nsa, o_fox, w_a, w_b, w_un, w_uf, w_o):
    n = x2d.shape[0]
    tok = lambda c: pl.BlockSpec((TOK_TILE, c), lambda i: (i, 0))
    return pl.pallas_call(
        _mixout_body,
        grid=(n // TOK_TILE,),
        in_specs=[tok(D_MODEL), _resident((1, D_MODEL)), tok(NSA_Q), tok(FOX_W),
                  _resident(w_a.shape), _resident(w_b.shape), _resident(w_un.shape),
                  _resident(w_uf.shape), _resident(w_o.shape)],
        out_specs=tok(D_MODEL),
        out_shape=jax.ShapeDtypeStruct((n, D_MODEL), F32),
        compiler_params=_params(("parallel",)),
        name="mixout",
    )(x2d, g.reshape(1, D_MODEL), o_nsa, o_fox, w_a, w_b, w_un, w_uf, w_o)


def _memkv_body(m_ref, g_ref, w_ref, k_ref, v_ref):
    h = _rms(m_ref[0], g_ref[...]).astype(BF16)
    z = _dot(h, w_ref[...])
    k_ref[0] = z[:, :D_MODEL].astype(BF16)
    v_ref[0] = z[:, D_MODEL:].astype(BF16)


def _memkv(mem, g, w_kv):
    b, ml, _ = mem.shape
    blk = pl.BlockSpec((1, ml, D_MODEL), lambda i: (i, 0, 0))
    sd = jax.ShapeDtypeStruct((b, ml, D_MODEL), BF16)
    return pl.pallas_call(
        _memkv_body,
        grid=(b,),
        in_specs=[blk, _resident((1, D_MODEL)), _resident(w_kv.shape)],
        out_specs=(blk, blk),
        out_shape=(sd, sd),
        compiler_params=_params(("parallel",)),
        name="memkv",
    )(mem, g.reshape(1, D_MODEL), w_kv)


def _memattn_body(x_ref, g_ref, k_ref, v_ref, wq_ref, wo_ref, o_ref):
    x = x_ref[0]
    h = _rms(x, g_ref[...]).astype(BF16)
    q = (_dot(h, wq_ref[...]) * (MEM_HEAD_DIM ** -0.5)).astype(BF16)
    outs = []
    for hd in range(MEM_HEADS):
        cols = slice(hd * MEM_HEAD_DIM, (hd + 1) * MEM_HEAD_DIM)
        s = _dot_nt(q[:, cols], k_ref[0, :, cols])
        e = jnp.exp(s - jnp.max(s, axis=-1, keepdims=True))
        p = e / jnp.sum(e, axis=-1, keepdims=True)
        outs.append(_dot(p.astype(BF16), v_ref[0, :, cols]))
    o = jnp.concatenate(outs, axis=1).astype(BF16)
    o_ref[0] = x + _dot(o, wo_ref[...])


def _memattn(x, g, k, v, w_q, w_o):
    b, s, _ = x.shape
    ml = k.shape[1]
    tok = pl.BlockSpec((1, TOK_TILE, D_MODEL), lambda i, j: (i, j, 0))
    kvb = pl.BlockSpec((1, ml, D_MODEL), lambda i, j: (i, 0, 0))
    return pl.pallas_call(
        _memattn_body,
        grid=(b, s // TOK_TILE),
        in_specs=[tok, _resident((1, D_MODEL)), kvb, kvb, _resident(w_q.shape), _resident(w_o.shape)],
        out_specs=tok,
        out_shape=jax.ShapeDtypeStruct((b, s, D_MODEL), F32),
        compiler_params=_params(("parallel", "parallel")),
        name="memattn",
    )(x, g.reshape(1, D_MODEL), k, v, w_q, w_o)


def kernel(x, mem, rel_bias_table, ffn1_norm, ffn1_w_gate, ffn1_w_up, ffn1_w_down, mix_norm, mix_w_in, mix_b_forget, cmp_pos_k, cmp_pos_v, cmp_k_w1, cmp_k_w2, cmp_v_w1, cmp_v_w2, w_up_nsa, w_up_fox, mix_w_out, mem_q_norm, mem_kv_norm, mem_w_q, mem_w_kv, mem_w_o, ffn2_norm, ffn2_w_gate, ffn2_w_up, ffn2_w_down, final_norm):
    b, s, d = x.shape
    depth = ffn1_norm.shape[0]
    bf = lambda w: w.astype(BF16)
    toep, bias_c = _relbias(rel_bias_table, s)
    x = x.reshape(b * s, d)
    for l in range(depth):
        last = l == depth - 1
        x = _ffn(x, ffn1_norm[l], bf(ffn1_w_gate[l]), bf(ffn1_w_up[l]), bf(ffn1_w_down[l]))

        w_main, w_a, w_b = _pack_w_in(mix_w_in[l])
        qn, kc, vc, ks, vs, kw, vw, qf, kf, vf, gates, flog = _inproj(x.reshape(b, s, d), mix_norm[l], w_main)
        kcc, vcc = _compress(kc, vc, cmp_pos_k[l], cmp_pos_v[l], cmp_k_w1[l], cmp_k_w2[l],
                             cmp_v_w1[l], cmp_v_w2[l])
        o_nsa = _nsa(qn, kcc, vcc, ks, vs, kw, vw, gates, toep, bias_c)
        c = _fcum(flog.transpose(0, 2, 1), mix_b_forget[l])
        o_fox = _fox(qf, kf, vf, c)
        x = _mixout(x, mix_norm[l], o_nsa.reshape(b * s, NSA_Q), o_fox.reshape(b * s, FOX_W),
                    w_a, w_b, bf(w_up_nsa[l]), bf(w_up_fox[l]), bf(mix_w_out[l]))

        mk, mv = _memkv(mem, mem_kv_norm[l], bf(mem_w_kv[l]))
        x = _memattn(x.reshape(b, s, d), mem_q_norm[l], mk, mv, bf(mem_w_q[l]), bf(mem_w_o[l]))
        x = _ffn(x.reshape(b * s, d), ffn2_norm[l], bf(ffn2_w_gate[l]), bf(ffn2_w_up[l]),
                 bf(ffn2_w_down[l]), final_g=final_norm if last else None)
    return x.reshape(b, s, d)
```

```python
import functools
import math

import numpy as np
import jax
import jax.numpy as jnp
from jax import lax
from jax.experimental import pallas as pl
from jax.experimental.pallas import tpu as pltpu

D_MODEL = 1024
D_FF = 2816
HEAD_DIM = 64
NSA_HEADS = 8
NSA_KV_GROUPS = 2
NSA_HPG = NSA_HEADS // NSA_KV_GROUPS
CMP_BLOCK = 32
CMP_STRIDE = 16
CMP_HIDDEN = 256
SEL_BLOCK = 64
SEL_TOP_N = 16
WINDOW = 512
FOX_HEADS = 8
MEM_HEADS = 4
MEM_HEAD_DIM = D_MODEL // MEM_HEADS
NUM_BUCKETS = 32
MAX_DISTANCE = 128
RMS_EPS = 1e-6
NEG_INF = -1e30
FORCE_BONUS = 1e4

NSA_Q = NSA_HEADS * HEAD_DIM
NSA_KV = NSA_KV_GROUPS * HEAD_DIM
FOX_W = FOX_HEADS * HEAD_DIM

LANES = 128
VMEM_LIMIT = 56 * 1024 * 1024

BF16 = jnp.bfloat16
F32 = jnp.float32

ATT_TILE = 256
TOK_TILE = 1024
FFN_TILE = 1024
FF_CHUNK = 256


def _bucket_thresholds():
    n = np.arange(0, 4 * MAX_DISTANCE)
    exact = NUM_BUCKETS // 2
    large = exact + (np.log(np.maximum(n, 1) / exact) / math.log(MAX_DISTANCE / exact)
                     * (NUM_BUCKETS - exact)).astype(np.int64)
    bucket = np.where(n < exact, n, np.minimum(large, NUM_BUCKETS - 1))
    assert np.all(np.diff(bucket) >= 0)
    return [int(np.argmax(bucket >= k)) for k in range(1, NUM_BUCKETS)]


BUCKET_THRESHOLDS = _bucket_thresholds()


def _dot(a, b):
    return jnp.dot(a, b, preferred_element_type=F32)


def _dot_nt(a, b):
    return lax.dot_general(a, b, (((1,), (1,)), ((), ())), preferred_element_type=F32)


def _row_halves(fn, a, b):
    h = a.shape[0] // 2
    return jnp.concatenate([fn(a[:h], b), fn(a[h:], b)], axis=0)


def _dot_pair(a, b):
    return _row_halves(_dot, a, b)


def _dot_nt_pair(a, b):
    return _row_halves(_dot_nt, a, b)


def _rms(x, g):
    return x * lax.rsqrt(jnp.mean(x * x, axis=-1, keepdims=True) + RMS_EPS) * g


def _resident(shape):
    nd = len(shape)
    return pl.BlockSpec(shape, lambda *_: (0,) * nd, pipeline_mode=pl.Buffered(1))


def _params(sem):
    return pltpu.CompilerParams(dimension_semantics=sem, vmem_limit_bytes=VMEM_LIMIT)


def _ffn_body(x_ref, g_ref, wg_ref, wu_ref, wd_ref, *rest, final):
    if final:
        fg_ref, o_ref = rest
    else:
        (o_ref,) = rest
    x = x_ref[...]
    h = _rms(x, g_ref[...]).astype(BF16)
    acc = jnp.zeros(x.shape, F32)
    for c in range(D_FF // FF_CHUNK):
        sl = slice(c * FF_CHUNK, (c + 1) * FF_CHUNK)
        a = _dot(h, wg_ref[:, sl])
        b = _dot(h, wu_ref[:, sl])
        t = (a * jax.nn.sigmoid(a)) * b
        acc = acc + _dot(t.astype(BF16), wd_ref[sl, :])
    y = x + 0.5 * acc
    if final:
        y = _rms(y, fg_ref[...])
    o_ref[...] = y


def _ffn(x2d, g, wg, wu, wd, final_g=None):
    n = x2d.shape[0]
    final = final_g is not None
    tok = pl.BlockSpec((FFN_TILE, D_MODEL), lambda i: (i, 0))
    in_specs = [tok, _resident((1, D_MODEL)), _resident((D_MODEL, D_FF)),
                _resident((D_MODEL, D_FF)), _resident((D_FF, D_MODEL))]
    args = [x2d, g.reshape(1, D_MODEL), wg, wu, wd]
    if final:
        in_specs.append(_resident((1, D_MODEL)))
        args.append(final_g.reshape(1, D_MODEL))
    return pl.pallas_call(
        functools.partial(_ffn_body, final=final),
        grid=(n // FFN_TILE,),
        in_specs=in_specs,
        out_specs=tok,
        out_shape=jax.ShapeDtypeStruct((n, D_MODEL), F32),
        compiler_params=_params(("parallel",)),
        name="ffn_final" if final else "ffn",
    )(*args)


_C_CMP = NSA_Q
_C_KV = _C_CMP + 2 * NSA_KV
_W_NSA = _C_KV + 4 * NSA_KV
GATE_PAD = 16
_C_FLOG = NSA_KV_GROUPS * GATE_PAD
N_CHUNKS_PAD = 128
CHUNK_W = CMP_STRIDE * NSA_KV


def _inproj_body(x_ref, g_ref, wn_ref, wf_ref, ws_ref, blk_ref, qn_ref, kc_ref, vc_ref, ks_ref, vs_ref,
                 kw_ref, vw_ref, qf_ref, kf_ref, vf_ref, gate_ref, fl_ref, cmp_ref):
    h = _rms(x_ref[0], g_ref[...]).astype(BF16)
    z = _dot(h, wn_ref[...])
    qn_ref[0] = (z[:, :_C_CMP] * (HEAD_DIM ** -0.5)).astype(BF16)
    rows = z.shape[0] // CMP_STRIDE
    for i, ref in enumerate((kc_ref, vc_ref)):
        cmp_ref[i] = z[:, _C_CMP + i * NSA_KV:_C_CMP + (i + 1) * NSA_KV]
        for r in range(CMP_STRIDE):
            tok = cmp_ref[i, pl.ds(r, rows, stride=CMP_STRIDE), :]
            ref[0, :, r * NSA_KV:(r + 1) * NSA_KV] = tok.astype(BF16)
    ones = jnp.ones((z.shape[0], HEAD_DIM), BF16)
    for i, (ref, extra) in enumerate(((ks_ref, blk_ref[...]), (vs_ref, ones), (kw_ref, None),
                                      (vw_ref, ones))):
        for g in range(NSA_KV_GROUPS):
            c0 = _C_KV + i * NSA_KV + g * HEAD_DIM
            val = z[:, c0:c0 + HEAD_DIM].astype(BF16)
            ref[0, g] = val if extra is None else jnp.concatenate([val, extra], axis=1)
    zf = _dot(h, wf_ref[...])
    for hd in range(FOX_HEADS):
        c0 = hd * HEAD_DIM
        qf_ref[0, hd] = (zf[:, c0:c0 + HEAD_DIM] * (HEAD_DIM ** -0.5)).astype(BF16)
        kf_ref[0, hd] = zf[:, c0 + FOX_W:c0 + FOX_W + HEAD_DIM].astype(BF16)
        v = zf[:, c0 + 2 * FOX_W:c0 + 2 * FOX_W + HEAD_DIM].astype(BF16)
        vf_ref[0, hd] = jnp.concatenate([v, ones], axis=1)
    zs = _dot(h, ws_ref[...])
    for g in range(NSA_KV_GROUPS):
        gate_ref[0, g] = zs[:, g * GATE_PAD:(g + 1) * GATE_PAD]
    fl_ref[0] = zs.T[_C_FLOG:_C_FLOG + FOX_HEADS]


def _pack_w_in(w_in):
    cols = np.cumsum((0, NSA_Q, NSA_KV, NSA_KV, NSA_KV, NSA_KV, NSA_KV, NSA_KV, 3 * NSA_HEADS,
                      FOX_W, FOX_W, FOX_W, FOX_HEADS, D_MODEL, D_MODEL))
    g0 = int(cols[7])
    gate_cols = []
    for g in range(NSA_KV_GROUPS):
        for br in range(3):
            c0 = g0 + br * NSA_HEADS + g * NSA_HPG
            gate_cols.append(w_in[:, c0:c0 + NSA_HPG])
        gate_cols.append(jnp.zeros((D_MODEL, GATE_PAD - 3 * NSA_HPG), w_in.dtype))
    small = jnp.concatenate(gate_cols + [w_in[:, int(cols[11]):int(cols[12])]], axis=1)
    small = jnp.pad(small, ((0, 0), (0, LANES - small.shape[1])))
    w_nsa = w_in[:, :g0]
    w_fox = w_in[:, int(cols[8]):int(cols[11])]
    w_a = w_in[:, int(cols[12]):int(cols[13])]
    w_b = w_in[:, int(cols[13]):int(cols[14])]
    return tuple(w.astype(BF16) for w in (w_nsa, w_fox, small, w_a, w_b))


def _inproj(x, g, w_nsa, w_fox, w_small):
    b, s, _ = x.shape
    grid = (b, s // TOK_TILE)
    tok = lambda c: pl.BlockSpec((1, TOK_TILE, c), lambda i, j: (i, j, 0))
    heads = lambda nh, w=HEAD_DIM: pl.BlockSpec((1, nh, TOK_TILE, w), lambda i, j: (i, 0, j, 0))
    chunks = pl.BlockSpec((1, TOK_TILE // CMP_STRIDE, CHUNK_W), lambda i, j: (i, j, 0))
    sds = jax.ShapeDtypeStruct
    kv = sds((b, NSA_KV_GROUPS, s, HEAD_DIM), BF16)
    kv2 = sds((b, NSA_KV_GROUPS, s, 2 * HEAD_DIM), BF16)
    fx = sds((b, FOX_HEADS, s, HEAD_DIM), BF16)
    cmp_in = sds((b, s // CMP_STRIDE, CHUNK_W), BF16)
    out_shape = (sds((b, s, NSA_Q), BF16), cmp_in, cmp_in,
                 kv2, kv2, kv, kv2, fx, fx, sds((b, FOX_HEADS, s, 2 * HEAD_DIM), BF16),
                 sds((b, NSA_KV_GROUPS, s, GATE_PAD), F32), sds((b, FOX_HEADS, s), F32))
    out_specs = (tok(NSA_Q), chunks, chunks,
                 heads(NSA_KV_GROUPS, 2 * HEAD_DIM), heads(NSA_KV_GROUPS, 2 * HEAD_DIM),
                 heads(NSA_KV_GROUPS), heads(NSA_KV_GROUPS, 2 * HEAD_DIM),
                 heads(FOX_HEADS), heads(FOX_HEADS), heads(FOX_HEADS, 2 * HEAD_DIM),
                 pl.BlockSpec((1, NSA_KV_GROUPS, TOK_TILE, GATE_PAD), lambda i, j: (i, 0, j, 0)),
                 pl.BlockSpec((1, FOX_HEADS, TOK_TILE), lambda i, j: (i, 0, j)))
    key_blk = np.arange(s) // SEL_BLOCK
    blk_onehot = jnp.asarray(key_blk[:, None] == np.arange(HEAD_DIM)[None, :], BF16)
    return pl.pallas_call(
        _inproj_body,
        grid=grid,
        in_specs=[tok(D_MODEL), _resident((1, D_MODEL)), _resident(w_nsa.shape),
                  _resident(w_fox.shape), _resident(w_small.shape),
                  pl.BlockSpec((TOK_TILE, HEAD_DIM), lambda i, j: (j, 0))],
        out_specs=out_specs,
        out_shape=out_shape,
        scratch_shapes=[pltpu.VMEM((2, TOK_TILE, NSA_KV), F32)],
        compiler_params=_params(("parallel", "parallel")),
        name="inproj",
    )(x, g.reshape(1, D_MODEL), w_nsa, w_fox, w_small, blk_onehot)


def _compress_body(xk_ref, xv_ref, pk_ref, pv_ref, wk1_ref, wv1_ref, wk2_ref, wv2_ref, ok_ref, ov_ref):
    for x_ref, p_ref, w1_ref, w2_ref, o_ref in ((xk_ref, pk_ref, wk1_ref, wk2_ref, ok_ref),
                                                (xv_ref, pv_ref, wv1_ref, wv2_ref, ov_ref)):
        x = x_ref[0]
        a0 = _dot(x, w1_ref[0])
        a1 = _dot(x, w1_ref[1])
        c = _dot(p_ref[0], w1_ref[0]) + _dot(p_ref[1], w1_ref[1])
        pre = a0 + pltpu.roll(a1, N_CHUNKS_PAD - 1, axis=0) + c[0:1, :]
        hid = jax.nn.gelu(pre).astype(BF16)
        out = _dot(hid, w2_ref[...])
        for g in range(NSA_KV_GROUPS):
            o_ref[0, g] = out[:, g * HEAD_DIM:(g + 1) * HEAD_DIM].astype(BF16)


def _pack_compress(pos, w1, w2):
    r = CMP_BLOCK // CMP_STRIDE
    assert NSA_KV_GROUPS == 2
    w1r = w1.astype(BF16).reshape(r, CMP_STRIDE, HEAD_DIM, CMP_HIDDEN)
    z1 = jnp.zeros_like(w1r)
    w1big = jnp.stack([jnp.concatenate([w1r, z1], axis=-1), jnp.concatenate([z1, w1r], axis=-1)], axis=2)
    w1big = w1big.reshape(r, CHUNK_W, NSA_KV_GROUPS * CMP_HIDDEN)
    z2 = jnp.zeros_like(w2)
    w2big = jnp.concatenate([jnp.concatenate([w2, z2], axis=1), jnp.concatenate([z2, w2], axis=1)], axis=0)
    p = pos.reshape(r, CMP_STRIDE, 1, HEAD_DIM)
    p = jnp.broadcast_to(p, (r, CMP_STRIDE, NSA_KV_GROUPS, HEAD_DIM)).reshape(r, 1, CHUNK_W)
    p = jnp.broadcast_to(p, (r, 16, CHUNK_W))
    return p.astype(BF16), w1big.astype(BF16), w2big.astype(BF16)


def _compress(xk, xv, pos_k, pos_v, k_w1, k_w2, v_w1, v_w2):
    b = xk.shape[0]
    assert xk.shape[1:] == (N_CHUNKS_PAD, CHUNK_W)
    pk, wk1, wk2 = _pack_compress(pos_k, k_w1, k_w2)
    pv, wv1, wv2 = _pack_compress(pos_v, v_w1, v_w2)
    xs = pl.BlockSpec((1, N_CHUNKS_PAD, CHUNK_W), lambda i: (i, 0, 0))
    os_ = pl.BlockSpec((1, NSA_KV_GROUPS, N_CHUNKS_PAD, HEAD_DIM), lambda i: (i, 0, 0, 0))
    osd = jax.ShapeDtypeStruct((b, NSA_KV_GROUPS, N_CHUNKS_PAD, HEAD_DIM), BF16)
    return pl.pallas_call(
        _compress_body,
        grid=(b,),
        in_specs=[xs, xs, _resident(pk.shape), _resident(pv.shape), _resident(wk1.shape),
                  _resident(wv1.shape), _resident(wk2.shape), _resident(wv2.shape)],
        out_specs=(os_, os_),
        out_shape=(osd, osd),
        compiler_params=_params(("parallel",)),
        name="compress",
    )(xk, xv, pk, pv, wk1, wv1, wk2, wv2)


def _bias_lookup(tbl_ref, head, dist):
    acc = jnp.full(dist.shape, tbl_ref[0, head], F32)
    for k, thr in enumerate(BUCKET_THRESHOLDS):
        acc = jnp.where(dist >= thr, tbl_ref[k + 1, head], acc)
    return acc - tbl_ref[NUM_BUCKETS - 1, head]


def _relbias_body(tbl_ref, toep_ref, bc_ref):
    head = pl.program_id(0)
    t = ATT_TILE
    i = lax.broadcasted_iota(jnp.int32, (t, 2 * t), 0)
    u = lax.broadcasted_iota(jnp.int32, (t, 2 * t), 1)
    d = i - u + t
    toep_ref[0] = jnp.where(d >= 0, _bias_lookup(tbl_ref, head, d), NEG_INF)
    n_tiles = bc_ref.shape[1] // t
    per_tile = t // CMP_STRIDE
    off = per_tile * (n_tiles - 1)
    width = 2 * N_CHUNKS_PAD
    assert off + N_CHUNKS_PAD <= width
    i = lax.broadcasted_iota(jnp.int32, (t, width), 0)
    u = lax.broadcasted_iota(jnp.int32, (t, width), 1)
    dc = i - ((u - off) * CMP_STRIDE + CMP_BLOCK - 1)
    strip = jnp.where(dc >= 0, _bias_lookup(tbl_ref, head, dc), NEG_INF)
    for q in range(n_tiles):
        lo = off - q * per_tile
        bc_ref[0, q * t:(q + 1) * t, :] = strip[:, lo:lo + N_CHUNKS_PAD]


def _relbias(tbl, s):
    return pl.pallas_call(
        _relbias_body,
        grid=(NSA_HEADS,),
        in_specs=[pl.BlockSpec(memory_space=pltpu.SMEM)],
        out_specs=(pl.BlockSpec((1, ATT_TILE, 2 * ATT_TILE), lambda h: (h, 0, 0)),
                   pl.BlockSpec((1, s, N_CHUNKS_PAD), lambda h: (h, 0, 0))),
        out_shape=(jax.ShapeDtypeStruct((NSA_HEADS, ATT_TILE, 2 * ATT_TILE), F32),
                   jax.ShapeDtypeStruct((NSA_HEADS, s, N_CHUNKS_PAD), F32)),
        compiler_params=_params(("parallel",)),
        name="relbias",
    )(tbl)


def _lane_fold(x, op):
    return functools.reduce(op, [x[:, i * LANES:(i + 1) * LANES] for i in range(x.shape[1] // LANES)])


def _lane_tile(x, width):
    return jnp.concatenate([x] * (width // LANES), axis=1)


def _row_max_tile(rmax):
    return jnp.broadcast_to(jnp.max(rmax, axis=-1, keepdims=True), rmax.shape)


def _normalize(acc):
    return (acc / pltpu.roll(acc, HEAD_DIM, axis=1))[:, :HEAD_DIM]


def _split3(x):
    hi = x.astype(BF16)
    r = x - hi.astype(F32)
    mid = r.astype(BF16)
    lo = (r - mid.astype(F32)).astype(BF16)
    return hi, mid, lo


def _nsa_body(q_ref, kc_ref, vc_ref, ks_ref, vs_ref, kw_ref, vw_ref, gate_a_ref, gate_b_ref, toep_ref,
              bc_ref, ov_ref, o_ref, s_ref, near_ref, mb_ref, acc_ref, part_ref, *, n_tiles):
    t = ATT_TILE
    j4 = NSA_HPG
    rows = j4 * t
    groups = range(NSA_KV_GROUPS)
    step = pl.program_id(1)
    n_sel = n_tiles * (t // SEL_BLOCK)

    def heads(x):
        return x.reshape(j4, t, x.shape[-1])

    def flat(x):
        return x.reshape(rows, x.shape[-1])

    def key_tile(ref, g, n):
        return ref[0, g, pl.ds(pl.multiple_of(n * t, t), t), :]

    def gated(gate_ref, g, branch, o):
        gate = jax.nn.sigmoid(gate_ref[0, g])
        return [gate[:, branch * j4 + j:branch * j4 + j + 1] * o[j * t:(j + 1) * t] for j in range(j4)]

    def probs(par, g, s):
        return jnp.exp(s - _lane_tile(mb_ref[par, g], t)).astype(BF16)

    def open_tile(m, par):
        m1 = jnp.maximum(m - 1, 0)
        m2 = jnp.maximum(m - 2, 0)
        no_prev = jnp.where(m >= 1, 0.0, NEG_INF)
        qs = []
        for g in groups:
            qt = q_ref[0, :, g * j4 * HEAD_DIM:(g + 1) * j4 * HEAD_DIM]
            qs.append(jnp.concatenate([qt[:, j * HEAD_DIM:(j + 1) * HEAD_DIM] for j in range(j4)], axis=0))

        def biases(g):
            toep = toep_ref[g * j4:(g + 1) * j4]
            return toep[:, :, t:], toep[:, :, :t] + no_prev

        cmp_out = [_nsa_compressed(qs[g], kc_ref[0, g], vc_ref[0, g], bc_ref[g * j4:(g + 1) * j4],
                                   ov_ref[...], m, n_sel) for g in groups]
        part = []
        for g in groups:
            diag_bias, prev_bias = biases(g)
            ii = lax.broadcasted_iota(jnp.int32, (t, t), 0)
            jj = lax.broadcasted_iota(jnp.int32, (t, t), 1)
            tri = jnp.where(jj > ii, 0.0, NEG_INF) + jnp.where(m >= 2, 0.0, NEG_INF)
            w0 = flat(heads(_dot_nt_pair(qs[g], key_tile(kw_ref, g, m))) + diag_bias)
            w1 = flat(heads(_dot_nt_pair(qs[g], key_tile(kw_ref, g, m1))) + prev_bias)
            w2 = flat(heads(_dot_nt_pair(qs[g], key_tile(kw_ref, g, m2))) + tri[None])
            wmax = _lane_tile(_row_max_tile(functools.reduce(
                jnp.maximum, [_lane_fold(w, jnp.maximum) for w in (w0, w1, w2)])), t)
            acc = sum(_dot_pair(jnp.exp(w - wmax).astype(BF16), key_tile(vw_ref, g, n))
                      for w, n in ((w0, m), (w1, m1), (w2, m2)))
            part += [a + b for a, b in zip(gated(gate_a_ref, g, 0, cmp_out[g][0]),
                                           gated(gate_a_ref, g, 2, _normalize(acc)))]
        part_ref[par] = jnp.concatenate(part, axis=1)

        q_aug = []
        for g in groups:
            diag_bias, prev_bias = biases(g)
            q_aug.append(jnp.concatenate([qs[g], jnp.concatenate([cmp_out[g][1]] * j4, axis=0)], axis=1))
            s0 = flat(heads(_dot_nt_pair(q_aug[g], key_tile(ks_ref, g, m))) + diag_bias)
            s1 = flat(heads(_dot_nt_pair(q_aug[g], key_tile(ks_ref, g, m1))) + prev_bias)
            s2 = _dot_nt_pair(q_aug[g], key_tile(ks_ref, g, m2))
            near_ref[par, g, 0] = s0
            near_ref[par, g, 1] = s1
            s_ref[g, m2] = s2
            far = jnp.where(m >= 2, _lane_fold(s2, jnp.maximum), NEG_INF)
            mb_ref[par, g] = jnp.maximum(jnp.maximum(_lane_fold(s0, jnp.maximum),
                                                     _lane_fold(s1, jnp.maximum)), far)
        return q_aug

    def open_far(par, q_aug, n):
        for g in groups:
            s = _dot_nt_pair(q_aug[g], key_tile(ks_ref, g, n))
            s_ref[g, n] = s
            mb_ref[par, g] = jnp.maximum(mb_ref[par, g], _lane_fold(s, jnp.maximum))

    def close_tile(m, par):
        m1 = jnp.maximum(m - 1, 0)
        for g in groups:
            mb_ref[par, g] = _row_max_tile(mb_ref[par, g])
            acc_ref[par, g] = (_dot_pair(probs(par, g, near_ref[par, g, 0]), key_tile(vs_ref, g, m))
                               + _dot_pair(probs(par, g, near_ref[par, g, 1]), key_tile(vs_ref, g, m1)))

    def close_far(par, n):
        for g in groups:
            acc_ref[par, g] += _dot_pair(probs(par, g, s_ref[g, n]), key_tile(vs_ref, g, n))

    def finish(par):
        outs = []
        for g in groups:
            outs += gated(gate_b_ref, g, 1, _normalize(acc_ref[par, g]))
        o_ref[0] = (part_ref[par] + jnp.concatenate(outs, axis=1)).astype(BF16)

    @pl.when(step == 0)
    def _():
        open_tile(step, 0)

    @pl.when(step == 1)
    def _():
        open_tile(step, 1)
        close_tile(step - 1, 0)

    for par in (0, 1):
        @pl.when(jnp.logical_and(jnp.logical_and(step >= 2, step < n_tiles), step % 2 == par))
        def _(par=par):
            finish(par)
            q_aug = open_tile(step, par)
            close_tile(step - 1, 1 - par)

            @pl.loop(0, step - 2)
            def _(n):
                close_far(1 - par, n)
                open_far(par, q_aug, n)

    last = (n_tiles - 1) % 2

    @pl.when(step == n_tiles)
    def _():
        finish(1 - last)
        close_tile(step - 1, last)

        @pl.loop(0, step - 2)
        def _(n):
            close_far(last, n)

    @pl.when(step == n_tiles + 1)
    def _():
        finish(last)


def _nsa_compressed(q, kc, vc, bc, ov, m, n_sel):
    t = ATT_TILE
    j4 = NSA_HPG

    def heads(x):
        return x.reshape(j4, t, x.shape[-1])

    def flat(x):
        return x.reshape(j4 * t, x.shape[-1])

    sc = heads(_dot_nt_pair(q, kc)) + bc
    valid = bc > 0.5 * NEG_INF
    e = jnp.where(valid, jnp.exp(sc - jnp.max(sc, axis=-1, keepdims=True)), 0.0)
    l = jnp.sum(e, axis=-1, keepdims=True)
    pc = jnp.where(l > 0.0, e / l, 0.0)
    o_c = _dot_pair(flat(pc).astype(BF16), vc)
    psum = pc[0] + pc[1] + pc[2] + pc[3]
    imp = sum(_dot(part, ov) for part in _split3(psum)).T[:n_sel]
    blk = lax.broadcasted_iota(jnp.int32, (n_sel, t), 0)
    tpos = m * t + lax.broadcasted_iota(jnp.int32, (n_sel, t), 1)
    cur = tpos // SEL_BLOCK
    bonus = jnp.where(blk == 0, FORCE_BONUS,
                      jnp.where(blk == cur, FORCE_BONUS, jnp.where(blk == cur - 1, FORCE_BONUS, 0.0)))
    imp = jnp.where(blk * SEL_BLOCK <= tpos, imp + bonus, NEG_INF)
    rank = jnp.zeros((n_sel, t), F32)
    for i in range(n_sel):
        row = imp[i:i + 1, :]
        before = jnp.where(blk > i, 1.0, 0.0)
        rank = rank + jnp.where(row > imp, 1.0, jnp.where(row == imp, before, 0.0))
    selb = jnp.where(rank < float(SEL_TOP_N), 0.0, NEG_INF)
    selb = jnp.concatenate([selb, jnp.zeros((LANES - n_sel, t), F32)], axis=0).T
    return o_c, selb[:, :HEAD_DIM].astype(BF16)


def _overlap_table(s):
    n_sel = s // SEL_BLOCK
    nc = N_CHUNKS_PAD
    c0 = np.arange(nc)[:, None] * CMP_STRIDE
    s0 = np.arange(n_sel)[None, :] * SEL_BLOCK
    ov = np.clip(np.minimum(c0 + CMP_BLOCK, s0 + SEL_BLOCK) - np.maximum(c0, s0), 0, None) / CMP_STRIDE
    ov[nc - 1] = 0.0
    return jnp.asarray(np.pad(ov, ((0, 0), (0, LANES - n_sel))), BF16)


def _nsa(qn, kc, vc, ks, vs, kw, vw, gates, toep, bc):
    b, s, _ = qn.shape
    t = ATT_TILE
    assert s // SEL_BLOCK <= HEAD_DIM
    ov = _overlap_table(s)
    rows = NSA_HPG * t
    ng = NSA_KV_GROUPS
    nq = s // t
    assert nq >= 2
    seq = lambda n, w=HEAD_DIM: pl.BlockSpec((1, ng, n, w), lambda i, j: (i, 0, 0, 0))
    opened = lambda j: jnp.minimum(j, nq - 1)
    closed = lambda j: jnp.maximum(j - 2, 0)
    return pl.pallas_call(
        functools.partial(_nsa_body, n_tiles=nq),
        grid=(b, nq + 2),
        in_specs=[pl.BlockSpec((1, t, NSA_Q), lambda i, j: (i, opened(j), 0)),
                  seq(N_CHUNKS_PAD), seq(N_CHUNKS_PAD),
                  seq(s, 2 * HEAD_DIM), seq(s, 2 * HEAD_DIM), seq(s), seq(s, 2 * HEAD_DIM),
                  pl.BlockSpec((1, ng, t, GATE_PAD), lambda i, j: (i, 0, opened(j), 0)),
                  pl.BlockSpec((1, ng, t, GATE_PAD), lambda i, j: (i, 0, closed(j), 0)),
                  _resident((NSA_HEADS, t, 2 * t)),
                  pl.BlockSpec((NSA_HEADS, t, N_CHUNKS_PAD), lambda i, j: (0, opened(j), 0)),
                  _resident(ov.shape)],
        out_specs=pl.BlockSpec((1, t, NSA_Q), lambda i, j: (i, closed(j), 0)),
        out_shape=jax.ShapeDtypeStruct((b, s, NSA_Q), BF16),
        scratch_shapes=[pltpu.VMEM((ng, nq - 2, rows, t), F32),
                        pltpu.VMEM((2, ng, 2, rows, t), F32),
                        pltpu.VMEM((2, ng, rows, LANES), F32),
                        pltpu.VMEM((2, ng, rows, 2 * HEAD_DIM), F32),
                        pltpu.VMEM((2, t, NSA_Q), F32)],
        compiler_params=_params(("parallel", "arbitrary")),
        name="nsa",
    )(qn, kc, vc, ks, vs, kw, vw, gates, gates, toep, bc, ov)


def _fcum_body(f_ref, b_ref, o_ref):
    z = f_ref[0] + b_ref[...]
    x = -(jnp.maximum(-z, 0.0) + jnp.log1p(jnp.exp(-jnp.abs(z))))
    n = x.shape[-1]
    lane = lax.broadcasted_iota(jnp.int32, x.shape, 1)
    sh = 1
    while sh < n:
        x = x + jnp.where(lane >= sh, pltpu.roll(x, sh, axis=1), 0.0)
        sh *= 2
    o_ref[0] = x


def _fcum(f_t, b_forget):
    b, h, s = f_t.shape
    blk = pl.BlockSpec((1, h, s), lambda i: (i, 0, 0))
    return pl.pallas_call(
        _fcum_body,
        grid=(b,),
        in_specs=[blk, _resident((h, 1))],
        out_specs=blk,
        out_shape=jax.ShapeDtypeStruct((b, h, s), F32),
        compiler_params=_params(("parallel",)),
        name="fcum",
    )(f_t, b_forget.reshape(h, 1))


FOX_PAIR = 2
FOX_TILE = 512


def _causal_bias(t):
    i = np.arange(t)
    return jnp.asarray(np.where(i[None, :] <= i[:, None], 0.0, NEG_INF), F32)


def _fox_body(q_ref, k_ref, v_ref, c_ref, cm_ref, o_ref, s_ref, mb_ref, acc_ref):
    t = FOX_TILE
    heads = range(FOX_PAIR)
    nq = q_ref.shape[2] // t

    def keys(n):
        return pl.ds(pl.multiple_of(n * t, t), t)

    def scores(m, hh, n):
        q = q_ref[0, hh, m * t:(m + 1) * t, :]
        return _dot_nt(q, k_ref[0, hh, keys(n), :]) - c_ref[0, hh, :, keys(n)]

    def scores_diag(m):
        for hh in heads:
            s = scores(m, hh, m) + cm_ref[...]
            s_ref[m % 2, hh, m] = s
            mb_ref[m % 2, hh] = _lane_fold(s, jnp.maximum)

    def scores_far(m, n):
        for hh in heads:
            s = scores(m, hh, n)
            s_ref[m % 2, hh, n] = s
            mb_ref[m % 2, hh] = jnp.maximum(mb_ref[m % 2, hh], _lane_fold(s, jnp.maximum))

    def finish_max(m):
        for hh in heads:
            mb_ref[m % 2, hh] = _row_max_tile(mb_ref[m % 2, hh])

    def weigh(m, n):
        for hh in heads:
            p = jnp.exp(s_ref[m % 2, hh, n] - _lane_tile(mb_ref[m % 2, hh], t)).astype(BF16)
            acc_ref[hh] += _dot(p, v_ref[0, hh, keys(n), :])

    scores_diag(0)
    finish_max(0)
    for m in range(nq):
        ahead = m + 1 < nq
        if ahead:
            scores_diag(m + 1)
        for hh in heads:
            acc_ref[hh] = jnp.zeros((t, 2 * HEAD_DIM), F32)

        @pl.loop(0, m + 1)
        def _(n, m=m, ahead=ahead):
            weigh(m, n)
            if ahead:
                scores_far(m + 1, n)

        if ahead:
            finish_max(m + 1)
        o_ref[0, m * t:(m + 1) * t, :] = jnp.concatenate(
            [_normalize(acc_ref[hh]) for hh in heads], axis=1).astype(BF16)


def _fox(qf, kf, vf, c):
    b, h, s, _ = qf.shape
    t = FOX_TILE
    seq = lambda w: pl.BlockSpec((1, FOX_PAIR, s, w), lambda i, p: (i, p, 0, 0))
    return pl.pallas_call(
        _fox_body,
        grid=(b, h // FOX_PAIR),
        in_specs=[seq(HEAD_DIM), seq(HEAD_DIM), seq(2 * HEAD_DIM),
                  pl.BlockSpec((1, FOX_PAIR, 1, s), lambda i, p: (i, p, 0, 0)),
                  _resident((t, t))],
        out_specs=pl.BlockSpec((1, s, FOX_PAIR * HEAD_DIM), lambda i, p: (i, 0, p)),
        out_shape=jax.ShapeDtypeStruct((b, s, h * HEAD_DIM), BF16),
        scratch_shapes=[pltpu.VMEM((2, FOX_PAIR, s // t, t, t), F32),
                        pltpu.VMEM((2, FOX_PAIR, t, LANES), F32),
                        pltpu.VMEM((FOX_PAIR, t, 2 * HEAD_DIM), F32)],
        compiler_params=_params(("parallel", "parallel")),
        name="fox",
    )(qf, kf, vf, c.reshape(b, h, 1, s), _causal_bias(t))


def _mixout_body(x_ref, g_ref, on_ref, of_ref, wa_ref, wb_ref, wun_ref, wuf_ref, wo_ref, o_ref):
    x = x_ref[...]
    h = _rms(x, g_ref[...]).astype(BF16)
    y = (jax.nn.sigmoid(_dot(h, wa_ref[...])) * _dot(on_ref[...], wun_ref[...])
         + jax.nn.sigmoid(_dot(h, wb_ref[...])) * _dot(of_ref[...], wuf_ref[...]))
    o_ref[...] = x + _dot(y.astype(BF16), wo_ref[...])


def _mixout(x2d, g, o_nsa, o_fox, w_a, w_b, w_un, w_uf, w_o):
    n = x2d.shape[0]
    tok = lambda c: pl.BlockSpec((TOK_TILE, c), lambda i: (i, 0))
    return pl.pallas_call(
        _mixout_body,
        grid=(n // TOK_TILE,),
        in_specs=[tok(D_MODEL), _resident((1, D_MODEL)), tok(NSA_Q), tok(FOX_W),
                  _resident(w_a.shape), _resident(w_b.shape), _resident(w_un.shape),
                  _resident(w_uf.shape), _resident(w_o.shape)],
        out_specs=tok(D_MODEL),
        out_shape=jax.ShapeDtypeStruct((n, D_MODEL), F32),
        compiler_params=_params(("parallel",)),
        name="mixout",
    )(x2d, g.reshape(1, D_MODEL), o_nsa, o_fox, w_a, w_b, w_un, w_uf, w_o)


def _memkv_body(m_ref, g_ref, w_ref, k_ref, v_ref):
    h = _rms(m_ref[0], g_ref[...]).astype(BF16)
    z = _dot(h, w_ref[...])
    k_ref[0] = z[:, :D_MODEL].astype(BF16)
    v_ref[0] = z[:, D_MODEL:].astype(BF16)


def _memkv(mem, g, w_kv):
    b, ml, _ = mem.shape
    blk = pl.BlockSpec((1, ml, D_MODEL), lambda i: (i, 0, 0))
    sd = jax.ShapeDtypeStruct((b, ml, D_MODEL), BF16)
    return pl.pallas_call(
        _memkv_body,
        grid=(b,),
        in_specs=[blk, _resident((1, D_MODEL)), _resident(w_kv.shape)],
        out_specs=(blk, blk),
        out_shape=(sd, sd),
        compiler_params=_params(("parallel",)),
        name="memkv",
    )(mem, g.reshape(1, D_MODEL), w_kv)


def _memattn_body(x_ref, g_ref, k_ref, v_ref, wq_ref, wo_ref, o_ref):
    x = x_ref[0]
    h = _rms(x, g_ref[...]).astype(BF16)
    q = (_dot(h, wq_ref[...]) * (MEM_HEAD_DIM ** -0.5)).astype(BF16)
    outs = []
    for hd in range(MEM_HEADS):
        cols = slice(hd * MEM_HEAD_DIM, (hd + 1) * MEM_HEAD_DIM)
        s = _dot_nt(q[:, cols], k_ref[0, :, cols])
        e = jnp.exp(s - jnp.max(s, axis=-1, keepdims=True))
        p = e / jnp.sum(e, axis=-1, keepdims=True)
        outs.append(_dot(p.astype(BF16), v_ref[0, :, cols]))
    o = jnp.concatenate(outs, axis=1).astype(BF16)
    o_ref[0] = x + _dot(o, wo_ref[...])


def _memattn(x, g, k, v, w_q, w_o):
    b, s, _ = x.shape
    ml = k.shape[1]
    tok = pl.BlockSpec((1, TOK_TILE, D_MODEL), lambda i, j: (i, j, 0))
    kvb = pl.BlockSpec((1, ml, D_MODEL), lambda i, j: (i, 0, 0))
    return pl.pallas_call(
        _memattn_body,
        grid=(b, s // TOK_TILE),
        in_specs=[tok, _resident((1, D_MODEL)), kvb, kvb, _resident(w_q.shape), _resident(w_o.shape)],
        out_specs=tok,
        out_shape=jax.ShapeDtypeStruct((b, s, D_MODEL), F32),
        compiler_params=_params(("parallel", "parallel")),
        name="memattn",
    )(x, g.reshape(1, D_MODEL), k, v, w_q, w_o)


def kernel(x, mem, rel_bias_table, ffn1_norm, ffn1_w_gate, ffn1_w_up, ffn1_w_down, mix_norm, mix_w_in, mix_b_forget, cmp_pos_k, cmp_pos_v, cmp_k_w1, cmp_k_w2, cmp_v_w1, cmp_v_w2, w_up_nsa, w_up_fox, mix_w_out, mem_q_norm, mem_kv_norm, mem_w_q, mem_w_kv, mem_w_o, ffn2_norm, ffn2_w_gate, ffn2_w_up, ffn2_w_down, final_norm):
    b, s, d = x.shape
    depth = ffn1_norm.shape[0]
    bf = lambda w: w.astype(BF16)
    toep, bias_c = _relbias(rel_bias_table, s)
    x = x.reshape(b * s, d)
    for l in range(depth):
        last = l == depth - 1
        x = _ffn(x, ffn1_norm[l], bf(ffn1_w_gate[l]), bf(ffn1_w_up[l]), bf(ffn1_w_down[l]))

        w_nsa, w_fox, w_small, w_a, w_b = _pack_w_in(mix_w_in[l])
        qn, kc, vc, ks, vs, kw, vw, qf, kf, vf, gates, flog = _inproj(
            x.reshape(b, s, d), mix_norm[l], w_nsa, w_fox, w_small)
        kcc, vcc = _compress(kc, vc, cmp_pos_k[l], cmp_pos_v[l], cmp_k_w1[l], cmp_k_w2[l],
                             cmp_v_w1[l], cmp_v_w2[l])
        o_nsa = _nsa(qn, kcc, vcc, ks, vs, kw, vw, gates, toep, bias_c)
        c = _fcum(flog, mix_b_forget[l])
        o_fox = _fox(qf, kf, vf, c)
        x = _mixout(x, mix_norm[l], o_nsa.reshape(b * s, NSA_Q), o_fox.reshape(b * s, FOX_W),
                    w_a, w_b, bf(w_up_nsa[l]), bf(w_up_fox[l]), bf(mix_w_out[l]))

        mk, mv = _memkv(mem, mem_kv_norm[l], bf(mem_w_kv[l]))
        x = _memattn(x.reshape(b, s, d), mem_q_norm[l], mk, mv, bf(mem_w_q[l]), bf(mem_w_o[l]))
        x = _ffn(x.reshape(b * s, d), ffn2_norm[l], bf(ffn2_w_gate[l]), bf(ffn2_w_up[l]),
                 bf(ffn2_w_down[l]), final_g=final_norm if last else None)
    return x.reshape(b, s, d)
```

```python
import functools
import math

import numpy as np
import jax
import jax.numpy as jnp
from jax import lax
from jax.experimental import pallas as pl
from jax.experimental.pallas import tpu as pltpu

D_MODEL = 1024
D_FF = 2816
HEAD_DIM = 64
NSA_HEADS = 8
NSA_KV_GROUPS = 2
NSA_HPG = NSA_HEADS // NSA_KV_GROUPS
CMP_BLOCK = 32
CMP_STRIDE = 16
CMP_HIDDEN = 256
SEL_BLOCK = 64
SEL_TOP_N = 16
WINDOW = 512
FOX_HEADS = 8
MEM_HEADS = 4
MEM_HEAD_DIM = D_MODEL // MEM_HEADS
NUM_BUCKETS = 32
MAX_DISTANCE = 128
RMS_EPS = 1e-6
NEG_INF = -1e30
FORCE_BONUS = 1e4

NSA_Q = NSA_HEADS * HEAD_DIM
NSA_KV = NSA_KV_GROUPS * HEAD_DIM
FOX_W = FOX_HEADS * HEAD_DIM

LANES = 128
VMEM_LIMIT = 56 * 1024 * 1024

BF16 = jnp.bfloat16
F32 = jnp.float32

ATT_TILE = 256
TOK_TILE = 1024
FFN_TILE = 1024
FF_CHUNK = 256


def _bucket_thresholds():
    n = np.arange(0, 4 * MAX_DISTANCE)
    exact = NUM_BUCKETS // 2
    large = exact + (np.log(np.maximum(n, 1) / exact) / math.log(MAX_DISTANCE / exact)
                     * (NUM_BUCKETS - exact)).astype(np.int64)
    bucket = np.where(n < exact, n, np.minimum(large, NUM_BUCKETS - 1))
    assert np.all(np.diff(bucket) >= 0)
    return [int(np.argmax(bucket >= k)) for k in range(1, NUM_BUCKETS)]


BUCKET_THRESHOLDS = _bucket_thresholds()


def _dot(a, b):
    return jnp.dot(a, b, preferred_element_type=F32)


def _dot_nt(a, b):
    return lax.dot_general(a, b, (((1,), (1,)), ((), ())), preferred_element_type=F32)


def _row_halves(fn, a, b):
    h = a.shape[0] // 2
    return jnp.concatenate([fn(a[:h], b), fn(a[h:], b)], axis=0)


def _dot_pair(a, b):
    return _row_halves(_dot, a, b)


def _dot_nt_pair(a, b):
    return _row_halves(_dot_nt, a, b)


def _rms(x, g):
    return x * lax.rsqrt(jnp.mean(x * x, axis=-1, keepdims=True) + RMS_EPS) * g


def _resident(shape):
    nd = len(shape)
    return pl.BlockSpec(shape, lambda *_: (0,) * nd, pipeline_mode=pl.Buffered(1))


def _params(sem):
    return pltpu.CompilerParams(dimension_semantics=sem, vmem_limit_bytes=VMEM_LIMIT)


def _ffn_body(x_ref, g_ref, wg_hbm, wu_hbm, wd_hbm, *rest, final):
    if final:
        fg_ref, o_ref, wg_ref, wu_ref, wd_ref, sg_ref, su_ref, sd_ref, sem = rest
    else:
        o_ref, wg_ref, wu_ref, wd_ref, sg_ref, su_ref, sd_ref, sem = rest
    n_chunks = D_FF // FF_CHUNK

    def chunk_copies(c):
        cols = slice(c * FF_CHUNK, (c + 1) * FF_CHUNK)
        slot = c % 2
        return (pltpu.make_async_copy(wg_hbm.at[:, cols], sg_ref.at[slot], sem.at[0, slot]),
                pltpu.make_async_copy(wu_hbm.at[:, cols], su_ref.at[slot], sem.at[1, slot]),
                pltpu.make_async_copy(wd_hbm.at[cols, :], sd_ref.at[slot], sem.at[2, slot]))

    @pl.when(pl.program_id(0) == 0)
    def _():
        for cp in chunk_copies(0):
            cp.start()
        for c in range(n_chunks):
            sl = slice(c * FF_CHUNK, (c + 1) * FF_CHUNK)
            if c + 1 < n_chunks:
                for cp in chunk_copies(c + 1):
                    cp.start()
            for cp in chunk_copies(c):
                cp.wait()
            wg_ref[:, sl] = sg_ref[c % 2].astype(BF16)
            wu_ref[:, sl] = su_ref[c % 2].astype(BF16)
            wd_ref[sl, :] = sd_ref[c % 2].astype(BF16)

    x = x_ref[...]
    h = _rms(x, g_ref[...]).astype(BF16)
    acc = jnp.zeros(x.shape, F32)
    for c in range(n_chunks):
        sl = slice(c * FF_CHUNK, (c + 1) * FF_CHUNK)
        a = _dot(h, wg_ref[:, sl])
        b = _dot(h, wu_ref[:, sl])
        t = (a * jax.nn.sigmoid(a)) * b
        acc = acc + _dot(t.astype(BF16), wd_ref[sl, :])
    y = x + 0.5 * acc
    if final:
        y = _rms(y, fg_ref[...])
    o_ref[...] = y


def _ffn(x2d, g, wg, wu, wd, final_g=None):
    n = x2d.shape[0]
    final = final_g is not None
    tok = pl.BlockSpec((FFN_TILE, D_MODEL), lambda i: (i, 0))
    hbm = pl.BlockSpec(memory_space=pl.ANY)
    in_specs = [tok, _resident((1, D_MODEL)), hbm, hbm, hbm]
    args = [x2d, g.reshape(1, D_MODEL), wg, wu, wd]
    if final:
        in_specs.append(_resident((1, D_MODEL)))
        args.append(final_g.reshape(1, D_MODEL))
    return pl.pallas_call(
        functools.partial(_ffn_body, final=final),
        grid=(n // FFN_TILE,),
        in_specs=in_specs,
        out_specs=tok,
        out_shape=jax.ShapeDtypeStruct((n, D_MODEL), F32),
        scratch_shapes=[pltpu.VMEM((D_MODEL, D_FF), BF16), pltpu.VMEM((D_MODEL, D_FF), BF16),
                        pltpu.VMEM((D_FF, D_MODEL), BF16),
                        pltpu.VMEM((2, D_MODEL, FF_CHUNK), F32), pltpu.VMEM((2, D_MODEL, FF_CHUNK), F32),
                        pltpu.VMEM((2, FF_CHUNK, D_MODEL), F32),
                        pltpu.SemaphoreType.DMA((3, 2))],
        compiler_params=_params(("arbitrary",)),
        name="ffn_final" if final else "ffn",
    )(*args)


_C_CMP = NSA_Q
_C_KV = _C_CMP + 2 * NSA_KV
_W_NSA = _C_KV + 4 * NSA_KV
GATE_PAD = 16
_C_FLOG = NSA_KV_GROUPS * GATE_PAD
N_CHUNKS_PAD = 128
CHUNK_W = CMP_STRIDE * NSA_KV


def _inproj_body(x_ref, g_ref, wn_ref, wf_ref, ws_ref, blk_ref, qn_ref, kc_ref, vc_ref, ks_ref, vs_ref,
                 kw_ref, vw_ref, qf_ref, kf_ref, vf_ref, gate_ref, fl_ref, cmp_ref):
    h = _rms(x_ref[0], g_ref[...]).astype(BF16)
    z = _dot(h, wn_ref[...])
    qn_ref[0] = (z[:, :_C_CMP] * (HEAD_DIM ** -0.5)).astype(BF16)
    rows = z.shape[0] // CMP_STRIDE
    for i, ref in enumerate((kc_ref, vc_ref)):
        cmp_ref[i] = z[:, _C_CMP + i * NSA_KV:_C_CMP + (i + 1) * NSA_KV]
        for r in range(CMP_STRIDE):
            tok = cmp_ref[i, pl.ds(r, rows, stride=CMP_STRIDE), :]
            ref[0, :, r * NSA_KV:(r + 1) * NSA_KV] = tok.astype(BF16)
    ones = jnp.ones((z.shape[0], HEAD_DIM), BF16)
    for i, (ref, extra) in enumerate(((ks_ref, blk_ref[...]), (vs_ref, ones), (kw_ref, None),
                                      (vw_ref, ones))):
        for g in range(NSA_KV_GROUPS):
            c0 = _C_KV + i * NSA_KV + g * HEAD_DIM
            val = z[:, c0:c0 + HEAD_DIM].astype(BF16)
            ref[0, g] = val if extra is None else jnp.concatenate([val, extra], axis=1)
    zf = _dot(h, wf_ref[...])
    for hd in range(FOX_HEADS):
        c0 = hd * HEAD_DIM
        qf_ref[0, hd] = (zf[:, c0:c0 + HEAD_DIM] * (HEAD_DIM ** -0.5)).astype(BF16)
        kf_ref[0, hd] = zf[:, c0 + FOX_W:c0 + FOX_W + HEAD_DIM].astype(BF16)
        v = zf[:, c0 + 2 * FOX_W:c0 + 2 * FOX_W + HEAD_DIM].astype(BF16)
        vf_ref[0, hd] = jnp.concatenate([v, ones], axis=1)
    zs = _dot(h, ws_ref[...])
    for g in range(NSA_KV_GROUPS):
        gate_ref[0, g] = zs[:, g * GATE_PAD:(g + 1) * GATE_PAD]
    fl_ref[0] = zs.T[_C_FLOG:_C_FLOG + FOX_HEADS]


def _pack_w_in(w_in):
    cols = np.cumsum((0, NSA_Q, NSA_KV, NSA_KV, NSA_KV, NSA_KV, NSA_KV, NSA_KV, 3 * NSA_HEADS,
                      FOX_W, FOX_W, FOX_W, FOX_HEADS, D_MODEL, D_MODEL))
    g0 = int(cols[7])
    gate_cols = []
    for g in range(NSA_KV_GROUPS):
        for br in range(3):
            c0 = g0 + br * NSA_HEADS + g * NSA_HPG
            gate_cols.append(w_in[:, c0:c0 + NSA_HPG])
        gate_cols.append(jnp.zeros((D_MODEL, GATE_PAD - 3 * NSA_HPG), w_in.dtype))
    small = jnp.concatenate(gate_cols + [w_in[:, int(cols[11]):int(cols[12])]], axis=1)
    small = jnp.pad(small, ((0, 0), (0, LANES - small.shape[1])))
    w_nsa = w_in[:, :g0]
    w_fox = w_in[:, int(cols[8]):int(cols[11])]
    w_a = w_in[:, int(cols[12]):int(cols[13])]
    w_b = w_in[:, int(cols[13]):int(cols[14])]
    return tuple(w.astype(BF16) for w in (w_nsa, w_fox, small, w_a, w_b))


def _inproj(x, g, w_nsa, w_fox, w_small):
    b, s, _ = x.shape
    grid = (b, s // TOK_TILE)
    tok = lambda c: pl.BlockSpec((1, TOK_TILE, c), lambda i, j: (i, j, 0))
    heads = lambda nh, w=HEAD_DIM: pl.BlockSpec((1, nh, TOK_TILE, w), lambda i, j: (i, 0, j, 0))
    chunks = pl.BlockSpec((1, TOK_TILE // CMP_STRIDE, CHUNK_W), lambda i, j: (i, j, 0))
    sds = jax.ShapeDtypeStruct
    kv = sds((b, NSA_KV_GROUPS, s, HEAD_DIM), BF16)
    kv2 = sds((b, NSA_KV_GROUPS, s, 2 * HEAD_DIM), BF16)
    fx = sds((b, FOX_HEADS, s, HEAD_DIM), BF16)
    cmp_in = sds((b, s // CMP_STRIDE, CHUNK_W), BF16)
    out_shape = (sds((b, s, NSA_Q), BF16), cmp_in, cmp_in,
                 kv2, kv2, kv, kv2, fx, fx, sds((b, FOX_HEADS, s, 2 * HEAD_DIM), BF16),
                 sds((b, NSA_KV_GROUPS, s, GATE_PAD), F32), sds((b, FOX_HEADS, s), F32))
    out_specs = (tok(NSA_Q), chunks, chunks,
                 heads(NSA_KV_GROUPS, 2 * HEAD_DIM), heads(NSA_KV_GROUPS, 2 * HEAD_DIM),
                 heads(NSA_KV_GROUPS), heads(NSA_KV_GROUPS, 2 * HEAD_DIM),
                 heads(FOX_HEADS), heads(FOX_HEADS), heads(FOX_HEADS, 2 * HEAD_DIM),
                 pl.BlockSpec((1, NSA_KV_GROUPS, TOK_TILE, GATE_PAD), lambda i, j: (i, 0, j, 0)),
                 pl.BlockSpec((1, FOX_HEADS, TOK_TILE), lambda i, j: (i, 0, j)))
    key_blk = np.arange(s) // SEL_BLOCK
    blk_onehot = jnp.asarray(key_blk[:, None] == np.arange(HEAD_DIM)[None, :], BF16)
    return pl.pallas_call(
        _inproj_body,
        grid=grid,
        in_specs=[tok(D_MODEL), _resident((1, D_MODEL)), _resident(w_nsa.shape),
                  _resident(w_fox.shape), _resident(w_small.shape),
                  pl.BlockSpec((TOK_TILE, HEAD_DIM), lambda i, j: (j, 0))],
        out_specs=out_specs,
        out_shape=out_shape,
        scratch_shapes=[pltpu.VMEM((2, TOK_TILE, NSA_KV), F32)],
        compiler_params=_params(("parallel", "parallel")),
        name="inproj",
    )(x, g.reshape(1, D_MODEL), w_nsa, w_fox, w_small, blk_onehot)


def _compress_body(xk_ref, xv_ref, pk_ref, pv_ref, wk1_ref, wv1_ref, wk2_ref, wv2_ref, ok_ref, ov_ref):
    for x_ref, p_ref, w1_ref, w2_ref, o_ref in ((xk_ref, pk_ref, wk1_ref, wk2_ref, ok_ref),
                                                (xv_ref, pv_ref, wv1_ref, wv2_ref, ov_ref)):
        x = x_ref[0]
        a0 = _dot(x, w1_ref[0])
        a1 = _dot(x, w1_ref[1])
        c = _dot(p_ref[0], w1_ref[0]) + _dot(p_ref[1], w1_ref[1])
        pre = a0 + pltpu.roll(a1, N_CHUNKS_PAD - 1, axis=0) + c[0:1, :]
        hid = jax.nn.gelu(pre).astype(BF16)
        out = _dot(hid, w2_ref[...])
        for g in range(NSA_KV_GROUPS):
            o_ref[0, g] = out[:, g * HEAD_DIM:(g + 1) * HEAD_DIM].astype(BF16)


def _pack_compress(pos, w1, w2):
    r = CMP_BLOCK // CMP_STRIDE
    assert NSA_KV_GROUPS == 2
    w1r = w1.astype(BF16).reshape(r, CMP_STRIDE, HEAD_DIM, CMP_HIDDEN)
    z1 = jnp.zeros_like(w1r)
    w1big = jnp.stack([jnp.concatenate([w1r, z1], axis=-1), jnp.concatenate([z1, w1r], axis=-1)], axis=2)
    w1big = w1big.reshape(r, CHUNK_W, NSA_KV_GROUPS * CMP_HIDDEN)
    z2 = jnp.zeros_like(w2)
    w2big = jnp.concatenate([jnp.concatenate([w2, z2], axis=1), jnp.concatenate([z2, w2], axis=1)], axis=0)
    p = pos.reshape(r, CMP_STRIDE, 1, HEAD_DIM)
    p = jnp.broadcast_to(p, (r, CMP_STRIDE, NSA_KV_GROUPS, HEAD_DIM)).reshape(r, 1, CHUNK_W)
    p = jnp.broadcast_to(p, (r, 16, CHUNK_W))
    return p.astype(BF16), w1big.astype(BF16), w2big.astype(BF16)


def _compress(xk, xv, pos_k, pos_v, k_w1, k_w2, v_w1, v_w2):
    b = xk.shape[0]
    assert xk.shape[1:] == (N_CHUNKS_PAD, CHUNK_W)
    pk, wk1, wk2 = _pack_compress(pos_k, k_w1, k_w2)
    pv, wv1, wv2 = _pack_compress(pos_v, v_w1, v_w2)
    xs = pl.BlockSpec((1, N_CHUNKS_PAD, CHUNK_W), lambda i: (i, 0, 0))
    os_ = pl.BlockSpec((1, NSA_KV_GROUPS, N_CHUNKS_PAD, HEAD_DIM), lambda i: (i, 0, 0, 0))
    osd = jax.ShapeDtypeStruct((b, NSA_KV_GROUPS, N_CHUNKS_PAD, HEAD_DIM), BF16)
    return pl.pallas_call(
        _compress_body,
        grid=(b,),
        in_specs=[xs, xs, _resident(pk.shape), _resident(pv.shape), _resident(wk1.shape),
                  _resident(wv1.shape), _resident(wk2.shape), _resident(wv2.shape)],
        out_specs=(os_, os_),
        out_shape=(osd, osd),
        compiler_params=_params(("parallel",)),
        name="compress",
    )(xk, xv, pk, pv, wk1, wv1, wk2, wv2)


def _bias_lookup(tbl_ref, head, dist):
    acc = jnp.full(dist.shape, tbl_ref[0, head], F32)
    for k, thr in enumerate(BUCKET_THRESHOLDS):
        acc = jnp.where(dist >= thr, tbl_ref[k + 1, head], acc)
    return acc - tbl_ref[NUM_BUCKETS - 1, head]


def _relbias_body(tbl_ref, toep_ref, bc_ref):
    head = pl.program_id(0)
    t = ATT_TILE
    i = lax.broadcasted_iota(jnp.int32, (t, 2 * t), 0)
    u = lax.broadcasted_iota(jnp.int32, (t, 2 * t), 1)
    d = i - u + t
    toep_ref[0] = jnp.where(d >= 0, _bias_lookup(tbl_ref, head, d), NEG_INF)
    n_tiles = bc_ref.shape[1] // t
    per_tile = t // CMP_STRIDE
    off = per_tile * (n_tiles - 1)
    width = 2 * N_CHUNKS_PAD
    assert off + N_CHUNKS_PAD <= width
    i = lax.broadcasted_iota(jnp.int32, (t, width), 0)
    u = lax.broadcasted_iota(jnp.int32, (t, width), 1)
    dc = i - ((u - off) * CMP_STRIDE + CMP_BLOCK - 1)
    strip = jnp.where(dc >= 0, _bias_lookup(tbl_ref, head, dc), NEG_INF)
    for q in range(n_tiles):
        lo = off - q * per_tile
        bc_ref[0, q * t:(q + 1) * t, :] = strip[:, lo:lo + N_CHUNKS_PAD]


def _relbias(tbl, s):
    return pl.pallas_call(
        _relbias_body,
        grid=(NSA_HEADS,),
        in_specs=[pl.BlockSpec(memory_space=pltpu.SMEM)],
        out_specs=(pl.BlockSpec((1, ATT_TILE, 2 * ATT_TILE), lambda h: (h, 0, 0)),
                   pl.BlockSpec((1, s, N_CHUNKS_PAD), lambda h: (h, 0, 0))),
        out_shape=(jax.ShapeDtypeStruct((NSA_HEADS, ATT_TILE, 2 * ATT_TILE), F32),
                   jax.ShapeDtypeStruct((NSA_HEADS, s, N_CHUNKS_PAD), F32)),
        compiler_params=_params(("parallel",)),
        name="relbias",
    )(tbl)


def _lane_fold(x, op):
    return functools.reduce(op, [x[:, i * LANES:(i + 1) * LANES] for i in range(x.shape[1] // LANES)])


def _lane_tile(x, width):
    return jnp.concatenate([x] * (width // LANES), axis=1)


def _row_max_tile(rmax):
    return jnp.broadcast_to(jnp.max(rmax, axis=-1, keepdims=True), rmax.shape)


def _normalize(acc):
    return (acc / pltpu.roll(acc, HEAD_DIM, axis=1))[:, :HEAD_DIM]


def _split3(x):
    hi = x.astype(BF16)
    r = x - hi.astype(F32)
    mid = r.astype(BF16)
    lo = (r - mid.astype(F32)).astype(BF16)
    return hi, mid, lo


def _nsa_body(q_ref, kc_ref, vc_ref, ks_ref, vs_ref, kw_ref, vw_ref, gate_a_ref, gate_b_ref, toep_ref,
              bc_ref, ov_ref, o_ref, s_ref, near_ref, mb_ref, acc_ref, part_ref, *, n_tiles):
    t = ATT_TILE
    j4 = NSA_HPG
    rows = j4 * t
    groups = range(NSA_KV_GROUPS)
    step = pl.program_id(1)
    n_sel = n_tiles * (t // SEL_BLOCK)

    def heads(x):
        return x.reshape(j4, t, x.shape[-1])

    def flat(x):
        return x.reshape(rows, x.shape[-1])

    def key_tile(ref, g, n):
        return ref[0, g, pl.ds(pl.multiple_of(n * t, t), t), :]

    def gated(gate_ref, g, branch, o):
        gate = jax.nn.sigmoid(gate_ref[0, g])
        return [gate[:, branch * j4 + j:branch * j4 + j + 1] * o[j * t:(j + 1) * t] for j in range(j4)]

    def probs(par, g, s):
        return jnp.exp(s - _lane_tile(mb_ref[par, g], t)).astype(BF16)

    def open_tile(m, par):
        m1 = jnp.maximum(m - 1, 0)
        m2 = jnp.maximum(m - 2, 0)
        no_prev = jnp.where(m >= 1, 0.0, NEG_INF)
        qs = []
        for g in groups:
            qt = q_ref[0, :, g * j4 * HEAD_DIM:(g + 1) * j4 * HEAD_DIM]
            qs.append(jnp.concatenate([qt[:, j * HEAD_DIM:(j + 1) * HEAD_DIM] for j in range(j4)], axis=0))

        def biases(g):
            toep = toep_ref[g * j4:(g + 1) * j4]
            return toep[:, :, t:], toep[:, :, :t] + no_prev

        cmp_out = [_nsa_compressed(qs[g], kc_ref[0, g], vc_ref[0, g], bc_ref[g * j4:(g + 1) * j4],
                                   ov_ref[...], m, n_sel) for g in groups]
        part = []
        for g in groups:
            diag_bias, prev_bias = biases(g)
            ii = lax.broadcasted_iota(jnp.int32, (t, t), 0)
            jj = lax.broadcasted_iota(jnp.int32, (t, t), 1)
            tri = jnp.where(jj > ii, 0.0, NEG_INF) + jnp.where(m >= 2, 0.0, NEG_INF)
            w0 = flat(heads(_dot_nt_pair(qs[g], key_tile(kw_ref, g, m))) + diag_bias)
            w1 = flat(heads(_dot_nt_pair(qs[g], key_tile(kw_ref, g, m1))) + prev_bias)
            w2 = flat(heads(_dot_nt_pair(qs[g], key_tile(kw_ref, g, m2))) + tri[None])
            wmax = _lane_tile(_row_max_tile(functools.reduce(
                jnp.maximum, [_lane_fold(w, jnp.maximum) for w in (w0, w1, w2)])), t)
            acc = sum(_dot_pair(jnp.exp(w - wmax).astype(BF16), key_tile(vw_ref, g, n))
                      for w, n in ((w0, m), (w1, m1), (w2, m2)))
            part += [a + b for a, b in zip(gated(gate_a_ref, g, 0, cmp_out[g][0]),
                                           gated(gate_a_ref, g, 2, _normalize(acc)))]
        part_ref[par] = jnp.concatenate(part, axis=1)

        q_aug = []
        for g in groups:
            diag_bias, prev_bias = biases(g)
            q_aug.append(jnp.concatenate([qs[g], jnp.concatenate([cmp_out[g][1]] * j4, axis=0)], axis=1))
            s0 = flat(heads(_dot_nt_pair(q_aug[g], key_tile(ks_ref, g, m))) + diag_bias)
            s1 = flat(heads(_dot_nt_pair(q_aug[g], key_tile(ks_ref, g, m1))) + prev_bias)
            s2 = _dot_nt_pair(q_aug[g], key_tile(ks_ref, g, m2))
            near_ref[par, g, 0] = s0
            near_ref[par, g, 1] = s1
            s_ref[g, m2] = s2
            far = jnp.where(m >= 2, _lane_fold(s2, jnp.maximum), NEG_INF)
            mb_ref[par, g] = jnp.maximum(jnp.maximum(_lane_fold(s0, jnp.maximum),
                                                     _lane_fold(s1, jnp.maximum)), far)
        return q_aug

    def open_far(par, q_aug, n):
        for g in groups:
            s = _dot_nt_pair(q_aug[g], key_tile(ks_ref, g, n))
            s_ref[g, n] = s
            mb_ref[par, g] = jnp.maximum(mb_ref[par, g], _lane_fold(s, jnp.maximum))

    def close_tile(m, par):
        m1 = jnp.maximum(m - 1, 0)
        for g in groups:
            mb_ref[par, g] = _row_max_tile(mb_ref[par, g])
            acc_ref[par, g] = (_dot_pair(probs(par, g, near_ref[par, g, 0]), key_tile(vs_ref, g, m))
                               + _dot_pair(probs(par, g, near_ref[par, g, 1]), key_tile(vs_ref, g, m1)))

    def close_far(par, n):
        for g in groups:
            acc_ref[par, g] += _dot_pair(probs(par, g, s_ref[g, n]), key_tile(vs_ref, g, n))

    def finish(par):
        outs = []
        for g in groups:
            outs += gated(gate_b_ref, g, 1, _normalize(acc_ref[par, g]))
        o_ref[0] = (part_ref[par] + jnp.concatenate(outs, axis=1)).astype(BF16)

    @pl.when(step == 0)
    def _():
        open_tile(step, 0)

    @pl.when(step == 1)
    def _():
        open_tile(step, 1)
        close_tile(step - 1, 0)

    for par in (0, 1):
        @pl.when(jnp.logical_and(jnp.logical_and(step >= 2, step < n_tiles), step % 2 == par))
        def _(par=par):
            finish(par)
            q_aug = open_tile(step, par)
            close_tile(step - 1, 1 - par)

            @pl.loop(0, step - 2)
            def _(n):
                close_far(1 - par, n)
                open_far(par, q_aug, n)

    last = (n_tiles - 1) % 2

    @pl.when(step == n_tiles)
    def _():
        finish(1 - last)
        close_tile(step - 1, last)

        @pl.loop(0, step - 2)
        def _(n):
            close_far(last, n)

    @pl.when(step == n_tiles + 1)
    def _():
        finish(last)


def _nsa_compressed(q, kc, vc, bc, ov, m, n_sel):
    t = ATT_TILE
    j4 = NSA_HPG

    def heads(x):
        return x.reshape(j4, t, x.shape[-1])

    def flat(x):
        return x.reshape(j4 * t, x.shape[-1])

    sc = heads(_dot_nt_pair(q, kc)) + bc
    valid = bc > 0.5 * NEG_INF
    e = jnp.where(valid, jnp.exp(sc - jnp.max(sc, axis=-1, keepdims=True)), 0.0)
    l = jnp.sum(e, axis=-1, keepdims=True)
    pc = jnp.where(l > 0.0, e / l, 0.0)
    o_c = _dot_pair(flat(pc).astype(BF16), vc)
    psum = pc[0] + pc[1] + pc[2] + pc[3]
    imp = sum(_dot(part, ov) for part in _split3(psum)).T[:n_sel]
    blk = lax.broadcasted_iota(jnp.int32, (n_sel, t), 0)
    tpos = m * t + lax.broadcasted_iota(jnp.int32, (n_sel, t), 1)
    cur = tpos // SEL_BLOCK
    bonus = jnp.where(blk == 0, FORCE_BONUS,
                      jnp.where(blk == cur, FORCE_BONUS, jnp.where(blk == cur - 1, FORCE_BONUS, 0.0)))
    imp = jnp.where(blk * SEL_BLOCK <= tpos, imp + bonus, NEG_INF)
    rank = jnp.zeros((n_sel, t), F32)
    for i in range(n_sel):
        row = imp[i:i + 1, :]
        before = jnp.where(blk > i, 1.0, 0.0)
        rank = rank + jnp.where(row > imp, 1.0, jnp.where(row == imp, before, 0.0))
    selb = jnp.where(rank < float(SEL_TOP_N), 0.0, NEG_INF)
    selb = jnp.concatenate([selb, jnp.zeros((LANES - n_sel, t), F32)], axis=0).T
    return o_c, selb[:, :HEAD_DIM].astype(BF16)


def _overlap_table(s):
    n_sel = s // SEL_BLOCK
    nc = N_CHUNKS_PAD
    c0 = np.arange(nc)[:, None] * CMP_STRIDE
    s0 = np.arange(n_sel)[None, :] * SEL_BLOCK
    ov = np.clip(np.minimum(c0 + CMP_BLOCK, s0 + SEL_BLOCK) - np.maximum(c0, s0), 0, None) / CMP_STRIDE
    ov[nc - 1] = 0.0
    return jnp.asarray(np.pad(ov, ((0, 0), (0, LANES - n_sel))), BF16)


def _nsa(qn, kc, vc, ks, vs, kw, vw, gates, toep, bc):
    b, s, _ = qn.shape
    t = ATT_TILE
    assert s // SEL_BLOCK <= HEAD_DIM
    ov = _overlap_table(s)
    rows = NSA_HPG * t
    ng = NSA_KV_GROUPS
    nq = s // t
    assert nq >= 2
    seq = lambda n, w=HEAD_DIM: pl.BlockSpec((1, ng, n, w), lambda i, j: (i, 0, 0, 0))
    opened = lambda j: jnp.minimum(j, nq - 1)
    closed = lambda j: jnp.maximum(j - 2, 0)
    return pl.pallas_call(
        functools.partial(_nsa_body, n_tiles=nq),
        grid=(b, nq + 2),
        in_specs=[pl.BlockSpec((1, t, NSA_Q), lambda i, j: (i, opened(j), 0)),
                  seq(N_CHUNKS_PAD), seq(N_CHUNKS_PAD),
                  seq(s, 2 * HEAD_DIM), seq(s, 2 * HEAD_DIM), seq(s), seq(s, 2 * HEAD_DIM),
                  pl.BlockSpec((1, ng, t, GATE_PAD), lambda i, j: (i, 0, opened(j), 0)),
                  pl.BlockSpec((1, ng, t, GATE_PAD), lambda i, j: (i, 0, closed(j), 0)),
                  _resident((NSA_HEADS, t, 2 * t)),
                  pl.BlockSpec((NSA_HEADS, t, N_CHUNKS_PAD), lambda i, j: (0, opened(j), 0)),
                  _resident(ov.shape)],
        out_specs=pl.BlockSpec((1, t, NSA_Q), lambda i, j: (i, closed(j), 0)),
        out_shape=jax.ShapeDtypeStruct((b, s, NSA_Q), BF16),
        scratch_shapes=[pltpu.VMEM((ng, nq - 2, rows, t), F32),
                        pltpu.VMEM((2, ng, 2, rows, t), F32),
                        pltpu.VMEM((2, ng, rows, LANES), F32),
                        pltpu.VMEM((2, ng, rows, 2 * HEAD_DIM), F32),
                        pltpu.VMEM((2, t, NSA_Q), F32)],
        compiler_params=_params(("parallel", "arbitrary")),
        name="nsa",
    )(qn, kc, vc, ks, vs, kw, vw, gates, gates, toep, bc, ov)


def _fcum_body(f_ref, b_ref, o_ref):
    z = f_ref[0] + b_ref[...]
    x = -(jnp.maximum(-z, 0.0) + jnp.log1p(jnp.exp(-jnp.abs(z))))
    n = x.shape[-1]
    lane = lax.broadcasted_iota(jnp.int32, x.shape, 1)
    sh = 1
    while sh < n:
        x = x + jnp.where(lane >= sh, pltpu.roll(x, sh, axis=1), 0.0)
        sh *= 2
    o_ref[0] = x


def _fcum(f_t, b_forget):
    b, h, s = f_t.shape
    blk = pl.BlockSpec((1, h, s), lambda i: (i, 0, 0))
    return pl.pallas_call(
        _fcum_body,
        grid=(b,),
        in_specs=[blk, _resident((h, 1))],
        out_specs=blk,
        out_shape=jax.ShapeDtypeStruct((b, h, s), F32),
        compiler_params=_params(("parallel",)),
        name="fcum",
    )(f_t, b_forget.reshape(h, 1))


FOX_PAIR = 2
FOX_TILE = 512


def _causal_bias(t):
    i = np.arange(t)
    return jnp.asarray(np.where(i[None, :] <= i[:, None], 0.0, NEG_INF), F32)


def _fox_body(q_ref, k_ref, v_ref, c_ref, cm_ref, o_ref, s_ref, mb_ref, acc_ref):
    t = FOX_TILE
    heads = range(FOX_PAIR)
    nq = q_ref.shape[2] // t

    def keys(n):
        return pl.ds(pl.multiple_of(n * t, t), t)

    def scores(m, hh, n):
        q = q_ref[0, hh, m * t:(m + 1) * t, :]
        return _dot_nt(q, k_ref[0, hh, keys(n), :]) - c_ref[0, hh, :, keys(n)]

    def scores_diag(m):
        for hh in heads:
            s = scores(m, hh, m) + cm_ref[...]
            s_ref[m % 2, hh, m] = s
            mb_ref[m % 2, hh] = _lane_fold(s, jnp.maximum)

    def scores_far(m, n):
        for hh in heads:
            s = scores(m, hh, n)
            s_ref[m % 2, hh, n] = s
            mb_ref[m % 2, hh] = jnp.maximum(mb_ref[m % 2, hh], _lane_fold(s, jnp.maximum))

    def finish_max(m):
        for hh in heads:
            mb_ref[m % 2, hh] = _row_max_tile(mb_ref[m % 2, hh])

    def weigh(m, n):
        for hh in heads:
            p = jnp.exp(s_ref[m % 2, hh, n] - _lane_tile(mb_ref[m % 2, hh], t)).astype(BF16)
            acc_ref[hh] += _dot(p, v_ref[0, hh, keys(n), :])

    scores_diag(0)
    finish_max(0)
    for m in range(nq):
        ahead = m + 1 < nq
        if ahead:
            scores_diag(m + 1)
        for hh in heads:
            acc_ref[hh] = jnp.zeros((t, 2 * HEAD_DIM), F32)

        @pl.loop(0, m + 1)
        def _(n, m=m, ahead=ahead):
            weigh(m, n)
            if ahead:
                scores_far(m + 1, n)

        if ahead:
            finish_max(m + 1)
        o_ref[0, m * t:(m + 1) * t, :] = jnp.concatenate(
            [_normalize(acc_ref[hh]) for hh in heads], axis=1).astype(BF16)


def _fox(qf, kf, vf, c):
    b, h, s, _ = qf.shape
    t = FOX_TILE
    seq = lambda w: pl.BlockSpec((1, FOX_PAIR, s, w), lambda i, p: (i, p, 0, 0))
    return pl.pallas_call(
        _fox_body,
        grid=(b, h // FOX_PAIR),
        in_specs=[seq(HEAD_DIM), seq(HEAD_DIM), seq(2 * HEAD_DIM),
                  pl.BlockSpec((1, FOX_PAIR, 1, s), lambda i, p: (i, p, 0, 0)),
                  _resident((t, t))],
        out_specs=pl.BlockSpec((1, s, FOX_PAIR * HEAD_DIM), lambda i, p: (i, 0, p)),
        out_shape=jax.ShapeDtypeStruct((b, s, h * HEAD_DIM), BF16),
        scratch_shapes=[pltpu.VMEM((2, FOX_PAIR, s // t, t, t), F32),
                        pltpu.VMEM((2, FOX_PAIR, t, LANES), F32),
                        pltpu.VMEM((FOX_PAIR, t, 2 * HEAD_DIM), F32)],
        compiler_params=_params(("parallel", "parallel")),
        name="fox",
    )(qf, kf, vf, c.reshape(b, h, 1, s), _causal_bias(t))


def _mixout_body(x_ref, g_ref, on_ref, of_ref, wa_ref, wb_ref, wun_ref, wuf_ref, wo_ref, o_ref):
    x = x_ref[...]
    h = _rms(x, g_ref[...]).astype(BF16)
    y = (jax.nn.sigmoid(_dot(h, wa_ref[...])) * _dot(on_ref[...], wun_ref[...])
         + jax.nn.sigmoid(_dot(h, wb_ref[...])) * _dot(of_ref[...], wuf_ref[...]))
    o_ref[...] = x + _dot(y.astype(BF16), wo_ref[...])


def _mixout(x2d, g, o_nsa, o_fox, w_a, w_b, w_un, w_uf, w_o):
    n = x2d.shape[0]
    tok = lambda c: pl.BlockSpec((TOK_TILE, c), lambda i: (i, 0))
    return pl.pallas_call(
        _mixout_body,
        grid=(n // TOK_TILE,),
        in_specs=[tok(D_MODEL), _resident((1, D_MODEL)), tok(NSA_Q), tok(FOX_W),
                  _resident(w_a.shape), _resident(w_b.shape), _resident(w_un.shape),
                  _resident(w_uf.shape), _resident(w_o.shape)],
        out_specs=tok(D_MODEL),
        out_shape=jax.ShapeDtypeStruct((n, D_MODEL), F32),
        compiler_params=_params(("parallel",)),
        name="mixout",
    )(x2d, g.reshape(1, D_MODEL), o_nsa, o_fox, w_a, w_b, w_un, w_uf, w_o)


def _memkv_body(m_ref, g_ref, w_ref, k_ref, v_ref):
    h = _rms(m_ref[0], g_ref[...]).astype(BF16)
    z = _dot(h, w_ref[...])
    k_ref[0] = z[:, :D_MODEL].astype(BF16)
    v_ref[0] = z[:, D_MODEL:].astype(BF16)


def _memkv(mem, g, w_kv):
    b, ml, _ = mem.shape
    blk = pl.BlockSpec((1, ml, D_MODEL), lambda i: (i, 0, 0))
    sd = jax.ShapeDtypeStruct((b, ml, D_MODEL), BF16)
    return pl.pallas_call(
        _memkv_body,
        grid=(b,),
        in_specs=[blk, _resident((1, D_MODEL)), _resident(w_kv.shape)],
        out_specs=(blk, blk),
        out_shape=(sd, sd),
        compiler_params=_params(("parallel",)),
        name="memkv",
    )(mem, g.reshape(1, D_MODEL), w_kv)


def _memattn_body(x_ref, g_ref, k_ref, v_ref, wq_ref, wo_ref, o_ref):
    x = x_ref[0]
    h = _rms(x, g_ref[...]).astype(BF16)
    q = (_dot(h, wq_ref[...]) * (MEM_HEAD_DIM ** -0.5)).astype(BF16)
    outs = []
    for hd in range(MEM_HEADS):
        cols = slice(hd * MEM_HEAD_DIM, (hd + 1) * MEM_HEAD_DIM)
        s = _dot_nt(q[:, cols], k_ref[0, :, cols])
        e = jnp.exp(s - jnp.max(s, axis=-1, keepdims=True))
        p = e / jnp.sum(e, axis=-1, keepdims=True)
        outs.append(_dot(p.astype(BF16), v_ref[0, :, cols]))
    o = jnp.concatenate(outs, axis=1).astype(BF16)
    o_ref[0] = x + _dot(o, wo_ref[...])


def _memattn(x, g, k, v, w_q, w_o):
    b, s, _ = x.shape
    ml = k.shape[1]
    tok = pl.BlockSpec((1, TOK_TILE, D_MODEL), lambda i, j: (i, j, 0))
    kvb = pl.BlockSpec((1, ml, D_MODEL), lambda i, j: (i, 0, 0))
    return pl.pallas_call(
        _memattn_body,
        grid=(b, s // TOK_TILE),
        in_specs=[tok, _resident((1, D_MODEL)), kvb, kvb, _resident(w_q.shape), _resident(w_o.shape)],
        out_specs=tok,
        out_shape=jax.ShapeDtypeStruct((b, s, D_MODEL), F32),
        compiler_params=_params(("parallel", "parallel")),
        name="memattn",
    )(x, g.reshape(1, D_MODEL), k, v, w_q, w_o)


def kernel(x, mem, rel_bias_table, ffn1_norm, ffn1_w_gate, ffn1_w_up, ffn1_w_down, mix_norm, mix_w_in, mix_b_forget, cmp_pos_k, cmp_pos_v, cmp_k_w1, cmp_k_w2, cmp_v_w1, cmp_v_w2, w_up_nsa, w_up_fox, mix_w_out, mem_q_norm, mem_kv_norm, mem_w_q, mem_w_kv, mem_w_o, ffn2_norm, ffn2_w_gate, ffn2_w_up, ffn2_w_down, final_norm):
    b, s, d = x.shape
    depth = ffn1_norm.shape[0]
    bf = lambda w: w.astype(BF16)
    toep, bias_c = _relbias(rel_bias_table, s)
    x = x.reshape(b * s, d)
    for l in range(depth):
        last = l == depth - 1
        x = _ffn(x, ffn1_norm[l], ffn1_w_gate[l], ffn1_w_up[l], ffn1_w_down[l])

        w_nsa, w_fox, w_small, w_a, w_b = _pack_w_in(mix_w_in[l])
        qn, kc, vc, ks, vs, kw, vw, qf, kf, vf, gates, flog = _inproj(
            x.reshape(b, s, d), mix_norm[l], w_nsa, w_fox, w_small)
        kcc, vcc = _compress(kc, vc, cmp_pos_k[l], cmp_pos_v[l], cmp_k_w1[l], cmp_k_w2[l],
                             cmp_v_w1[l], cmp_v_w2[l])
        o_nsa = _nsa(qn, kcc, vcc, ks, vs, kw, vw, gates, toep, bias_c)
        c = _fcum(flog, mix_b_forget[l])
        o_fox = _fox(qf, kf, vf, c)
        x = _mixout(x, mix_norm[l], o_nsa.reshape(b * s, NSA_Q), o_fox.reshape(b * s, FOX_W),
                    w_a, w_b, bf(w_up_nsa[l]), bf(w_up_fox[l]), bf(mix_w_out[l]))

        mk, mv = _memkv(mem, mem_kv_norm[l], bf(mem_w_kv[l]))
        x = _memattn(x.reshape(b, s, d), mem_q_norm[l], mk, mv, bf(mem_w_q[l]), bf(mem_w_o[l]))
        x = _ffn(x.reshape(b * s, d), ffn2_norm[l], ffn2_w_gate[l], ffn2_w_up[l],
                 ffn2_w_down[l], final_g=final_norm if last else None)
    return x.reshape(b, s, d)
```

```python
import functools
import math

import numpy as np
import jax
import jax.numpy as jnp
from jax import lax
from jax.experimental import pallas as pl
from jax.experimental.pallas import tpu as pltpu

D_MODEL = 1024
D_FF = 2816
HEAD_DIM = 64
NSA_HEADS = 8
NSA_KV_GROUPS = 2
NSA_HPG = NSA_HEADS // NSA_KV_GROUPS
CMP_BLOCK = 32
CMP_STRIDE = 16
CMP_HIDDEN = 256
SEL_BLOCK = 64
SEL_TOP_N = 16
WINDOW = 512
FOX_HEADS = 8
MEM_HEADS = 4
MEM_HEAD_DIM = D_MODEL // MEM_HEADS
NUM_BUCKETS = 32
MAX_DISTANCE = 128
RMS_EPS = 1e-6
NEG_INF = -1e30
FORCE_BONUS = 1e4

NSA_Q = NSA_HEADS * HEAD_DIM
NSA_KV = NSA_KV_GROUPS * HEAD_DIM
FOX_W = FOX_HEADS * HEAD_DIM

LANES = 128
VMEM_LIMIT = 56 * 1024 * 1024

BF16 = jnp.bfloat16
F32 = jnp.float32

ATT_TILE = 256
TOK_TILE = 1024
FFN_TILE = 1024
FF_CHUNK = 256


def _bucket_thresholds():
    n = np.arange(0, 4 * MAX_DISTANCE)
    exact = NUM_BUCKETS // 2
    large = exact + (np.log(np.maximum(n, 1) / exact) / math.log(MAX_DISTANCE / exact)
                     * (NUM_BUCKETS - exact)).astype(np.int64)
    bucket = np.where(n < exact, n, np.minimum(large, NUM_BUCKETS - 1))
    assert np.all(np.diff(bucket) >= 0)
    return [int(np.argmax(bucket >= k)) for k in range(1, NUM_BUCKETS)]


BUCKET_THRESHOLDS = _bucket_thresholds()


def _dot(a, b):
    return jnp.dot(a, b, preferred_element_type=F32)


def _dot_nt(a, b):
    return lax.dot_general(a, b, (((1,), (1,)), ((), ())), preferred_element_type=F32)


def _row_halves(fn, a, b):
    h = a.shape[0] // 2
    return jnp.concatenate([fn(a[:h], b), fn(a[h:], b)], axis=0)


def _dot_pair(a, b):
    return _row_halves(_dot, a, b)


def _dot_nt_pair(a, b):
    return _row_halves(_dot_nt, a, b)


def _rms(x, g):
    return x * lax.rsqrt(jnp.mean(x * x, axis=-1, keepdims=True) + RMS_EPS) * g


def _resident(shape):
    nd = len(shape)
    return pl.BlockSpec(shape, lambda *_: (0,) * nd, pipeline_mode=pl.Buffered(1))


def _params(sem):
    return pltpu.CompilerParams(dimension_semantics=sem, vmem_limit_bytes=VMEM_LIMIT)


def _ffn_body(x_ref, g_ref, wg_hbm, wu_hbm, wd_hbm, *rest, final):
    if final:
        fg_ref, o_ref, wg_ref, wu_ref, wd_ref, sg_ref, su_ref, sd_ref, sem = rest
    else:
        o_ref, wg_ref, wu_ref, wd_ref, sg_ref, su_ref, sd_ref, sem = rest
    n_chunks = D_FF // FF_CHUNK

    def chunk_copies(c):
        cols = slice(c * FF_CHUNK, (c + 1) * FF_CHUNK)
        slot = c % 2
        return (pltpu.make_async_copy(wg_hbm.at[:, cols], sg_ref.at[slot], sem.at[0, slot]),
                pltpu.make_async_copy(wu_hbm.at[:, cols], su_ref.at[slot], sem.at[1, slot]),
                pltpu.make_async_copy(wd_hbm.at[cols, :], sd_ref.at[slot], sem.at[2, slot]))

    @pl.when(pl.program_id(0) == 0)
    def _():
        for cp in chunk_copies(0):
            cp.start()
        for c in range(n_chunks):
            sl = slice(c * FF_CHUNK, (c + 1) * FF_CHUNK)
            if c + 1 < n_chunks:
                for cp in chunk_copies(c + 1):
                    cp.start()
            for cp in chunk_copies(c):
                cp.wait()
            wg_ref[:, sl] = sg_ref[c % 2].astype(BF16)
            wu_ref[:, sl] = su_ref[c % 2].astype(BF16)
            wd_ref[sl, :] = sd_ref[c % 2].astype(BF16)

    x = x_ref[...]
    h = _rms(x, g_ref[...]).astype(BF16)
    acc = jnp.zeros(x.shape, F32)
    for c in range(n_chunks):
        sl = slice(c * FF_CHUNK, (c + 1) * FF_CHUNK)
        a = _dot(h, wg_ref[:, sl])
        b = _dot(h, wu_ref[:, sl])
        t = (a * jax.nn.sigmoid(a)) * b
        acc = acc + _dot(t.astype(BF16), wd_ref[sl, :])
    y = x + 0.5 * acc
    if final:
        y = _rms(y, fg_ref[...])
    o_ref[...] = y


def _ffn(x2d, g, wg, wu, wd, final_g=None):
    n = x2d.shape[0]
    final = final_g is not None
    tok = pl.BlockSpec((FFN_TILE, D_MODEL), lambda i: (i, 0))
    hbm = pl.BlockSpec(memory_space=pl.ANY)
    in_specs = [tok, _resident((1, D_MODEL)), hbm, hbm, hbm]
    args = [x2d, g.reshape(1, D_MODEL), wg, wu, wd]
    if final:
        in_specs.append(_resident((1, D_MODEL)))
        args.append(final_g.reshape(1, D_MODEL))
    return pl.pallas_call(
        functools.partial(_ffn_body, final=final),
        grid=(n // FFN_TILE,),
        in_specs=in_specs,
        out_specs=tok,
        out_shape=jax.ShapeDtypeStruct((n, D_MODEL), F32),
        scratch_shapes=[pltpu.VMEM((D_MODEL, D_FF), BF16), pltpu.VMEM((D_MODEL, D_FF), BF16),
                        pltpu.VMEM((D_FF, D_MODEL), BF16),
                        pltpu.VMEM((2, D_MODEL, FF_CHUNK), F32), pltpu.VMEM((2, D_MODEL, FF_CHUNK), F32),
                        pltpu.VMEM((2, FF_CHUNK, D_MODEL), F32),
                        pltpu.SemaphoreType.DMA((3, 2))],
        compiler_params=_params(("arbitrary",)),
        name="ffn_final" if final else "ffn",
    )(*args)


_C_CMP = NSA_Q
_C_KV = _C_CMP + 2 * NSA_KV
_W_NSA = _C_KV + 4 * NSA_KV
GATE_PAD = 16
_C_FLOG = NSA_KV_GROUPS * GATE_PAD
N_CHUNKS_PAD = 128
CHUNK_W = CMP_STRIDE * NSA_KV


def _inproj_body(x_ref, g_ref, wn_ref, wf_ref, ws_ref, blk_ref, qn_ref, kc_ref, vc_ref, ks_ref, vs_ref,
                 kw_ref, vw_ref, qf_ref, kf_ref, vf_ref, gate_ref, fl_ref, cmp_ref):
    h = _rms(x_ref[0], g_ref[...]).astype(BF16)
    z = _dot(h, wn_ref[...])
    qn_ref[0] = (z[:, :_C_CMP] * (HEAD_DIM ** -0.5)).astype(BF16)
    rows = z.shape[0] // CMP_STRIDE
    for i, ref in enumerate((kc_ref, vc_ref)):
        cmp_ref[i] = z[:, _C_CMP + i * NSA_KV:_C_CMP + (i + 1) * NSA_KV]
        for r in range(CMP_STRIDE):
            tok = cmp_ref[i, pl.ds(r, rows, stride=CMP_STRIDE), :]
            ref[0, :, r * NSA_KV:(r + 1) * NSA_KV] = tok.astype(BF16)
    ones = jnp.ones((z.shape[0], HEAD_DIM), BF16)
    for i, (ref, extra) in enumerate(((ks_ref, blk_ref[...]), (vs_ref, ones), (kw_ref, None),
                                      (vw_ref, ones))):
        for g in range(NSA_KV_GROUPS):
            c0 = _C_KV + i * NSA_KV + g * HEAD_DIM
            val = z[:, c0:c0 + HEAD_DIM].astype(BF16)
            ref[0, g] = val if extra is None else jnp.concatenate([val, extra], axis=1)
    zf = _dot(h, wf_ref[...])
    for hd in range(FOX_HEADS):
        c0 = hd * HEAD_DIM
        qf_ref[0, hd] = (zf[:, c0:c0 + HEAD_DIM] * (HEAD_DIM ** -0.5)).astype(BF16)
        kf_ref[0, hd] = zf[:, c0 + FOX_W:c0 + FOX_W + HEAD_DIM].astype(BF16)
        v = zf[:, c0 + 2 * FOX_W:c0 + 2 * FOX_W + HEAD_DIM].astype(BF16)
        vf_ref[0, hd] = jnp.concatenate([v, ones], axis=1)
    zs = _dot(h, ws_ref[...])
    for g in range(NSA_KV_GROUPS):
        gate_ref[0, g] = zs[:, g * GATE_PAD:(g + 1) * GATE_PAD]
    fl_ref[0] = zs.T[_C_FLOG:_C_FLOG + FOX_HEADS]


_IN_COLS = np.cumsum((0, NSA_Q, NSA_KV, NSA_KV, NSA_KV, NSA_KV, NSA_KV, NSA_KV, 3 * NSA_HEADS,
                      FOX_W, FOX_W, FOX_W, FOX_HEADS, D_MODEL, D_MODEL)).tolist()
W_PREP_ROWS = 256


def _wprep_body(w_ref, wn_ref, wf_ref, ws_ref, wa_ref, wb_ref):
    w = w_ref[...]
    c = _IN_COLS
    rows = w.shape[0]
    wn_ref[...] = w[:, :c[7]].astype(BF16)
    wf_ref[...] = w[:, c[8]:c[11]].astype(BF16)
    wa_ref[...] = w[:, c[12]:c[13]].astype(BF16)
    wb_ref[...] = w[:, c[13]:c[14]].astype(BF16)
    small = []
    for g in range(NSA_KV_GROUPS):
        for br in range(3):
            c0 = c[7] + br * NSA_HEADS + g * NSA_HPG
            small.append(w[:, c0:c0 + NSA_HPG])
        small.append(jnp.zeros((rows, GATE_PAD - 3 * NSA_HPG), F32))
    small.append(w[:, c[11]:c[12]])
    small.append(jnp.zeros((rows, LANES - _C_FLOG - FOX_HEADS), F32))
    ws_ref[...] = jnp.concatenate(small, axis=1).astype(BF16)


def _pack_w_in(w_in):
    k, n = w_in.shape
    c = _IN_COLS
    assert n == c[14]
    widths = (c[7], c[11] - c[8], LANES, D_MODEL, D_MODEL)
    return pl.pallas_call(
        _wprep_body,
        grid=(k // W_PREP_ROWS,),
        in_specs=[pl.BlockSpec((W_PREP_ROWS, n), lambda i: (i, 0))],
        out_specs=tuple(pl.BlockSpec((W_PREP_ROWS, w), lambda i: (i, 0)) for w in widths),
        out_shape=tuple(jax.ShapeDtypeStruct((k, w), BF16) for w in widths),
        compiler_params=_params(("parallel",)),
        name="wprep",
    )(w_in)


def _inproj(x, g, w_nsa, w_fox, w_small):
    b, s, _ = x.shape
    grid = (b, s // TOK_TILE)
    tok = lambda c: pl.BlockSpec((1, TOK_TILE, c), lambda i, j: (i, j, 0))
    heads = lambda nh, w=HEAD_DIM: pl.BlockSpec((1, nh, TOK_TILE, w), lambda i, j: (i, 0, j, 0))
    chunks = pl.BlockSpec((1, TOK_TILE // CMP_STRIDE, CHUNK_W), lambda i, j: (i, j, 0))
    sds = jax.ShapeDtypeStruct
    kv = sds((b, NSA_KV_GROUPS, s, HEAD_DIM), BF16)
    kv2 = sds((b, NSA_KV_GROUPS, s, 2 * HEAD_DIM), BF16)
    fx = sds((b, FOX_HEADS, s, HEAD_DIM), BF16)
    cmp_in = sds((b, s // CMP_STRIDE, CHUNK_W), BF16)
    out_shape = (sds((b, s, NSA_Q), BF16), cmp_in, cmp_in,
                 kv2, kv2, kv, kv2, fx, fx, sds((b, FOX_HEADS, s, 2 * HEAD_DIM), BF16),
                 sds((b, NSA_KV_GROUPS, s, GATE_PAD), F32), sds((b, FOX_HEADS, s), F32))
    out_specs = (tok(NSA_Q), chunks, chunks,
                 heads(NSA_KV_GROUPS, 2 * HEAD_DIM), heads(NSA_KV_GROUPS, 2 * HEAD_DIM),
                 heads(NSA_KV_GROUPS), heads(NSA_KV_GROUPS, 2 * HEAD_DIM),
                 heads(FOX_HEADS), heads(FOX_HEADS), heads(FOX_HEADS, 2 * HEAD_DIM),
                 pl.BlockSpec((1, NSA_KV_GROUPS, TOK_TILE, GATE_PAD), lambda i, j: (i, 0, j, 0)),
                 pl.BlockSpec((1, FOX_HEADS, TOK_TILE), lambda i, j: (i, 0, j)))
    key_blk = np.arange(s) // SEL_BLOCK
    blk_onehot = jnp.asarray(key_blk[:, None] == np.arange(HEAD_DIM)[None, :], BF16)
    return pl.pallas_call(
        _inproj_body,
        grid=grid,
        in_specs=[tok(D_MODEL), _resident((1, D_MODEL)), _resident(w_nsa.shape),
                  _resident(w_fox.shape), _resident(w_small.shape),
                  pl.BlockSpec((TOK_TILE, HEAD_DIM), lambda i, j: (j, 0))],
        out_specs=out_specs,
        out_shape=out_shape,
        scratch_shapes=[pltpu.VMEM((2, TOK_TILE, NSA_KV), F32)],
        compiler_params=_params(("parallel", "parallel")),
        name="inproj",
    )(x, g.reshape(1, D_MODEL), w_nsa, w_fox, w_small, blk_onehot)


CMP_BATCH = 4

def _compress_body(xk_ref, xv_ref, pk_ref, pv_ref, wk1_ref, wv1_ref, wk2_ref, wv2_ref, ok_ref, ov_ref):
    nb = xk_ref.shape[0]
    n = nb * N_CHUNKS_PAD
    for x_ref, p_ref, w1_ref, w2_ref, o_ref in ((xk_ref, pk_ref, wk1_ref, wk2_ref, ok_ref),
                                                (xv_ref, pv_ref, wv1_ref, wv2_ref, ov_ref)):
        x = x_ref[...].reshape(n, CHUNK_W)
        a0 = _dot(x, w1_ref[0])
        a1 = _dot(x, w1_ref[1])
        c = _dot(p_ref[0], w1_ref[0]) + _dot(p_ref[1], w1_ref[1])
        pre = a0 + pltpu.roll(a1, n - 1, axis=0) + c[0:1, :]
        hid = jax.nn.gelu(pre).astype(BF16)
        out = _dot(hid, w2_ref[...]).astype(BF16)
        for i in range(nb):
            for g in range(NSA_KV_GROUPS):
                o_ref[i, g] = out[i * N_CHUNKS_PAD:(i + 1) * N_CHUNKS_PAD, g * HEAD_DIM:(g + 1) * HEAD_DIM]


def _pack_compress(pos, w1, w2):
    r = CMP_BLOCK // CMP_STRIDE
    assert NSA_KV_GROUPS == 2
    w1r = w1.astype(BF16).reshape(r, CMP_STRIDE, HEAD_DIM, CMP_HIDDEN)
    z1 = jnp.zeros_like(w1r)
    w1big = jnp.stack([jnp.concatenate([w1r, z1], axis=-1), jnp.concatenate([z1, w1r], axis=-1)], axis=2)
    w1big = w1big.reshape(r, CHUNK_W, NSA_KV_GROUPS * CMP_HIDDEN)
    z2 = jnp.zeros_like(w2)
    w2big = jnp.concatenate([jnp.concatenate([w2, z2], axis=1), jnp.concatenate([z2, w2], axis=1)], axis=0)
    p = pos.reshape(r, CMP_STRIDE, 1, HEAD_DIM)
    p = jnp.broadcast_to(p, (r, CMP_STRIDE, NSA_KV_GROUPS, HEAD_DIM)).reshape(r, 1, CHUNK_W)
    p = jnp.broadcast_to(p, (r, 16, CHUNK_W))
    return p.astype(BF16), w1big.astype(BF16), w2big.astype(BF16)


def _compress(xk, xv, pos_k, pos_v, k_w1, k_w2, v_w1, v_w2):
    b = xk.shape[0]
    assert xk.shape[1:] == (N_CHUNKS_PAD, CHUNK_W)
    pk, wk1, wk2 = _pack_compress(pos_k, k_w1, k_w2)
    pv, wv1, wv2 = _pack_compress(pos_v, v_w1, v_w2)
    nb = CMP_BATCH if b % CMP_BATCH == 0 else 1
    xs = pl.BlockSpec((nb, N_CHUNKS_PAD, CHUNK_W), lambda i: (i, 0, 0))
    os_ = pl.BlockSpec((nb, NSA_KV_GROUPS, N_CHUNKS_PAD, HEAD_DIM), lambda i: (i, 0, 0, 0))
    osd = jax.ShapeDtypeStruct((b, NSA_KV_GROUPS, N_CHUNKS_PAD, HEAD_DIM), BF16)
    return pl.pallas_call(
        _compress_body,
        grid=(b // nb,),
        in_specs=[xs, xs, _resident(pk.shape), _resident(pv.shape), _resident(wk1.shape),
                  _resident(wv1.shape), _resident(wk2.shape), _resident(wv2.shape)],
        out_specs=(os_, os_),
        out_shape=(osd, osd),
        compiler_params=_params(("parallel",)),
        name="compress",
    )(xk, xv, pk, pv, wk1, wv1, wk2, wv2)


def _bias_lookup(tbl_ref, head, dist):
    acc = jnp.full(dist.shape, tbl_ref[0, head], F32)
    for k, thr in enumerate(BUCKET_THRESHOLDS):
        acc = jnp.where(dist >= thr, tbl_ref[k + 1, head], acc)
    return acc - tbl_ref[NUM_BUCKETS - 1, head]


def _relbias_body(tbl_ref, toep_ref, bc_ref):
    head = pl.program_id(0)
    t = ATT_TILE
    i = lax.broadcasted_iota(jnp.int32, (t, 2 * t), 0)
    u = lax.broadcasted_iota(jnp.int32, (t, 2 * t), 1)
    d = i - u + t
    toep_ref[0] = jnp.where(d >= 0, _bias_lookup(tbl_ref, head, d), NEG_INF)
    n_tiles = bc_ref.shape[1] // t
    per_tile = t // CMP_STRIDE
    off = per_tile * (n_tiles - 1)
    width = 2 * N_CHUNKS_PAD
    assert off + N_CHUNKS_PAD <= width
    i = lax.broadcasted_iota(jnp.int32, (t, width), 0)
    u = lax.broadcasted_iota(jnp.int32, (t, width), 1)
    dc = i - ((u - off) * CMP_STRIDE + CMP_BLOCK - 1)
    strip = jnp.where(dc >= 0, _bias_lookup(tbl_ref, head, dc), NEG_INF)
    for q in range(n_tiles):
        lo = off - q * per_tile
        bc_ref[0, q * t:(q + 1) * t, :] = strip[:, lo:lo + N_CHUNKS_PAD]


def _relbias(tbl, s):
    return pl.pallas_call(
        _relbias_body,
        grid=(NSA_HEADS,),
        in_specs=[pl.BlockSpec(memory_space=pltpu.SMEM)],
        out_specs=(pl.BlockSpec((1, ATT_TILE, 2 * ATT_TILE), lambda h: (h, 0, 0)),
                   pl.BlockSpec((1, s, N_CHUNKS_PAD), lambda h: (h, 0, 0))),
        out_shape=(jax.ShapeDtypeStruct((NSA_HEADS, ATT_TILE, 2 * ATT_TILE), F32),
                   jax.ShapeDtypeStruct((NSA_HEADS, s, N_CHUNKS_PAD), F32)),
        compiler_params=_params(("parallel",)),
        name="relbias",
    )(tbl)


def _lane_fold(x, op):
    return functools.reduce(op, [x[:, i * LANES:(i + 1) * LANES] for i in range(x.shape[1] // LANES)])


def _lane_tile(x, width):
    return jnp.concatenate([x] * (width // LANES), axis=1)


def _row_max_tile(rmax):
    return jnp.broadcast_to(jnp.max(rmax, axis=-1, keepdims=True), rmax.shape)


def _normalize(acc):
    return (acc / pltpu.roll(acc, HEAD_DIM, axis=1))[:, :HEAD_DIM]


def _split3(x):
    hi = x.astype(BF16)
    r = x - hi.astype(F32)
    mid = r.astype(BF16)
    lo = (r - mid.astype(F32)).astype(BF16)
    return hi, mid, lo


def _nsa_body(q_ref, kc_ref, vc_ref, ks_ref, vs_ref, kw_ref, vw_ref, gate_a_ref, gate_b_ref, toep_ref,
              bc_ref, ov_ref, o_ref, s_ref, near_ref, mb_ref, acc_ref, part_ref, *, n_tiles):
    t = ATT_TILE
    j4 = NSA_HPG
    rows = j4 * t
    groups = range(NSA_KV_GROUPS)
    step = pl.program_id(1)
    n_sel = n_tiles * (t // SEL_BLOCK)

    def heads(x):
        return x.reshape(j4, t, x.shape[-1])

    def flat(x):
        return x.reshape(rows, x.shape[-1])

    def key_tile(ref, g, n):
        return ref[0, g, pl.ds(pl.multiple_of(n * t, t), t), :]

    def gated(gate_ref, g, branch, o):
        gate = jax.nn.sigmoid(gate_ref[0, g])
        return [gate[:, branch * j4 + j:branch * j4 + j + 1] * o[j * t:(j + 1) * t] for j in range(j4)]

    def probs(par, g, s):
        return jnp.exp(s - _lane_tile(mb_ref[par, g], t)).astype(BF16)

    def open_tile(m, par):
        m1 = jnp.maximum(m - 1, 0)
        m2 = jnp.maximum(m - 2, 0)
        no_prev = jnp.where(m >= 1, 0.0, NEG_INF)
        qs = []
        for g in groups:
            qt = q_ref[0, :, g * j4 * HEAD_DIM:(g + 1) * j4 * HEAD_DIM]
            qs.append(jnp.concatenate([qt[:, j * HEAD_DIM:(j + 1) * HEAD_DIM] for j in range(j4)], axis=0))

        def biases(g):
            toep = toep_ref[g * j4:(g + 1) * j4]
            return toep[:, :, t:], toep[:, :, :t] + no_prev

        cmp_out = [_nsa_compressed(qs[g], kc_ref[0, g], vc_ref[0, g], bc_ref[g * j4:(g + 1) * j4],
                                   ov_ref[...], m, n_sel) for g in groups]
        part = []
        for g in groups:
            diag_bias, prev_bias = biases(g)
            ii = lax.broadcasted_iota(jnp.int32, (t, t), 0)
            jj = lax.broadcasted_iota(jnp.int32, (t, t), 1)
            tri = jnp.where(jj > ii, 0.0, NEG_INF) + jnp.where(m >= 2, 0.0, NEG_INF)
            w0 = flat(heads(_dot_nt_pair(qs[g], key_tile(kw_ref, g, m))) + diag_bias)
            w1 = flat(heads(_dot_nt_pair(qs[g], key_tile(kw_ref, g, m1))) + prev_bias)
            w2 = flat(heads(_dot_nt_pair(qs[g], key_tile(kw_ref, g, m2))) + tri[None])
            wmax = _lane_tile(_row_max_tile(functools.reduce(
                jnp.maximum, [_lane_fold(w, jnp.maximum) for w in (w0, w1, w2)])), t)
            acc = sum(_dot_pair(jnp.exp(w - wmax).astype(BF16), key_tile(vw_ref, g, n))
                      for w, n in ((w0, m), (w1, m1), (w2, m2)))
            part += [a + b for a, b in zip(gated(gate_a_ref, g, 0, cmp_out[g][0]),
                                           gated(gate_a_ref, g, 2, _normalize(acc)))]
        part_ref[par] = jnp.concatenate(part, axis=1)

        q_aug = []
        for g in groups:
            diag_bias, prev_bias = biases(g)
            q_aug.append(jnp.concatenate([qs[g], jnp.concatenate([cmp_out[g][1]] * j4, axis=0)], axis=1))
            s0 = flat(heads(_dot_nt_pair(q_aug[g], key_tile(ks_ref, g, m))) + diag_bias)
            s1 = flat(heads(_dot_nt_pair(q_aug[g], key_tile(ks_ref, g, m1))) + prev_bias)
            s2 = _dot_nt_pair(q_aug[g], key_tile(ks_ref, g, m2))
            near_ref[par, g, 0] = s0
            near_ref[par, g, 1] = s1
            s_ref[g, m2] = s2
            far = jnp.where(m >= 2, _lane_fold(s2, jnp.maximum), NEG_INF)
            mb_ref[par, g] = jnp.maximum(jnp.maximum(_lane_fold(s0, jnp.maximum),
                                                     _lane_fold(s1, jnp.maximum)), far)
        return q_aug

    def open_far(par, q_aug, n):
        for g in groups:
            s = _dot_nt_pair(q_aug[g], key_tile(ks_ref, g, n))
            s_ref[g, n] = s
            mb_ref[par, g] = jnp.maximum(mb_ref[par, g], _lane_fold(s, jnp.maximum))

    def close_tile(m, par):
        m1 = jnp.maximum(m - 1, 0)
        for g in groups:
            mb_ref[par, g] = _row_max_tile(mb_ref[par, g])
            acc_ref[par, g] = (_dot_pair(probs(par, g, near_ref[par, g, 0]), key_tile(vs_ref, g, m))
                               + _dot_pair(probs(par, g, near_ref[par, g, 1]), key_tile(vs_ref, g, m1)))

    def close_far(par, n):
        for g in groups:
            acc_ref[par, g] += _dot_pair(probs(par, g, s_ref[g, n]), key_tile(vs_ref, g, n))

    def finish(par):
        outs = []
        for g in groups:
            outs += gated(gate_b_ref, g, 1, _normalize(acc_ref[par, g]))
        o_ref[0] = (part_ref[par] + jnp.concatenate(outs, axis=1)).astype(BF16)

    @pl.when(step == 0)
    def _():
        open_tile(step, 0)

    @pl.when(step == 1)
    def _():
        open_tile(step, 1)
        close_tile(step - 1, 0)

    for par in (0, 1):
        @pl.when(jnp.logical_and(jnp.logical_and(step >= 2, step < n_tiles), step % 2 == par))
        def _(par=par):
            finish(par)
            q_aug = open_tile(step, par)
            close_tile(step - 1, 1 - par)

            @pl.loop(0, step - 2)
            def _(n):
                close_far(1 - par, n)
                open_far(par, q_aug, n)

    last = (n_tiles - 1) % 2

    @pl.when(step == n_tiles)
    def _():
        finish(1 - last)
        close_tile(step - 1, last)

        @pl.loop(0, step - 2)
        def _(n):
            close_far(last, n)

    @pl.when(step == n_tiles + 1)
    def _():
        finish(last)


def _nsa_compressed(q, kc, vc, bc, ov, m, n_sel):
    t = ATT_TILE
    j4 = NSA_HPG

    def heads(x):
        return x.reshape(j4, t, x.shape[-1])

    def flat(x):
        return x.reshape(j4 * t, x.shape[-1])

    sc = heads(_dot_nt_pair(q, kc)) + bc
    valid = bc > 0.5 * NEG_INF
    e = jnp.where(valid, jnp.exp(sc - jnp.max(sc, axis=-1, keepdims=True)), 0.0)
    l = jnp.sum(e, axis=-1, keepdims=True)
    pc = jnp.where(l > 0.0, e / l, 0.0)
    o_c = _dot_pair(flat(pc).astype(BF16), vc)
    psum = pc[0] + pc[1] + pc[2] + pc[3]
    imp = sum(_dot(part, ov) for part in _split3(psum)).T[:n_sel]
    blk = lax.broadcasted_iota(jnp.int32, (n_sel, t), 0)
    tpos = m * t + lax.broadcasted_iota(jnp.int32, (n_sel, t), 1)
    cur = tpos // SEL_BLOCK
    bonus = jnp.where(blk == 0, FORCE_BONUS,
                      jnp.where(blk == cur, FORCE_BONUS, jnp.where(blk == cur - 1, FORCE_BONUS, 0.0)))
    imp = jnp.where(blk * SEL_BLOCK <= tpos, imp + bonus, NEG_INF)
    rank = jnp.zeros((n_sel, t), F32)
    for i in range(n_sel):
        row = imp[i:i + 1, :]
        before = jnp.where(blk > i, 1.0, 0.0)
        rank = rank + jnp.where(row > imp, 1.0, jnp.where(row == imp, before, 0.0))
    selb = jnp.where(rank < float(SEL_TOP_N), 0.0, NEG_INF)
    selb = jnp.concatenate([selb, jnp.zeros((LANES - n_sel, t), F32)], axis=0).T
    return o_c, selb[:, :HEAD_DIM].astype(BF16)


def _overlap_table(s):
    n_sel = s // SEL_BLOCK
    nc = N_CHUNKS_PAD
    c0 = np.arange(nc)[:, None] * CMP_STRIDE
    s0 = np.arange(n_sel)[None, :] * SEL_BLOCK
    ov = np.clip(np.minimum(c0 + CMP_BLOCK, s0 + SEL_BLOCK) - np.maximum(c0, s0), 0, None) / CMP_STRIDE
    ov[nc - 1] = 0.0
    return jnp.asarray(np.pad(ov, ((0, 0), (0, LANES - n_sel))), BF16)


def _nsa(qn, kc, vc, ks, vs, kw, vw, gates, toep, bc):
    b, s, _ = qn.shape
    t = ATT_TILE
    assert s // SEL_BLOCK <= HEAD_DIM
    ov = _overlap_table(s)
    rows = NSA_HPG * t
    ng = NSA_KV_GROUPS
    nq = s // t
    assert nq >= 2
    seq = lambda n, w=HEAD_DIM: pl.BlockSpec((1, ng, n, w), lambda i, j: (i, 0, 0, 0))
    opened = lambda j: jnp.minimum(j, nq - 1)
    closed = lambda j: jnp.maximum(j - 2, 0)
    return pl.pallas_call(
        functools.partial(_nsa_body, n_tiles=nq),
        grid=(b, nq + 2),
        in_specs=[pl.BlockSpec((1, t, NSA_Q), lambda i, j: (i, opened(j), 0)),
                  seq(N_CHUNKS_PAD), seq(N_CHUNKS_PAD),
                  seq(s, 2 * HEAD_DIM), seq(s, 2 * HEAD_DIM), seq(s), seq(s, 2 * HEAD_DIM),
                  pl.BlockSpec((1, ng, t, GATE_PAD), lambda i, j: (i, 0, opened(j), 0)),
                  pl.BlockSpec((1, ng, t, GATE_PAD), lambda i, j: (i, 0, closed(j), 0)),
                  _resident((NSA_HEADS, t, 2 * t)),
                  pl.BlockSpec((NSA_HEADS, t, N_CHUNKS_PAD), lambda i, j: (0, opened(j), 0)),
                  _resident(ov.shape)],
        out_specs=pl.BlockSpec((1, t, NSA_Q), lambda i, j: (i, closed(j), 0)),
        out_shape=jax.ShapeDtypeStruct((b, s, NSA_Q), BF16),
        scratch_shapes=[pltpu.VMEM((ng, nq - 2, rows, t), F32),
                        pltpu.VMEM((2, ng, 2, rows, t), F32),
                        pltpu.VMEM((2, ng, rows, LANES), F32),
                        pltpu.VMEM((2, ng, rows, 2 * HEAD_DIM), F32),
                        pltpu.VMEM((2, t, NSA_Q), F32)],
        compiler_params=_params(("parallel", "arbitrary")),
        name="nsa",
    )(qn, kc, vc, ks, vs, kw, vw, gates, gates, toep, bc, ov)


def _fcum_body(f_ref, b_ref, o_ref):
    z = f_ref[0] + b_ref[...]
    x = -(jnp.maximum(-z, 0.0) + jnp.log1p(jnp.exp(-jnp.abs(z))))
    n = x.shape[-1]
    lane = lax.broadcasted_iota(jnp.int32, x.shape, 1)
    sh = 1
    while sh < n:
        x = x + jnp.where(lane >= sh, pltpu.roll(x, sh, axis=1), 0.0)
        sh *= 2
    o_ref[0] = x


def _fcum(f_t, b_forget):
    b, h, s = f_t.shape
    blk = pl.BlockSpec((1, h, s), lambda i: (i, 0, 0))
    return pl.pallas_call(
        _fcum_body,
        grid=(b,),
        in_specs=[blk, _resident((h, 1))],
        out_specs=blk,
        out_shape=jax.ShapeDtypeStruct((b, h, s), F32),
        compiler_params=_params(("parallel",)),
        name="fcum",
    )(f_t, b_forget.reshape(h, 1))


FOX_PAIR = 2
FOX_TILE = 512


def _causal_bias(t):
    i = np.arange(t)
    return jnp.asarray(np.where(i[None, :] <= i[:, None], 0.0, NEG_INF), F32)


def _fox_body(q_ref, k_ref, v_ref, c_ref, cm_ref, o_ref, s_ref, mb_ref, acc_ref):
    t = FOX_TILE
    heads = range(FOX_PAIR)
    nq = q_ref.shape[2] // t

    def keys(n):
        return pl.ds(pl.multiple_of(n * t, t), t)

    def scores(m, hh, n):
        q = q_ref[0, hh, m * t:(m + 1) * t, :]
        return _dot_nt(q, k_ref[0, hh, keys(n), :]) - c_ref[0, hh, :, keys(n)]

    def scores_diag(m):
        for hh in heads:
            s = scores(m, hh, m) + cm_ref[...]
            s_ref[m % 2, hh, m] = s
            mb_ref[m % 2, hh] = _lane_fold(s, jnp.maximum)

    def scores_far(m, n):
        for hh in heads:
            s = scores(m, hh, n)
            s_ref[m % 2, hh, n] = s
            mb_ref[m % 2, hh] = jnp.maximum(mb_ref[m % 2, hh], _lane_fold(s, jnp.maximum))

    def finish_max(m):
        for hh in heads:
            mb_ref[m % 2, hh] = _row_max_tile(mb_ref[m % 2, hh])

    def weigh(m, n):
        for hh in heads:
            p = jnp.exp(s_ref[m % 2, hh, n] - _lane_tile(mb_ref[m % 2, hh], t)).astype(BF16)
            acc_ref[hh] += _dot(p, v_ref[0, hh, keys(n), :])

    scores_diag(0)
    finish_max(0)
    for m in range(nq):
        ahead = m + 1 < nq
        if ahead:
            scores_diag(m + 1)
        for hh in heads:
            acc_ref[hh] = jnp.zeros((t, 2 * HEAD_DIM), F32)

        @pl.loop(0, m + 1)
        def _(n, m=m, ahead=ahead):
            weigh(m, n)
            if ahead:
                scores_far(m + 1, n)

        if ahead:
            finish_max(m + 1)
        o_ref[0, m * t:(m + 1) * t, :] = jnp.concatenate(
            [_normalize(acc_ref[hh]) for hh in heads], axis=1).astype(BF16)


def _fox(qf, kf, vf, c):
    b, h, s, _ = qf.shape
    t = FOX_TILE
    seq = lambda w: pl.BlockSpec((1, FOX_PAIR, s, w), lambda i, p: (i, p, 0, 0))
    return pl.pallas_call(
        _fox_body,
        grid=(b, h // FOX_PAIR),
        in_specs=[seq(HEAD_DIM), seq(HEAD_DIM), seq(2 * HEAD_DIM),
                  pl.BlockSpec((1, FOX_PAIR, 1, s), lambda i, p: (i, p, 0, 0)),
                  _resident((t, t))],
        out_specs=pl.BlockSpec((1, s, FOX_PAIR * HEAD_DIM), lambda i, p: (i, 0, p)),
        out_shape=jax.ShapeDtypeStruct((b, s, h * HEAD_DIM), BF16),
        scratch_shapes=[pltpu.VMEM((2, FOX_PAIR, s // t, t, t), F32),
                        pltpu.VMEM((2, FOX_PAIR, t, LANES), F32),
                        pltpu.VMEM((FOX_PAIR, t, 2 * HEAD_DIM), F32)],
        compiler_params=_params(("parallel", "parallel")),
        name="fox",
    )(qf, kf, vf, c.reshape(b, h, 1, s), _causal_bias(t))


def _mixout_body(x_ref, g_ref, on_ref, of_ref, wa_ref, wb_ref, wun_ref, wuf_ref, wo_ref, o_ref):
    x = x_ref[...]
    h = _rms(x, g_ref[...]).astype(BF16)
    y = (jax.nn.sigmoid(_dot(h, wa_ref[...])) * _dot(on_ref[...], wun_ref[...])
         + jax.nn.sigmoid(_dot(h, wb_ref[...])) * _dot(of_ref[...], wuf_ref[...]))
    o_ref[...] = x + _dot(y.astype(BF16), wo_ref[...])


def _mixout(x2d, g, o_nsa, o_fox, w_a, w_b, w_un, w_uf, w_o):
    n = x2d.shape[0]
    tok = lambda c: pl.BlockSpec((TOK_TILE, c), lambda i: (i, 0))
    return pl.pallas_call(
        _mixout_body,
        grid=(n // TOK_TILE,),
        in_specs=[tok(D_MODEL), _resident((1, D_MODEL)), tok(NSA_Q), tok(FOX_W),
                  _resident(w_a.shape), _resident(w_b.shape), _resident(w_un.shape),
                  _resident(w_uf.shape), _resident(w_o.shape)],
        out_specs=tok(D_MODEL),
        out_shape=jax.ShapeDtypeStruct((n, D_MODEL), F32),
        compiler_params=_params(("parallel",)),
        name="mixout",
    )(x2d, g.reshape(1, D_MODEL), o_nsa, o_fox, w_a, w_b, w_un, w_uf, w_o)


MEM_BATCH = 4


def _memkv_body(m_ref, g_ref, w_ref, k_ref, v_ref):
    nb, ml, d = m_ref.shape
    h = _rms(m_ref[...].reshape(nb * ml, d), g_ref[...]).astype(BF16)
    z = _dot(h, w_ref[...]).astype(BF16)
    k_ref[...] = z[:, :D_MODEL].reshape(nb, ml, d)
    v_ref[...] = z[:, D_MODEL:].reshape(nb, ml, d)


def _memkv(mem, g, w_kv):
    b, ml, _ = mem.shape
    nb = MEM_BATCH if b % MEM_BATCH == 0 else 1
    blk = pl.BlockSpec((nb, ml, D_MODEL), lambda i: (i, 0, 0))
    sd = jax.ShapeDtypeStruct((b, ml, D_MODEL), BF16)
    return pl.pallas_call(
        _memkv_body,
        grid=(b // nb,),
        in_specs=[blk, _resident((1, D_MODEL)), _resident(w_kv.shape)],
        out_specs=(blk, blk),
        out_shape=(sd, sd),
        compiler_params=_params(("parallel",)),
        name="memkv",
    )(mem, g.reshape(1, D_MODEL), w_kv)


def _memattn_body(x_ref, g_ref, k_ref, v_ref, wq_ref, wo_ref, o_ref):
    x = x_ref[0]
    h = _rms(x, g_ref[...]).astype(BF16)
    q = (_dot(h, wq_ref[...]) * (MEM_HEAD_DIM ** -0.5)).astype(BF16)
    outs = []
    for hd in range(MEM_HEADS):
        cols = slice(hd * MEM_HEAD_DIM, (hd + 1) * MEM_HEAD_DIM)
        s = _dot_nt(q[:, cols], k_ref[0, :, cols])
        e = jnp.exp(s - jnp.max(s, axis=-1, keepdims=True))
        p = e / jnp.sum(e, axis=-1, keepdims=True)
        outs.append(_dot(p.astype(BF16), v_ref[0, :, cols]))
    o = jnp.concatenate(outs, axis=1).astype(BF16)
    o_ref[0] = x + _dot(o, wo_ref[...])


def _memattn(x, g, k, v, w_q, w_o):
    b, s, _ = x.shape
    ml = k.shape[1]
    tok = pl.BlockSpec((1, TOK_TILE, D_MODEL), lambda i, j: (i, j, 0))
    kvb = pl.BlockSpec((1, ml, D_MODEL), lambda i, j: (i, 0, 0))
    return pl.pallas_call(
        _memattn_body,
        grid=(b, s // TOK_TILE),
        in_specs=[tok, _resident((1, D_MODEL)), kvb, kvb, _resident(w_q.shape), _resident(w_o.shape)],
        out_specs=tok,
        out_shape=jax.ShapeDtypeStruct((b, s, D_MODEL), F32),
        compiler_params=_params(("parallel", "parallel")),
        name="memattn",
    )(x, g.reshape(1, D_MODEL), k, v, w_q, w_o)


def kernel(x, mem, rel_bias_table, ffn1_norm, ffn1_w_gate, ffn1_w_up, ffn1_w_down, mix_norm, mix_w_in, mix_b_forget, cmp_pos_k, cmp_pos_v, cmp_k_w1, cmp_k_w2, cmp_v_w1, cmp_v_w2, w_up_nsa, w_up_fox, mix_w_out, mem_q_norm, mem_kv_norm, mem_w_q, mem_w_kv, mem_w_o, ffn2_norm, ffn2_w_gate, ffn2_w_up, ffn2_w_down, final_norm):
    b, s, d = x.shape
    depth = ffn1_norm.shape[0]
    bf = lambda w: w.astype(BF16)
    toep, bias_c = _relbias(rel_bias_table, s)
    x = x.reshape(b * s, d)
    for l in range(depth):
        last = l == depth - 1
        x = _ffn(x, ffn1_norm[l], ffn1_w_gate[l], ffn1_w_up[l], ffn1_w_down[l])

        w_nsa, w_fox, w_small, w_a, w_b = _pack_w_in(mix_w_in[l])
        qn, kc, vc, ks, vs, kw, vw, qf, kf, vf, gates, flog = _inproj(
            x.reshape(b, s, d), mix_norm[l], w_nsa, w_fox, w_small)
        kcc, vcc = _compress(kc, vc, cmp_pos_k[l], cmp_pos_v[l], cmp_k_w1[l], cmp_k_w2[l],
                             cmp_v_w1[l], cmp_v_w2[l])
        o_nsa = _nsa(qn, kcc, vcc, ks, vs, kw, vw, gates, toep, bias_c)
        c = _fcum(flog, mix_b_forget[l])
        o_fox = _fox(qf, kf, vf, c)
        x = _mixout(x, mix_norm[l], o_nsa.reshape(b * s, NSA_Q), o_fox.reshape(b * s, FOX_W),
                    w_a, w_b, bf(w_up_nsa[l]), bf(w_up_fox[l]), bf(mix_w_out[l]))

        mk, mv = _memkv(mem, mem_kv_norm[l], bf(mem_w_kv[l]))
        x = _memattn(x.reshape(b, s, d), mem_q_norm[l], mk, mv, bf(mem_w_q[l]), bf(mem_w_o[l]))
        x = _ffn(x.reshape(b * s, d), ffn2_norm[l], ffn2_w_gate[l], ffn2_w_up[l],
                 ffn2_w_down[l], final_g=final_norm if last else None)
    return x.reshape(b, s, d)
```

```python
import functools
import math

import numpy as np
import jax
import jax.numpy as jnp
from jax import lax
from jax.experimental import pallas as pl
from jax.experimental.pallas import tpu as pltpu

D_MODEL = 1024
D_FF = 2816
HEAD_DIM = 64
NSA_HEADS = 8
NSA_KV_GROUPS = 2
NSA_HPG = NSA_HEADS // NSA_KV_GROUPS
CMP_BLOCK = 32
CMP_STRIDE = 16
CMP_HIDDEN = 256
SEL_BLOCK = 64
SEL_TOP_N = 16
WINDOW = 512
FOX_HEADS = 8
MEM_HEADS = 4
MEM_HEAD_DIM = D_MODEL // MEM_HEADS
NUM_BUCKETS = 32
MAX_DISTANCE = 128
RMS_EPS = 1e-6
NEG_INF = -1e30
FORCE_BONUS = 1e4

NSA_Q = NSA_HEADS * HEAD_DIM
NSA_KV = NSA_KV_GROUPS * HEAD_DIM
FOX_W = FOX_HEADS * HEAD_DIM

LANES = 128
VMEM_LIMIT = 56 * 1024 * 1024

BF16 = jnp.bfloat16
F32 = jnp.float32

ATT_TILE = 256
TOK_TILE = 1024
FFN_TILE = 1024
FF_CHUNK = 256


def _bucket_thresholds():
    n = np.arange(0, 4 * MAX_DISTANCE)
    exact = NUM_BUCKETS // 2
    large = exact + (np.log(np.maximum(n, 1) / exact) / math.log(MAX_DISTANCE / exact)
                     * (NUM_BUCKETS - exact)).astype(np.int64)
    bucket = np.where(n < exact, n, np.minimum(large, NUM_BUCKETS - 1))
    assert np.all(np.diff(bucket) >= 0)
    return [int(np.argmax(bucket >= k)) for k in range(1, NUM_BUCKETS)]


BUCKET_THRESHOLDS = _bucket_thresholds()


def _dot(a, b):
    return jnp.dot(a, b, preferred_element_type=F32)


def _dot_nt(a, b):
    return lax.dot_general(a, b, (((1,), (1,)), ((), ())), preferred_element_type=F32)


def _row_halves(fn, a, b):
    h = a.shape[0] // 2
    return jnp.concatenate([fn(a[:h], b), fn(a[h:], b)], axis=0)


def _dot_pair(a, b):
    return _row_halves(_dot, a, b)


def _dot_nt_pair(a, b):
    return _row_halves(_dot_nt, a, b)


def _rms(x, g):
    return x * lax.rsqrt(jnp.mean(x * x, axis=-1, keepdims=True) + RMS_EPS) * g


def _resident(shape):
    nd = len(shape)
    return pl.BlockSpec(shape, lambda *_: (0,) * nd, pipeline_mode=pl.Buffered(1))


def _params(sem):
    return pltpu.CompilerParams(dimension_semantics=sem, vmem_limit_bytes=VMEM_LIMIT)


def _ffn_body(x_ref, g_ref, wg_hbm, wu_hbm, wd_hbm, *rest, final):
    if final:
        fg_ref, o_ref, wg_ref, wu_ref, wd_ref, sg_ref, su_ref, sd_ref, sem = rest
    else:
        o_ref, wg_ref, wu_ref, wd_ref, sg_ref, su_ref, sd_ref, sem = rest
    n_chunks = D_FF // FF_CHUNK

    def chunk_copies(c):
        cols = slice(c * FF_CHUNK, (c + 1) * FF_CHUNK)
        slot = c % 2
        return (pltpu.make_async_copy(wg_hbm.at[:, cols], sg_ref.at[slot], sem.at[0, slot]),
                pltpu.make_async_copy(wu_hbm.at[:, cols], su_ref.at[slot], sem.at[1, slot]),
                pltpu.make_async_copy(wd_hbm.at[cols, :], sd_ref.at[slot], sem.at[2, slot]))

    @pl.when(pl.program_id(0) == 0)
    def _():
        for cp in chunk_copies(0):
            cp.start()
        for c in range(n_chunks):
            sl = slice(c * FF_CHUNK, (c + 1) * FF_CHUNK)
            if c + 1 < n_chunks:
                for cp in chunk_copies(c + 1):
                    cp.start()
            for cp in chunk_copies(c):
                cp.wait()
            wg_ref[:, sl] = sg_ref[c % 2].astype(BF16)
            wu_ref[:, sl] = su_ref[c % 2].astype(BF16)
            wd_ref[sl, :] = sd_ref[c % 2].astype(BF16)

    x = x_ref[...]
    h = _rms(x, g_ref[...]).astype(BF16)
    acc = jnp.zeros(x.shape, F32)
    for c in range(n_chunks):
        sl = slice(c * FF_CHUNK, (c + 1) * FF_CHUNK)
        a = _dot(h, wg_ref[:, sl])
        b = _dot(h, wu_ref[:, sl])
        t = (a * jax.nn.sigmoid(a)) * b
        acc = acc + _dot(t.astype(BF16), wd_ref[sl, :])
    y = x + 0.5 * acc
    if final:
        y = _rms(y, fg_ref[...])
    o_ref[...] = y


def _ffn(x2d, g, wg, wu, wd, final_g=None):
    n = x2d.shape[0]
    final = final_g is not None
    tok = pl.BlockSpec((FFN_TILE, D_MODEL), lambda i: (i, 0))
    hbm = pl.BlockSpec(memory_space=pl.ANY)
    in_specs = [tok, _resident((1, D_MODEL)), hbm, hbm, hbm]
    args = [x2d, g.reshape(1, D_MODEL), wg, wu, wd]
    if final:
        in_specs.append(_resident((1, D_MODEL)))
        args.append(final_g.reshape(1, D_MODEL))
    return pl.pallas_call(
        functools.partial(_ffn_body, final=final),
        grid=(n // FFN_TILE,),
        in_specs=in_specs,
        out_specs=tok,
        out_shape=jax.ShapeDtypeStruct((n, D_MODEL), F32),
        scratch_shapes=[pltpu.VMEM((D_MODEL, D_FF), BF16), pltpu.VMEM((D_MODEL, D_FF), BF16),
                        pltpu.VMEM((D_FF, D_MODEL), BF16),
                        pltpu.VMEM((2, D_MODEL, FF_CHUNK), F32), pltpu.VMEM((2, D_MODEL, FF_CHUNK), F32),
                        pltpu.VMEM((2, FF_CHUNK, D_MODEL), F32),
                        pltpu.SemaphoreType.DMA((3, 2))],
        compiler_params=_params(("arbitrary",)),
        name="ffn_final" if final else "ffn",
    )(*args)


_C_CMP = NSA_Q
_C_KV = _C_CMP + 2 * NSA_KV
_W_NSA = _C_KV + 4 * NSA_KV
GATE_PAD = 16
_C_FLOG = NSA_KV_GROUPS * GATE_PAD
N_CHUNKS_PAD = 128
CHUNK_W = CMP_STRIDE * NSA_KV


def _inproj_body(x_ref, g_ref, wn_ref, wf_ref, ws_ref, blk_ref, qn_ref, kc_ref, vc_ref, ks_ref, vs_ref,
                 kw_ref, vw_ref, qf_ref, kf_ref, vf_ref, gate_ref, fl_ref, cmp_ref):
    h = _rms(x_ref[0], g_ref[...]).astype(BF16)
    z = _dot(h, wn_ref[...])
    qn_ref[0] = (z[:, :_C_CMP] * (HEAD_DIM ** -0.5)).astype(BF16)
    rows = z.shape[0] // CMP_STRIDE
    for i, ref in enumerate((kc_ref, vc_ref)):
        cmp_ref[i] = z[:, _C_CMP + i * NSA_KV:_C_CMP + (i + 1) * NSA_KV]
        for r in range(CMP_STRIDE):
            tok = cmp_ref[i, pl.ds(r, rows, stride=CMP_STRIDE), :]
            ref[0, :, r * NSA_KV:(r + 1) * NSA_KV] = tok.astype(BF16)
    ones = jnp.ones((z.shape[0], HEAD_DIM), BF16)
    for i, (ref, extra) in enumerate(((ks_ref, blk_ref[...]), (vs_ref, ones), (kw_ref, None),
                                      (vw_ref, ones))):
        for g in range(NSA_KV_GROUPS):
            c0 = _C_KV + i * NSA_KV + g * HEAD_DIM
            val = z[:, c0:c0 + HEAD_DIM].astype(BF16)
            ref[0, g] = val if extra is None else jnp.concatenate([val, extra], axis=1)
    zf = _dot(h, wf_ref[...])
    for hd in range(FOX_HEADS):
        c0 = hd * HEAD_DIM
        qf_ref[0, hd] = (zf[:, c0:c0 + HEAD_DIM] * (HEAD_DIM ** -0.5)).astype(BF16)
        kf_ref[0, hd] = zf[:, c0 + FOX_W:c0 + FOX_W + HEAD_DIM].astype(BF16)
        v = zf[:, c0 + 2 * FOX_W:c0 + 2 * FOX_W + HEAD_DIM].astype(BF16)
        vf_ref[0, hd] = jnp.concatenate([v, ones], axis=1)
    zs = _dot(h, ws_ref[...])
    for g in range(NSA_KV_GROUPS):
        gate_ref[0, g] = zs[:, g * GATE_PAD:(g + 1) * GATE_PAD]
    fl_ref[0] = zs.T[_C_FLOG:_C_FLOG + FOX_HEADS]


_IN_COLS = np.cumsum((0, NSA_Q, NSA_KV, NSA_KV, NSA_KV, NSA_KV, NSA_KV, NSA_KV, 3 * NSA_HEADS,
                      FOX_W, FOX_W, FOX_W, FOX_HEADS, D_MODEL, D_MODEL)).tolist()
W_PREP_ROWS = 256


def _wprep_body(w_ref, wn_ref, wf_ref, ws_ref, wa_ref, wb_ref):
    wt = w_ref[0]
    c = _IN_COLS
    k = wt.shape[1]
    wn_ref[...] = wt[:c[7]].T.astype(BF16)
    wf_ref[...] = wt[c[8]:c[11]].T.astype(BF16)
    wa_ref[...] = wt[c[12]:c[13]].T.astype(BF16)
    wb_ref[...] = wt[c[13]:c[14]].T.astype(BF16)
    small = []
    for g in range(NSA_KV_GROUPS):
        for br in range(3):
            c0 = c[7] + br * NSA_HEADS + g * NSA_HPG
            small.append(wt[c0:c0 + NSA_HPG])
        small.append(jnp.zeros((GATE_PAD - 3 * NSA_HPG, k), F32))
    small.append(wt[c[11]:c[12]])
    small.append(jnp.zeros((LANES - _C_FLOG - FOX_HEADS, k), F32))
    ws_ref[...] = jnp.concatenate(small, axis=0).T.astype(BF16)


def _pack_w_in(w_in, layer):
    _, k, n = w_in.shape
    c = _IN_COLS
    assert n == c[14]
    widths = (c[7], c[11] - c[8], LANES, D_MODEL, D_MODEL)
    return pl.pallas_call(
        _wprep_body,
        grid=(k // W_PREP_ROWS,),
        in_specs=[pl.BlockSpec((1, n, W_PREP_ROWS), lambda i: (layer, 0, i))],
        out_specs=tuple(pl.BlockSpec((W_PREP_ROWS, w), lambda i: (i, 0)) for w in widths),
        out_shape=tuple(jax.ShapeDtypeStruct((k, w), BF16) for w in widths),
        compiler_params=_params(("parallel",)),
        name="wprep",
    )(jnp.swapaxes(w_in, 1, 2))


def _inproj(x, g, w_nsa, w_fox, w_small):
    b, s, _ = x.shape
    grid = (b, s // TOK_TILE)
    tok = lambda c: pl.BlockSpec((1, TOK_TILE, c), lambda i, j: (i, j, 0))
    heads = lambda nh, w=HEAD_DIM: pl.BlockSpec((1, nh, TOK_TILE, w), lambda i, j: (i, 0, j, 0))
    chunks = pl.BlockSpec((1, TOK_TILE // CMP_STRIDE, CHUNK_W), lambda i, j: (i, j, 0))
    sds = jax.ShapeDtypeStruct
    kv = sds((b, NSA_KV_GROUPS, s, HEAD_DIM), BF16)
    kv2 = sds((b, NSA_KV_GROUPS, s, 2 * HEAD_DIM), BF16)
    fx = sds((b, FOX_HEADS, s, HEAD_DIM), BF16)
    cmp_in = sds((b, s // CMP_STRIDE, CHUNK_W), BF16)
    out_shape = (sds((b, s, NSA_Q), BF16), cmp_in, cmp_in,
                 kv2, kv2, kv, kv2, fx, fx, sds((b, FOX_HEADS, s, 2 * HEAD_DIM), BF16),
                 sds((b, NSA_KV_GROUPS, s, GATE_PAD), F32), sds((b, FOX_HEADS, s), F32))
    out_specs = (tok(NSA_Q), chunks, chunks,
                 heads(NSA_KV_GROUPS, 2 * HEAD_DIM), heads(NSA_KV_GROUPS, 2 * HEAD_DIM),
                 heads(NSA_KV_GROUPS), heads(NSA_KV_GROUPS, 2 * HEAD_DIM),
                 heads(FOX_HEADS), heads(FOX_HEADS), heads(FOX_HEADS, 2 * HEAD_DIM),
                 pl.BlockSpec((1, NSA_KV_GROUPS, TOK_TILE, GATE_PAD), lambda i, j: (i, 0, j, 0)),
                 pl.BlockSpec((1, FOX_HEADS, TOK_TILE), lambda i, j: (i, 0, j)))
    key_blk = np.arange(s) // SEL_BLOCK
    blk_onehot = jnp.asarray(key_blk[:, None] == np.arange(HEAD_DIM)[None, :], BF16)
    return pl.pallas_call(
        _inproj_body,
        grid=grid,
        in_specs=[tok(D_MODEL), _resident((1, D_MODEL)), _resident(w_nsa.shape),
                  _resident(w_fox.shape), _resident(w_small.shape),
                  pl.BlockSpec((TOK_TILE, HEAD_DIM), lambda i, j: (j, 0))],
        out_specs=out_specs,
        out_shape=out_shape,
        scratch_shapes=[pltpu.VMEM((2, TOK_TILE, NSA_KV), F32)],
        compiler_params=_params(("parallel", "parallel")),
        name="inproj",
    )(x, g.reshape(1, D_MODEL), w_nsa, w_fox, w_small, blk_onehot)


CMP_BATCH = 4

def _compress_body(xk_ref, xv_ref, pk_ref, pv_ref, wk1_ref, wv1_ref, wk2_ref, wv2_ref, ok_ref, ov_ref):
    nb = xk_ref.shape[0]
    n = nb * N_CHUNKS_PAD
    for x_ref, p_ref, w1_ref, w2_ref, o_ref in ((xk_ref, pk_ref, wk1_ref, wk2_ref, ok_ref),
                                                (xv_ref, pv_ref, wv1_ref, wv2_ref, ov_ref)):
        x = x_ref[...].reshape(n, CHUNK_W)
        a0 = _dot(x, w1_ref[0])
        a1 = _dot(x, w1_ref[1])
        c = _dot(p_ref[0], w1_ref[0]) + _dot(p_ref[1], w1_ref[1])
        pre = a0 + pltpu.roll(a1, n - 1, axis=0) + c[0:1, :]
        hid = jax.nn.gelu(pre).astype(BF16)
        out = _dot(hid, w2_ref[...]).astype(BF16)
        for i in range(nb):
            for g in range(NSA_KV_GROUPS):
                o_ref[i, g] = out[i * N_CHUNKS_PAD:(i + 1) * N_CHUNKS_PAD, g * HEAD_DIM:(g + 1) * HEAD_DIM]


def _pack_compress(pos, w1, w2):
    r = CMP_BLOCK // CMP_STRIDE
    assert NSA_KV_GROUPS == 2
    w1r = w1.astype(BF16).reshape(r, CMP_STRIDE, HEAD_DIM, CMP_HIDDEN)
    z1 = jnp.zeros_like(w1r)
    w1big = jnp.stack([jnp.concatenate([w1r, z1], axis=-1), jnp.concatenate([z1, w1r], axis=-1)], axis=2)
    w1big = w1big.reshape(r, CHUNK_W, NSA_KV_GROUPS * CMP_HIDDEN)
    z2 = jnp.zeros_like(w2)
    w2big = jnp.concatenate([jnp.concatenate([w2, z2], axis=1), jnp.concatenate([z2, w2], axis=1)], axis=0)
    p = pos.reshape(r, CMP_STRIDE, 1, HEAD_DIM)
    p = jnp.broadcast_to(p, (r, CMP_STRIDE, NSA_KV_GROUPS, HEAD_DIM)).reshape(r, 1, CHUNK_W)
    p = jnp.broadcast_to(p, (r, 16, CHUNK_W))
    return p.astype(BF16), w1big.astype(BF16), w2big.astype(BF16)


def _compress(xk, xv, pos_k, pos_v, k_w1, k_w2, v_w1, v_w2):
    b = xk.shape[0]
    assert xk.shape[1:] == (N_CHUNKS_PAD, CHUNK_W)
    pk, wk1, wk2 = _pack_compress(pos_k, k_w1, k_w2)
    pv, wv1, wv2 = _pack_compress(pos_v, v_w1, v_w2)
    nb = CMP_BATCH if b % CMP_BATCH == 0 else 1
    xs = pl.BlockSpec((nb, N_CHUNKS_PAD, CHUNK_W), lambda i: (i, 0, 0))
    os_ = pl.BlockSpec((nb, NSA_KV_GROUPS, N_CHUNKS_PAD, HEAD_DIM), lambda i: (i, 0, 0, 0))
    osd = jax.ShapeDtypeStruct((b, NSA_KV_GROUPS, N_CHUNKS_PAD, HEAD_DIM), BF16)
    return pl.pallas_call(
        _compress_body,
        grid=(b // nb,),
        in_specs=[xs, xs, _resident(pk.shape), _resident(pv.shape), _resident(wk1.shape),
                  _resident(wv1.shape), _resident(wk2.shape), _resident(wv2.shape)],
        out_specs=(os_, os_),
        out_shape=(osd, osd),
        compiler_params=_params(("parallel",)),
        name="compress",
    )(xk, xv, pk, pv, wk1, wv1, wk2, wv2)


def _bias_lookup(tbl_ref, head, dist):
    acc = jnp.full(dist.shape, tbl_ref[0, head], F32)
    for k, thr in enumerate(BUCKET_THRESHOLDS):
        acc = jnp.where(dist >= thr, tbl_ref[k + 1, head], acc)
    return acc - tbl_ref[NUM_BUCKETS - 1, head]


def _relbias_body(tbl_ref, toep_ref, bc_ref):
    head = pl.program_id(0)
    t = ATT_TILE
    i = lax.broadcasted_iota(jnp.int32, (t, 2 * t), 0)
    u = lax.broadcasted_iota(jnp.int32, (t, 2 * t), 1)
    d = i - u + t
    toep_ref[0] = jnp.where(d >= 0, _bias_lookup(tbl_ref, head, d), NEG_INF)
    n_tiles = bc_ref.shape[1] // t
    per_tile = t // CMP_STRIDE
    off = per_tile * (n_tiles - 1)
    width = 2 * N_CHUNKS_PAD
    assert off + N_CHUNKS_PAD <= width
    i = lax.broadcasted_iota(jnp.int32, (t, width), 0)
    u = lax.broadcasted_iota(jnp.int32, (t, width), 1)
    dc = i - ((u - off) * CMP_STRIDE + CMP_BLOCK - 1)
    strip = jnp.where(dc >= 0, _bias_lookup(tbl_ref, head, dc), NEG_INF)
    for q in range(n_tiles):
        lo = off - q * per_tile
        bc_ref[0, q * t:(q + 1) * t, :] = strip[:, lo:lo + N_CHUNKS_PAD]


def _relbias(tbl, s):
    return pl.pallas_call(
        _relbias_body,
        grid=(NSA_HEADS,),
        in_specs=[pl.BlockSpec(memory_space=pltpu.SMEM)],
        out_specs=(pl.BlockSpec((1, ATT_TILE, 2 * ATT_TILE), lambda h: (h, 0, 0)),
                   pl.BlockSpec((1, s, N_CHUNKS_PAD), lambda h: (h, 0, 0))),
        out_shape=(jax.ShapeDtypeStruct((NSA_HEADS, ATT_TILE, 2 * ATT_TILE), F32),
                   jax.ShapeDtypeStruct((NSA_HEADS, s, N_CHUNKS_PAD), F32)),
        compiler_params=_params(("parallel",)),
        name="relbias",
    )(tbl)


def _lane_fold(x, op):
    return functools.reduce(op, [x[:, i * LANES:(i + 1) * LANES] for i in range(x.shape[1] // LANES)])


def _lane_tile(x, width):
    return jnp.concatenate([x] * (width // LANES), axis=1)


def _row_max_tile(rmax):
    return jnp.broadcast_to(jnp.max(rmax, axis=-1, keepdims=True), rmax.shape)


def _normalize(acc):
    return (acc / pltpu.roll(acc, HEAD_DIM, axis=1))[:, :HEAD_DIM]


def _split3(x):
    hi = x.astype(BF16)
    r = x - hi.astype(F32)
    mid = r.astype(BF16)
    lo = (r - mid.astype(F32)).astype(BF16)
    return hi, mid, lo


def _nsa_body(q_ref, kc_ref, vc_ref, ks_ref, vs_ref, kw_ref, vw_ref, gate_a_ref, gate_b_ref, toep_ref,
              bc_ref, ov_ref, o_ref, s_ref, near_ref, mb_ref, acc_ref, part_ref, *, n_tiles):
    t = ATT_TILE
    j4 = NSA_HPG
    rows = j4 * t
    groups = range(NSA_KV_GROUPS)
    step = pl.program_id(1)
    n_sel = n_tiles * (t // SEL_BLOCK)

    def heads(x):
        return x.reshape(j4, t, x.shape[-1])

    def flat(x):
        return x.reshape(rows, x.shape[-1])

    def key_tile(ref, g, n):
        return ref[0, g, pl.ds(pl.multiple_of(n * t, t), t), :]

    def gated(gate_ref, g, branch, o):
        gate = jax.nn.sigmoid(gate_ref[0, g])
        return [gate[:, branch * j4 + j:branch * j4 + j + 1] * o[j * t:(j + 1) * t] for j in range(j4)]

    def probs(par, g, s):
        return jnp.exp(s - _lane_tile(mb_ref[par, g], t)).astype(BF16)

    def open_tile(m, par):
        m1 = jnp.maximum(m - 1, 0)
        m2 = jnp.maximum(m - 2, 0)
        no_prev = jnp.where(m >= 1, 0.0, NEG_INF)
        qs = []
        for g in groups:
            qt = q_ref[0, :, g * j4 * HEAD_DIM:(g + 1) * j4 * HEAD_DIM]
            qs.append(jnp.concatenate([qt[:, j * HEAD_DIM:(j + 1) * HEAD_DIM] for j in range(j4)], axis=0))

        def biases(g):
            toep = toep_ref[g * j4:(g + 1) * j4]
            return toep[:, :, t:], toep[:, :, :t] + no_prev

        cmp_out = [_nsa_compressed(qs[g], kc_ref[0, g], vc_ref[0, g], bc_ref[g * j4:(g + 1) * j4],
                                   ov_ref[...], m, n_sel) for g in groups]
        part = []
        for g in groups:
            diag_bias, prev_bias = biases(g)
            ii = lax.broadcasted_iota(jnp.int32, (t, t), 0)
            jj = lax.broadcasted_iota(jnp.int32, (t, t), 1)
            tri = jnp.where(jj > ii, 0.0, NEG_INF) + jnp.where(m >= 2, 0.0, NEG_INF)
            w0 = flat(heads(_dot_nt_pair(qs[g], key_tile(kw_ref, g, m))) + diag_bias)
            w1 = flat(heads(_dot_nt_pair(qs[g], key_tile(kw_ref, g, m1))) + prev_bias)
            w2 = flat(heads(_dot_nt_pair(qs[g], key_tile(kw_ref, g, m2))) + tri[None])
            wmax = _lane_tile(_row_max_tile(functools.reduce(
                jnp.maximum, [_lane_fold(w, jnp.maximum) for w in (w0, w1, w2)])), t)
            acc = sum(_dot_pair(jnp.exp(w - wmax).astype(BF16), key_tile(vw_ref, g, n))
                      for w, n in ((w0, m), (w1, m1), (w2, m2)))
            part += [a + b for a, b in zip(gated(gate_a_ref, g, 0, cmp_out[g][0]),
                                           gated(gate_a_ref, g, 2, _normalize(acc)))]
        part_ref[par] = jnp.concatenate(part, axis=1)

        q_aug = []
        for g in groups:
            diag_bias, prev_bias = biases(g)
            q_aug.append(jnp.concatenate([qs[g], jnp.concatenate([cmp_out[g][1]] * j4, axis=0)], axis=1))
            s0 = flat(heads(_dot_nt_pair(q_aug[g], key_tile(ks_ref, g, m))) + diag_bias)
            s1 = flat(heads(_dot_nt_pair(q_aug[g], key_tile(ks_ref, g, m1))) + prev_bias)
            s2 = _dot_nt_pair(q_aug[g], key_tile(ks_ref, g, m2))
            near_ref[par, g, 0] = s0
            near_ref[par, g, 1] = s1
            s_ref[g, m2] = s2
            far = jnp.where(m >= 2, _lane_fold(s2, jnp.maximum), NEG_INF)
            mb_ref[par, g] = jnp.maximum(jnp.maximum(_lane_fold(s0, jnp.maximum),
                                                     _lane_fold(s1, jnp.maximum)), far)
        return q_aug

    def open_far(par, q_aug, n):
        for g in groups:
            s = _dot_nt_pair(q_aug[g], key_tile(ks_ref, g, n))
            s_ref[g, n] = s
            mb_ref[par, g] = jnp.maximum(mb_ref[par, g], _lane_fold(s, jnp.maximum))

    def close_tile(m, par):
        m1 = jnp.maximum(m - 1, 0)
        for g in groups:
            mb_ref[par, g] = _row_max_tile(mb_ref[par, g])
            acc_ref[par, g] = (_dot_pair(probs(par, g, near_ref[par, g, 0]), key_tile(vs_ref, g, m))
                               + _dot_pair(probs(par, g, near_ref[par, g, 1]), key_tile(vs_ref, g, m1)))

    def close_far(par, n):
        for g in groups:
            acc_ref[par, g] += _dot_pair(probs(par, g, s_ref[g, n]), key_tile(vs_ref, g, n))

    def finish(par):
        outs = []
        for g in groups:
            outs += gated(gate_b_ref, g, 1, _normalize(acc_ref[par, g]))
        o_ref[0] = (part_ref[par] + jnp.concatenate(outs, axis=1)).astype(BF16)

    @pl.when(step == 0)
    def _():
        open_tile(step, 0)

    @pl.when(step == 1)
    def _():
        open_tile(step, 1)
        close_tile(step - 1, 0)

    for par in (0, 1):
        @pl.when(jnp.logical_and(jnp.logical_and(step >= 2, step < n_tiles), step % 2 == par))
        def _(par=par):
            finish(par)
            q_aug = open_tile(step, par)
            close_tile(step - 1, 1 - par)

            @pl.loop(0, step - 2)
            def _(n):
                close_far(1 - par, n)
                open_far(par, q_aug, n)

    last = (n_tiles - 1) % 2

    @pl.when(step == n_tiles)
    def _():
        finish(1 - last)
        close_tile(step - 1, last)

        @pl.loop(0, step - 2)
        def _(n):
            close_far(last, n)

    @pl.when(step == n_tiles + 1)
    def _():
        finish(last)


def _nsa_compressed(q, kc, vc, bc, ov, m, n_sel):
    t = ATT_TILE
    j4 = NSA_HPG

    def heads(x):
        return x.reshape(j4, t, x.shape[-1])

    def flat(x):
        return x.reshape(j4 * t, x.shape[-1])

    sc = heads(_dot_nt_pair(q, kc)) + bc
    valid = bc > 0.5 * NEG_INF
    e = jnp.where(valid, jnp.exp(sc - jnp.max(sc, axis=-1, keepdims=True)), 0.0)
    l = jnp.sum(e, axis=-1, keepdims=True)
    pc = jnp.where(l > 0.0, e / l, 0.0)
    o_c = _dot_pair(flat(pc).astype(BF16), vc)
    psum = pc[0] + pc[1] + pc[2] + pc[3]
    imp = sum(_dot(part, ov) for part in _split3(psum)).T[:n_sel]
    blk = lax.broadcasted_iota(jnp.int32, (n_sel, t), 0)
    tpos = m * t + lax.broadcasted_iota(jnp.int32, (n_sel, t), 1)
    cur = tpos // SEL_BLOCK
    bonus = jnp.where(blk == 0, FORCE_BONUS,
                      jnp.where(blk == cur, FORCE_BONUS, jnp.where(blk == cur - 1, FORCE_BONUS, 0.0)))
    imp = jnp.where(blk * SEL_BLOCK <= tpos, imp + bonus, NEG_INF)
    lanes = 4
    partial = [jnp.zeros((n_sel, t), F32) for _ in range(lanes)]
    for i in range(n_sel):
        row = imp[i:i + 1, :]
        before = jnp.where(blk > i, 1.0, 0.0)
        partial[i % lanes] = partial[i % lanes] + jnp.where(row > imp, 1.0,
                                                            jnp.where(row == imp, before, 0.0))
    rank = (partial[0] + partial[1]) + (partial[2] + partial[3])
    selb = jnp.where(rank < float(SEL_TOP_N), 0.0, NEG_INF)
    selb = jnp.concatenate([selb, jnp.zeros((LANES - n_sel, t), F32)], axis=0).T
    return o_c, selb[:, :HEAD_DIM].astype(BF16)


def _overlap_table(s):
    n_sel = s // SEL_BLOCK
    nc = N_CHUNKS_PAD
    c0 = np.arange(nc)[:, None] * CMP_STRIDE
    s0 = np.arange(n_sel)[None, :] * SEL_BLOCK
    ov = np.clip(np.minimum(c0 + CMP_BLOCK, s0 + SEL_BLOCK) - np.maximum(c0, s0), 0, None) / CMP_STRIDE
    ov[nc - 1] = 0.0
    return jnp.asarray(np.pad(ov, ((0, 0), (0, LANES - n_sel))), BF16)


def _nsa(qn, kc, vc, ks, vs, kw, vw, gates, toep, bc):
    b, s, _ = qn.shape
    t = ATT_TILE
    assert s // SEL_BLOCK <= HEAD_DIM
    ov = _overlap_table(s)
    rows = NSA_HPG * t
    ng = NSA_KV_GROUPS
    nq = s // t
    assert nq >= 2
    seq = lambda n, w=HEAD_DIM: pl.BlockSpec((1, ng, n, w), lambda i, j: (i, 0, 0, 0))
    opened = lambda j: jnp.minimum(j, nq - 1)
    closed = lambda j: jnp.maximum(j - 2, 0)
    return pl.pallas_call(
        functools.partial(_nsa_body, n_tiles=nq),
        grid=(b, nq + 2),
        in_specs=[pl.BlockSpec((1, t, NSA_Q), lambda i, j: (i, opened(j), 0)),
                  seq(N_CHUNKS_PAD), seq(N_CHUNKS_PAD),
                  seq(s, 2 * HEAD_DIM), seq(s, 2 * HEAD_DIM), seq(s), seq(s, 2 * HEAD_DIM),
                  pl.BlockSpec((1, ng, t, GATE_PAD), lambda i, j: (i, 0, opened(j), 0)),
                  pl.BlockSpec((1, ng, t, GATE_PAD), lambda i, j: (i, 0, closed(j), 0)),
                  _resident((NSA_HEADS, t, 2 * t)),
                  pl.BlockSpec((NSA_HEADS, t, N_CHUNKS_PAD), lambda i, j: (0, opened(j), 0)),
                  _resident(ov.shape)],
        out_specs=pl.BlockSpec((1, t, NSA_Q), lambda i, j: (i, closed(j), 0)),
        out_shape=jax.ShapeDtypeStruct((b, s, NSA_Q), BF16),
        scratch_shapes=[pltpu.VMEM((ng, nq - 2, rows, t), F32),
                        pltpu.VMEM((2, ng, 2, rows, t), F32),
                        pltpu.VMEM((2, ng, rows, LANES), F32),
                        pltpu.VMEM((2, ng, rows, 2 * HEAD_DIM), F32),
                        pltpu.VMEM((2, t, NSA_Q), F32)],
        compiler_params=_params(("parallel", "arbitrary")),
        name="nsa",
    )(qn, kc, vc, ks, vs, kw, vw, gates, gates, toep, bc, ov)


def _fcum_body(f_ref, b_ref, o_ref):
    z = f_ref[0] + b_ref[...]
    x = -(jnp.maximum(-z, 0.0) + jnp.log1p(jnp.exp(-jnp.abs(z))))
    n = x.shape[-1]
    lane = lax.broadcasted_iota(jnp.int32, x.shape, 1)
    sh = 1
    while sh < n:
        x = x + jnp.where(lane >= sh, pltpu.roll(x, sh, axis=1), 0.0)
        sh *= 2
    o_ref[0] = x


def _fcum(f_t, b_forget):
    b, h, s = f_t.shape
    blk = pl.BlockSpec((1, h, s), lambda i: (i, 0, 0))
    return pl.pallas_call(
        _fcum_body,
        grid=(b,),
        in_specs=[blk, _resident((h, 1))],
        out_specs=blk,
        out_shape=jax.ShapeDtypeStruct((b, h, s), F32),
        compiler_params=_params(("parallel",)),
        name="fcum",
    )(f_t, b_forget.reshape(h, 1))


FOX_PAIR = 2
FOX_TILE = 512


def _causal_bias(t):
    i = np.arange(t)
    return jnp.asarray(np.where(i[None, :] <= i[:, None], 0.0, NEG_INF), F32)


def _fox_body(q_ref, k_ref, v_ref, c_ref, cm_ref, o_ref, s_ref, mb_ref, acc_ref):
    t = FOX_TILE
    heads = range(FOX_PAIR)
    nq = q_ref.shape[2] // t

    def keys(n):
        return pl.ds(pl.multiple_of(n * t, t), t)

    def scores(m, hh, n):
        q = q_ref[0, hh, m * t:(m + 1) * t, :]
        return _dot_nt(q, k_ref[0, hh, keys(n), :]) - c_ref[0, hh, :, keys(n)]

    def scores_diag(m):
        for hh in heads:
            s = scores(m, hh, m) + cm_ref[...]
            s_ref[m % 2, hh, m] = s
            mb_ref[m % 2, hh] = _lane_fold(s, jnp.maximum)

    def scores_far(m, n):
        for hh in heads:
            s = scores(m, hh, n)
            s_ref[m % 2, hh, n] = s
            mb_ref[m % 2, hh] = jnp.maximum(mb_ref[m % 2, hh], _lane_fold(s, jnp.maximum))

    def finish_max(m):
        for hh in heads:
            mb_ref[m % 2, hh] = _row_max_tile(mb_ref[m % 2, hh])

    def weigh(m, n):
        for hh in heads:
            p = jnp.exp(s_ref[m % 2, hh, n] - _lane_tile(mb_ref[m % 2, hh], t)).astype(BF16)
            acc_ref[hh] += _dot(p, v_ref[0, hh, keys(n), :])

    scores_diag(0)
    finish_max(0)
    for m in range(nq):
        ahead = m + 1 < nq
        if ahead:
            scores_diag(m + 1)
        for hh in heads:
            acc_ref[hh] = jnp.zeros((t, 2 * HEAD_DIM), F32)

        @pl.loop(0, m + 1)
        def _(n, m=m, ahead=ahead):
            weigh(m, n)
            if ahead:
                scores_far(m + 1, n)

        if ahead:
            finish_max(m + 1)
        o_ref[0, m * t:(m + 1) * t, :] = jnp.concatenate(
            [_normalize(acc_ref[hh]) for hh in heads], axis=1).astype(BF16)


def _fox(qf, kf, vf, c):
    b, h, s, _ = qf.shape
    t = FOX_TILE
    seq = lambda w: pl.BlockSpec((1, FOX_PAIR, s, w), lambda i, p: (i, p, 0, 0))
    return pl.pallas_call(
        _fox_body,
        grid=(b, h // FOX_PAIR),
        in_specs=[seq(HEAD_DIM), seq(HEAD_DIM), seq(2 * HEAD_DIM),
                  pl.BlockSpec((1, FOX_PAIR, 1, s), lambda i, p: (i, p, 0, 0)),
                  _resident((t, t))],
        out_specs=pl.BlockSpec((1, s, FOX_PAIR * HEAD_DIM), lambda i, p: (i, 0, p)),
        out_shape=jax.ShapeDtypeStruct((b, s, h * HEAD_DIM), BF16),
        scratch_shapes=[pltpu.VMEM((2, FOX_PAIR, s // t, t, t), F32),
                        pltpu.VMEM((2, FOX_PAIR, t, LANES), F32),
                        pltpu.VMEM((FOX_PAIR, t, 2 * HEAD_DIM), F32)],
        compiler_params=_params(("parallel", "parallel")),
        name="fox",
    )(qf, kf, vf, c.reshape(b, h, 1, s), _causal_bias(t))


def _mixout_body(x_ref, g_ref, on_ref, of_ref, wa_ref, wb_ref, wun_ref, wuf_ref, wo_ref, o_ref):
    x = x_ref[...]
    h = _rms(x, g_ref[...]).astype(BF16)
    y = (jax.nn.sigmoid(_dot(h, wa_ref[...])) * _dot(on_ref[...], wun_ref[...])
         + jax.nn.sigmoid(_dot(h, wb_ref[...])) * _dot(of_ref[...], wuf_ref[...]))
    o_ref[...] = x + _dot(y.astype(BF16), wo_ref[...])


def _mixout(x2d, g, o_nsa, o_fox, w_a, w_b, w_un, w_uf, w_o):
    n = x2d.shape[0]
    tok = lambda c: pl.BlockSpec((TOK_TILE, c), lambda i: (i, 0))
    return pl.pallas_call(
        _mixout_body,
        grid=(n // TOK_TILE,),
        in_specs=[tok(D_MODEL), _resident((1, D_MODEL)), tok(NSA_Q), tok(FOX_W),
                  _resident(w_a.shape), _resident(w_b.shape), _resident(w_un.shape),
                  _resident(w_uf.shape), _resident(w_o.shape)],
        out_specs=tok(D_MODEL),
        out_shape=jax.ShapeDtypeStruct((n, D_MODEL), F32),
        compiler_params=_params(("parallel",)),
        name="mixout",
    )(x2d, g.reshape(1, D_MODEL), o_nsa, o_fox, w_a, w_b, w_un, w_uf, w_o)


MEM_BATCH = 4


def _memkv_body(m_ref, g_ref, w_ref, k_ref, v_ref):
    nb, ml, d = m_ref.shape
    h = _rms(m_ref[...].reshape(nb * ml, d), g_ref[...]).astype(BF16)
    z = _dot(h, w_ref[...]).astype(BF16)
    k_ref[...] = z[:, :D_MODEL].reshape(nb, ml, d)
    v_ref[...] = z[:, D_MODEL:].reshape(nb, ml, d)


def _memkv(mem, g, w_kv):
    b, ml, _ = mem.shape
    nb = MEM_BATCH if b % MEM_BATCH == 0 else 1
    blk = pl.BlockSpec((nb, ml, D_MODEL), lambda i: (i, 0, 0))
    sd = jax.ShapeDtypeStruct((b, ml, D_MODEL), BF16)
    return pl.pallas_call(
        _memkv_body,
        grid=(b // nb,),
        in_specs=[blk, _resident((1, D_MODEL)), _resident(w_kv.shape)],
        out_specs=(blk, blk),
        out_shape=(sd, sd),
        compiler_params=_params(("parallel",)),
        name="memkv",
    )(mem, g.reshape(1, D_MODEL), w_kv)


def _memattn_body(x_ref, g_ref, k_ref, v_ref, wq_ref, wo_ref, o_ref):
    x = x_ref[0]
    h = _rms(x, g_ref[...]).astype(BF16)
    q = (_dot(h, wq_ref[...]) * (MEM_HEAD_DIM ** -0.5)).astype(BF16)
    outs = []
    for hd in range(MEM_HEADS):
        cols = slice(hd * MEM_HEAD_DIM, (hd + 1) * MEM_HEAD_DIM)
        s = _dot_nt(q[:, cols], k_ref[0, :, cols])
        e = jnp.exp(s - jnp.max(s, axis=-1, keepdims=True))
        p = e / jnp.sum(e, axis=-1, keepdims=True)
        outs.append(_dot(p.astype(BF16), v_ref[0, :, cols]))
    o = jnp.concatenate(outs, axis=1).astype(BF16)
    o_ref[0] = x + _dot(o, wo_ref[...])


def _memattn(x, g, k, v, w_q, w_o):
    b, s, _ = x.shape
    ml = k.shape[1]
    tok = pl.BlockSpec((1, TOK_TILE, D_MODEL), lambda i, j: (i, j, 0))
    kvb = pl.BlockSpec((1, ml, D_MODEL), lambda i, j: (i, 0, 0))
    return pl.pallas_call(
        _memattn_body,
        grid=(b, s // TOK_TILE),
        in_specs=[tok, _resident((1, D_MODEL)), kvb, kvb, _resident(w_q.shape), _resident(w_o.shape)],
        out_specs=tok,
        out_shape=jax.ShapeDtypeStruct((b, s, D_MODEL), F32),
        compiler_params=_params(("parallel", "parallel")),
        name="memattn",
    )(x, g.reshape(1, D_MODEL), k, v, w_q, w_o)


def kernel(x, mem, rel_bias_table, ffn1_norm, ffn1_w_gate, ffn1_w_up, ffn1_w_down, mix_norm, mix_w_in, mix_b_forget, cmp_pos_k, cmp_pos_v, cmp_k_w1, cmp_k_w2, cmp_v_w1, cmp_v_w2, w_up_nsa, w_up_fox, mix_w_out, mem_q_norm, mem_kv_norm, mem_w_q, mem_w_kv, mem_w_o, ffn2_norm, ffn2_w_gate, ffn2_w_up, ffn2_w_down, final_norm):
    b, s, d = x.shape
    depth = ffn1_norm.shape[0]
    bf = lambda w: w.astype(BF16)
    toep, bias_c = _relbias(rel_bias_table, s)
    x = x.reshape(b * s, d)
    for l in range(depth):
        last = l == depth - 1
        x = _ffn(x, ffn1_norm[l], ffn1_w_gate[l], ffn1_w_up[l], ffn1_w_down[l])

        w_nsa, w_fox, w_small, w_a, w_b = _pack_w_in(mix_w_in, l)
        qn, kc, vc, ks, vs, kw, vw, qf, kf, vf, gates, flog = _inproj(
            x.reshape(b, s, d), mix_norm[l], w_nsa, w_fox, w_small)
        kcc, vcc = _compress(kc, vc, cmp_pos_k[l], cmp_pos_v[l], cmp_k_w1[l], cmp_k_w2[l],
                             cmp_v_w1[l], cmp_v_w2[l])
        o_nsa = _nsa(qn, kcc, vcc, ks, vs, kw, vw, gates, toep, bias_c)
        c = _fcum(flog, mix_b_forget[l])
        o_fox = _fox(qf, kf, vf, c)
        x = _mixout(x, mix_norm[l], o_nsa.reshape(b * s, NSA_Q), o_fox.reshape(b * s, FOX_W),
                    w_a, w_b, bf(w_up_nsa[l]), bf(w_up_fox[l]), bf(mix_w_out[l]))

        mk, mv = _memkv(mem, mem_kv_norm[l], bf(mem_w_kv[l]))
        x = _memattn(x.reshape(b, s, d), mem_q_norm[l], mk, mv, bf(mem_w_q[l]), bf(mem_w_o[l]))
        x = _ffn(x.reshape(b * s, d), ffn2_norm[l], ffn2_w_gate[l], ffn2_w_up[l],
                 ffn2_w_down[l], final_g=final_norm if last else None)
    return x.reshape(b, s, d)
```

```python
import functools
import math

import numpy as np
import jax
import jax.numpy as jnp
from jax import lax
from jax.experimental import pallas as pl
from jax.experimental.pallas import tpu as pltpu

D_MODEL = 1024
D_FF = 2816
HEAD_DIM = 64
NSA_HEADS = 8
NSA_KV_GROUPS = 2
NSA_HPG = NSA_HEADS // NSA_KV_GROUPS
CMP_BLOCK = 32
CMP_STRIDE = 16
CMP_HIDDEN = 256
SEL_BLOCK = 64
SEL_TOP_N = 16
WINDOW = 512
FOX_HEADS = 8
MEM_HEADS = 4
MEM_HEAD_DIM = D_MODEL // MEM_HEADS
NUM_BUCKETS = 32
MAX_DISTANCE = 128
RMS_EPS = 1e-6
NEG_INF = -1e30
FORCE_BONUS = 1e4

NSA_Q = NSA_HEADS * HEAD_DIM
NSA_KV = NSA_KV_GROUPS * HEAD_DIM
FOX_W = FOX_HEADS * HEAD_DIM

LANES = 128
VMEM_LIMIT = 56 * 1024 * 1024

BF16 = jnp.bfloat16
F32 = jnp.float32

LOG2_E = math.log2(math.e)
Q_SCALE = HEAD_DIM ** -0.5 * LOG2_E

ATT_TILE = 256
TOK_TILE = 1024
FFN_TILE = 1024
FF_CHUNK = 256


def _bucket_thresholds():
    n = np.arange(0, 4 * MAX_DISTANCE)
    exact = NUM_BUCKETS // 2
    large = exact + (np.log(np.maximum(n, 1) / exact) / math.log(MAX_DISTANCE / exact)
                     * (NUM_BUCKETS - exact)).astype(np.int64)
    bucket = np.where(n < exact, n, np.minimum(large, NUM_BUCKETS - 1))
    assert np.all(np.diff(bucket) >= 0)
    return [int(np.argmax(bucket >= k)) for k in range(1, NUM_BUCKETS)]


BUCKET_THRESHOLDS = _bucket_thresholds()


def _dot(a, b):
    return jnp.dot(a, b, preferred_element_type=F32)


def _dot_nt(a, b):
    return lax.dot_general(a, b, (((1,), (1,)), ((), ())), preferred_element_type=F32)


def _row_halves(fn, a, b):
    h = a.shape[0] // 2
    return jnp.concatenate([fn(a[:h], b), fn(a[h:], b)], axis=0)


def _dot_pair(a, b):
    return _row_halves(_dot, a, b)


def _dot_nt_pair(a, b):
    return _row_halves(_dot_nt, a, b)


def _rms(x, g):
    return x * lax.rsqrt(jnp.mean(x * x, axis=-1, keepdims=True) + RMS_EPS) * g


def _resident(shape):
    nd = len(shape)
    return pl.BlockSpec(shape, lambda *_: (0,) * nd, pipeline_mode=pl.Buffered(1))


def _params(sem):
    return pltpu.CompilerParams(dimension_semantics=sem, vmem_limit_bytes=VMEM_LIMIT)


def _ffn_body(x_ref, g_ref, wg_hbm, wu_hbm, wd_hbm, *rest, final):
    if final:
        fg_ref, o_ref, wg_ref, wu_ref, wd_ref, sg_ref, su_ref, sd_ref, sem = rest
    else:
        o_ref, wg_ref, wu_ref, wd_ref, sg_ref, su_ref, sd_ref, sem = rest
    n_chunks = D_FF // FF_CHUNK

    def chunk_copies(c):
        cols = slice(c * FF_CHUNK, (c + 1) * FF_CHUNK)
        slot = c % 2
        return (pltpu.make_async_copy(wg_hbm.at[:, cols], sg_ref.at[slot], sem.at[0, slot]),
                pltpu.make_async_copy(wu_hbm.at[:, cols], su_ref.at[slot], sem.at[1, slot]),
                pltpu.make_async_copy(wd_hbm.at[cols, :], sd_ref.at[slot], sem.at[2, slot]))

    @pl.when(pl.program_id(0) == 0)
    def _():
        for cp in chunk_copies(0):
            cp.start()
        for c in range(n_chunks):
            sl = slice(c * FF_CHUNK, (c + 1) * FF_CHUNK)
            if c + 1 < n_chunks:
                for cp in chunk_copies(c + 1):
                    cp.start()
            for cp in chunk_copies(c):
                cp.wait()
            wg_ref[:, sl] = sg_ref[c % 2].astype(BF16)
            wu_ref[:, sl] = su_ref[c % 2].astype(BF16)
            wd_ref[sl, :] = sd_ref[c % 2].astype(BF16)

    x = x_ref[...]
    h = _rms(x, g_ref[...]).astype(BF16)
    acc = jnp.zeros(x.shape, F32)
    for c in range(n_chunks):
        sl = slice(c * FF_CHUNK, (c + 1) * FF_CHUNK)
        a = _dot(h, wg_ref[:, sl])
        b = _dot(h, wu_ref[:, sl])
        t = (a * jax.nn.sigmoid(a)) * b
        acc = acc + _dot(t.astype(BF16), wd_ref[sl, :])
    y = x + 0.5 * acc
    if final:
        y = _rms(y, fg_ref[...])
    o_ref[...] = y


def _ffn(x2d, g, wg, wu, wd, final_g=None):
    n = x2d.shape[0]
    final = final_g is not None
    tok = pl.BlockSpec((FFN_TILE, D_MODEL), lambda i: (i, 0))
    hbm = pl.BlockSpec(memory_space=pl.ANY)
    in_specs = [tok, _resident((1, D_MODEL)), hbm, hbm, hbm]
    args = [x2d, g.reshape(1, D_MODEL), wg, wu, wd]
    if final:
        in_specs.append(_resident((1, D_MODEL)))
        args.append(final_g.reshape(1, D_MODEL))
    return pl.pallas_call(
        functools.partial(_ffn_body, final=final),
        grid=(n // FFN_TILE,),
        in_specs=in_specs,
        out_specs=tok,
        out_shape=jax.ShapeDtypeStruct((n, D_MODEL), F32),
        scratch_shapes=[pltpu.VMEM((D_MODEL, D_FF), BF16), pltpu.VMEM((D_MODEL, D_FF), BF16),
                        pltpu.VMEM((D_FF, D_MODEL), BF16),
                        pltpu.VMEM((2, D_MODEL, FF_CHUNK), F32), pltpu.VMEM((2, D_MODEL, FF_CHUNK), F32),
                        pltpu.VMEM((2, FF_CHUNK, D_MODEL), F32),
                        pltpu.SemaphoreType.DMA((3, 2))],
        compiler_params=_params(("arbitrary",)),
        name="ffn_final" if final else "ffn",
    )(*args)


_C_CMP = NSA_Q
_C_KV = _C_CMP + 2 * NSA_KV
_W_NSA = _C_KV + 4 * NSA_KV
GATE_PAD = 16
_C_FLOG = NSA_KV_GROUPS * GATE_PAD
N_CHUNKS_PAD = 128
CHUNK_W = CMP_STRIDE * NSA_KV


def _inproj_body(x_ref, g_ref, wn_ref, wf_ref, ws_ref, blk_ref, qn_ref, kc_ref, vc_ref, ks_ref, vs_ref,
                 kw_ref, vw_ref, qf_ref, kf_ref, vf_ref, gate_ref, fl_ref, cmp_ref):
    h = _rms(x_ref[0], g_ref[...]).astype(BF16)
    z = _dot(h, wn_ref[...])
    qn_ref[0] = (z[:, :_C_CMP] * Q_SCALE).astype(BF16)
    rows = z.shape[0] // CMP_STRIDE
    for i, ref in enumerate((kc_ref, vc_ref)):
        cmp_ref[i] = z[:, _C_CMP + i * NSA_KV:_C_CMP + (i + 1) * NSA_KV]
        for r in range(CMP_STRIDE):
            tok = cmp_ref[i, pl.ds(r, rows, stride=CMP_STRIDE), :]
            ref[0, :, r * NSA_KV:(r + 1) * NSA_KV] = tok.astype(BF16)
    ones = jnp.ones((z.shape[0], HEAD_DIM), BF16)
    for i, (ref, extra) in enumerate(((ks_ref, blk_ref[...]), (vs_ref, ones), (kw_ref, None),
                                      (vw_ref, ones))):
        for g in range(NSA_KV_GROUPS):
            c0 = _C_KV + i * NSA_KV + g * HEAD_DIM
            val = z[:, c0:c0 + HEAD_DIM].astype(BF16)
            ref[0, g] = val if extra is None else jnp.concatenate([val, extra], axis=1)
    zf = _dot(h, wf_ref[...])
    for hd in range(FOX_HEADS):
        c0 = hd * HEAD_DIM
        qf_ref[0, hd] = (zf[:, c0:c0 + HEAD_DIM] * Q_SCALE).astype(BF16)
        kf_ref[0, hd] = zf[:, c0 + FOX_W:c0 + FOX_W + HEAD_DIM].astype(BF16)
        v = zf[:, c0 + 2 * FOX_W:c0 + 2 * FOX_W + HEAD_DIM].astype(BF16)
        vf_ref[0, hd] = jnp.concatenate([v, ones], axis=1)
    zs = _dot(h, ws_ref[...])
    for g in range(NSA_KV_GROUPS):
        gate_ref[0, g] = zs[:, g * GATE_PAD:(g + 1) * GATE_PAD]
    fl_ref[0] = zs.T[_C_FLOG:_C_FLOG + FOX_HEADS]


_IN_COLS = np.cumsum((0, NSA_Q, NSA_KV, NSA_KV, NSA_KV, NSA_KV, NSA_KV, NSA_KV, 3 * NSA_HEADS,
                      FOX_W, FOX_W, FOX_W, FOX_HEADS, D_MODEL, D_MODEL)).tolist()
W_PREP_ROWS = 256


def _wprep_body(w_ref, wn_ref, wf_ref, ws_ref, wa_ref, wb_ref):
    wt = w_ref[0]
    c = _IN_COLS
    k = wt.shape[1]
    wn_ref[...] = wt[:c[7]].T.astype(BF16)
    wf_ref[...] = wt[c[8]:c[11]].T.astype(BF16)
    wa_ref[...] = wt[c[12]:c[13]].T.astype(BF16)
    wb_ref[...] = wt[c[13]:c[14]].T.astype(BF16)
    small = []
    for g in range(NSA_KV_GROUPS):
        for br in range(3):
            c0 = c[7] + br * NSA_HEADS + g * NSA_HPG
            small.append(wt[c0:c0 + NSA_HPG])
        small.append(jnp.zeros((GATE_PAD - 3 * NSA_HPG, k), F32))
    small.append(wt[c[11]:c[12]])
    small.append(jnp.zeros((LANES - _C_FLOG - FOX_HEADS, k), F32))
    ws_ref[...] = jnp.concatenate(small, axis=0).T.astype(BF16)


def _pack_w_in(w_in, layer):
    _, k, n = w_in.shape
    c = _IN_COLS
    assert n == c[14]
    widths = (c[7], c[11] - c[8], LANES, D_MODEL, D_MODEL)
    return pl.pallas_call(
        _wprep_body,
        grid=(k // W_PREP_ROWS,),
        in_specs=[pl.BlockSpec((1, n, W_PREP_ROWS), lambda i: (layer, 0, i))],
        out_specs=tuple(pl.BlockSpec((W_PREP_ROWS, w), lambda i: (i, 0)) for w in widths),
        out_shape=tuple(jax.ShapeDtypeStruct((k, w), BF16) for w in widths),
        compiler_params=_params(("parallel",)),
        name="wprep",
    )(jnp.swapaxes(w_in, 1, 2))


def _inproj(x, g, w_nsa, w_fox, w_small):
    b, s, _ = x.shape
    grid = (b, s // TOK_TILE)
    tok = lambda c: pl.BlockSpec((1, TOK_TILE, c), lambda i, j: (i, j, 0))
    heads = lambda nh, w=HEAD_DIM: pl.BlockSpec((1, nh, TOK_TILE, w), lambda i, j: (i, 0, j, 0))
    chunks = pl.BlockSpec((1, TOK_TILE // CMP_STRIDE, CHUNK_W), lambda i, j: (i, j, 0))
    sds = jax.ShapeDtypeStruct
    kv = sds((b, NSA_KV_GROUPS, s, HEAD_DIM), BF16)
    kv2 = sds((b, NSA_KV_GROUPS, s, 2 * HEAD_DIM), BF16)
    fx = sds((b, FOX_HEADS, s, HEAD_DIM), BF16)
    cmp_in = sds((b, s // CMP_STRIDE, CHUNK_W), BF16)
    out_shape = (sds((b, s, NSA_Q), BF16), cmp_in, cmp_in,
                 kv2, kv2, kv, kv2, fx, fx, sds((b, FOX_HEADS, s, 2 * HEAD_DIM), BF16),
                 sds((b, NSA_KV_GROUPS, s, GATE_PAD), F32), sds((b, FOX_HEADS, s), F32))
    out_specs = (tok(NSA_Q), chunks, chunks,
                 heads(NSA_KV_GROUPS, 2 * HEAD_DIM), heads(NSA_KV_GROUPS, 2 * HEAD_DIM),
                 heads(NSA_KV_GROUPS), heads(NSA_KV_GROUPS, 2 * HEAD_DIM),
                 heads(FOX_HEADS), heads(FOX_HEADS), heads(FOX_HEADS, 2 * HEAD_DIM),
                 pl.BlockSpec((1, NSA_KV_GROUPS, TOK_TILE, GATE_PAD), lambda i, j: (i, 0, j, 0)),
                 pl.BlockSpec((1, FOX_HEADS, TOK_TILE), lambda i, j: (i, 0, j)))
    key_blk = np.arange(s) // SEL_BLOCK
    blk_onehot = jnp.asarray(key_blk[:, None] == np.arange(HEAD_DIM)[None, :], BF16)
    return pl.pallas_call(
        _inproj_body,
        grid=grid,
        in_specs=[tok(D_MODEL), _resident((1, D_MODEL)), _resident(w_nsa.shape),
                  _resident(w_fox.shape), _resident(w_small.shape),
                  pl.BlockSpec((TOK_TILE, HEAD_DIM), lambda i, j: (j, 0))],
        out_specs=out_specs,
        out_shape=out_shape,
        scratch_shapes=[pltpu.VMEM((2, TOK_TILE, NSA_KV), F32)],
        compiler_params=_params(("parallel", "parallel")),
        name="inproj",
    )(x, g.reshape(1, D_MODEL), w_nsa, w_fox, w_small, blk_onehot)


CMP_BATCH = 4

def _compress_body(xk_ref, xv_ref, pk_ref, pv_ref, wk1_ref, wv1_ref, wk2_ref, wv2_ref, ok_ref, ov_ref):
    nb = xk_ref.shape[0]
    n = nb * N_CHUNKS_PAD
    for x_ref, p_ref, w1_ref, w2_ref, o_ref in ((xk_ref, pk_ref, wk1_ref, wk2_ref, ok_ref),
                                                (xv_ref, pv_ref, wv1_ref, wv2_ref, ov_ref)):
        x = x_ref[...].reshape(n, CHUNK_W)
        a0 = _dot(x, w1_ref[0])
        a1 = _dot(x, w1_ref[1])
        c = _dot(p_ref[0], w1_ref[0]) + _dot(p_ref[1], w1_ref[1])
        pre = a0 + pltpu.roll(a1, n - 1, axis=0) + c[0:1, :]
        hid = jax.nn.gelu(pre).astype(BF16)
        out = _dot(hid, w2_ref[...]).astype(BF16)
        for i in range(nb):
            for g in range(NSA_KV_GROUPS):
                o_ref[i, g] = out[i * N_CHUNKS_PAD:(i + 1) * N_CHUNKS_PAD, g * HEAD_DIM:(g + 1) * HEAD_DIM]


def _pack_compress(pos, w1, w2):
    r = CMP_BLOCK // CMP_STRIDE
    assert NSA_KV_GROUPS == 2
    w1r = w1.astype(BF16).reshape(r, CMP_STRIDE, HEAD_DIM, CMP_HIDDEN)
    z1 = jnp.zeros_like(w1r)
    w1big = jnp.stack([jnp.concatenate([w1r, z1], axis=-1), jnp.concatenate([z1, w1r], axis=-1)], axis=2)
    w1big = w1big.reshape(r, CHUNK_W, NSA_KV_GROUPS * CMP_HIDDEN)
    z2 = jnp.zeros_like(w2)
    w2big = jnp.concatenate([jnp.concatenate([w2, z2], axis=1), jnp.concatenate([z2, w2], axis=1)], axis=0)
    p = pos.reshape(r, CMP_STRIDE, 1, HEAD_DIM)
    p = jnp.broadcast_to(p, (r, CMP_STRIDE, NSA_KV_GROUPS, HEAD_DIM)).reshape(r, 1, CHUNK_W)
    p = jnp.broadcast_to(p, (r, 16, CHUNK_W))
    return p.astype(BF16), w1big.astype(BF16), w2big.astype(BF16)


def _compress(xk, xv, pos_k, pos_v, k_w1, k_w2, v_w1, v_w2):
    b = xk.shape[0]
    assert xk.shape[1:] == (N_CHUNKS_PAD, CHUNK_W)
    pk, wk1, wk2 = _pack_compress(pos_k, k_w1, k_w2)
    pv, wv1, wv2 = _pack_compress(pos_v, v_w1, v_w2)
    nb = CMP_BATCH if b % CMP_BATCH == 0 else 1
    xs = pl.BlockSpec((nb, N_CHUNKS_PAD, CHUNK_W), lambda i: (i, 0, 0))
    os_ = pl.BlockSpec((nb, NSA_KV_GROUPS, N_CHUNKS_PAD, HEAD_DIM), lambda i: (i, 0, 0, 0))
    osd = jax.ShapeDtypeStruct((b, NSA_KV_GROUPS, N_CHUNKS_PAD, HEAD_DIM), BF16)
    return pl.pallas_call(
        _compress_body,
        grid=(b // nb,),
        in_specs=[xs, xs, _resident(pk.shape), _resident(pv.shape), _resident(wk1.shape),
                  _resident(wv1.shape), _resident(wk2.shape), _resident(wv2.shape)],
        out_specs=(os_, os_),
        out_shape=(osd, osd),
        compiler_params=_params(("parallel",)),
        name="compress",
    )(xk, xv, pk, pv, wk1, wv1, wk2, wv2)


def _bias_lookup(tbl_ref, head, dist):
    acc = jnp.full(dist.shape, tbl_ref[0, head], F32)
    for k, thr in enumerate(BUCKET_THRESHOLDS):
        acc = jnp.where(dist >= thr, tbl_ref[k + 1, head], acc)
    return (acc - tbl_ref[NUM_BUCKETS - 1, head]) * LOG2_E


def _relbias_body(tbl_ref, toep_ref, bc_ref):
    head = pl.program_id(0)
    t = ATT_TILE
    i = lax.broadcasted_iota(jnp.int32, (t, 2 * t), 0)
    u = lax.broadcasted_iota(jnp.int32, (t, 2 * t), 1)
    d = i - u + t
    toep_ref[0] = jnp.where(d >= 0, _bias_lookup(tbl_ref, head, d), NEG_INF)
    n_tiles = bc_ref.shape[1] // t
    per_tile = t // CMP_STRIDE
    off = per_tile * (n_tiles - 1)
    width = 2 * N_CHUNKS_PAD
    assert off + N_CHUNKS_PAD <= width
    i = lax.broadcasted_iota(jnp.int32, (t, width), 0)
    u = lax.broadcasted_iota(jnp.int32, (t, width), 1)
    dc = i - ((u - off) * CMP_STRIDE + CMP_BLOCK - 1)
    strip = jnp.where(dc >= 0, _bias_lookup(tbl_ref, head, dc), NEG_INF)
    for q in range(n_tiles):
        lo = off - q * per_tile
        bc_ref[0, q * t:(q + 1) * t, :] = strip[:, lo:lo + N_CHUNKS_PAD]


def _relbias(tbl, s):
    return pl.pallas_call(
        _relbias_body,
        grid=(NSA_HEADS,),
        in_specs=[pl.BlockSpec(memory_space=pltpu.SMEM)],
        out_specs=(pl.BlockSpec((1, ATT_TILE, 2 * ATT_TILE), lambda h: (h, 0, 0)),
                   pl.BlockSpec((1, s, N_CHUNKS_PAD), lambda h: (h, 0, 0))),
        out_shape=(jax.ShapeDtypeStruct((NSA_HEADS, ATT_TILE, 2 * ATT_TILE), F32),
                   jax.ShapeDtypeStruct((NSA_HEADS, s, N_CHUNKS_PAD), F32)),
        compiler_params=_params(("parallel",)),
        name="relbias",
    )(tbl)


def _lane_fold(x, op):
    return functools.reduce(op, [x[:, i * LANES:(i + 1) * LANES] for i in range(x.shape[1] // LANES)])


def _lane_tile(x, width):
    return jnp.concatenate([x] * (width // LANES), axis=1)


def _row_max_tile(rmax):
    return jnp.broadcast_to(jnp.max(rmax, axis=-1, keepdims=True), rmax.shape)


def _normalize(acc):
    return (acc / pltpu.roll(acc, HEAD_DIM, axis=1))[:, :HEAD_DIM]


def _split3(x):
    hi = x.astype(BF16)
    r = x - hi.astype(F32)
    mid = r.astype(BF16)
    lo = (r - mid.astype(F32)).astype(BF16)
    return hi, mid, lo


def _nsa_body(q_ref, kc_ref, vc_ref, ks_ref, vs_ref, kw_ref, vw_ref, gate_a_ref, gate_b_ref, toep_ref,
              bc_ref, ov_ref, o_ref, s_ref, near_ref, mb_ref, acc_ref, part_ref, *, n_tiles):
    t = ATT_TILE
    j4 = NSA_HPG
    rows = j4 * t
    groups = range(NSA_KV_GROUPS)
    step = pl.program_id(1)
    n_sel = n_tiles * (t // SEL_BLOCK)

    def heads(x):
        return x.reshape(j4, t, x.shape[-1])

    def flat(x):
        return x.reshape(rows, x.shape[-1])

    def key_tile(ref, g, n):
        return ref[0, g, pl.ds(pl.multiple_of(n * t, t), t), :]

    def gated(gate_ref, g, branch, o):
        gate = jax.nn.sigmoid(gate_ref[0, g])
        return [gate[:, branch * j4 + j:branch * j4 + j + 1] * o[j * t:(j + 1) * t] for j in range(j4)]

    def probs(par, g, s):
        return jnp.exp2(s - _lane_tile(mb_ref[par, g], t)).astype(BF16)

    def open_tile(m, par):
        m1 = jnp.maximum(m - 1, 0)
        m2 = jnp.maximum(m - 2, 0)
        no_prev = jnp.where(m >= 1, 0.0, NEG_INF)
        qs = []
        for g in groups:
            qt = q_ref[0, :, g * j4 * HEAD_DIM:(g + 1) * j4 * HEAD_DIM]
            qs.append(jnp.concatenate([qt[:, j * HEAD_DIM:(j + 1) * HEAD_DIM] for j in range(j4)], axis=0))

        def biases(g):
            toep = toep_ref[g * j4:(g + 1) * j4]
            return toep[:, :, t:], toep[:, :, :t] + no_prev

        cmp_out = [_nsa_compressed(qs[g], kc_ref[0, g], vc_ref[0, g], bc_ref[g * j4:(g + 1) * j4],
                                   ov_ref[...], m, n_sel) for g in groups]
        part = []
        for g in groups:
            diag_bias, prev_bias = biases(g)
            ii = lax.broadcasted_iota(jnp.int32, (t, t), 0)
            jj = lax.broadcasted_iota(jnp.int32, (t, t), 1)
            tri = jnp.where(jj > ii, 0.0, NEG_INF) + jnp.where(m >= 2, 0.0, NEG_INF)
            w0 = flat(heads(_dot_nt_pair(qs[g], key_tile(kw_ref, g, m))) + diag_bias)
            w1 = flat(heads(_dot_nt_pair(qs[g], key_tile(kw_ref, g, m1))) + prev_bias)
            w2 = flat(heads(_dot_nt_pair(qs[g], key_tile(kw_ref, g, m2))) + tri[None])
            wmax = _lane_tile(_row_max_tile(functools.reduce(
                jnp.maximum, [_lane_fold(w, jnp.maximum) for w in (w0, w1, w2)])), t)
            acc = sum(_dot_pair(jnp.exp2(w - wmax).astype(BF16), key_tile(vw_ref, g, n))
                      for w, n in ((w0, m), (w1, m1), (w2, m2)))
            part += [a + b for a, b in zip(gated(gate_a_ref, g, 0, cmp_out[g][0]),
                                           gated(gate_a_ref, g, 2, _normalize(acc)))]
        part_ref[par] = jnp.concatenate(part, axis=1)

        q_aug = []
        for g in groups:
            diag_bias, prev_bias = biases(g)
            q_aug.append(jnp.concatenate([qs[g], jnp.concatenate([cmp_out[g][1]] * j4, axis=0)], axis=1))
            s0 = flat(heads(_dot_nt_pair(q_aug[g], key_tile(ks_ref, g, m))) + diag_bias)
            s1 = flat(heads(_dot_nt_pair(q_aug[g], key_tile(ks_ref, g, m1))) + prev_bias)
            s2 = _dot_nt_pair(q_aug[g], key_tile(ks_ref, g, m2))
            near_ref[par, g, 0] = s0
            near_ref[par, g, 1] = s1
            s_ref[g, m2] = s2
            far = jnp.where(m >= 2, _lane_fold(s2, jnp.maximum), NEG_INF)
            mb_ref[par, g] = jnp.maximum(jnp.maximum(_lane_fold(s0, jnp.maximum),
                                                     _lane_fold(s1, jnp.maximum)), far)
        return q_aug

    def open_far(par, q_aug, n):
        for g in groups:
            s = _dot_nt_pair(q_aug[g], key_tile(ks_ref, g, n))
            s_ref[g, n] = s
            mb_ref[par, g] = jnp.maximum(mb_ref[par, g], _lane_fold(s, jnp.maximum))

    def close_tile(m, par):
        m1 = jnp.maximum(m - 1, 0)
        for g in groups:
            mb_ref[par, g] = _row_max_tile(mb_ref[par, g])
            acc_ref[par, g] = (_dot_pair(probs(par, g, near_ref[par, g, 0]), key_tile(vs_ref, g, m))
                               + _dot_pair(probs(par, g, near_ref[par, g, 1]), key_tile(vs_ref, g, m1)))

    def close_far(par, n):
        for g in groups:
            acc_ref[par, g] += _dot_pair(probs(par, g, s_ref[g, n]), key_tile(vs_ref, g, n))

    def finish(par):
        outs = []
        for g in groups:
            outs += gated(gate_b_ref, g, 1, _normalize(acc_ref[par, g]))
        o_ref[0] = (part_ref[par] + jnp.concatenate(outs, axis=1)).astype(BF16)

    @pl.when(step == 0)
    def _():
        open_tile(step, 0)

    @pl.when(step == 1)
    def _():
        open_tile(step, 1)
        close_tile(step - 1, 0)

    for par in (0, 1):
        @pl.when(jnp.logical_and(jnp.logical_and(step >= 2, step < n_tiles), step % 2 == par))
        def _(par=par):
            finish(par)
            q_aug = open_tile(step, par)
            close_tile(step - 1, 1 - par)

            @pl.loop(0, step - 2)
            def _(n):
                close_far(1 - par, n)
                open_far(par, q_aug, n)

    last = (n_tiles - 1) % 2

    @pl.when(step == n_tiles)
    def _():
        finish(1 - last)
        close_tile(step - 1, last)

        @pl.loop(0, step - 2)
        def _(n):
            close_far(last, n)

    @pl.when(step == n_tiles + 1)
    def _():
        finish(last)


def _nsa_compressed(q, kc, vc, bc, ov, m, n_sel):
    t = ATT_TILE
    j4 = NSA_HPG

    def heads(x):
        return x.reshape(j4, t, x.shape[-1])

    def flat(x):
        return x.reshape(j4 * t, x.shape[-1])

    sc = heads(_dot_nt_pair(q, kc)) + bc
    valid = bc > 0.5 * NEG_INF
    e = jnp.where(valid, jnp.exp2(sc - jnp.max(sc, axis=-1, keepdims=True)), 0.0)
    l = jnp.sum(e, axis=-1, keepdims=True)
    pc = jnp.where(l > 0.0, e / l, 0.0)
    o_c = _dot_pair(flat(pc).astype(BF16), vc)
    psum = pc[0] + pc[1] + pc[2] + pc[3]
    imp = sum(_dot(part, ov) for part in _split3(psum)).T[:n_sel]
    blk = lax.broadcasted_iota(jnp.int32, (n_sel, t), 0)
    tpos = m * t + lax.broadcasted_iota(jnp.int32, (n_sel, t), 1)
    cur = tpos // SEL_BLOCK
    bonus = jnp.where(blk == 0, FORCE_BONUS,
                      jnp.where(blk == cur, FORCE_BONUS, jnp.where(blk == cur - 1, FORCE_BONUS, 0.0)))
    imp = jnp.where(blk * SEL_BLOCK <= tpos, imp + bonus, NEG_INF)
    lanes = 4
    partial = [jnp.zeros((n_sel, t), F32) for _ in range(lanes)]
    for i in range(n_sel):
        row = imp[i:i + 1, :]
        before = jnp.where(blk > i, 1.0, 0.0)
        partial[i % lanes] = partial[i % lanes] + jnp.where(row > imp, 1.0,
                                                            jnp.where(row == imp, before, 0.0))
    rank = (partial[0] + partial[1]) + (partial[2] + partial[3])
    selb = jnp.where(rank < float(SEL_TOP_N), 0.0, NEG_INF)
    selb = jnp.concatenate([selb, jnp.zeros((LANES - n_sel, t), F32)], axis=0).T
    return o_c, selb[:, :HEAD_DIM].astype(BF16)


def _overlap_table(s):
    n_sel = s // SEL_BLOCK
    nc = N_CHUNKS_PAD
    c0 = np.arange(nc)[:, None] * CMP_STRIDE
    s0 = np.arange(n_sel)[None, :] * SEL_BLOCK
    ov = np.clip(np.minimum(c0 + CMP_BLOCK, s0 + SEL_BLOCK) - np.maximum(c0, s0), 0, None) / CMP_STRIDE
    ov[nc - 1] = 0.0
    return jnp.asarray(np.pad(ov, ((0, 0), (0, LANES - n_sel))), BF16)


def _nsa(qn, kc, vc, ks, vs, kw, vw, gates, toep, bc):
    b, s, _ = qn.shape
    t = ATT_TILE
    assert s // SEL_BLOCK <= HEAD_DIM
    ov = _overlap_table(s)
    rows = NSA_HPG * t
    ng = NSA_KV_GROUPS
    nq = s // t
    assert nq >= 2
    seq = lambda n, w=HEAD_DIM: pl.BlockSpec((1, ng, n, w), lambda i, j: (i, 0, 0, 0))
    opened = lambda j: jnp.minimum(j, nq - 1)
    closed = lambda j: jnp.maximum(j - 2, 0)
    return pl.pallas_call(
        functools.partial(_nsa_body, n_tiles=nq),
        grid=(b, nq + 2),
        in_specs=[pl.BlockSpec((1, t, NSA_Q), lambda i, j: (i, opened(j), 0)),
                  seq(N_CHUNKS_PAD), seq(N_CHUNKS_PAD),
                  seq(s, 2 * HEAD_DIM), seq(s, 2 * HEAD_DIM), seq(s), seq(s, 2 * HEAD_DIM),
                  pl.BlockSpec((1, ng, t, GATE_PAD), lambda i, j: (i, 0, opened(j), 0)),
                  pl.BlockSpec((1, ng, t, GATE_PAD), lambda i, j: (i, 0, closed(j), 0)),
                  _resident((NSA_HEADS, t, 2 * t)),
                  pl.BlockSpec((NSA_HEADS, t, N_CHUNKS_PAD), lambda i, j: (0, opened(j), 0)),
                  _resident(ov.shape)],
        out_specs=pl.BlockSpec((1, t, NSA_Q), lambda i, j: (i, closed(j), 0)),
        out_shape=jax.ShapeDtypeStruct((b, s, NSA_Q), BF16),
        scratch_shapes=[pltpu.VMEM((ng, nq - 2, rows, t), F32),
                        pltpu.VMEM((2, ng, 2, rows, t), F32),
                        pltpu.VMEM((2, ng, rows, LANES), F32),
                        pltpu.VMEM((2, ng, rows, 2 * HEAD_DIM), F32),
                        pltpu.VMEM((2, t, NSA_Q), F32)],
        compiler_params=_params(("parallel", "arbitrary")),
        name="nsa",
    )(qn, kc, vc, ks, vs, kw, vw, gates, gates, toep, bc, ov)


def _fcum_body(f_ref, b_ref, o_ref):
    z = f_ref[0] + b_ref[...]
    x = -(jnp.maximum(-z, 0.0) + jnp.log1p(jnp.exp(-jnp.abs(z))))
    n = x.shape[-1]
    lane = lax.broadcasted_iota(jnp.int32, x.shape, 1)
    sh = 1
    while sh < n:
        x = x + jnp.where(lane >= sh, pltpu.roll(x, sh, axis=1), 0.0)
        sh *= 2
    o_ref[0] = x * LOG2_E


def _fcum(f_t, b_forget):
    b, h, s = f_t.shape
    blk = pl.BlockSpec((1, h, s), lambda i: (i, 0, 0))
    return pl.pallas_call(
        _fcum_body,
        grid=(b,),
        in_specs=[blk, _resident((h, 1))],
        out_specs=blk,
        out_shape=jax.ShapeDtypeStruct((b, h, s), F32),
        compiler_params=_params(("parallel",)),
        name="fcum",
    )(f_t, b_forget.reshape(h, 1))


FOX_PAIR = 2
FOX_TILE = 512


def _causal_bias(t):
    i = np.arange(t)
    return jnp.asarray(np.where(i[None, :] <= i[:, None], 0.0, NEG_INF), F32)


def _fox_body(q_ref, k_ref, v_ref, c_ref, cm_ref, o_ref, s_ref, mb_ref, acc_ref):
    t = FOX_TILE
    heads = range(FOX_PAIR)
    nq = q_ref.shape[2] // t

    def keys(n):
        return pl.ds(pl.multiple_of(n * t, t), t)

    def scores(m, hh, n):
        q = q_ref[0, hh, m * t:(m + 1) * t, :]
        return _dot_nt(q, k_ref[0, hh, keys(n), :]) - c_ref[0, hh, :, keys(n)]

    def scores_diag(m):
        for hh in heads:
            s = scores(m, hh, m) + cm_ref[...]
            s_ref[m % 2, hh, m] = s
            mb_ref[m % 2, hh] = _lane_fold(s, jnp.maximum)

    def scores_far(m, n):
        for hh in heads:
            s = scores(m, hh, n)
            s_ref[m % 2, hh, n] = s
            mb_ref[m % 2, hh] = jnp.maximum(mb_ref[m % 2, hh], _lane_fold(s, jnp.maximum))

    def finish_max(m):
        for hh in heads:
            mb_ref[m % 2, hh] = _row_max_tile(mb_ref[m % 2, hh])

    def weigh(m, n):
        for hh in heads:
            p = jnp.exp2(s_ref[m % 2, hh, n] - _lane_tile(mb_ref[m % 2, hh], t)).astype(BF16)
            acc_ref[hh] += _dot(p, v_ref[0, hh, keys(n), :])

    scores_diag(0)
    finish_max(0)
    for m in range(nq):
        ahead = m + 1 < nq
        if ahead:
            scores_diag(m + 1)
        for hh in heads:
            acc_ref[hh] = jnp.zeros((t, 2 * HEAD_DIM), F32)

        @pl.loop(0, m + 1)
        def _(n, m=m, ahead=ahead):
            weigh(m, n)
            if ahead:
                scores_far(m + 1, n)

        if ahead:
            finish_max(m + 1)
        o_ref[0, m * t:(m + 1) * t, :] = jnp.concatenate(
            [_normalize(acc_ref[hh]) for hh in heads], axis=1).astype(BF16)


def _fox(qf, kf, vf, c):
    b, h, s, _ = qf.shape
    t = FOX_TILE
    seq = lambda w: pl.BlockSpec((1, FOX_PAIR, s, w), lambda i, p: (i, p, 0, 0))
    return pl.pallas_call(
        _fox_body,
        grid=(b, h // FOX_PAIR),
        in_specs=[seq(HEAD_DIM), seq(HEAD_DIM), seq(2 * HEAD_DIM),
                  pl.BlockSpec((1, FOX_PAIR, 1, s), lambda i, p: (i, p, 0, 0)),
                  _resident((t, t))],
        out_specs=pl.BlockSpec((1, s, FOX_PAIR * HEAD_DIM), lambda i, p: (i, 0, p)),
        out_shape=jax.ShapeDtypeStruct((b, s, h * HEAD_DIM), BF16),
        scratch_shapes=[pltpu.VMEM((2, FOX_PAIR, s // t, t, t), F32),
                        pltpu.VMEM((2, FOX_PAIR, t, LANES), F32),
                        pltpu.VMEM((FOX_PAIR, t, 2 * HEAD_DIM), F32)],
        compiler_params=_params(("parallel", "parallel")),
        name="fox",
    )(qf, kf, vf, c.reshape(b, h, 1, s), _causal_bias(t))


def _mixout_body(x_ref, g_ref, on_ref, of_ref, wa_ref, wb_ref, wun_ref, wuf_ref, wo_ref, o_ref):
    x = x_ref[...]
    h = _rms(x, g_ref[...]).astype(BF16)
    y = (jax.nn.sigmoid(_dot(h, wa_ref[...])) * _dot(on_ref[...], wun_ref[...])
         + jax.nn.sigmoid(_dot(h, wb_ref[...])) * _dot(of_ref[...], wuf_ref[...]))
    o_ref[...] = x + _dot(y.astype(BF16), wo_ref[...])


def _mixout(x2d, g, o_nsa, o_fox, w_a, w_b, w_un, w_uf, w_o):
    n = x2d.shape[0]
    tok = lambda c: pl.BlockSpec((TOK_TILE, c), lambda i: (i, 0))
    return pl.pallas_call(
        _mixout_body,
        grid=(n // TOK_TILE,),
        in_specs=[tok(D_MODEL), _resident((1, D_MODEL)), tok(NSA_Q), tok(FOX_W),
                  _resident(w_a.shape), _resident(w_b.shape), _resident(w_un.shape),
                  _resident(w_uf.shape), _resident(w_o.shape)],
        out_specs=tok(D_MODEL),
        out_shape=jax.ShapeDtypeStruct((n, D_MODEL), F32),
        compiler_params=_params(("parallel",)),
        name="mixout",
    )(x2d, g.reshape(1, D_MODEL), o_nsa, o_fox, w_a, w_b, w_un, w_uf, w_o)


MEM_BATCH = 4


def _memkv_body(m_ref, g_ref, w_ref, k_ref, v_ref):
    nb, ml, d = m_ref.shape
    h = _rms(m_ref[...].reshape(nb * ml, d), g_ref[...]).astype(BF16)
    z = _dot(h, w_ref[...]).astype(BF16)
    k_ref[...] = z[:, :D_MODEL].reshape(nb, ml, d)
    v_ref[...] = z[:, D_MODEL:].reshape(nb, ml, d)


def _memkv(mem, g, w_kv):
    b, ml, _ = mem.shape
    nb = MEM_BATCH if b % MEM_BATCH == 0 else 1
    blk = pl.BlockSpec((nb, ml, D_MODEL), lambda i: (i, 0, 0))
    sd = jax.ShapeDtypeStruct((b, ml, D_MODEL), BF16)
    return pl.pallas_call(
        _memkv_body,
        grid=(b // nb,),
        in_specs=[blk, _resident((1, D_MODEL)), _resident(w_kv.shape)],
        out_specs=(blk, blk),
        out_shape=(sd, sd),
        compiler_params=_params(("parallel",)),
        name="memkv",
    )(mem, g.reshape(1, D_MODEL), w_kv)


def _memattn_body(x_ref, g_ref, k_ref, v_ref, wq_ref, wo_ref, o_ref):
    x = x_ref[0]
    h = _rms(x, g_ref[...]).astype(BF16)
    q = (_dot(h, wq_ref[...]) * (MEM_HEAD_DIM ** -0.5)).astype(BF16)
    outs = []
    for hd in range(MEM_HEADS):
        cols = slice(hd * MEM_HEAD_DIM, (hd + 1) * MEM_HEAD_DIM)
        s = _dot_nt(q[:, cols], k_ref[0, :, cols])
        e = jnp.exp(s - jnp.max(s, axis=-1, keepdims=True))
        p = e / jnp.sum(e, axis=-1, keepdims=True)
        outs.append(_dot(p.astype(BF16), v_ref[0, :, cols]))
    o = jnp.concatenate(outs, axis=1).astype(BF16)
    o_ref[0] = x + _dot(o, wo_ref[...])


def _memattn(x, g, k, v, w_q, w_o):
    b, s, _ = x.shape
    ml = k.shape[1]
    tok = pl.BlockSpec((1, TOK_TILE, D_MODEL), lambda i, j: (i, j, 0))
    kvb = pl.BlockSpec((1, ml, D_MODEL), lambda i, j: (i, 0, 0))
    return pl.pallas_call(
        _memattn_body,
        grid=(b, s // TOK_TILE),
        in_specs=[tok, _resident((1, D_MODEL)), kvb, kvb, _resident(w_q.shape), _resident(w_o.shape)],
        out_specs=tok,
        out_shape=jax.ShapeDtypeStruct((b, s, D_MODEL), F32),
        compiler_params=_params(("parallel", "parallel")),
        name="memattn",
    )(x, g.reshape(1, D_MODEL), k, v, w_q, w_o)


def kernel(x, mem, rel_bias_table, ffn1_norm, ffn1_w_gate, ffn1_w_up, ffn1_w_down, mix_norm, mix_w_in, mix_b_forget, cmp_pos_k, cmp_pos_v, cmp_k_w1, cmp_k_w2, cmp_v_w1, cmp_v_w2, w_up_nsa, w_up_fox, mix_w_out, mem_q_norm, mem_kv_norm, mem_w_q, mem_w_kv, mem_w_o, ffn2_norm, ffn2_w_gate, ffn2_w_up, ffn2_w_down, final_norm):
    b, s, d = x.shape
    depth = ffn1_norm.shape[0]
    bf = lambda w: w.astype(BF16)
    toep, bias_c = _relbias(rel_bias_table, s)
    x = x.reshape(b * s, d)
    for l in range(depth):
        last = l == depth - 1
        x = _ffn(x, ffn1_norm[l], ffn1_w_gate[l], ffn1_w_up[l], ffn1_w_down[l])

        w_nsa, w_fox, w_small, w_a, w_b = _pack_w_in(mix_w_in, l)
        qn, kc, vc, ks, vs, kw, vw, qf, kf, vf, gates, flog = _inproj(
            x.reshape(b, s, d), mix_norm[l], w_nsa, w_fox, w_small)
        kcc, vcc = _compress(kc, vc, cmp_pos_k[l], cmp_pos_v[l], cmp_k_w1[l], cmp_k_w2[l],
                             cmp_v_w1[l], cmp_v_w2[l])
        o_nsa = _nsa(qn, kcc, vcc, ks, vs, kw, vw, gates, toep, bias_c)
        c = _fcum(flog, mix_b_forget[l])
        o_fox = _fox(qf, kf, vf, c)
        x = _mixout(x, mix_norm[l], o_nsa.reshape(b * s, NSA_Q), o_fox.reshape(b * s, FOX_W),
                    w_a, w_b, bf(w_up_nsa[l]), bf(w_up_fox[l]), bf(mix_w_out[l]))

        mk, mv = _memkv(mem, mem_kv_norm[l], bf(mem_w_kv[l]))
        x = _memattn(x.reshape(b, s, d), mem_q_norm[l], mk, mv, bf(mem_w_q[l]), bf(mem_w_o[l]))
        x = _ffn(x.reshape(b * s, d), ffn2_norm[l], ffn2_w_gate[l], ffn2_w_up[l],
                 ffn2_w_down[l], final_g=final_norm if last else None)
    return x.reshape(b, s, d)
```

```python
import functools
import math

import numpy as np
import jax
import jax.numpy as jnp
from jax import lax
from jax.experimental import pallas as pl
from jax.experimental.pallas import tpu as pltpu

D_MODEL = 1024
D_FF = 2816
HEAD_DIM = 64
NSA_HEADS = 8
NSA_KV_GROUPS = 2
NSA_HPG = NSA_HEADS // NSA_KV_GROUPS
CMP_BLOCK = 32
CMP_STRIDE = 16
CMP_HIDDEN = 256
SEL_BLOCK = 64
SEL_TOP_N = 16
WINDOW = 512
FOX_HEADS = 8
MEM_HEADS = 4
MEM_HEAD_DIM = D_MODEL // MEM_HEADS
NUM_BUCKETS = 32
MAX_DISTANCE = 128
RMS_EPS = 1e-6
NEG_INF = -1e30
FORCE_BONUS = 1e4

NSA_Q = NSA_HEADS * HEAD_DIM
NSA_KV = NSA_KV_GROUPS * HEAD_DIM
FOX_W = FOX_HEADS * HEAD_DIM

LANES = 128
VMEM_LIMIT = 56 * 1024 * 1024

BF16 = jnp.bfloat16
F32 = jnp.float32

ATT_TILE = 256
TOK_TILE = 1024
FFN_TILE = 1024
FF_CHUNK = 256


def _bucket_thresholds():
    n = np.arange(0, 4 * MAX_DISTANCE)
    exact = NUM_BUCKETS // 2
    large = exact + (np.log(np.maximum(n, 1) / exact) / math.log(MAX_DISTANCE / exact)
                     * (NUM_BUCKETS - exact)).astype(np.int64)
    bucket = np.where(n < exact, n, np.minimum(large, NUM_BUCKETS - 1))
    assert np.all(np.diff(bucket) >= 0)
    return [int(np.argmax(bucket >= k)) for k in range(1, NUM_BUCKETS)]


BUCKET_THRESHOLDS = _bucket_thresholds()


def _dot(a, b):
    return jnp.dot(a, b, preferred_element_type=F32)


def _dot_nt(a, b):
    return lax.dot_general(a, b, (((1,), (1,)), ((), ())), preferred_element_type=F32)


def _row_halves(fn, a, b):
    h = a.shape[0] // 2
    return jnp.concatenate([fn(a[:h], b), fn(a[h:], b)], axis=0)


def _dot_pair(a, b):
    return _row_halves(_dot, a, b)


def _dot_nt_pair(a, b):
    return _row_halves(_dot_nt, a, b)


def _rms(x, g):
    return x * lax.rsqrt(jnp.mean(x * x, axis=-1, keepdims=True) + RMS_EPS) * g


def _resident(shape):
    nd = len(shape)
    return pl.BlockSpec(shape, lambda *_: (0,) * nd, pipeline_mode=pl.Buffered(1))


def _params(sem):
    return pltpu.CompilerParams(dimension_semantics=sem, vmem_limit_bytes=VMEM_LIMIT)


def _ffn_body(x_ref, g_ref, wg_hbm, wu_hbm, wd_hbm, *rest, final):
    if final:
        fg_ref, o_ref, wg_ref, wu_ref, wd_ref, sg_ref, su_ref, sd_ref, sem = rest
    else:
        o_ref, wg_ref, wu_ref, wd_ref, sg_ref, su_ref, sd_ref, sem = rest
    n_chunks = D_FF // FF_CHUNK

    def chunk_copies(c):
        cols = slice(c * FF_CHUNK, (c + 1) * FF_CHUNK)
        slot = c % 2
        return (pltpu.make_async_copy(wg_hbm.at[:, cols], sg_ref.at[slot], sem.at[0, slot]),
                pltpu.make_async_copy(wu_hbm.at[:, cols], su_ref.at[slot], sem.at[1, slot]),
                pltpu.make_async_copy(wd_hbm.at[cols, :], sd_ref.at[slot], sem.at[2, slot]))

    @pl.when(pl.program_id(0) == 0)
    def _():
        for cp in chunk_copies(0):
            cp.start()
        for c in range(n_chunks):
            sl = slice(c * FF_CHUNK, (c + 1) * FF_CHUNK)
            if c + 1 < n_chunks:
                for cp in chunk_copies(c + 1):
                    cp.start()
            for cp in chunk_copies(c):
                cp.wait()
            wg_ref[:, sl] = sg_ref[c % 2].astype(BF16)
            wu_ref[:, sl] = su_ref[c % 2].astype(BF16)
            wd_ref[sl, :] = sd_ref[c % 2].astype(BF16)

    x = x_ref[...]
    h = _rms(x, g_ref[...]).astype(BF16)
    acc = jnp.zeros(x.shape, F32)
    for c in range(n_chunks):
        sl = slice(c * FF_CHUNK, (c + 1) * FF_CHUNK)
        a = _dot(h, wg_ref[:, sl])
        b = _dot(h, wu_ref[:, sl])
        t = (a * jax.nn.sigmoid(a)) * b
        acc = acc + _dot(t.astype(BF16), wd_ref[sl, :])
    y = x + 0.5 * acc
    if final:
        y = _rms(y, fg_ref[...])
    o_ref[...] = y


def _ffn(x2d, g, wg, wu, wd, final_g=None):
    n = x2d.shape[0]
    final = final_g is not None
    tok = pl.BlockSpec((FFN_TILE, D_MODEL), lambda i: (i, 0))
    hbm = pl.BlockSpec(memory_space=pl.ANY)
    in_specs = [tok, _resident((1, D_MODEL)), hbm, hbm, hbm]
    args = [x2d, g.reshape(1, D_MODEL), wg, wu, wd]
    if final:
        in_specs.append(_resident((1, D_MODEL)))
        args.append(final_g.reshape(1, D_MODEL))
    return pl.pallas_call(
        functools.partial(_ffn_body, final=final),
        grid=(n // FFN_TILE,),
        in_specs=in_specs,
        out_specs=tok,
        out_shape=jax.ShapeDtypeStruct((n, D_MODEL), F32),
        scratch_shapes=[pltpu.VMEM((D_MODEL, D_FF), BF16), pltpu.VMEM((D_MODEL, D_FF), BF16),
                        pltpu.VMEM((D_FF, D_MODEL), BF16),
                        pltpu.VMEM((2, D_MODEL, FF_CHUNK), F32), pltpu.VMEM((2, D_MODEL, FF_CHUNK), F32),
                        pltpu.VMEM((2, FF_CHUNK, D_MODEL), F32),
                        pltpu.SemaphoreType.DMA((3, 2))],
        compiler_params=_params(("arbitrary",)),
        name="ffn_final" if final else "ffn",
    )(*args)


_C_CMP = NSA_Q
_C_KV = _C_CMP + 2 * NSA_KV
_W_NSA = _C_KV + 4 * NSA_KV
GATE_PAD = 16
_C_FLOG = NSA_KV_GROUPS * GATE_PAD
N_CHUNKS_PAD = 128
CHUNK_W = CMP_STRIDE * NSA_KV


def _inproj_body(x_ref, g_ref, wn_ref, wf_ref, ws_ref, blk_ref, qn_ref, kc_ref, vc_ref, ks_ref, vs_ref,
                 kw_ref, vw_ref, qf_ref, kf_ref, vf_ref, gate_ref, fl_ref, cmp_ref):
    h = _rms(x_ref[0], g_ref[...]).astype(BF16)
    z = _dot(h, wn_ref[...])
    qn_ref[0] = (z[:, :_C_CMP] * (HEAD_DIM ** -0.5)).astype(BF16)
    rows = z.shape[0] // CMP_STRIDE
    for i, ref in enumerate((kc_ref, vc_ref)):
        cmp_ref[i] = z[:, _C_CMP + i * NSA_KV:_C_CMP + (i + 1) * NSA_KV]
        for r in range(CMP_STRIDE):
            tok = cmp_ref[i, pl.ds(r, rows, stride=CMP_STRIDE), :]
            ref[0, :, r * NSA_KV:(r + 1) * NSA_KV] = tok.astype(BF16)
    ones = jnp.ones((z.shape[0], HEAD_DIM), BF16)
    for i, (ref, extra) in enumerate(((ks_ref, blk_ref[...]), (vs_ref, ones), (kw_ref, None),
                                      (vw_ref, ones))):
        for g in range(NSA_KV_GROUPS):
            c0 = _C_KV + i * NSA_KV + g * HEAD_DIM
            val = z[:, c0:c0 + HEAD_DIM].astype(BF16)
            ref[0, g] = val if extra is None else jnp.concatenate([val, extra], axis=1)
    zf = _dot(h, wf_ref[...])
    for hd in range(FOX_HEADS):
        c0 = hd * HEAD_DIM
        qf_ref[0, hd] = (zf[:, c0:c0 + HEAD_DIM] * (HEAD_DIM ** -0.5)).astype(BF16)
        kf_ref[0, hd] = zf[:, c0 + FOX_W:c0 + FOX_W + HEAD_DIM].astype(BF16)
        v = zf[:, c0 + 2 * FOX_W:c0 + 2 * FOX_W + HEAD_DIM].astype(BF16)
        vf_ref[0, hd] = jnp.concatenate([v, ones], axis=1)
    zs = _dot(h, ws_ref[...])
    for g in range(NSA_KV_GROUPS):
        gate_ref[0, g] = zs[:, g * GATE_PAD:(g + 1) * GATE_PAD]
    fl_ref[0] = zs.T[_C_FLOG:_C_FLOG + FOX_HEADS]


_IN_COLS = np.cumsum((0, NSA_Q, NSA_KV, NSA_KV, NSA_KV, NSA_KV, NSA_KV, NSA_KV, 3 * NSA_HEADS,
                      FOX_W, FOX_W, FOX_W, FOX_HEADS, D_MODEL, D_MODEL)).tolist()
W_PREP_ROWS = 256


def _wprep_body(w_ref, wn_ref, wf_ref, ws_ref, wa_ref, wb_ref):
    wt = w_ref[0]
    c = _IN_COLS
    k = wt.shape[1]
    wn_ref[...] = wt[:c[7]].T.astype(BF16)
    wf_ref[...] = wt[c[8]:c[11]].T.astype(BF16)
    wa_ref[...] = wt[c[12]:c[13]].T.astype(BF16)
    wb_ref[...] = wt[c[13]:c[14]].T.astype(BF16)
    small = []
    for g in range(NSA_KV_GROUPS):
        for br in range(3):
            c0 = c[7] + br * NSA_HEADS + g * NSA_HPG
            small.append(wt[c0:c0 + NSA_HPG])
        small.append(jnp.zeros((GATE_PAD - 3 * NSA_HPG, k), F32))
    small.append(wt[c[11]:c[12]])
    small.append(jnp.zeros((LANES - _C_FLOG - FOX_HEADS, k), F32))
    ws_ref[...] = jnp.concatenate(small, axis=0).T.astype(BF16)


def _pack_w_in(w_in, layer):
    _, k, n = w_in.shape
    c = _IN_COLS
    assert n == c[14]
    widths = (c[7], c[11] - c[8], LANES, D_MODEL, D_MODEL)
    return pl.pallas_call(
        _wprep_body,
        grid=(k // W_PREP_ROWS,),
        in_specs=[pl.BlockSpec((1, n, W_PREP_ROWS), lambda i: (layer, 0, i))],
        out_specs=tuple(pl.BlockSpec((W_PREP_ROWS, w), lambda i: (i, 0)) for w in widths),
        out_shape=tuple(jax.ShapeDtypeStruct((k, w), BF16) for w in widths),
        compiler_params=_params(("parallel",)),
        name="wprep",
    )(jnp.swapaxes(w_in, 1, 2))


def _inproj(x, g, w_nsa, w_fox, w_small):
    b, s, _ = x.shape
    grid = (b, s // TOK_TILE)
    tok = lambda c: pl.BlockSpec((1, TOK_TILE, c), lambda i, j: (i, j, 0))
    heads = lambda nh, w=HEAD_DIM: pl.BlockSpec((1, nh, TOK_TILE, w), lambda i, j: (i, 0, j, 0))
    chunks = pl.BlockSpec((1, TOK_TILE // CMP_STRIDE, CHUNK_W), lambda i, j: (i, j, 0))
    sds = jax.ShapeDtypeStruct
    kv = sds((b, NSA_KV_GROUPS, s, HEAD_DIM), BF16)
    kv2 = sds((b, NSA_KV_GROUPS, s, 2 * HEAD_DIM), BF16)
    fx = sds((b, FOX_HEADS, s, HEAD_DIM), BF16)
    cmp_in = sds((b, s // CMP_STRIDE, CHUNK_W), BF16)
    out_shape = (sds((b, s, NSA_Q), BF16), cmp_in, cmp_in,
                 kv2, kv2, kv, kv2, fx, fx, sds((b, FOX_HEADS, s, 2 * HEAD_DIM), BF16),
                 sds((b, NSA_KV_GROUPS, s, GATE_PAD), F32), sds((b, FOX_HEADS, s), F32))
    out_specs = (tok(NSA_Q), chunks, chunks,
                 heads(NSA_KV_GROUPS, 2 * HEAD_DIM), heads(NSA_KV_GROUPS, 2 * HEAD_DIM),
                 heads(NSA_KV_GROUPS), heads(NSA_KV_GROUPS, 2 * HEAD_DIM),
                 heads(FOX_HEADS), heads(FOX_HEADS), heads(FOX_HEADS, 2 * HEAD_DIM),
                 pl.BlockSpec((1, NSA_KV_GROUPS, TOK_TILE, GATE_PAD), lambda i, j: (i, 0, j, 0)),
                 pl.BlockSpec((1, FOX_HEADS, TOK_TILE), lambda i, j: (i, 0, j)))
    key_blk = np.arange(s) // SEL_BLOCK
    blk_onehot = jnp.asarray(key_blk[:, None] == np.arange(HEAD_DIM)[None, :], BF16)
    return pl.pallas_call(
        _inproj_body,
        grid=grid,
        in_specs=[tok(D_MODEL), _resident((1, D_MODEL)), _resident(w_nsa.shape),
                  _resident(w_fox.shape), _resident(w_small.shape),
                  pl.BlockSpec((TOK_TILE, HEAD_DIM), lambda i, j: (j, 0))],
        out_specs=out_specs,
        out_shape=out_shape,
        scratch_shapes=[pltpu.VMEM((2, TOK_TILE, NSA_KV), F32)],
        compiler_params=_params(("parallel", "parallel")),
        name="inproj",
    )(x, g.reshape(1, D_MODEL), w_nsa, w_fox, w_small, blk_onehot)


CMP_BATCH = 4

def _compress_body(xk_ref, xv_ref, pk_ref, pv_ref, wk1_ref, wv1_ref, wk2_ref, wv2_ref, ok_ref, ov_ref):
    nb = xk_ref.shape[0]
    n = nb * N_CHUNKS_PAD
    for x_ref, p_ref, w1_ref, w2_ref, o_ref in ((xk_ref, pk_ref, wk1_ref, wk2_ref, ok_ref),
                                                (xv_ref, pv_ref, wv1_ref, wv2_ref, ov_ref)):
        x = x_ref[...].reshape(n, CHUNK_W)
        a0 = _dot(x, w1_ref[0])
        a1 = _dot(x, w1_ref[1])
        c = _dot(p_ref[0], w1_ref[0]) + _dot(p_ref[1], w1_ref[1])
        pre = a0 + pltpu.roll(a1, n - 1, axis=0) + c[0:1, :]
        hid = jax.nn.gelu(pre).astype(BF16)
        out = _dot(hid, w2_ref[...]).astype(BF16)
        for i in range(nb):
            for g in range(NSA_KV_GROUPS):
                o_ref[i, g] = out[i * N_CHUNKS_PAD:(i + 1) * N_CHUNKS_PAD, g * HEAD_DIM:(g + 1) * HEAD_DIM]


def _pack_compress(pos, w1, w2):
    r = CMP_BLOCK // CMP_STRIDE
    assert NSA_KV_GROUPS == 2
    w1r = w1.astype(BF16).reshape(r, CMP_STRIDE, HEAD_DIM, CMP_HIDDEN)
    z1 = jnp.zeros_like(w1r)
    w1big = jnp.stack([jnp.concatenate([w1r, z1], axis=-1), jnp.concatenate([z1, w1r], axis=-1)], axis=2)
    w1big = w1big.reshape(r, CHUNK_W, NSA_KV_GROUPS * CMP_HIDDEN)
    z2 = jnp.zeros_like(w2)
    w2big = jnp.concatenate([jnp.concatenate([w2, z2], axis=1), jnp.concatenate([z2, w2], axis=1)], axis=0)
    p = pos.reshape(r, CMP_STRIDE, 1, HEAD_DIM)
    p = jnp.broadcast_to(p, (r, CMP_STRIDE, NSA_KV_GROUPS, HEAD_DIM)).reshape(r, 1, CHUNK_W)
    p = jnp.broadcast_to(p, (r, 16, CHUNK_W))
    return p.astype(BF16), w1big.astype(BF16), w2big.astype(BF16)


def _compress(xk, xv, pos_k, pos_v, k_w1, k_w2, v_w1, v_w2):
    b = xk.shape[0]
    assert xk.shape[1:] == (N_CHUNKS_PAD, CHUNK_W)
    pk, wk1, wk2 = _pack_compress(pos_k, k_w1, k_w2)
    pv, wv1, wv2 = _pack_compress(pos_v, v_w1, v_w2)
    nb = CMP_BATCH if b % CMP_BATCH == 0 else 1
    xs = pl.BlockSpec((nb, N_CHUNKS_PAD, CHUNK_W), lambda i: (i, 0, 0))
    os_ = pl.BlockSpec((nb, NSA_KV_GROUPS, N_CHUNKS_PAD, HEAD_DIM), lambda i: (i, 0, 0, 0))
    osd = jax.ShapeDtypeStruct((b, NSA_KV_GROUPS, N_CHUNKS_PAD, HEAD_DIM), BF16)
    return pl.pallas_call(
        _compress_body,
        grid=(b // nb,),
        in_specs=[xs, xs, _resident(pk.shape), _resident(pv.shape), _resident(wk1.shape),
                  _resident(wv1.shape), _resident(wk2.shape), _resident(wv2.shape)],
        out_specs=(os_, os_),
        out_shape=(osd, osd),
        compiler_params=_params(("parallel",)),
        name="compress",
    )(xk, xv, pk, pv, wk1, wv1, wk2, wv2)


def _bias_lookup(tbl_ref, head, dist):
    acc = jnp.full(dist.shape, tbl_ref[0, head], F32)
    for k, thr in enumerate(BUCKET_THRESHOLDS):
        acc = jnp.where(dist >= thr, tbl_ref[k + 1, head], acc)
    return acc - tbl_ref[NUM_BUCKETS - 1, head]


def _relbias_body(tbl_ref, toep_ref, bc_ref):
    head = pl.program_id(0)
    t = ATT_TILE
    i = lax.broadcasted_iota(jnp.int32, (t, 2 * t), 0)
    u = lax.broadcasted_iota(jnp.int32, (t, 2 * t), 1)
    d = i - u + t
    toep_ref[0] = jnp.where(d >= 0, _bias_lookup(tbl_ref, head, d), NEG_INF)
    n_tiles = bc_ref.shape[1] // t
    per_tile = t // CMP_STRIDE
    off = per_tile * (n_tiles - 1)
    width = 2 * N_CHUNKS_PAD
    assert off + N_CHUNKS_PAD <= width
    i = lax.broadcasted_iota(jnp.int32, (t, width), 0)
    u = lax.broadcasted_iota(jnp.int32, (t, width), 1)
    dc = i - ((u - off) * CMP_STRIDE + CMP_BLOCK - 1)
    strip = jnp.where(dc >= 0, _bias_lookup(tbl_ref, head, dc), NEG_INF)
    for q in range(n_tiles):
        lo = off - q * per_tile
        bc_ref[0, q * t:(q + 1) * t, :] = strip[:, lo:lo + N_CHUNKS_PAD]


def _relbias(tbl, s):
    return pl.pallas_call(
        _relbias_body,
        grid=(NSA_HEADS,),
        in_specs=[pl.BlockSpec(memory_space=pltpu.SMEM)],
        out_specs=(pl.BlockSpec((1, ATT_TILE, 2 * ATT_TILE), lambda h: (h, 0, 0)),
                   pl.BlockSpec((1, s, N_CHUNKS_PAD), lambda h: (h, 0, 0))),
        out_shape=(jax.ShapeDtypeStruct((NSA_HEADS, ATT_TILE, 2 * ATT_TILE), F32),
                   jax.ShapeDtypeStruct((NSA_HEADS, s, N_CHUNKS_PAD), F32)),
        compiler_params=_params(("parallel",)),
        name="relbias",
    )(tbl)


def _lane_fold(x, op):
    return functools.reduce(op, [x[:, i * LANES:(i + 1) * LANES] for i in range(x.shape[1] // LANES)])


def _lane_tile(x, width):
    return jnp.concatenate([x] * (width // LANES), axis=1)


def _row_max_tile(rmax):
    return jnp.broadcast_to(jnp.max(rmax, axis=-1, keepdims=True), rmax.shape)


def _normalize(acc):
    return (acc / pltpu.roll(acc, HEAD_DIM, axis=1))[:, :HEAD_DIM]


def _split3(x):
    hi = x.astype(BF16)
    r = x - hi.astype(F32)
    mid = r.astype(BF16)
    lo = (r - mid.astype(F32)).astype(BF16)
    return hi, mid, lo


def _nsa_body(q_ref, kc_ref, vc_ref, ks_ref, vs_ref, kw_ref, vw_ref, gate_a_ref, gate_b_ref, toep_ref,
              bc_ref, ov_ref, o_ref, s_ref, near_ref, mb_ref, acc_ref, part_ref, *, n_tiles):
    t = ATT_TILE
    j4 = NSA_HPG
    rows = j4 * t
    groups = range(NSA_KV_GROUPS)
    step = pl.program_id(1)
    n_sel = n_tiles * (t // SEL_BLOCK)

    def heads(x):
        return x.reshape(j4, t, x.shape[-1])

    def flat(x):
        return x.reshape(rows, x.shape[-1])

    def key_tile(ref, g, n):
        if isinstance(n, int):
            return ref[0, g, n * t:(n + 1) * t, :]
        return ref[0, g, pl.ds(pl.multiple_of(n * t, t), t), :]

    def gated(gate_ref, g, branch, o):
        gate = jax.nn.sigmoid(gate_ref[0, g])
        return [gate[:, branch * j4 + j:branch * j4 + j + 1] * o[j * t:(j + 1) * t] for j in range(j4)]

    def probs(par, g, s):
        return jnp.exp(s - _lane_tile(mb_ref[par, g], t)).astype(BF16)

    def open_tile(m, par):
        has_prev = m >= 1 if isinstance(m, int) else True
        has_far = m >= 2 if isinstance(m, int) else True
        qs = []
        for g in groups:
            qt = q_ref[0, :, g * j4 * HEAD_DIM:(g + 1) * j4 * HEAD_DIM]
            qs.append(jnp.concatenate([qt[:, j * HEAD_DIM:(j + 1) * HEAD_DIM] for j in range(j4)], axis=0))

        def biases(g):
            toep = toep_ref[g * j4:(g + 1) * j4]
            return toep[:, :, t:], toep[:, :, :t]

        cmp_out = [_nsa_compressed(qs[g], kc_ref[0, g], vc_ref[0, g], bc_ref[g * j4:(g + 1) * j4],
                                   ov_ref[...], m, n_sel) for g in groups]
        part = []
        for g in groups:
            diag_bias, prev_bias = biases(g)
            tiles = [(m, flat(heads(_dot_nt_pair(qs[g], key_tile(kw_ref, g, m))) + diag_bias))]
            if has_prev:
                tiles.append((m - 1, flat(heads(_dot_nt_pair(qs[g], key_tile(kw_ref, g, m - 1))) + prev_bias)))
            if has_far:
                ii = lax.broadcasted_iota(jnp.int32, (t, t), 0)
                jj = lax.broadcasted_iota(jnp.int32, (t, t), 1)
                tri = jnp.where(jj > ii, 0.0, NEG_INF)
                tiles.append((m - 2, flat(heads(_dot_nt_pair(qs[g], key_tile(kw_ref, g, m - 2))) + tri[None])))
            wmax = _lane_tile(_row_max_tile(functools.reduce(
                jnp.maximum, [_lane_fold(w, jnp.maximum) for _, w in tiles])), t)
            acc = sum(_dot_pair(jnp.exp(w - wmax).astype(BF16), key_tile(vw_ref, g, n)) for n, w in tiles)
            part += [a + b for a, b in zip(gated(gate_a_ref, g, 0, cmp_out[g][0]),
                                           gated(gate_a_ref, g, 2, _normalize(acc)))]
        part_ref[par] = jnp.concatenate(part, axis=1)

        q_aug = []
        for g in groups:
            diag_bias, prev_bias = biases(g)
            q_aug.append(jnp.concatenate([qs[g], jnp.concatenate([cmp_out[g][1]] * j4, axis=0)], axis=1))
            s0 = flat(heads(_dot_nt_pair(q_aug[g], key_tile(ks_ref, g, m))) + diag_bias)
            near_ref[par, g, 0] = s0
            rmax = _lane_fold(s0, jnp.maximum)
            if has_prev:
                s1 = flat(heads(_dot_nt_pair(q_aug[g], key_tile(ks_ref, g, m - 1))) + prev_bias)
                near_ref[par, g, 1] = s1
                rmax = jnp.maximum(rmax, _lane_fold(s1, jnp.maximum))
            if has_far:
                s2 = _dot_nt_pair(q_aug[g], key_tile(ks_ref, g, m - 2))
                s_ref[g, m - 2] = s2
                rmax = jnp.maximum(rmax, _lane_fold(s2, jnp.maximum))
            mb_ref[par, g] = rmax
        return q_aug

    def open_far(par, q_aug, n):
        for g in groups:
            s = _dot_nt_pair(q_aug[g], key_tile(ks_ref, g, n))
            s_ref[g, n] = s
            mb_ref[par, g] = jnp.maximum(mb_ref[par, g], _lane_fold(s, jnp.maximum))

    def close_tile(m, par):
        for g in groups:
            mb_ref[par, g] = _row_max_tile(mb_ref[par, g])
            acc = _dot_pair(probs(par, g, near_ref[par, g, 0]), key_tile(vs_ref, g, m))
            if not (isinstance(m, int) and m == 0):
                acc = acc + _dot_pair(probs(par, g, near_ref[par, g, 1]), key_tile(vs_ref, g, m - 1))
            acc_ref[par, g] = acc

    def close_far(par, n):
        for g in groups:
            acc_ref[par, g] += _dot_pair(probs(par, g, s_ref[g, n]), key_tile(vs_ref, g, n))

    def finish(par):
        outs = []
        for g in groups:
            outs += gated(gate_b_ref, g, 1, _normalize(acc_ref[par, g]))
        o_ref[0] = (part_ref[par] + jnp.concatenate(outs, axis=1)).astype(BF16)

    @pl.when(step == 0)
    def _():
        open_tile(0, 0)

    @pl.when(step == 1)
    def _():
        open_tile(1, 1)
        close_tile(0, 0)

    for par in (0, 1):
        @pl.when(jnp.logical_and(jnp.logical_and(step >= 2, step < n_tiles), step % 2 == par))
        def _(par=par):
            finish(par)
            q_aug = open_tile(step, par)
            close_tile(step - 1, 1 - par)

            @pl.loop(0, step - 2)
            def _(n):
                close_far(1 - par, n)
                open_far(par, q_aug, n)

    last = (n_tiles - 1) % 2

    @pl.when(step == n_tiles)
    def _():
        finish(1 - last)
        close_tile(step - 1, last)

        @pl.loop(0, step - 2)
        def _(n):
            close_far(last, n)

    @pl.when(step == n_tiles + 1)
    def _():
        finish(last)


def _nsa_compressed(q, kc, vc, bc, ov, m, n_sel):
    t = ATT_TILE
    j4 = NSA_HPG

    def heads(x):
        return x.reshape(j4, t, x.shape[-1])

    def flat(x):
        return x.reshape(j4 * t, x.shape[-1])

    sc = heads(_dot_nt_pair(q, kc)) + bc
    valid = bc > 0.5 * NEG_INF
    e = jnp.where(valid, jnp.exp(sc - jnp.max(sc, axis=-1, keepdims=True)), 0.0)
    l = jnp.sum(e, axis=-1, keepdims=True)
    pc = jnp.where(l > 0.0, e / l, 0.0)
    o_c = _dot_pair(flat(pc).astype(BF16), vc)
    psum = pc[0] + pc[1] + pc[2] + pc[3]
    imp = sum(_dot(part, ov) for part in _split3(psum)).T[:n_sel]
    blk = lax.broadcasted_iota(jnp.int32, (n_sel, t), 0)
    tpos = m * t + lax.broadcasted_iota(jnp.int32, (n_sel, t), 1)
    cur = tpos // SEL_BLOCK
    bonus = jnp.where(blk == 0, FORCE_BONUS,
                      jnp.where(blk == cur, FORCE_BONUS, jnp.where(blk == cur - 1, FORCE_BONUS, 0.0)))
    imp = jnp.where(blk * SEL_BLOCK <= tpos, imp + bonus, NEG_INF)
    lanes = 4
    partial = [jnp.zeros((n_sel, t), F32) for _ in range(lanes)]
    for i in range(n_sel):
        row = imp[i:i + 1, :]
        before = jnp.where(blk > i, 1.0, 0.0)
        partial[i % lanes] = partial[i % lanes] + jnp.where(row > imp, 1.0,
                                                            jnp.where(row == imp, before, 0.0))
    rank = (partial[0] + partial[1]) + (partial[2] + partial[3])
    selb = jnp.where(rank < float(SEL_TOP_N), 0.0, NEG_INF)
    selb = jnp.concatenate([selb, jnp.zeros((LANES - n_sel, t), F32)], axis=0).T
    return o_c, selb[:, :HEAD_DIM].astype(BF16)


def _overlap_table(s):
    n_sel = s // SEL_BLOCK
    nc = N_CHUNKS_PAD
    c0 = np.arange(nc)[:, None] * CMP_STRIDE
    s0 = np.arange(n_sel)[None, :] * SEL_BLOCK
    ov = np.clip(np.minimum(c0 + CMP_BLOCK, s0 + SEL_BLOCK) - np.maximum(c0, s0), 0, None) / CMP_STRIDE
    ov[nc - 1] = 0.0
    return jnp.asarray(np.pad(ov, ((0, 0), (0, LANES - n_sel))), BF16)


def _nsa(qn, kc, vc, ks, vs, kw, vw, gates, toep, bc):
    b, s, _ = qn.shape
    t = ATT_TILE
    assert s // SEL_BLOCK <= HEAD_DIM
    ov = _overlap_table(s)
    rows = NSA_HPG * t
    ng = NSA_KV_GROUPS
    nq = s // t
    assert nq >= 2
    seq = lambda n, w=HEAD_DIM: pl.BlockSpec((1, ng, n, w), lambda i, j: (i, 0, 0, 0))
    opened = lambda j: jnp.minimum(j, nq - 1)
    closed = lambda j: jnp.maximum(j - 2, 0)
    return pl.pallas_call(
        functools.partial(_nsa_body, n_tiles=nq),
        grid=(b, nq + 2),
        in_specs=[pl.BlockSpec((1, t, NSA_Q), lambda i, j: (i, opened(j), 0)),
                  seq(N_CHUNKS_PAD), seq(N_CHUNKS_PAD),
                  seq(s, 2 * HEAD_DIM), seq(s, 2 * HEAD_DIM), seq(s), seq(s, 2 * HEAD_DIM),
                  pl.BlockSpec((1, ng, t, GATE_PAD), lambda i, j: (i, 0, opened(j), 0)),
                  pl.BlockSpec((1, ng, t, GATE_PAD), lambda i, j: (i, 0, closed(j), 0)),
                  _resident((NSA_HEADS, t, 2 * t)),
                  pl.BlockSpec((NSA_HEADS, t, N_CHUNKS_PAD), lambda i, j: (0, opened(j), 0)),
                  _resident(ov.shape)],
        out_specs=pl.BlockSpec((1, t, NSA_Q), lambda i, j: (i, closed(j), 0)),
        out_shape=jax.ShapeDtypeStruct((b, s, NSA_Q), BF16),
        scratch_shapes=[pltpu.VMEM((ng, nq - 2, rows, t), F32),
                        pltpu.VMEM((2, ng, 2, rows, t), F32),
                        pltpu.VMEM((2, ng, rows, LANES), F32),
                        pltpu.VMEM((2, ng, rows, 2 * HEAD_DIM), F32),
                        pltpu.VMEM((2, t, NSA_Q), F32)],
        compiler_params=_params(("parallel", "arbitrary")),
        name="nsa",
    )(qn, kc, vc, ks, vs, kw, vw, gates, gates, toep, bc, ov)


def _fcum_body(f_ref, b_ref, o_ref):
    z = f_ref[0] + b_ref[...]
    x = -(jnp.maximum(-z, 0.0) + jnp.log1p(jnp.exp(-jnp.abs(z))))
    n = x.shape[-1]
    lane = lax.broadcasted_iota(jnp.int32, x.shape, 1)
    sh = 1
    while sh < n:
        x = x + jnp.where(lane >= sh, pltpu.roll(x, sh, axis=1), 0.0)
        sh *= 2
    o_ref[0] = x


def _fcum(f_t, b_forget):
    b, h, s = f_t.shape
    blk = pl.BlockSpec((1, h, s), lambda i: (i, 0, 0))
    return pl.pallas_call(
        _fcum_body,
        grid=(b,),
        in_specs=[blk, _resident((h, 1))],
        out_specs=blk,
        out_shape=jax.ShapeDtypeStruct((b, h, s), F32),
        compiler_params=_params(("parallel",)),
        name="fcum",
    )(f_t, b_forget.reshape(h, 1))


FOX_PAIR = 2
FOX_TILE = 512


def _causal_bias(t):
    i = np.arange(t)
    return jnp.asarray(np.where(i[None, :] <= i[:, None], 0.0, NEG_INF), F32)


def _fox_body(q_ref, k_ref, v_ref, c_ref, cm_ref, o_ref, s_ref, mb_ref, acc_ref):
    t = FOX_TILE
    heads = range(FOX_PAIR)
    nq = q_ref.shape[2] // t

    def keys(n):
        return pl.ds(pl.multiple_of(n * t, t), t)

    def scores(m, hh, n):
        q = q_ref[0, hh, m * t:(m + 1) * t, :]
        return _dot_nt(q, k_ref[0, hh, keys(n), :]) - c_ref[0, hh, :, keys(n)]

    def scores_diag(m):
        for hh in heads:
            s = scores(m, hh, m) + cm_ref[...]
            s_ref[m % 2, hh, m] = s
            mb_ref[m % 2, hh] = _lane_fold(s, jnp.maximum)

    def scores_far(m, n):
        for hh in heads:
            s = scores(m, hh, n)
            s_ref[m % 2, hh, n] = s
            mb_ref[m % 2, hh] = jnp.maximum(mb_ref[m % 2, hh], _lane_fold(s, jnp.maximum))

    def finish_max(m):
        for hh in heads:
            mb_ref[m % 2, hh] = _row_max_tile(mb_ref[m % 2, hh])

    def weigh(m, n):
        for hh in heads:
            p = jnp.exp(s_ref[m % 2, hh, n] - _lane_tile(mb_ref[m % 2, hh], t)).astype(BF16)
            acc_ref[hh] += _dot(p, v_ref[0, hh, keys(n), :])

    scores_diag(0)
    finish_max(0)
    for m in range(nq):
        ahead = m + 1 < nq
        if ahead:
            scores_diag(m + 1)
        for hh in heads:
            acc_ref[hh] = jnp.zeros((t, 2 * HEAD_DIM), F32)

        @pl.loop(0, m + 1)
        def _(n, m=m, ahead=ahead):
            weigh(m, n)
            if ahead:
                scores_far(m + 1, n)

        if ahead:
            finish_max(m + 1)
        o_ref[0, m * t:(m + 1) * t, :] = jnp.concatenate(
            [_normalize(acc_ref[hh]) for hh in heads], axis=1).astype(BF16)


def _fox(qf, kf, vf, c):
    b, h, s, _ = qf.shape
    t = FOX_TILE
    seq = lambda w: pl.BlockSpec((1, FOX_PAIR, s, w), lambda i, p: (i, p, 0, 0))
    return pl.pallas_call(
        _fox_body,
        grid=(b, h // FOX_PAIR),
        in_specs=[seq(HEAD_DIM), seq(HEAD_DIM), seq(2 * HEAD_DIM),
                  pl.BlockSpec((1, FOX_PAIR, 1, s), lambda i, p: (i, p, 0, 0)),
                  _resident((t, t))],
        out_specs=pl.BlockSpec((1, s, FOX_PAIR * HEAD_DIM), lambda i, p: (i, 0, p)),
        out_shape=jax.ShapeDtypeStruct((b, s, h * HEAD_DIM), BF16),
        scratch_shapes=[pltpu.VMEM((2, FOX_PAIR, s // t, t, t), F32),
                        pltpu.VMEM((2, FOX_PAIR, t, LANES), F32),
                        pltpu.VMEM((FOX_PAIR, t, 2 * HEAD_DIM), F32)],
        compiler_params=_params(("parallel", "parallel")),
        name="fox",
    )(qf, kf, vf, c.reshape(b, h, 1, s), _causal_bias(t))


def _mixout_body(x_ref, g_ref, on_ref, of_ref, wa_ref, wb_ref, wun_ref, wuf_ref, wo_ref, o_ref):
    x = x_ref[...]
    h = _rms(x, g_ref[...]).astype(BF16)
    y = (jax.nn.sigmoid(_dot(h, wa_ref[...])) * _dot(on_ref[...], wun_ref[...])
         + jax.nn.sigmoid(_dot(h, wb_ref[...])) * _dot(of_ref[...], wuf_ref[...]))
    o_ref[...] = x + _dot(y.astype(BF16), wo_ref[...])


def _mixout(x2d, g, o_nsa, o_fox, w_a, w_b, w_un, w_uf, w_o):
    n = x2d.shape[0]
    tok = lambda c: pl.BlockSpec((TOK_TILE, c), lambda i: (i, 0))
    return pl.pallas_call(
        _mixout_body,
        grid=(n // TOK_TILE,),
        in_specs=[tok(D_MODEL), _resident((1, D_MODEL)), tok(NSA_Q), tok(FOX_W),
                  _resident(w_a.shape), _resident(w_b.shape), _resident(w_un.shape),
                  _resident(w_uf.shape), _resident(w_o.shape)],
        out_specs=tok(D_MODEL),
        out_shape=jax.ShapeDtypeStruct((n, D_MODEL), F32),
        compiler_params=_params(("parallel",)),
        name="mixout",
    )(x2d, g.reshape(1, D_MODEL), o_nsa, o_fox, w_a, w_b, w_un, w_uf, w_o)


MEM_BATCH = 4


def _memkv_body(m_ref, g_ref, w_ref, k_ref, v_ref):
    nb, ml, d = m_ref.shape
    h = _rms(m_ref[...].reshape(nb * ml, d), g_ref[...]).astype(BF16)
    z = _dot(h, w_ref[...]).astype(BF16)
    k_ref[...] = z[:, :D_MODEL].reshape(nb, ml, d)
    v_ref[...] = z[:, D_MODEL:].reshape(nb, ml, d)


def _memkv(mem, g, w_kv):
    b, ml, _ = mem.shape
    nb = MEM_BATCH if b % MEM_BATCH == 0 else 1
    blk = pl.BlockSpec((nb, ml, D_MODEL), lambda i: (i, 0, 0))
    sd = jax.ShapeDtypeStruct((b, ml, D_MODEL), BF16)
    return pl.pallas_call(
        _memkv_body,
        grid=(b // nb,),
        in_specs=[blk, _resident((1, D_MODEL)), _resident(w_kv.shape)],
        out_specs=(blk, blk),
        out_shape=(sd, sd),
        compiler_params=_params(("parallel",)),
        name="memkv",
    )(mem, g.reshape(1, D_MODEL), w_kv)


def _memattn_body(x_ref, g_ref, k_ref, v_ref, wq_ref, wo_ref, o_ref):
    x = x_ref[0]
    h = _rms(x, g_ref[...]).astype(BF16)
    q = (_dot(h, wq_ref[...]) * (MEM_HEAD_DIM ** -0.5)).astype(BF16)
    outs = []
    for hd in range(MEM_HEADS):
        cols = slice(hd * MEM_HEAD_DIM, (hd + 1) * MEM_HEAD_DIM)
        s = _dot_nt(q[:, cols], k_ref[0, :, cols])
        e = jnp.exp(s - jnp.max(s, axis=-1, keepdims=True))
        p = e / jnp.sum(e, axis=-1, keepdims=True)
        outs.append(_dot(p.astype(BF16), v_ref[0, :, cols]))
    o = jnp.concatenate(outs, axis=1).astype(BF16)
    o_ref[0] = x + _dot(o, wo_ref[...])


def _memattn(x, g, k, v, w_q, w_o):
    b, s, _ = x.shape
    ml = k.shape[1]
    tok = pl.BlockSpec((1, TOK_TILE, D_MODEL), lambda i, j: (i, j, 0))
    kvb = pl.BlockSpec((1, ml, D_MODEL), lambda i, j: (i, 0, 0))
    return pl.pallas_call(
        _memattn_body,
        grid=(b, s // TOK_TILE),
        in_specs=[tok, _resident((1, D_MODEL)), kvb, kvb, _resident(w_q.shape), _resident(w_o.shape)],
        out_specs=tok,
        out_shape=jax.ShapeDtypeStruct((b, s, D_MODEL), F32),
        compiler_params=_params(("parallel", "parallel")),
        name="memattn",
    )(x, g.reshape(1, D_MODEL), k, v, w_q, w_o)


def kernel(x, mem, rel_bias_table, ffn1_norm, ffn1_w_gate, ffn1_w_up, ffn1_w_down, mix_norm, mix_w_in, mix_b_forget, cmp_pos_k, cmp_pos_v, cmp_k_w1, cmp_k_w2, cmp_v_w1, cmp_v_w2, w_up_nsa, w_up_fox, mix_w_out, mem_q_norm, mem_kv_norm, mem_w_q, mem_w_kv, mem_w_o, ffn2_norm, ffn2_w_gate, ffn2_w_up, ffn2_w_down, final_norm):
    b, s, d = x.shape
    depth = ffn1_norm.shape[0]
    bf = lambda w: w.astype(BF16)
    toep, bias_c = _relbias(rel_bias_table, s)
    x = x.reshape(b * s, d)
    for l in range(depth):
        last = l == depth - 1
        x = _ffn(x, ffn1_norm[l], ffn1_w_gate[l], ffn1_w_up[l], ffn1_w_down[l])

        w_nsa, w_fox, w_small, w_a, w_b = _pack_w_in(mix_w_in, l)
        qn, kc, vc, ks, vs, kw, vw, qf, kf, vf, gates, flog = _inproj(
            x.reshape(b, s, d), mix_norm[l], w_nsa, w_fox, w_small)
        kcc, vcc = _compress(kc, vc, cmp_pos_k[l], cmp_pos_v[l], cmp_k_w1[l], cmp_k_w2[l],
                             cmp_v_w1[l], cmp_v_w2[l])
        o_nsa = _nsa(qn, kcc, vcc, ks, vs, kw, vw, gates, toep, bias_c)
        c = _fcum(flog, mix_b_forget[l])
        o_fox = _fox(qf, kf, vf, c)
        x = _mixout(x, mix_norm[l], o_nsa.reshape(b * s, NSA_Q), o_fox.reshape(b * s, FOX_W),
                    w_a, w_b, bf(w_up_nsa[l]), bf(w_up_fox[l]), bf(mix_w_out[l]))

        mk, mv = _memkv(mem, mem_kv_norm[l], bf(mem_w_kv[l]))
        x = _memattn(x.reshape(b, s, d), mem_q_norm[l], mk, mv, bf(mem_w_q[l]), bf(mem_w_o[l]))
        x = _ffn(x.reshape(b * s, d), ffn2_norm[l], ffn2_w_gate[l], ffn2_w_up[l],
                 ffn2_w_down[l], final_g=final_norm if last else None)
    return x.reshape(b, s, d)
```

```python
import functools
import math

import numpy as np
import jax
import jax.numpy as jnp
from jax import lax
from jax.experimental import pallas as pl
from jax.experimental.pallas import tpu as pltpu

D_MODEL = 1024
D_FF = 2816
HEAD_DIM = 64
NSA_HEADS = 8
NSA_KV_GROUPS = 2
NSA_HPG = NSA_HEADS // NSA_KV_GROUPS
CMP_BLOCK = 32
CMP_STRIDE = 16
CMP_HIDDEN = 256
SEL_BLOCK = 64
SEL_TOP_N = 16
WINDOW = 512
FOX_HEADS = 8
MEM_HEADS = 4
MEM_HEAD_DIM = D_MODEL // MEM_HEADS
NUM_BUCKETS = 32
MAX_DISTANCE = 128
RMS_EPS = 1e-6
NEG_INF = -1e30
FORCE_BONUS = 1e4

NSA_Q = NSA_HEADS * HEAD_DIM
NSA_KV = NSA_KV_GROUPS * HEAD_DIM
FOX_W = FOX_HEADS * HEAD_DIM

LANES = 128
VMEM_LIMIT = 56 * 1024 * 1024

BF16 = jnp.bfloat16
F32 = jnp.float32

ATT_TILE = 256
TOK_TILE = 1024
FFN_TILE = 1024
FF_CHUNK = 256


def _bucket_thresholds():
    n = np.arange(0, 4 * MAX_DISTANCE)
    exact = NUM_BUCKETS // 2
    large = exact + (np.log(np.maximum(n, 1) / exact) / math.log(MAX_DISTANCE / exact)
                     * (NUM_BUCKETS - exact)).astype(np.int64)
    bucket = np.where(n < exact, n, np.minimum(large, NUM_BUCKETS - 1))
    assert np.all(np.diff(bucket) >= 0)
    return [int(np.argmax(bucket >= k)) for k in range(1, NUM_BUCKETS)]


BUCKET_THRESHOLDS = _bucket_thresholds()


def _dot(a, b):
    return jnp.dot(a, b, preferred_element_type=F32)


def _dot_nt(a, b):
    return lax.dot_general(a, b, (((1,), (1,)), ((), ())), preferred_element_type=F32)


def _row_halves(fn, a, b):
    h = a.shape[0] // 2
    return jnp.concatenate([fn(a[:h], b), fn(a[h:], b)], axis=0)


def _dot_pair(a, b):
    return _row_halves(_dot, a, b)


def _dot_nt_pair(a, b):
    return _row_halves(_dot_nt, a, b)


def _rms(x, g):
    return x * lax.rsqrt(jnp.mean(x * x, axis=-1, keepdims=True) + RMS_EPS) * g


def _resident(shape):
    nd = len(shape)
    return pl.BlockSpec(shape, lambda *_: (0,) * nd, pipeline_mode=pl.Buffered(1))


def _params(sem):
    return pltpu.CompilerParams(dimension_semantics=sem, vmem_limit_bytes=VMEM_LIMIT)


def _ffn_body(x_ref, g_ref, wg_hbm, wu_hbm, wd_hbm, *rest, final):
    if final:
        fg_ref, o_ref, wg_ref, wu_ref, wd_ref, sg_ref, su_ref, sd_ref, sem = rest
    else:
        o_ref, wg_ref, wu_ref, wd_ref, sg_ref, su_ref, sd_ref, sem = rest
    n_chunks = D_FF // FF_CHUNK

    def chunk_copies(c):
        cols = slice(c * FF_CHUNK, (c + 1) * FF_CHUNK)
        slot = c % 2
        return (pltpu.make_async_copy(wg_hbm.at[:, cols], sg_ref.at[slot], sem.at[0, slot]),
                pltpu.make_async_copy(wu_hbm.at[:, cols], su_ref.at[slot], sem.at[1, slot]),
                pltpu.make_async_copy(wd_hbm.at[cols, :], sd_ref.at[slot], sem.at[2, slot]))

    @pl.when(pl.program_id(0) == 0)
    def _():
        for cp in chunk_copies(0):
            cp.start()
        for c in range(n_chunks):
            sl = slice(c * FF_CHUNK, (c + 1) * FF_CHUNK)
            if c + 1 < n_chunks:
                for cp in chunk_copies(c + 1):
                    cp.start()
            for cp in chunk_copies(c):
                cp.wait()
            wg_ref[:, sl] = sg_ref[c % 2].astype(BF16)
            wu_ref[:, sl] = su_ref[c % 2].astype(BF16)
            wd_ref[sl, :] = sd_ref[c % 2].astype(BF16)

    x = x_ref[...]
    h = _rms(x, g_ref[...]).astype(BF16)
    acc = jnp.zeros(x.shape, F32)
    for c in range(n_chunks):
        sl = slice(c * FF_CHUNK, (c + 1) * FF_CHUNK)
        a = _dot(h, wg_ref[:, sl])
        b = _dot(h, wu_ref[:, sl])
        t = (a * jax.nn.sigmoid(a)) * b
        acc = acc + _dot(t.astype(BF16), wd_ref[sl, :])
    y = x + 0.5 * acc
    if final:
        y = _rms(y, fg_ref[...])
    o_ref[...] = y


def _ffn(x2d, g, wg, wu, wd, final_g=None):
    n = x2d.shape[0]
    final = final_g is not None
    tok = pl.BlockSpec((FFN_TILE, D_MODEL), lambda i: (i, 0))
    hbm = pl.BlockSpec(memory_space=pl.ANY)
    in_specs = [tok, _resident((1, D_MODEL)), hbm, hbm, hbm]
    args = [x2d, g.reshape(1, D_MODEL), wg, wu, wd]
    if final:
        in_specs.append(_resident((1, D_MODEL)))
        args.append(final_g.reshape(1, D_MODEL))
    return pl.pallas_call(
        functools.partial(_ffn_body, final=final),
        grid=(n // FFN_TILE,),
        in_specs=in_specs,
        out_specs=tok,
        out_shape=jax.ShapeDtypeStruct((n, D_MODEL), F32),
        scratch_shapes=[pltpu.VMEM((D_MODEL, D_FF), BF16), pltpu.VMEM((D_MODEL, D_FF), BF16),
                        pltpu.VMEM((D_FF, D_MODEL), BF16),
                        pltpu.VMEM((2, D_MODEL, FF_CHUNK), F32), pltpu.VMEM((2, D_MODEL, FF_CHUNK), F32),
                        pltpu.VMEM((2, FF_CHUNK, D_MODEL), F32),
                        pltpu.SemaphoreType.DMA((3, 2))],
        compiler_params=_params(("arbitrary",)),
        name="ffn_final" if final else "ffn",
    )(*args)


_C_CMP = NSA_Q
_C_KV = _C_CMP + 2 * NSA_KV
_W_NSA = _C_KV + 4 * NSA_KV
GATE_PAD = 16
_C_FLOG = NSA_KV_GROUPS * GATE_PAD
N_CHUNKS_PAD = 128
CHUNK_W = CMP_STRIDE * NSA_KV


def _inproj_body(x_ref, g_ref, wn_ref, wf_ref, ws_ref, blk_ref, qn_ref, kc_ref, vc_ref, ks_ref, vs_ref,
                 kw_ref, vw_ref, qf_ref, kf_ref, vf_ref, gate_ref, fl_ref, cmp_ref):
    h = _rms(x_ref[0], g_ref[...]).astype(BF16)
    z = _dot(h, wn_ref[...])
    qn_ref[0] = (z[:, :_C_CMP] * (HEAD_DIM ** -0.5)).astype(BF16)
    rows = z.shape[0] // CMP_STRIDE
    for i, ref in enumerate((kc_ref, vc_ref)):
        cmp_ref[i] = z[:, _C_CMP + i * NSA_KV:_C_CMP + (i + 1) * NSA_KV]
        for r in range(CMP_STRIDE):
            tok = cmp_ref[i, pl.ds(r, rows, stride=CMP_STRIDE), :]
            ref[0, :, r * NSA_KV:(r + 1) * NSA_KV] = tok.astype(BF16)
    ones = jnp.ones((z.shape[0], HEAD_DIM), BF16)
    for i, (ref, extra) in enumerate(((ks_ref, blk_ref[...]), (vs_ref, ones), (kw_ref, None),
                                      (vw_ref, ones))):
        for g in range(NSA_KV_GROUPS):
            c0 = _C_KV + i * NSA_KV + g * HEAD_DIM
            val = z[:, c0:c0 + HEAD_DIM].astype(BF16)
            ref[0, g] = val if extra is None else jnp.concatenate([val, extra], axis=1)
    zf = _dot(h, wf_ref[...])
    for hd in range(FOX_HEADS):
        c0 = hd * HEAD_DIM
        qf_ref[0, hd] = (zf[:, c0:c0 + HEAD_DIM] * (HEAD_DIM ** -0.5)).astype(BF16)
        kf_ref[0, hd] = zf[:, c0 + FOX_W:c0 + FOX_W + HEAD_DIM].astype(BF16)
        v = zf[:, c0 + 2 * FOX_W:c0 + 2 * FOX_W + HEAD_DIM].astype(BF16)
        vf_ref[0, hd] = jnp.concatenate([v, ones], axis=1)
    zs = _dot(h, ws_ref[...])
    for g in range(NSA_KV_GROUPS):
        gate_ref[0, g] = zs[:, g * GATE_PAD:(g + 1) * GATE_PAD]
    fl_ref[0] = zs.T[_C_FLOG:_C_FLOG + FOX_HEADS]


_IN_COLS = np.cumsum((0, NSA_Q, NSA_KV, NSA_KV, NSA_KV, NSA_KV, NSA_KV, NSA_KV, 3 * NSA_HEADS,
                      FOX_W, FOX_W, FOX_W, FOX_HEADS, D_MODEL, D_MODEL)).tolist()
W_PREP_ROWS = 256


def _wprep_body(w_ref, wn_ref, wf_ref, ws_ref, wa_ref, wb_ref):
    wt = w_ref[0]
    c = _IN_COLS
    k = wt.shape[1]
    wn_ref[...] = wt[:c[7]].T.astype(BF16)
    wf_ref[...] = wt[c[8]:c[11]].T.astype(BF16)
    wa_ref[...] = wt[c[12]:c[13]].T.astype(BF16)
    wb_ref[...] = wt[c[13]:c[14]].T.astype(BF16)
    small = []
    for g in range(NSA_KV_GROUPS):
        for br in range(3):
            c0 = c[7] + br * NSA_HEADS + g * NSA_HPG
            small.append(wt[c0:c0 + NSA_HPG])
        small.append(jnp.zeros((GATE_PAD - 3 * NSA_HPG, k), F32))
    small.append(wt[c[11]:c[12]])
    small.append(jnp.zeros((LANES - _C_FLOG - FOX_HEADS, k), F32))
    ws_ref[...] = jnp.concatenate(small, axis=0).T.astype(BF16)


def _pack_w_in(w_in, layer):
    _, k, n = w_in.shape
    c = _IN_COLS
    assert n == c[14]
    widths = (c[7], c[11] - c[8], LANES, D_MODEL, D_MODEL)
    return pl.pallas_call(
        _wprep_body,
        grid=(k // W_PREP_ROWS,),
        in_specs=[pl.BlockSpec((1, n, W_PREP_ROWS), lambda i: (layer, 0, i))],
        out_specs=tuple(pl.BlockSpec((W_PREP_ROWS, w), lambda i: (i, 0)) for w in widths),
        out_shape=tuple(jax.ShapeDtypeStruct((k, w), BF16) for w in widths),
        compiler_params=_params(("parallel",)),
        name="wprep",
    )(jnp.swapaxes(w_in, 1, 2))


def _inproj(x, g, w_nsa, w_fox, w_small):
    b, s, _ = x.shape
    grid = (b, s // TOK_TILE)
    tok = lambda c: pl.BlockSpec((1, TOK_TILE, c), lambda i, j: (i, j, 0))
    heads = lambda nh, w=HEAD_DIM: pl.BlockSpec((1, nh, TOK_TILE, w), lambda i, j: (i, 0, j, 0))
    chunks = pl.BlockSpec((1, TOK_TILE // CMP_STRIDE, CHUNK_W), lambda i, j: (i, j, 0))
    sds = jax.ShapeDtypeStruct
    kv = sds((b, NSA_KV_GROUPS, s, HEAD_DIM), BF16)
    kv2 = sds((b, NSA_KV_GROUPS, s, 2 * HEAD_DIM), BF16)
    fx = sds((b, FOX_HEADS, s, HEAD_DIM), BF16)
    cmp_in = sds((b, s // CMP_STRIDE, CHUNK_W), BF16)
    out_shape = (sds((b, s, NSA_Q), BF16), cmp_in, cmp_in,
                 kv2, kv2, kv, kv2, fx, fx, sds((b, FOX_HEADS, s, 2 * HEAD_DIM), BF16),
                 sds((b, NSA_KV_GROUPS, s, GATE_PAD), F32), sds((b, FOX_HEADS, s), F32))
    out_specs = (tok(NSA_Q), chunks, chunks,
                 heads(NSA_KV_GROUPS, 2 * HEAD_DIM), heads(NSA_KV_GROUPS, 2 * HEAD_DIM),
                 heads(NSA_KV_GROUPS), heads(NSA_KV_GROUPS, 2 * HEAD_DIM),
                 heads(FOX_HEADS), heads(FOX_HEADS), heads(FOX_HEADS, 2 * HEAD_DIM),
                 pl.BlockSpec((1, NSA_KV_GROUPS, TOK_TILE, GATE_PAD), lambda i, j: (i, 0, j, 0)),
                 pl.BlockSpec((1, FOX_HEADS, TOK_TILE), lambda i, j: (i, 0, j)))
    key_blk = np.arange(s) // SEL_BLOCK
    blk_onehot = jnp.asarray(key_blk[:, None] == np.arange(HEAD_DIM)[None, :], BF16)
    return pl.pallas_call(
        _inproj_body,
        grid=grid,
        in_specs=[tok(D_MODEL), _resident((1, D_MODEL)), _resident(w_nsa.shape),
                  _resident(w_fox.shape), _resident(w_small.shape),
                  pl.BlockSpec((TOK_TILE, HEAD_DIM), lambda i, j: (j, 0))],
        out_specs=out_specs,
        out_shape=out_shape,
        scratch_shapes=[pltpu.VMEM((2, TOK_TILE, NSA_KV), F32)],
        compiler_params=_params(("parallel", "parallel")),
        name="inproj",
    )(x, g.reshape(1, D_MODEL), w_nsa, w_fox, w_small, blk_onehot)


CMP_BATCH = 4

def _compress_body(xk_ref, xv_ref, pk_ref, pv_ref, wk1_ref, wv1_ref, wk2_ref, wv2_ref, ok_ref, ov_ref):
    nb = xk_ref.shape[0]
    n = nb * N_CHUNKS_PAD
    for x_ref, p_ref, w1_ref, w2_ref, o_ref in ((xk_ref, pk_ref, wk1_ref, wk2_ref, ok_ref),
                                                (xv_ref, pv_ref, wv1_ref, wv2_ref, ov_ref)):
        x = x_ref[...].reshape(n, CHUNK_W)
        a0 = _dot(x, w1_ref[0])
        a1 = _dot(x, w1_ref[1])
        c = _dot(p_ref[0], w1_ref[0]) + _dot(p_ref[1], w1_ref[1])
        pre = a0 + pltpu.roll(a1, n - 1, axis=0) + c[0:1, :]
        hid = jax.nn.gelu(pre).astype(BF16)
        out = _dot(hid, w2_ref[...]).astype(BF16)
        for i in range(nb):
            for g in range(NSA_KV_GROUPS):
                o_ref[i, g] = out[i * N_CHUNKS_PAD:(i + 1) * N_CHUNKS_PAD, g * HEAD_DIM:(g + 1) * HEAD_DIM]


def _pack_compress(pos, w1, w2):
    r = CMP_BLOCK // CMP_STRIDE
    assert NSA_KV_GROUPS == 2
    w1r = w1.astype(BF16).reshape(r, CMP_STRIDE, HEAD_DIM, CMP_HIDDEN)
    z1 = jnp.zeros_like(w1r)
    w1big = jnp.stack([jnp.concatenate([w1r, z1], axis=-1), jnp.concatenate([z1, w1r], axis=-1)], axis=2)
    w1big = w1big.reshape(r, CHUNK_W, NSA_KV_GROUPS * CMP_HIDDEN)
    z2 = jnp.zeros_like(w2)
    w2big = jnp.concatenate([jnp.concatenate([w2, z2], axis=1), jnp.concatenate([z2, w2], axis=1)], axis=0)
    p = pos.reshape(r, CMP_STRIDE, 1, HEAD_DIM)
    p = jnp.broadcast_to(p, (r, CMP_STRIDE, NSA_KV_GROUPS, HEAD_DIM)).reshape(r, 1, CHUNK_W)
    p = jnp.broadcast_to(p, (r, 16, CHUNK_W))
    return p.astype(BF16), w1big.astype(BF16), w2big.astype(BF16)


def _compress(xk, xv, pos_k, pos_v, k_w1, k_w2, v_w1, v_w2):
    b = xk.shape[0]
    assert xk.shape[1:] == (N_CHUNKS_PAD, CHUNK_W)
    pk, wk1, wk2 = _pack_compress(pos_k, k_w1, k_w2)
    pv, wv1, wv2 = _pack_compress(pos_v, v_w1, v_w2)
    nb = CMP_BATCH if b % CMP_BATCH == 0 else 1
    xs = pl.BlockSpec((nb, N_CHUNKS_PAD, CHUNK_W), lambda i: (i, 0, 0))
    os_ = pl.BlockSpec((nb, NSA_KV_GROUPS, N_CHUNKS_PAD, HEAD_DIM), lambda i: (i, 0, 0, 0))
    osd = jax.ShapeDtypeStruct((b, NSA_KV_GROUPS, N_CHUNKS_PAD, HEAD_DIM), BF16)
    return pl.pallas_call(
        _compress_body,
        grid=(b // nb,),
        in_specs=[xs, xs, _resident(pk.shape), _resident(pv.shape), _resident(wk1.shape),
                  _resident(wv1.shape), _resident(wk2.shape), _resident(wv2.shape)],
        out_specs=(os_, os_),
        out_shape=(osd, osd),
        compiler_params=_params(("parallel",)),
        name="compress",
    )(xk, xv, pk, pv, wk1, wv1, wk2, wv2)


def _bias_lookup(tbl_ref, head, dist):
    acc = jnp.full(dist.shape, tbl_ref[0, head], F32)
    for k, thr in enumerate(BUCKET_THRESHOLDS):
        acc = jnp.where(dist >= thr, tbl_ref[k + 1, head], acc)
    return acc - tbl_ref[NUM_BUCKETS - 1, head]


def _relbias_body(tbl_ref, toep_ref, bc_ref):
    head = pl.program_id(0)
    t = ATT_TILE
    i = lax.broadcasted_iota(jnp.int32, (t, 2 * t), 0)
    u = lax.broadcasted_iota(jnp.int32, (t, 2 * t), 1)
    d = i - u + t
    toep_ref[0] = jnp.where(d >= 0, _bias_lookup(tbl_ref, head, d), NEG_INF)
    n_tiles = bc_ref.shape[1] // t
    per_tile = t // CMP_STRIDE
    off = per_tile * (n_tiles - 1)
    width = 2 * N_CHUNKS_PAD
    assert off + N_CHUNKS_PAD <= width
    i = lax.broadcasted_iota(jnp.int32, (t, width), 0)
    u = lax.broadcasted_iota(jnp.int32, (t, width), 1)
    dc = i - ((u - off) * CMP_STRIDE + CMP_BLOCK - 1)
    strip = jnp.where(dc >= 0, _bias_lookup(tbl_ref, head, dc), NEG_INF)
    for q in range(n_tiles):
        lo = off - q * per_tile
        bc_ref[0, q * t:(q + 1) * t, :] = strip[:, lo:lo + N_CHUNKS_PAD]


def _relbias(tbl, s):
    return pl.pallas_call(
        _relbias_body,
        grid=(NSA_HEADS,),
        in_specs=[pl.BlockSpec(memory_space=pltpu.SMEM)],
        out_specs=(pl.BlockSpec((1, ATT_TILE, 2 * ATT_TILE), lambda h: (h, 0, 0)),
                   pl.BlockSpec((1, s, N_CHUNKS_PAD), lambda h: (h, 0, 0))),
        out_shape=(jax.ShapeDtypeStruct((NSA_HEADS, ATT_TILE, 2 * ATT_TILE), F32),
                   jax.ShapeDtypeStruct((NSA_HEADS, s, N_CHUNKS_PAD), F32)),
        compiler_params=_params(("parallel",)),
        name="relbias",
    )(tbl)


def _lane_fold(x, op):
    return functools.reduce(op, [x[:, i * LANES:(i + 1) * LANES] for i in range(x.shape[1] // LANES)])


def _lane_tile(x, width):
    return jnp.concatenate([x] * (width // LANES), axis=1)


def _row_max_tile(rmax):
    return jnp.broadcast_to(jnp.max(rmax, axis=-1, keepdims=True), rmax.shape)


def _normalize(acc):
    return (acc / pltpu.roll(acc, HEAD_DIM, axis=1))[:, :HEAD_DIM]


def _split3(x):
    hi = x.astype(BF16)
    r = x - hi.astype(F32)
    mid = r.astype(BF16)
    lo = (r - mid.astype(F32)).astype(BF16)
    return hi, mid, lo


def _nsa_body(q_ref, kc_ref, vc_ref, ks_ref, vs_ref, kw_ref, vw_ref, gate_a_ref, gate_b_ref, toep_ref,
              bc_ref, ov_ref, o_ref, s_ref, near_ref, mb_ref, acc_ref, part_ref, *, n_tiles):
    t = ATT_TILE
    j4 = NSA_HPG
    rows = j4 * t
    groups = range(NSA_KV_GROUPS)
    step = pl.program_id(1)
    n_sel = n_tiles * (t // SEL_BLOCK)

    def heads(x):
        return x.reshape(j4, t, x.shape[-1])

    def flat(x):
        return x.reshape(rows, x.shape[-1])

    def key_tile(ref, g, n):
        if isinstance(n, int):
            return ref[0, g, n * t:(n + 1) * t, :]
        return ref[0, g, pl.ds(pl.multiple_of(n * t, t), t), :]

    def gated(gate_ref, g, branch, o):
        gate = jax.nn.sigmoid(gate_ref[0, g])
        return [gate[:, branch * j4 + j:branch * j4 + j + 1] * o[j * t:(j + 1) * t] for j in range(j4)]

    def probs(par, g, s):
        return jnp.exp(s - _lane_tile(mb_ref[par, g], t)).astype(BF16)

    def open_tile(m, par):
        has_prev = m >= 1 if isinstance(m, int) else True
        has_far = m >= 2 if isinstance(m, int) else True
        qs = []
        for g in groups:
            qt = q_ref[0, :, g * j4 * HEAD_DIM:(g + 1) * j4 * HEAD_DIM]
            qs.append(jnp.concatenate([qt[:, j * HEAD_DIM:(j + 1) * HEAD_DIM] for j in range(j4)], axis=0))

        def biases(g):
            toep = toep_ref[g * j4:(g + 1) * j4]
            return toep[:, :, t:], toep[:, :, :t]

        cmp_out = [_nsa_compressed(qs[g], kc_ref[0, g], vc_ref[0, g], bc_ref[g * j4:(g + 1) * j4],
                                   ov_ref[...], m, n_sel) for g in groups]
        part = []
        for g in groups:
            diag_bias, prev_bias = biases(g)
            tiles = [(m, flat(heads(_dot_nt_pair(qs[g], key_tile(kw_ref, g, m))) + diag_bias))]
            if has_prev:
                tiles.append((m - 1, flat(heads(_dot_nt_pair(qs[g], key_tile(kw_ref, g, m - 1))) + prev_bias)))
            if has_far:
                ii = lax.broadcasted_iota(jnp.int32, (t, t), 0)
                jj = lax.broadcasted_iota(jnp.int32, (t, t), 1)
                tri = jnp.where(jj > ii, 0.0, NEG_INF)
                tiles.append((m - 2, flat(heads(_dot_nt_pair(qs[g], key_tile(kw_ref, g, m - 2))) + tri[None])))
            wmax = _lane_tile(_row_max_tile(functools.reduce(
                jnp.maximum, [_lane_fold(w, jnp.maximum) for _, w in tiles])), t)
            acc = sum(_dot_pair(jnp.exp(w - wmax).astype(BF16), key_tile(vw_ref, g, n)) for n, w in tiles)
            part += [a + b for a, b in zip(gated(gate_a_ref, g, 0, cmp_out[g][0]),
                                           gated(gate_a_ref, g, 2, _normalize(acc)))]
        part_ref[par] = jnp.concatenate(part, axis=1)

        q_aug = []
        for g in groups:
            diag_bias, prev_bias = biases(g)
            q_aug.append(jnp.concatenate([qs[g], jnp.concatenate([cmp_out[g][1]] * j4, axis=0)], axis=1))
            s0 = flat(heads(_dot_nt_pair(q_aug[g], key_tile(ks_ref, g, m))) + diag_bias)
            near_ref[par, g, 0] = s0
            rmax = _lane_fold(s0, jnp.maximum)
            if has_prev:
                s1 = flat(heads(_dot_nt_pair(q_aug[g], key_tile(ks_ref, g, m - 1))) + prev_bias)
                near_ref[par, g, 1] = s1
                rmax = jnp.maximum(rmax, _lane_fold(s1, jnp.maximum))
            if has_far:
                s2 = _dot_nt_pair(q_aug[g], key_tile(ks_ref, g, m - 2))
                s_ref[g, m - 2] = s2
                rmax = jnp.maximum(rmax, _lane_fold(s2, jnp.maximum))
            mb_ref[par, g] = rmax
        return q_aug

    def open_far(par, q_aug, n):
        for g in groups:
            s = _dot_nt_pair(q_aug[g], key_tile(ks_ref, g, n))
            s_ref[g, n] = s
            mb_ref[par, g] = jnp.maximum(mb_ref[par, g], _lane_fold(s, jnp.maximum))

    def close_tile(m, par):
        for g in groups:
            mb_ref[par, g] = _row_max_tile(mb_ref[par, g])
            acc = _dot_pair(probs(par, g, near_ref[par, g, 0]), key_tile(vs_ref, g, m))
            if not (isinstance(m, int) and m == 0):
                acc = acc + _dot_pair(probs(par, g, near_ref[par, g, 1]), key_tile(vs_ref, g, m - 1))
            acc_ref[par, g] = acc

    def close_far(par, n):
        for g in groups:
            acc_ref[par, g] += _dot_pair(probs(par, g, s_ref[g, n]), key_tile(vs_ref, g, n))

    def finish(par):
        outs = []
        for g in groups:
            outs += gated(gate_b_ref, g, 1, _normalize(acc_ref[par, g]))
        o_ref[0] = (part_ref[par] + jnp.concatenate(outs, axis=1)).astype(BF16)

    n_static = max(2, min(n_tiles, SEL_TOP_N * SEL_BLOCK // t))
    for j in range(n_static):
        @pl.when(step == j)
        def _(j=j):
            if j >= 2:
                finish(j % 2)
            q_aug = open_tile(j, j % 2)
            if j >= 1:
                close_tile(j - 1, (j - 1) % 2)
            for n in range(max(j - 2, 0)):
                close_far((j - 1) % 2, n)
                open_far(j % 2, q_aug, n)

    for par in (0, 1):
        @pl.when(jnp.logical_and(jnp.logical_and(step >= n_static, step < n_tiles), step % 2 == par))
        def _(par=par):
            finish(par)
            q_aug = open_tile(step, par)
            close_tile(step - 1, 1 - par)

            @pl.loop(0, step - 2)
            def _(n):
                close_far(1 - par, n)
                open_far(par, q_aug, n)

    last = (n_tiles - 1) % 2

    @pl.when(step == n_tiles)
    def _():
        finish(1 - last)
        close_tile(step - 1, last)

        @pl.loop(0, step - 2)
        def _(n):
            close_far(last, n)

    @pl.when(step == n_tiles + 1)
    def _():
        finish(last)


def _nsa_compressed(q, kc, vc, bc, ov, m, n_sel):
    t = ATT_TILE
    j4 = NSA_HPG

    def heads(x):
        return x.reshape(j4, t, x.shape[-1])

    def flat(x):
        return x.reshape(j4 * t, x.shape[-1])

    sc = heads(_dot_nt_pair(q, kc)) + bc
    valid = bc > 0.5 * NEG_INF
    e = jnp.where(valid, jnp.exp(sc - jnp.max(sc, axis=-1, keepdims=True)), 0.0)
    l = jnp.sum(e, axis=-1, keepdims=True)
    pc = jnp.where(l > 0.0, e / l, 0.0)
    o_c = _dot_pair(flat(pc).astype(BF16), vc)
    if isinstance(m, int) and (m + 1) * t <= SEL_TOP_N * SEL_BLOCK:
        return o_c, jnp.zeros((t, HEAD_DIM), BF16)
    psum = pc[0] + pc[1] + pc[2] + pc[3]
    imp = sum(_dot(part, ov) for part in _split3(psum)).T[:n_sel]
    blk = lax.broadcasted_iota(jnp.int32, (n_sel, t), 0)
    tpos = m * t + lax.broadcasted_iota(jnp.int32, (n_sel, t), 1)
    cur = tpos // SEL_BLOCK
    bonus = jnp.where(blk == 0, FORCE_BONUS,
                      jnp.where(blk == cur, FORCE_BONUS, jnp.where(blk == cur - 1, FORCE_BONUS, 0.0)))
    imp = jnp.where(blk * SEL_BLOCK <= tpos, imp + bonus, NEG_INF)
    lanes = 4
    partial = [jnp.zeros((n_sel, t), F32) for _ in range(lanes)]
    for i in range(n_sel):
        row = imp[i:i + 1, :]
        before = jnp.where(blk > i, 1.0, 0.0)
        partial[i % lanes] = partial[i % lanes] + jnp.where(row > imp, 1.0,
                                                            jnp.where(row == imp, before, 0.0))
    rank = (partial[0] + partial[1]) + (partial[2] + partial[3])
    selb = jnp.where(rank < float(SEL_TOP_N), 0.0, NEG_INF)
    selb = jnp.concatenate([selb, jnp.zeros((LANES - n_sel, t), F32)], axis=0).T
    return o_c, selb[:, :HEAD_DIM].astype(BF16)


def _overlap_table(s):
    n_sel = s // SEL_BLOCK
    nc = N_CHUNKS_PAD
    c0 = np.arange(nc)[:, None] * CMP_STRIDE
    s0 = np.arange(n_sel)[None, :] * SEL_BLOCK
    ov = np.clip(np.minimum(c0 + CMP_BLOCK, s0 + SEL_BLOCK) - np.maximum(c0, s0), 0, None) / CMP_STRIDE
    ov[nc - 1] = 0.0
    return jnp.asarray(np.pad(ov, ((0, 0), (0, LANES - n_sel))), BF16)


def _nsa(qn, kc, vc, ks, vs, kw, vw, gates, toep, bc):
    b, s, _ = qn.shape
    t = ATT_TILE
    assert s // SEL_BLOCK <= HEAD_DIM
    ov = _overlap_table(s)
    rows = NSA_HPG * t
    ng = NSA_KV_GROUPS
    nq = s // t
    assert nq >= 2
    seq = lambda n, w=HEAD_DIM: pl.BlockSpec((1, ng, n, w), lambda i, j: (i, 0, 0, 0))
    opened = lambda j: jnp.minimum(j, nq - 1)
    closed = lambda j: jnp.maximum(j - 2, 0)
    return pl.pallas_call(
        functools.partial(_nsa_body, n_tiles=nq),
        grid=(b, nq + 2),
        in_specs=[pl.BlockSpec((1, t, NSA_Q), lambda i, j: (i, opened(j), 0)),
                  seq(N_CHUNKS_PAD), seq(N_CHUNKS_PAD),
                  seq(s, 2 * HEAD_DIM), seq(s, 2 * HEAD_DIM), seq(s), seq(s, 2 * HEAD_DIM),
                  pl.BlockSpec((1, ng, t, GATE_PAD), lambda i, j: (i, 0, opened(j), 0)),
                  pl.BlockSpec((1, ng, t, GATE_PAD), lambda i, j: (i, 0, closed(j), 0)),
                  _resident((NSA_HEADS, t, 2 * t)),
                  pl.BlockSpec((NSA_HEADS, t, N_CHUNKS_PAD), lambda i, j: (0, opened(j), 0)),
                  _resident(ov.shape)],
        out_specs=pl.BlockSpec((1, t, NSA_Q), lambda i, j: (i, closed(j), 0)),
        out_shape=jax.ShapeDtypeStruct((b, s, NSA_Q), BF16),
        scratch_shapes=[pltpu.VMEM((ng, nq - 2, rows, t), F32),
                        pltpu.VMEM((2, ng, 2, rows, t), F32),
                        pltpu.VMEM((2, ng, rows, LANES), F32),
                        pltpu.VMEM((2, ng, rows, 2 * HEAD_DIM), F32),
                        pltpu.VMEM((2, t, NSA_Q), F32)],
        compiler_params=_params(("parallel", "arbitrary")),
        name="nsa",
    )(qn, kc, vc, ks, vs, kw, vw, gates, gates, toep, bc, ov)


def _fcum_body(f_ref, b_ref, o_ref):
    z = f_ref[0] + b_ref[...]
    x = -(jnp.maximum(-z, 0.0) + jnp.log1p(jnp.exp(-jnp.abs(z))))
    n = x.shape[-1]
    lane = lax.broadcasted_iota(jnp.int32, x.shape, 1)
    sh = 1
    while sh < n:
        x = x + jnp.where(lane >= sh, pltpu.roll(x, sh, axis=1), 0.0)
        sh *= 2
    o_ref[0] = x


def _fcum(f_t, b_forget):
    b, h, s = f_t.shape
    blk = pl.BlockSpec((1, h, s), lambda i: (i, 0, 0))
    return pl.pallas_call(
        _fcum_body,
        grid=(b,),
        in_specs=[blk, _resident((h, 1))],
        out_specs=blk,
        out_shape=jax.ShapeDtypeStruct((b, h, s), F32),
        compiler_params=_params(("parallel",)),
        name="fcum",
    )(f_t, b_forget.reshape(h, 1))


FOX_PAIR = 2
FOX_TILE = 512


def _causal_bias(t):
    i = np.arange(t)
    return jnp.asarray(np.where(i[None, :] <= i[:, None], 0.0, NEG_INF), F32)


def _fox_body(q_ref, k_ref, v_ref, c_ref, cm_ref, o_ref, s_ref, mb_ref, acc_ref):
    t = FOX_TILE
    heads = range(FOX_PAIR)
    nq = q_ref.shape[2] // t

    def keys(n):
        return pl.ds(pl.multiple_of(n * t, t), t)

    def scores(m, hh, n):
        q = q_ref[0, hh, m * t:(m + 1) * t, :]
        return _dot_nt(q, k_ref[0, hh, keys(n), :]) - c_ref[0, hh, :, keys(n)]

    def scores_diag(m):
        for hh in heads:
            s = scores(m, hh, m) + cm_ref[...]
            s_ref[m % 2, hh, m] = s
            mb_ref[m % 2, hh] = _lane_fold(s, jnp.maximum)

    def scores_far(m, n):
        for hh in heads:
            s = scores(m, hh, n)
            s_ref[m % 2, hh, n] = s
            mb_ref[m % 2, hh] = jnp.maximum(mb_ref[m % 2, hh], _lane_fold(s, jnp.maximum))

    def finish_max(m):
        for hh in heads:
            mb_ref[m % 2, hh] = _row_max_tile(mb_ref[m % 2, hh])

    def weigh(m, n):
        for hh in heads:
            p = jnp.exp(s_ref[m % 2, hh, n] - _lane_tile(mb_ref[m % 2, hh], t)).astype(BF16)
            acc_ref[hh] += _dot(p, v_ref[0, hh, keys(n), :])

    scores_diag(0)
    finish_max(0)
    for m in range(nq):
        ahead = m + 1 < nq
        if ahead:
            scores_diag(m + 1)
        for hh in heads:
            acc_ref[hh] = jnp.zeros((t, 2 * HEAD_DIM), F32)

        @pl.loop(0, m + 1)
        def _(n, m=m, ahead=ahead):
            weigh(m, n)
            if ahead:
                scores_far(m + 1, n)

        if ahead:
            finish_max(m + 1)
        o_ref[0, m * t:(m + 1) * t, :] = jnp.concatenate(
            [_normalize(acc_ref[hh]) for hh in heads], axis=1).astype(BF16)


def _fox(qf, kf, vf, c):
    b, h, s, _ = qf.shape
    t = FOX_TILE
    seq = lambda w: pl.BlockSpec((1, FOX_PAIR, s, w), lambda i, p: (i, p, 0, 0))
    return pl.pallas_call(
        _fox_body,
        grid=(b, h // FOX_PAIR),
        in_specs=[seq(HEAD_DIM), seq(HEAD_DIM), seq(2 * HEAD_DIM),
                  pl.BlockSpec((1, FOX_PAIR, 1, s), lambda i, p: (i, p, 0, 0)),
                  _resident((t, t))],
        out_specs=pl.BlockSpec((1, s, FOX_PAIR * HEAD_DIM), lambda i, p: (i, 0, p)),
        out_shape=jax.ShapeDtypeStruct((b, s, h * HEAD_DIM), BF16),
        scratch_shapes=[pltpu.VMEM((2, FOX_PAIR, s // t, t, t), F32),
                        pltpu.VMEM((2, FOX_PAIR, t, LANES), F32),
                        pltpu.VMEM((FOX_PAIR, t, 2 * HEAD_DIM), F32)],
        compiler_params=_params(("parallel", "parallel")),
        name="fox",
    )(qf, kf, vf, c.reshape(b, h, 1, s), _causal_bias(t))


def _mixout_body(x_ref, g_ref, on_ref, of_ref, wa_ref, wb_ref, wun_ref, wuf_ref, wo_ref, o_ref):
    x = x_ref[...]
    h = _rms(x, g_ref[...]).astype(BF16)
    y = (jax.nn.sigmoid(_dot(h, wa_ref[...])) * _dot(on_ref[...], wun_ref[...])
         + jax.nn.sigmoid(_dot(h, wb_ref[...])) * _dot(of_ref[...], wuf_ref[...]))
    o_ref[...] = x + _dot(y.astype(BF16), wo_ref[...])


def _mixout(x2d, g, o_nsa, o_fox, w_a, w_b, w_un, w_uf, w_o):
    n = x2d.shape[0]
    tok = lambda c: pl.BlockSpec((TOK_TILE, c), lambda i: (i, 0))
    return pl.pallas_call(
        _mixout_body,
        grid=(n // TOK_TILE,),
        in_specs=[tok(D_MODEL), _resident((1, D_MODEL)), tok(NSA_Q), tok(FOX_W),
                  _resident(w_a.shape), _resident(w_b.shape), _resident(w_un.shape),
                  _resident(w_uf.shape), _resident(w_o.shape)],
        out_specs=tok(D_MODEL),
        out_shape=jax.ShapeDtypeStruct((n, D_MODEL), F32),
        compiler_params=_params(("parallel",)),
        name="mixout",
    )(x2d, g.reshape(1, D_MODEL), o_nsa, o_fox, w_a, w_b, w_un, w_uf, w_o)


MEM_BATCH = 4


def _memkv_body(m_ref, g_ref, w_ref, k_ref, v_ref):
    nb, ml, d = m_ref.shape
    h = _rms(m_ref[...].reshape(nb * ml, d), g_ref[...]).astype(BF16)
    z = _dot(h, w_ref[...]).astype(BF16)
    k_ref[...] = z[:, :D_MODEL].reshape(nb, ml, d)
    v_ref[...] = z[:, D_MODEL:].reshape(nb, ml, d)


def _memkv(mem, g, w_kv):
    b, ml, _ = mem.shape
    nb = MEM_BATCH if b % MEM_BATCH == 0 else 1
    blk = pl.BlockSpec((nb, ml, D_MODEL), lambda i: (i, 0, 0))
    sd = jax.ShapeDtypeStruct((b, ml, D_MODEL), BF16)
    return pl.pallas_call(
        _memkv_body,
        grid=(b // nb,),
        in_specs=[blk, _resident((1, D_MODEL)), _resident(w_kv.shape)],
        out_specs=(blk, blk),
        out_shape=(sd, sd),
        compiler_params=_params(("parallel",)),
        name="memkv",
    )(mem, g.reshape(1, D_MODEL), w_kv)


def _memattn_body(x_ref, g_ref, k_ref, v_ref, wq_ref, wo_ref, o_ref):
    x = x_ref[0]
    h = _rms(x, g_ref[...]).astype(BF16)
    q = (_dot(h, wq_ref[...]) * (MEM_HEAD_DIM ** -0.5)).astype(BF16)
    outs = []
    for hd in range(MEM_HEADS):
        cols = slice(hd * MEM_HEAD_DIM, (hd + 1) * MEM_HEAD_DIM)
        s = _dot_nt(q[:, cols], k_ref[0, :, cols])
        e = jnp.exp(s - jnp.max(s, axis=-1, keepdims=True))
        p = e / jnp.sum(e, axis=-1, keepdims=True)
        outs.append(_dot(p.astype(BF16), v_ref[0, :, cols]))
    o = jnp.concatenate(outs, axis=1).astype(BF16)
    o_ref[0] = x + _dot(o, wo_ref[...])


def _memattn(x, g, k, v, w_q, w_o):
    b, s, _ = x.shape
    ml = k.shape[1]
    tok = pl.BlockSpec((1, TOK_TILE, D_MODEL), lambda i, j: (i, j, 0))
    kvb = pl.BlockSpec((1, ml, D_MODEL), lambda i, j: (i, 0, 0))
    return pl.pallas_call(
        _memattn_body,
        grid=(b, s // TOK_TILE),
        in_specs=[tok, _resident((1, D_MODEL)), kvb, kvb, _resident(w_q.shape), _resident(w_o.shape)],
        out_specs=tok,
        out_shape=jax.ShapeDtypeStruct((b, s, D_MODEL), F32),
        compiler_params=_params(("parallel", "parallel")),
        name="memattn",
    )(x, g.reshape(1, D_MODEL), k, v, w_q, w_o)


def kernel(x, mem, rel_bias_table, ffn1_norm, ffn1_w_gate, ffn1_w_up, ffn1_w_down, mix_norm, mix_w_in, mix_b_forget, cmp_pos_k, cmp_pos_v, cmp_k_w1, cmp_k_w2, cmp_v_w1, cmp_v_w2, w_up_nsa, w_up_fox, mix_w_out, mem_q_norm, mem_kv_norm, mem_w_q, mem_w_kv, mem_w_o, ffn2_norm, ffn2_w_gate, ffn2_w_up, ffn2_w_down, final_norm):
    b, s, d = x.shape
    depth = ffn1_norm.shape[0]
    bf = lambda w: w.astype(BF16)
    toep, bias_c = _relbias(rel_bias_table, s)
    x = x.reshape(b * s, d)
    for l in range(depth):
        last = l == depth - 1
        x = _ffn(x, ffn1_norm[l], ffn1_w_gate[l], ffn1_w_up[l], ffn1_w_down[l])

        w_nsa, w_fox, w_small, w_a, w_b = _pack_w_in(mix_w_in, l)
        qn, kc, vc, ks, vs, kw, vw, qf, kf, vf, gates, flog = _inproj(
            x.reshape(b, s, d), mix_norm[l], w_nsa, w_fox, w_small)
        kcc, vcc = _compress(kc, vc, cmp_pos_k[l], cmp_pos_v[l], cmp_k_w1[l], cmp_k_w2[l],
                             cmp_v_w1[l], cmp_v_w2[l])
        o_nsa = _nsa(qn, kcc, vcc, ks, vs, kw, vw, gates, toep, bias_c)
        c = _fcum(flog, mix_b_forget[l])
        o_fox = _fox(qf, kf, vf, c)
        x = _mixout(x, mix_norm[l], o_nsa.reshape(b * s, NSA_Q), o_fox.reshape(b * s, FOX_W),
                    w_a, w_b, bf(w_up_nsa[l]), bf(w_up_fox[l]), bf(mix_w_out[l]))

        mk, mv = _memkv(mem, mem_kv_norm[l], bf(mem_w_kv[l]))
        x = _memattn(x.reshape(b, s, d), mem_q_norm[l], mk, mv, bf(mem_w_q[l]), bf(mem_w_o[l]))
        x = _ffn(x.reshape(b * s, d), ffn2_norm[l], ffn2_w_gate[l], ffn2_w_up[l],
                 ffn2_w_down[l], final_g=final_norm if last else None)
    return x.reshape(b, s, d)
```

```python
import functools
import math

import numpy as np
import jax
import jax.numpy as jnp
from jax import lax
from jax.experimental import pallas as pl
from jax.experimental.pallas import tpu as pltpu

D_MODEL = 1024
D_FF = 2816
HEAD_DIM = 64
NSA_HEADS = 8
NSA_KV_GROUPS = 2
NSA_HPG = NSA_HEADS // NSA_KV_GROUPS
CMP_BLOCK = 32
CMP_STRIDE = 16
CMP_HIDDEN = 256
SEL_BLOCK = 64
SEL_TOP_N = 16
WINDOW = 512
FOX_HEADS = 8
MEM_HEADS = 4
MEM_HEAD_DIM = D_MODEL // MEM_HEADS
NUM_BUCKETS = 32
MAX_DISTANCE = 128
RMS_EPS = 1e-6
NEG_INF = -1e30
FORCE_BONUS = 1e4

NSA_Q = NSA_HEADS * HEAD_DIM
NSA_KV = NSA_KV_GROUPS * HEAD_DIM
FOX_W = FOX_HEADS * HEAD_DIM

LANES = 128
VMEM_LIMIT = 56 * 1024 * 1024

BF16 = jnp.bfloat16
F32 = jnp.float32

ATT_TILE = 256
TOK_TILE = 1024
FFN_TILE = 1024
FF_CHUNK = 256


def _bucket_thresholds():
    n = np.arange(0, 4 * MAX_DISTANCE)
    exact = NUM_BUCKETS // 2
    large = exact + (np.log(np.maximum(n, 1) / exact) / math.log(MAX_DISTANCE / exact)
                     * (NUM_BUCKETS - exact)).astype(np.int64)
    bucket = np.where(n < exact, n, np.minimum(large, NUM_BUCKETS - 1))
    assert np.all(np.diff(bucket) >= 0)
    return [int(np.argmax(bucket >= k)) for k in range(1, NUM_BUCKETS)]


BUCKET_THRESHOLDS = _bucket_thresholds()


def _dot(a, b):
    return jnp.dot(a, b, preferred_element_type=F32)


def _dot_nt(a, b):
    return lax.dot_general(a, b, (((1,), (1,)), ((), ())), preferred_element_type=F32)


def _row_halves(fn, a, b):
    h = a.shape[0] // 2
    return jnp.concatenate([fn(a[:h], b), fn(a[h:], b)], axis=0)


def _dot_pair(a, b):
    return _row_halves(_dot, a, b)


def _dot_nt_pair(a, b):
    return _row_halves(_dot_nt, a, b)


def _rms(x, g):
    return x * lax.rsqrt(jnp.mean(x * x, axis=-1, keepdims=True) + RMS_EPS) * g


def _resident(shape):
    nd = len(shape)
    return pl.BlockSpec(shape, lambda *_: (0,) * nd, pipeline_mode=pl.Buffered(1))


def _params(sem):
    return pltpu.CompilerParams(dimension_semantics=sem, vmem_limit_bytes=VMEM_LIMIT)


def _ffn_body(x_ref, g_ref, wg_hbm, wu_hbm, wd_hbm, *rest, final):
    if final:
        fg_ref, o_ref, wg_ref, wu_ref, wd_ref, sg_ref, su_ref, sd_ref, sem = rest
    else:
        o_ref, wg_ref, wu_ref, wd_ref, sg_ref, su_ref, sd_ref, sem = rest
    n_chunks = D_FF // FF_CHUNK

    def chunk_copies(c):
        cols = slice(c * FF_CHUNK, (c + 1) * FF_CHUNK)
        slot = c % 2
        return (pltpu.make_async_copy(wg_hbm.at[:, cols], sg_ref.at[slot], sem.at[0, slot]),
                pltpu.make_async_copy(wu_hbm.at[:, cols], su_ref.at[slot], sem.at[1, slot]),
                pltpu.make_async_copy(wd_hbm.at[cols, :], sd_ref.at[slot], sem.at[2, slot]))

    @pl.when(pl.program_id(0) == 0)
    def _():
        for cp in chunk_copies(0):
            cp.start()
        for c in range(n_chunks):
            sl = slice(c * FF_CHUNK, (c + 1) * FF_CHUNK)
            if c + 1 < n_chunks:
                for cp in chunk_copies(c + 1):
                    cp.start()
            for cp in chunk_copies(c):
                cp.wait()
            wg_ref[:, sl] = sg_ref[c % 2].astype(BF16)
            wu_ref[:, sl] = su_ref[c % 2].astype(BF16)
            wd_ref[sl, :] = sd_ref[c % 2].astype(BF16)

    x = x_ref[...]
    h = _rms(x, g_ref[...]).astype(BF16)
    acc = jnp.zeros(x.shape, F32)
    for c in range(n_chunks):
        sl = slice(c * FF_CHUNK, (c + 1) * FF_CHUNK)
        a = _dot(h, wg_ref[:, sl])
        b = _dot(h, wu_ref[:, sl])
        t = (a * jax.nn.sigmoid(a)) * b
        acc = acc + _dot(t.astype(BF16), wd_ref[sl, :])
    y = x + 0.5 * acc
    if final:
        y = _rms(y, fg_ref[...])
    o_ref[...] = y


def _ffn(x2d, g, wg, wu, wd, final_g=None):
    n = x2d.shape[0]
    final = final_g is not None
    tok = pl.BlockSpec((FFN_TILE, D_MODEL), lambda i: (i, 0))
    hbm = pl.BlockSpec(memory_space=pl.ANY)
    in_specs = [tok, _resident((1, D_MODEL)), hbm, hbm, hbm]
    args = [x2d, g.reshape(1, D_MODEL), wg, wu, wd]
    if final:
        in_specs.append(_resident((1, D_MODEL)))
        args.append(final_g.reshape(1, D_MODEL))
    return pl.pallas_call(
        functools.partial(_ffn_body, final=final),
        grid=(n // FFN_TILE,),
        in_specs=in_specs,
        out_specs=tok,
        out_shape=jax.ShapeDtypeStruct((n, D_MODEL), F32),
        scratch_shapes=[pltpu.VMEM((D_MODEL, D_FF), BF16), pltpu.VMEM((D_MODEL, D_FF), BF16),
                        pltpu.VMEM((D_FF, D_MODEL), BF16),
                        pltpu.VMEM((2, D_MODEL, FF_CHUNK), F32), pltpu.VMEM((2, D_MODEL, FF_CHUNK), F32),
                        pltpu.VMEM((2, FF_CHUNK, D_MODEL), F32),
                        pltpu.SemaphoreType.DMA((3, 2))],
        compiler_params=_params(("arbitrary",)),
        name="ffn_final" if final else "ffn",
    )(*args)


_C_CMP = NSA_Q
_C_KV = _C_CMP + 2 * NSA_KV
_W_NSA = _C_KV + 4 * NSA_KV
GATE_PAD = 16
_C_FLOG = NSA_KV_GROUPS * GATE_PAD
N_CHUNKS_PAD = 128
CHUNK_W = CMP_STRIDE * NSA_KV


def _inproj_body(x_ref, g_ref, wn_ref, wf_ref, ws_ref, blk_ref, qn_ref, kc_ref, vc_ref, ks_ref, vs_ref,
                 kw_ref, vw_ref, qf_ref, kf_ref, vf_ref, gate_ref, fl_ref, cmp_ref):
    h = _rms(x_ref[0], g_ref[...]).astype(BF16)
    z = _dot(h, wn_ref[...])
    qn_ref[0] = (z[:, :_C_CMP] * (HEAD_DIM ** -0.5)).astype(BF16)
    rows = z.shape[0] // CMP_STRIDE
    for i, ref in enumerate((kc_ref, vc_ref)):
        cmp_ref[i] = z[:, _C_CMP + i * NSA_KV:_C_CMP + (i + 1) * NSA_KV]
        for r in range(CMP_STRIDE):
            tok = cmp_ref[i, pl.ds(r, rows, stride=CMP_STRIDE), :]
            ref[0, :, r * NSA_KV:(r + 1) * NSA_KV] = tok.astype(BF16)
    ones = jnp.ones((z.shape[0], HEAD_DIM), BF16)
    for i, (ref, extra) in enumerate(((ks_ref, blk_ref[...]), (vs_ref, ones), (kw_ref, None),
                                      (vw_ref, ones))):
        for g in range(NSA_KV_GROUPS):
            c0 = _C_KV + i * NSA_KV + g * HEAD_DIM
            val = z[:, c0:c0 + HEAD_DIM].astype(BF16)
            ref[0, g] = val if extra is None else jnp.concatenate([val, extra], axis=1)
    zf = _dot(h, wf_ref[...])
    for hd in range(FOX_HEADS):
        c0 = hd * HEAD_DIM
        qf_ref[0, hd] = (zf[:, c0:c0 + HEAD_DIM] * (HEAD_DIM ** -0.5)).astype(BF16)
        kf_ref[0, hd] = zf[:, c0 + FOX_W:c0 + FOX_W + HEAD_DIM].astype(BF16)
        v = zf[:, c0 + 2 * FOX_W:c0 + 2 * FOX_W + HEAD_DIM].astype(BF16)
        vf_ref[0, hd] = jnp.concatenate([v, ones], axis=1)
    zs = _dot(h, ws_ref[...])
    for g in range(NSA_KV_GROUPS):
        gate_ref[0, g] = zs[:, g * GATE_PAD:(g + 1) * GATE_PAD]
    fl_ref[0] = zs.T[_C_FLOG:_C_FLOG + FOX_HEADS]


_IN_COLS = np.cumsum((0, NSA_Q, NSA_KV, NSA_KV, NSA_KV, NSA_KV, NSA_KV, NSA_KV, 3 * NSA_HEADS,
                      FOX_W, FOX_W, FOX_W, FOX_HEADS, D_MODEL, D_MODEL)).tolist()
W_PREP_ROWS = 256


def _wprep_body(w_ref, wn_ref, wf_ref, ws_ref, wa_ref, wb_ref):
    wt = w_ref[0]
    c = _IN_COLS
    k = wt.shape[1]
    wn_ref[...] = wt[:c[7]].T.astype(BF16)
    wf_ref[...] = wt[c[8]:c[11]].T.astype(BF16)
    wa_ref[...] = wt[c[12]:c[13]].T.astype(BF16)
    wb_ref[...] = wt[c[13]:c[14]].T.astype(BF16)
    small = []
    for g in range(NSA_KV_GROUPS):
        for br in range(3):
            c0 = c[7] + br * NSA_HEADS + g * NSA_HPG
            small.append(wt[c0:c0 + NSA_HPG])
        small.append(jnp.zeros((GATE_PAD - 3 * NSA_HPG, k), F32))
    small.append(wt[c[11]:c[12]])
    small.append(jnp.zeros((LANES - _C_FLOG - FOX_HEADS, k), F32))
    ws_ref[...] = jnp.concatenate(small, axis=0).T.astype(BF16)


def _pack_w_in(w_in, layer):
    _, k, n = w_in.shape
    c = _IN_COLS
    assert n == c[14]
    widths = (c[7], c[11] - c[8], LANES, D_MODEL, D_MODEL)
    return pl.pallas_call(
        _wprep_body,
        grid=(k // W_PREP_ROWS,),
        in_specs=[pl.BlockSpec((1, n, W_PREP_ROWS), lambda i: (layer, 0, i))],
        out_specs=tuple(pl.BlockSpec((W_PREP_ROWS, w), lambda i: (i, 0)) for w in widths),
        out_shape=tuple(jax.ShapeDtypeStruct((k, w), BF16) for w in widths),
        compiler_params=_params(("parallel",)),
        name="wprep",
    )(jnp.swapaxes(w_in, 1, 2))


def _inproj(x, g, w_nsa, w_fox, w_small):
    b, s, _ = x.shape
    grid = (b, s // TOK_TILE)
    tok = lambda c: pl.BlockSpec((1, TOK_TILE, c), lambda i, j: (i, j, 0))
    heads = lambda nh, w=HEAD_DIM: pl.BlockSpec((1, nh, TOK_TILE, w), lambda i, j: (i, 0, j, 0))
    chunks = pl.BlockSpec((1, TOK_TILE // CMP_STRIDE, CHUNK_W), lambda i, j: (i, j, 0))
    sds = jax.ShapeDtypeStruct
    kv = sds((b, NSA_KV_GROUPS, s, HEAD_DIM), BF16)
    kv2 = sds((b, NSA_KV_GROUPS, s, 2 * HEAD_DIM), BF16)
    fx = sds((b, FOX_HEADS, s, HEAD_DIM), BF16)
    cmp_in = sds((b, s // CMP_STRIDE, CHUNK_W), BF16)
    out_shape = (sds((b, s, NSA_Q), BF16), cmp_in, cmp_in,
                 kv2, kv2, kv, kv2, fx, fx, sds((b, FOX_HEADS, s, 2 * HEAD_DIM), BF16),
                 sds((b, NSA_KV_GROUPS, s, GATE_PAD), F32), sds((b, FOX_HEADS, s), F32))
    out_specs = (tok(NSA_Q), chunks, chunks,
                 heads(NSA_KV_GROUPS, 2 * HEAD_DIM), heads(NSA_KV_GROUPS, 2 * HEAD_DIM),
                 heads(NSA_KV_GROUPS), heads(NSA_KV_GROUPS, 2 * HEAD_DIM),
                 heads(FOX_HEADS), heads(FOX_HEADS), heads(FOX_HEADS, 2 * HEAD_DIM),
                 pl.BlockSpec((1, NSA_KV_GROUPS, TOK_TILE, GATE_PAD), lambda i, j: (i, 0, j, 0)),
                 pl.BlockSpec((1, FOX_HEADS, TOK_TILE), lambda i, j: (i, 0, j)))
    key_blk = np.arange(s) // SEL_BLOCK
    blk_onehot = jnp.asarray(key_blk[:, None] == np.arange(HEAD_DIM)[None, :], BF16)
    return pl.pallas_call(
        _inproj_body,
        grid=grid,
        in_specs=[tok(D_MODEL), _resident((1, D_MODEL)), _resident(w_nsa.shape),
                  _resident(w_fox.shape), _resident(w_small.shape),
                  pl.BlockSpec((TOK_TILE, HEAD_DIM), lambda i, j: (j, 0))],
        out_specs=out_specs,
        out_shape=out_shape,
        scratch_shapes=[pltpu.VMEM((2, TOK_TILE, NSA_KV), F32)],
        compiler_params=_params(("parallel", "parallel")),
        name="inproj",
    )(x, g.reshape(1, D_MODEL), w_nsa, w_fox, w_small, blk_onehot)


CMP_BATCH = 4

def _compress_body(xk_ref, xv_ref, pk_ref, pv_ref, wk1_ref, wv1_ref, wk2_ref, wv2_ref, ok_ref, ov_ref):
    nb = xk_ref.shape[0]
    n = nb * N_CHUNKS_PAD
    for x_ref, p_ref, w1_ref, w2_ref, o_ref in ((xk_ref, pk_ref, wk1_ref, wk2_ref, ok_ref),
                                                (xv_ref, pv_ref, wv1_ref, wv2_ref, ov_ref)):
        x = x_ref[...].reshape(n, CHUNK_W)
        a0 = _dot(x, w1_ref[0])
        a1 = _dot(x, w1_ref[1])
        c = _dot(p_ref[0], w1_ref[0]) + _dot(p_ref[1], w1_ref[1])
        pre = a0 + pltpu.roll(a1, n - 1, axis=0) + c[0:1, :]
        hid = jax.nn.gelu(pre).astype(BF16)
        out = _dot(hid, w2_ref[...]).astype(BF16)
        for i in range(nb):
            for g in range(NSA_KV_GROUPS):
                o_ref[i, g] = out[i * N_CHUNKS_PAD:(i + 1) * N_CHUNKS_PAD, g * HEAD_DIM:(g + 1) * HEAD_DIM]


def _pack_compress(pos, w1, w2):
    r = CMP_BLOCK // CMP_STRIDE
    assert NSA_KV_GROUPS == 2
    w1r = w1.astype(BF16).reshape(r, CMP_STRIDE, HEAD_DIM, CMP_HIDDEN)
    z1 = jnp.zeros_like(w1r)
    w1big = jnp.stack([jnp.concatenate([w1r, z1], axis=-1), jnp.concatenate([z1, w1r], axis=-1)], axis=2)
    w1big = w1big.reshape(r, CHUNK_W, NSA_KV_GROUPS * CMP_HIDDEN)
    z2 = jnp.zeros_like(w2)
    w2big = jnp.concatenate([jnp.concatenate([w2, z2], axis=1), jnp.concatenate([z2, w2], axis=1)], axis=0)
    p = pos.reshape(r, CMP_STRIDE, 1, HEAD_DIM)
    p = jnp.broadcast_to(p, (r, CMP_STRIDE, NSA_KV_GROUPS, HEAD_DIM)).reshape(r, 1, CHUNK_W)
    p = jnp.broadcast_to(p, (r, 16, CHUNK_W))
    return p.astype(BF16), w1big.astype(BF16), w2big.astype(BF16)


def _compress(xk, xv, pos_k, pos_v, k_w1, k_w2, v_w1, v_w2):
    b = xk.shape[0]
    assert xk.shape[1:] == (N_CHUNKS_PAD, CHUNK_W)
    pk, wk1, wk2 = _pack_compress(pos_k, k_w1, k_w2)
    pv, wv1, wv2 = _pack_compress(pos_v, v_w1, v_w2)
    nb = CMP_BATCH if b % CMP_BATCH == 0 else 1
    xs = pl.BlockSpec((nb, N_CHUNKS_PAD, CHUNK_W), lambda i: (i, 0, 0))
    os_ = pl.BlockSpec((nb, NSA_KV_GROUPS, N_CHUNKS_PAD, HEAD_DIM), lambda i: (i, 0, 0, 0))
    osd = jax.ShapeDtypeStruct((b, NSA_KV_GROUPS, N_CHUNKS_PAD, HEAD_DIM), BF16)
    return pl.pallas_call(
        _compress_body,
        grid=(b // nb,),
        in_specs=[xs, xs, _resident(pk.shape), _resident(pv.shape), _resident(wk1.shape),
                  _resident(wv1.shape), _resident(wk2.shape), _resident(wv2.shape)],
        out_specs=(os_, os_),
        out_shape=(osd, osd),
        compiler_params=_params(("parallel",)),
        name="compress",
    )(xk, xv, pk, pv, wk1, wv1, wk2, wv2)


def _bias_lookup(tbl_ref, head, dist):
    acc = jnp.full(dist.shape, tbl_ref[0, head], F32)
    for k, thr in enumerate(BUCKET_THRESHOLDS):
        acc = jnp.where(dist >= thr, tbl_ref[k + 1, head], acc)
    return acc - tbl_ref[NUM_BUCKETS - 1, head]


def _relbias_body(tbl_ref, toep_ref, bc_ref):
    head = pl.program_id(0)
    t = ATT_TILE
    i = lax.broadcasted_iota(jnp.int32, (t, 2 * t), 0)
    u = lax.broadcasted_iota(jnp.int32, (t, 2 * t), 1)
    d = i - u + t
    toep_ref[0] = jnp.where(d >= 0, _bias_lookup(tbl_ref, head, d), NEG_INF)
    n_tiles = bc_ref.shape[1] // t
    per_tile = t // CMP_STRIDE
    off = per_tile * (n_tiles - 1)
    width = 2 * N_CHUNKS_PAD
    assert off + N_CHUNKS_PAD <= width
    i = lax.broadcasted_iota(jnp.int32, (t, width), 0)
    u = lax.broadcasted_iota(jnp.int32, (t, width), 1)
    dc = i - ((u - off) * CMP_STRIDE + CMP_BLOCK - 1)
    strip = jnp.where(dc >= 0, _bias_lookup(tbl_ref, head, dc), NEG_INF)
    for q in range(n_tiles):
        lo = off - q * per_tile
        bc_ref[0, q * t:(q + 1) * t, :] = strip[:, lo:lo + N_CHUNKS_PAD]


def _relbias(tbl, s):
    return pl.pallas_call(
        _relbias_body,
        grid=(NSA_HEADS,),
        in_specs=[pl.BlockSpec(memory_space=pltpu.SMEM)],
        out_specs=(pl.BlockSpec((1, ATT_TILE, 2 * ATT_TILE), lambda h: (h, 0, 0)),
                   pl.BlockSpec((1, s, N_CHUNKS_PAD), lambda h: (h, 0, 0))),
        out_shape=(jax.ShapeDtypeStruct((NSA_HEADS, ATT_TILE, 2 * ATT_TILE), F32),
                   jax.ShapeDtypeStruct((NSA_HEADS, s, N_CHUNKS_PAD), F32)),
        compiler_params=_params(("parallel",)),
        name="relbias",
    )(tbl)


def _lane_fold(x, op):
    return functools.reduce(op, [x[:, i * LANES:(i + 1) * LANES] for i in range(x.shape[1] // LANES)])


def _lane_tile(x, width):
    return jnp.concatenate([x] * (width // LANES), axis=1)


def _row_max_tile(rmax):
    return jnp.broadcast_to(jnp.max(rmax, axis=-1, keepdims=True), rmax.shape)


def _normalize(acc):
    return (acc / pltpu.roll(acc, HEAD_DIM, axis=1))[:, :HEAD_DIM]


def _split3(x):
    hi = x.astype(BF16)
    r = x - hi.astype(F32)
    mid = r.astype(BF16)
    lo = (r - mid.astype(F32)).astype(BF16)
    return hi, mid, lo


def _nsa_body(q_ref, kc_ref, vc_ref, ks_ref, vs_ref, kw_ref, vw_ref, gate_a_ref, gate_b_ref, toep_ref,
              bc_ref, ov_ref, o_ref, s_ref, near_ref, mb_ref, acc_ref, part_ref, *, n_tiles):
    t = ATT_TILE
    j4 = NSA_HPG
    rows = j4 * t
    groups = range(NSA_KV_GROUPS)
    step = pl.program_id(1)
    n_sel = n_tiles * (t // SEL_BLOCK)

    def heads(x):
        return x.reshape(j4, t, x.shape[-1])

    def flat(x):
        return x.reshape(rows, x.shape[-1])

    def key_tile(ref, g, n):
        if isinstance(n, int):
            return ref[0, g, n * t:(n + 1) * t, :]
        return ref[0, g, pl.ds(pl.multiple_of(n * t, t), t), :]

    def gated(gate_ref, g, branch, o):
        gate = jax.nn.sigmoid(gate_ref[0, g])
        return [gate[:, branch * j4 + j:branch * j4 + j + 1] * o[j * t:(j + 1) * t] for j in range(j4)]

    def probs(par, g, s):
        return jnp.exp(s - _lane_tile(mb_ref[par, g], t)).astype(BF16)

    def open_tile(m, par):
        has_prev = m >= 1 if isinstance(m, int) else True
        has_far = m >= 2 if isinstance(m, int) else True
        qs = []
        for g in groups:
            qt = q_ref[0, :, g * j4 * HEAD_DIM:(g + 1) * j4 * HEAD_DIM]
            qs.append(jnp.concatenate([qt[:, j * HEAD_DIM:(j + 1) * HEAD_DIM] for j in range(j4)], axis=0))

        def biases(g):
            toep = toep_ref[g * j4:(g + 1) * j4]
            return toep[:, :, t:], toep[:, :, :t]

        cmp_out = [_nsa_compressed(qs[g], kc_ref[0, g], vc_ref[0, g], bc_ref[g * j4:(g + 1) * j4],
                                   ov_ref[...], m, n_sel) for g in groups]
        part = []
        for g in groups:
            diag_bias, prev_bias = biases(g)
            tiles = [(m, flat(heads(_dot_nt_pair(qs[g], key_tile(kw_ref, g, m))) + diag_bias))]
            if has_prev:
                tiles.append((m - 1, flat(heads(_dot_nt_pair(qs[g], key_tile(kw_ref, g, m - 1))) + prev_bias)))
            if has_far:
                ii = lax.broadcasted_iota(jnp.int32, (t, t), 0)
                jj = lax.broadcasted_iota(jnp.int32, (t, t), 1)
                tri = jnp.where(jj > ii, 0.0, NEG_INF)
                tiles.append((m - 2, flat(heads(_dot_nt_pair(qs[g], key_tile(kw_ref, g, m - 2))) + tri[None])))
            wmax = _lane_tile(_row_max_tile(functools.reduce(
                jnp.maximum, [_lane_fold(w, jnp.maximum) for _, w in tiles])), t)
            acc = sum(_dot_pair(jnp.exp(w - wmax).astype(BF16), key_tile(vw_ref, g, n)) for n, w in tiles)
            part += [a + b for a, b in zip(gated(gate_a_ref, g, 0, cmp_out[g][0]),
                                           gated(gate_a_ref, g, 2, _normalize(acc)))]
        part_ref[par] = jnp.concatenate(part, axis=1)

        q_aug = []
        for g in groups:
            diag_bias, prev_bias = biases(g)
            q_aug.append(jnp.concatenate([qs[g], jnp.concatenate([cmp_out[g][1]] * j4, axis=0)], axis=1))
            s0 = flat(heads(_dot_nt_pair(q_aug[g], key_tile(ks_ref, g, m))) + diag_bias)
            near_ref[par, g, 0] = s0
            rmax = _lane_fold(s0, jnp.maximum)
            if has_prev:
                s1 = flat(heads(_dot_nt_pair(q_aug[g], key_tile(ks_ref, g, m - 1))) + prev_bias)
                near_ref[par, g, 1] = s1
                rmax = jnp.maximum(rmax, _lane_fold(s1, jnp.maximum))
            if has_far:
                s2 = _dot_nt_pair(q_aug[g], key_tile(ks_ref, g, m - 2))
                s_ref[g, m - 2] = s2
                rmax = jnp.maximum(rmax, _lane_fold(s2, jnp.maximum))
            mb_ref[par, g] = rmax
        return q_aug

    def open_far(par, q_aug, n):
        for g in groups:
            s = _dot_nt_pair(q_aug[g], key_tile(ks_ref, g, n))
            s_ref[g, n] = s
            mb_ref[par, g] = jnp.maximum(mb_ref[par, g], _lane_fold(s, jnp.maximum))

    def close_tile(m, par):
        for g in groups:
            mb_ref[par, g] = _row_max_tile(mb_ref[par, g])
            acc = _dot_pair(probs(par, g, near_ref[par, g, 0]), key_tile(vs_ref, g, m))
            if not (isinstance(m, int) and m == 0):
                acc = acc + _dot_pair(probs(par, g, near_ref[par, g, 1]), key_tile(vs_ref, g, m - 1))
            acc_ref[par, g] = acc

    def close_far(par, n):
        for g in groups:
            acc_ref[par, g] += _dot_pair(probs(par, g, s_ref[g, n]), key_tile(vs_ref, g, n))

    def finish(par):
        outs = []
        for g in groups:
            outs += gated(gate_b_ref, g, 1, _normalize(acc_ref[par, g]))
        o_ref[0] = (part_ref[par] + jnp.concatenate(outs, axis=1)).astype(BF16)

    n_static = 2
    for j in range(n_static):
        @pl.when(step == j)
        def _(j=j):
            if j >= 2:
                finish(j % 2)
            q_aug = open_tile(j, j % 2)
            if j >= 1:
                close_tile(j - 1, (j - 1) % 2)
            for n in range(max(j - 2, 0)):
                close_far((j - 1) % 2, n)
                open_far(j % 2, q_aug, n)

    @pl.when(jnp.logical_and(step >= n_static, step < n_tiles))
    def _():
        par = step % 2
        finish(par)
        q_aug = open_tile(step, par)
        close_tile(step - 1, 1 - par)

        @pl.loop(0, step - 2)
        def _(n):
            close_far(1 - par, n)
            open_far(par, q_aug, n)

    last = (n_tiles - 1) % 2

    @pl.when(step == n_tiles)
    def _():
        finish(1 - last)
        close_tile(step - 1, last)

        @pl.loop(0, step - 2)
        def _(n):
            close_far(last, n)

    @pl.when(step == n_tiles + 1)
    def _():
        finish(last)


def _nsa_compressed(q, kc, vc, bc, ov, m, n_sel):
    t = ATT_TILE
    j4 = NSA_HPG

    def heads(x):
        return x.reshape(j4, t, x.shape[-1])

    def flat(x):
        return x.reshape(j4 * t, x.shape[-1])

    sc = heads(_dot_nt_pair(q, kc)) + bc
    valid = bc > 0.5 * NEG_INF
    e = jnp.where(valid, jnp.exp(sc - jnp.max(sc, axis=-1, keepdims=True)), 0.0)
    l = jnp.sum(e, axis=-1, keepdims=True)
    pc = jnp.where(l > 0.0, e / l, 0.0)
    o_c = _dot_pair(flat(pc).astype(BF16), vc)
    if isinstance(m, int) and (m + 1) * t <= SEL_TOP_N * SEL_BLOCK:
        return o_c, jnp.zeros((t, HEAD_DIM), BF16)
    psum = pc[0] + pc[1] + pc[2] + pc[3]
    imp = sum(_dot(part, ov) for part in _split3(psum)).T[:n_sel]
    blk = lax.broadcasted_iota(jnp.int32, (n_sel, t), 0)
    tpos = m * t + lax.broadcasted_iota(jnp.int32, (n_sel, t), 1)
    cur = tpos // SEL_BLOCK
    bonus = jnp.where(blk == 0, FORCE_BONUS,
                      jnp.where(blk == cur, FORCE_BONUS, jnp.where(blk == cur - 1, FORCE_BONUS, 0.0)))
    imp = jnp.where(blk * SEL_BLOCK <= tpos, imp + bonus, NEG_INF)
    lanes = 4
    partial = [jnp.zeros((n_sel, t), F32) for _ in range(lanes)]
    for i in range(n_sel):
        row = imp[i:i + 1, :]
        before = jnp.where(blk > i, 1.0, 0.0)
        partial[i % lanes] = partial[i % lanes] + jnp.where(row > imp, 1.0,
                                                            jnp.where(row == imp, before, 0.0))
    rank = (partial[0] + partial[1]) + (partial[2] + partial[3])
    selb = jnp.where(rank < float(SEL_TOP_N), 0.0, NEG_INF)
    selb = jnp.concatenate([selb, jnp.zeros((LANES - n_sel, t), F32)], axis=0).T
    return o_c, selb[:, :HEAD_DIM].astype(BF16)


def _overlap_table(s):
    n_sel = s // SEL_BLOCK
    nc = N_CHUNKS_PAD
    c0 = np.arange(nc)[:, None] * CMP_STRIDE
    s0 = np.arange(n_sel)[None, :] * SEL_BLOCK
    ov = np.clip(np.minimum(c0 + CMP_BLOCK, s0 + SEL_BLOCK) - np.maximum(c0, s0), 0, None) / CMP_STRIDE
    ov[nc - 1] = 0.0
    return jnp.asarray(np.pad(ov, ((0, 0), (0, LANES - n_sel))), BF16)


def _nsa(qn, kc, vc, ks, vs, kw, vw, gates, toep, bc):
    b, s, _ = qn.shape
    t = ATT_TILE
    assert s // SEL_BLOCK <= HEAD_DIM
    ov = _overlap_table(s)
    rows = NSA_HPG * t
    ng = NSA_KV_GROUPS
    nq = s // t
    assert nq >= 2
    seq = lambda n, w=HEAD_DIM: pl.BlockSpec((1, ng, n, w), lambda i, j: (i, 0, 0, 0))
    opened = lambda j: jnp.minimum(j, nq - 1)
    closed = lambda j: jnp.maximum(j - 2, 0)
    return pl.pallas_call(
        functools.partial(_nsa_body, n_tiles=nq),
        grid=(b, nq + 2),
        in_specs=[pl.BlockSpec((1, t, NSA_Q), lambda i, j: (i, opened(j), 0)),
                  seq(N_CHUNKS_PAD), seq(N_CHUNKS_PAD),
                  seq(s, 2 * HEAD_DIM), seq(s, 2 * HEAD_DIM), seq(s), seq(s, 2 * HEAD_DIM),
                  pl.BlockSpec((1, ng, t, GATE_PAD), lambda i, j: (i, 0, opened(j), 0)),
                  pl.BlockSpec((1, ng, t, GATE_PAD), lambda i, j: (i, 0, closed(j), 0)),
                  _resident((NSA_HEADS, t, 2 * t)),
                  pl.BlockSpec((NSA_HEADS, t, N_CHUNKS_PAD), lambda i, j: (0, opened(j), 0)),
                  _resident(ov.shape)],
        out_specs=pl.BlockSpec((1, t, NSA_Q), lambda i, j: (i, closed(j), 0)),
        out_shape=jax.ShapeDtypeStruct((b, s, NSA_Q), BF16),
        scratch_shapes=[pltpu.VMEM((ng, nq - 2, rows, t), F32),
                        pltpu.VMEM((2, ng, 2, rows, t), F32),
                        pltpu.VMEM((2, ng, rows, LANES), F32),
                        pltpu.VMEM((2, ng, rows, 2 * HEAD_DIM), F32),
                        pltpu.VMEM((2, t, NSA_Q), F32)],
        compiler_params=_params(("parallel", "arbitrary")),
        name="nsa",
    )(qn, kc, vc, ks, vs, kw, vw, gates, gates, toep, bc, ov)


def _fcum_body(f_ref, b_ref, o_ref):
    z = f_ref[0] + b_ref[...]
    x = -(jnp.maximum(-z, 0.0) + jnp.log1p(jnp.exp(-jnp.abs(z))))
    n = x.shape[-1]
    lane = lax.broadcasted_iota(jnp.int32, x.shape, 1)
    sh = 1
    while sh < n:
        x = x + jnp.where(lane >= sh, pltpu.roll(x, sh, axis=1), 0.0)
        sh *= 2
    o_ref[0] = x


def _fcum(f_t, b_forget):
    b, h, s = f_t.shape
    blk = pl.BlockSpec((1, h, s), lambda i: (i, 0, 0))
    return pl.pallas_call(
        _fcum_body,
        grid=(b,),
        in_specs=[blk, _resident((h, 1))],
        out_specs=blk,
        out_shape=jax.ShapeDtypeStruct((b, h, s), F32),
        compiler_params=_params(("parallel",)),
        name="fcum",
    )(f_t, b_forget.reshape(h, 1))


FOX_PAIR = 2
FOX_TILE = 512


def _causal_bias(t):
    i = np.arange(t)
    return jnp.asarray(np.where(i[None, :] <= i[:, None], 0.0, NEG_INF), F32)


def _fox_body(q_ref, k_ref, v_ref, c_ref, cm_ref, o_ref, s_ref, mb_ref, acc_ref):
    t = FOX_TILE
    heads = range(FOX_PAIR)
    nq = q_ref.shape[2] // t

    def keys(n):
        return pl.ds(pl.multiple_of(n * t, t), t)

    def scores(m, hh, n):
        q = q_ref[0, hh, m * t:(m + 1) * t, :]
        return _dot_nt(q, k_ref[0, hh, keys(n), :]) - c_ref[0, hh, :, keys(n)]

    def scores_diag(m):
        for hh in heads:
            s = scores(m, hh, m) + cm_ref[...]
            s_ref[m % 2, hh, m] = s
            mb_ref[m % 2, hh] = _lane_fold(s, jnp.maximum)

    def scores_far(m, n):
        for hh in heads:
            s = scores(m, hh, n)
            s_ref[m % 2, hh, n] = s
            mb_ref[m % 2, hh] = jnp.maximum(mb_ref[m % 2, hh], _lane_fold(s, jnp.maximum))

    def finish_max(m):
        for hh in heads:
            mb_ref[m % 2, hh] = _row_max_tile(mb_ref[m % 2, hh])

    def weigh(m, n):
        for hh in heads:
            p = jnp.exp(s_ref[m % 2, hh, n] - _lane_tile(mb_ref[m % 2, hh], t)).astype(BF16)
            acc_ref[hh] += _dot(p, v_ref[0, hh, keys(n), :])

    scores_diag(0)
    finish_max(0)
    for m in range(nq):
        ahead = m + 1 < nq
        if ahead:
            scores_diag(m + 1)
        for hh in heads:
            acc_ref[hh] = jnp.zeros((t, 2 * HEAD_DIM), F32)

        @pl.loop(0, m + 1)
        def _(n, m=m, ahead=ahead):
            weigh(m, n)
            if ahead:
                scores_far(m + 1, n)

        if ahead:
            finish_max(m + 1)
        o_ref[0, m * t:(m + 1) * t, :] = jnp.concatenate(
            [_normalize(acc_ref[hh]) for hh in heads], axis=1).astype(BF16)


def _fox(qf, kf, vf, c):
    b, h, s, _ = qf.shape
    t = FOX_TILE
    seq = lambda w: pl.BlockSpec((1, FOX_PAIR, s, w), lambda i, p: (i, p, 0, 0))
    return pl.pallas_call(
        _fox_body,
        grid=(b, h // FOX_PAIR),
        in_specs=[seq(HEAD_DIM), seq(HEAD_DIM), seq(2 * HEAD_DIM),
                  pl.BlockSpec((1, FOX_PAIR, 1, s), lambda i, p: (i, p, 0, 0)),
                  _resident((t, t))],
        out_specs=pl.BlockSpec((1, s, FOX_PAIR * HEAD_DIM), lambda i, p: (i, 0, p)),
        out_shape=jax.ShapeDtypeStruct((b, s, h * HEAD_DIM), BF16),
        scratch_shapes=[pltpu.VMEM((2, FOX_PAIR, s // t, t, t), F32),
                        pltpu.VMEM((2, FOX_PAIR, t, LANES), F32),
                        pltpu.VMEM((FOX_PAIR, t, 2 * HEAD_DIM), F32)],
        compiler_params=_params(("parallel", "parallel")),
        name="fox",
    )(qf, kf, vf, c.reshape(b, h, 1, s), _causal_bias(t))


def _mixout_body(x_ref, g_ref, on_ref, of_ref, wa_ref, wb_ref, wun_ref, wuf_ref, wo_ref, o_ref):
    x = x_ref[...]
    h = _rms(x, g_ref[...]).astype(BF16)
    y = (jax.nn.sigmoid(_dot(h, wa_ref[...])) * _dot(on_ref[...], wun_ref[...])
         + jax.nn.sigmoid(_dot(h, wb_ref[...])) * _dot(of_ref[...], wuf_ref[...]))
    o_ref[...] = x + _dot(y.astype(BF16), wo_ref[...])


def _mixout(x2d, g, o_nsa, o_fox, w_a, w_b, w_un, w_uf, w_o):
    n = x2d.shape[0]
    tok = lambda c: pl.BlockSpec((TOK_TILE, c), lambda i: (i, 0))
    return pl.pallas_call(
        _mixout_body,
        grid=(n // TOK_TILE,),
        in_specs=[tok(D_MODEL), _resident((1, D_MODEL)), tok(NSA_Q), tok(FOX_W),
                  _resident(w_a.shape), _resident(w_b.shape), _resident(w_un.shape),
                  _resident(w_uf.shape), _resident(w_o.shape)],
        out_specs=tok(D_MODEL),
        out_shape=jax.ShapeDtypeStruct((n, D_MODEL), F32),
        compiler_params=_params(("parallel",)),
        name="mixout",
    )(x2d, g.reshape(1, D_MODEL), o_nsa, o_fox, w_a, w_b, w_un, w_uf, w_o)


MEM_BATCH = 4


def _memkv_body(m_ref, g_ref, w_ref, k_ref, v_ref):
    nb, ml, d = m_ref.shape
    h = _rms(m_ref[...].reshape(nb * ml, d), g_ref[...]).astype(BF16)
    z = _dot(h, w_ref[...]).astype(BF16)
    k_ref[...] = z[:, :D_MODEL].reshape(nb, ml, d)
    v_ref[...] = z[:, D_MODEL:].reshape(nb, ml, d)


def _memkv(mem, g, w_kv):
    b, ml, _ = mem.shape
    nb = MEM_BATCH if b % MEM_BATCH == 0 else 1
    blk = pl.BlockSpec((nb, ml, D_MODEL), lambda i: (i, 0, 0))
    sd = jax.ShapeDtypeStruct((b, ml, D_MODEL), BF16)
    return pl.pallas_call(
        _memkv_body,
        grid=(b // nb,),
        in_specs=[blk, _resident((1, D_MODEL)), _resident(w_kv.shape)],
        out_specs=(blk, blk),
        out_shape=(sd, sd),
        compiler_params=_params(("parallel",)),
        name="memkv",
    )(mem, g.reshape(1, D_MODEL), w_kv)


def _memattn_body(x_ref, g_ref, k_ref, v_ref, wq_ref, wo_ref, o_ref):
    x = x_ref[0]
    h = _rms(x, g_ref[...]).astype(BF16)
    q = (_dot(h, wq_ref[...]) * (MEM_HEAD_DIM ** -0.5)).astype(BF16)
    outs = []
    for hd in range(MEM_HEADS):
        cols = slice(hd * MEM_HEAD_DIM, (hd + 1) * MEM_HEAD_DIM)
        s = _dot_nt(q[:, cols], k_ref[0, :, cols])
        e = jnp.exp(s - jnp.max(s, axis=-1, keepdims=True))
        p = e / jnp.sum(e, axis=-1, keepdims=True)
        outs.append(_dot(p.astype(BF16), v_ref[0, :, cols]))
    o = jnp.concatenate(outs, axis=1).astype(BF16)
    o_ref[0] = x + _dot(o, wo_ref[...])


def _memattn(x, g, k, v, w_q, w_o):
    b, s, _ = x.shape
    ml = k.shape[1]
    tok = pl.BlockSpec((1, TOK_TILE, D_MODEL), lambda i, j: (i, j, 0))
    kvb = pl.BlockSpec((1, ml, D_MODEL), lambda i, j: (i, 0, 0))
    return pl.pallas_call(
        _memattn_body,
        grid=(b, s // TOK_TILE),
        in_specs=[tok, _resident((1, D_MODEL)), kvb, kvb, _resident(w_q.shape), _resident(w_o.shape)],
        out_specs=tok,
        out_shape=jax.ShapeDtypeStruct((b, s, D_MODEL), F32),
        compiler_params=_params(("parallel", "parallel")),
        name="memattn",
    )(x, g.reshape(1, D_MODEL), k, v, w_q, w_o)


def kernel(x, mem, rel_bias_table, ffn1_norm, ffn1_w_gate, ffn1_w_up, ffn1_w_down, mix_norm, mix_w_in, mix_b_forget, cmp_pos_k, cmp_pos_v, cmp_k_w1, cmp_k_w2, cmp_v_w1, cmp_v_w2, w_up_nsa, w_up_fox, mix_w_out, mem_q_norm, mem_kv_norm, mem_w_q, mem_w_kv, mem_w_o, ffn2_norm, ffn2_w_gate, ffn2_w_up, ffn2_w_down, final_norm):
    b, s, d = x.shape
    depth = ffn1_norm.shape[0]
    bf = lambda w: w.astype(BF16)
    toep, bias_c = _relbias(rel_bias_table, s)
    x = x.reshape(b * s, d)
    for l in range(depth):
        last = l == depth - 1
        x = _ffn(x, ffn1_norm[l], ffn1_w_gate[l], ffn1_w_up[l], ffn1_w_down[l])

        w_nsa, w_fox, w_small, w_a, w_b = _pack_w_in(mix_w_in, l)
        qn, kc, vc, ks, vs, kw, vw, qf, kf, vf, gates, flog = _inproj(
            x.reshape(b, s, d), mix_norm[l], w_nsa, w_fox, w_small)
        kcc, vcc = _compress(kc, vc, cmp_pos_k[l], cmp_pos_v[l], cmp_k_w1[l], cmp_k_w2[l],
                             cmp_v_w1[l], cmp_v_w2[l])
        o_nsa = _nsa(qn, kcc, vcc, ks, vs, kw, vw, gates, toep, bias_c)
        c = _fcum(flog, mix_b_forget[l])
        o_fox = _fox(qf, kf, vf, c)
        x = _mixout(x, mix_norm[l], o_nsa.reshape(b * s, NSA_Q), o_fox.reshape(b * s, FOX_W),
                    w_a, w_b, bf(w_up_nsa[l]), bf(w_up_fox[l]), bf(mix_w_out[l]))

        mk, mv = _memkv(mem, mem_kv_norm[l], bf(mem_w_kv[l]))
        x = _memattn(x.reshape(b, s, d), mem_q_norm[l], mk, mv, bf(mem_w_q[l]), bf(mem_w_o[l]))
        x = _ffn(x.reshape(b * s, d), ffn2_norm[l], ffn2_w_gate[l], ffn2_w_up[l],
                 ffn2_w_down[l], final_g=final_norm if last else None)
    return x.reshape(b, s, d)
```

```python
import functools
import math

import numpy as np
import jax
import jax.numpy as jnp
from jax import lax
from jax.experimental import pallas as pl
from jax.experimental.pallas import tpu as pltpu

D_MODEL = 1024
D_FF = 2816
HEAD_DIM = 64
NSA_HEADS = 8
NSA_KV_GROUPS = 2
NSA_HPG = NSA_HEADS // NSA_KV_GROUPS
CMP_BLOCK = 32
CMP_STRIDE = 16
CMP_HIDDEN = 256
SEL_BLOCK = 64
SEL_TOP_N = 16
WINDOW = 512
FOX_HEADS = 8
MEM_HEADS = 4
MEM_HEAD_DIM = D_MODEL // MEM_HEADS
NUM_BUCKETS = 32
MAX_DISTANCE = 128
RMS_EPS = 1e-6
NEG_INF = -1e30
FORCE_BONUS = 1e4

NSA_Q = NSA_HEADS * HEAD_DIM
NSA_KV = NSA_KV_GROUPS * HEAD_DIM
FOX_W = FOX_HEADS * HEAD_DIM

LANES = 128
VMEM_LIMIT = 56 * 1024 * 1024

BF16 = jnp.bfloat16
F32 = jnp.float32

ATT_TILE = 256
TOK_TILE = 1024
FFN_TILE = 1024
FF_CHUNK = 256


def _bucket_thresholds():
    n = np.arange(0, 4 * MAX_DISTANCE)
    exact = NUM_BUCKETS // 2
    large = exact + (np.log(np.maximum(n, 1) / exact) / math.log(MAX_DISTANCE / exact)
                     * (NUM_BUCKETS - exact)).astype(np.int64)
    bucket = np.where(n < exact, n, np.minimum(large, NUM_BUCKETS - 1))
    assert np.all(np.diff(bucket) >= 0)
    return [int(np.argmax(bucket >= k)) for k in range(1, NUM_BUCKETS)]


BUCKET_THRESHOLDS = _bucket_thresholds()


def _dot(a, b):
    return jnp.dot(a, b, preferred_element_type=F32)


def _dot_nt(a, b):
    return lax.dot_general(a, b, (((1,), (1,)), ((), ())), preferred_element_type=F32)


def _row_halves(fn, a, b):
    h = a.shape[0] // 2
    return jnp.concatenate([fn(a[:h], b), fn(a[h:], b)], axis=0)


def _dot_pair(a, b):
    return _row_halves(_dot, a, b)


def _dot_nt_pair(a, b):
    return _row_halves(_dot_nt, a, b)


def _rms(x, g):
    return x * lax.rsqrt(jnp.mean(x * x, axis=-1, keepdims=True) + RMS_EPS) * g


def _resident(shape):
    nd = len(shape)
    return pl.BlockSpec(shape, lambda *_: (0,) * nd, pipeline_mode=pl.Buffered(1))


def _params(sem):
    return pltpu.CompilerParams(dimension_semantics=sem, vmem_limit_bytes=VMEM_LIMIT)


def _ffn_body(x_ref, g_ref, wg_hbm, wu_hbm, wd_hbm, *rest, final):
    if final:
        fg_ref, o_ref, wg_ref, wu_ref, wd_ref, sg_ref, su_ref, sd_ref, sem = rest
    else:
        o_ref, wg_ref, wu_ref, wd_ref, sg_ref, su_ref, sd_ref, sem = rest
    n_chunks = D_FF // FF_CHUNK

    def chunk_copies(c):
        cols = slice(c * FF_CHUNK, (c + 1) * FF_CHUNK)
        slot = c % 2
        return (pltpu.make_async_copy(wg_hbm.at[:, cols], sg_ref.at[slot], sem.at[0, slot]),
                pltpu.make_async_copy(wu_hbm.at[:, cols], su_ref.at[slot], sem.at[1, slot]),
                pltpu.make_async_copy(wd_hbm.at[cols, :], sd_ref.at[slot], sem.at[2, slot]))

    @pl.when(pl.program_id(0) == 0)
    def _():
        for cp in chunk_copies(0):
            cp.start()
        for c in range(n_chunks):
            sl = slice(c * FF_CHUNK, (c + 1) * FF_CHUNK)
            if c + 1 < n_chunks:
                for cp in chunk_copies(c + 1):
                    cp.start()
            for cp in chunk_copies(c):
                cp.wait()
            wg_ref[:, sl] = sg_ref[c % 2].astype(BF16)
            wu_ref[:, sl] = su_ref[c % 2].astype(BF16)
            wd_ref[sl, :] = sd_ref[c % 2].astype(BF16)

    x = x_ref[...]
    h = _rms(x, g_ref[...]).astype(BF16)
    acc = jnp.zeros(x.shape, F32)
    for c in range(n_chunks):
        sl = slice(c * FF_CHUNK, (c + 1) * FF_CHUNK)
        a = _dot(h, wg_ref[:, sl])
        b = _dot(h, wu_ref[:, sl])
        t = (a * jax.nn.sigmoid(a)) * b
        acc = acc + _dot(t.astype(BF16), wd_ref[sl, :])
    y = x + 0.5 * acc
    if final:
        y = _rms(y, fg_ref[...])
    o_ref[...] = y


def _ffn(x2d, g, wg, wu, wd, final_g=None):
    n = x2d.shape[0]
    final = final_g is not None
    tok = pl.BlockSpec((FFN_TILE, D_MODEL), lambda i: (i, 0))
    hbm = pl.BlockSpec(memory_space=pl.ANY)
    in_specs = [tok, _resident((1, D_MODEL)), hbm, hbm, hbm]
    args = [x2d, g.reshape(1, D_MODEL), wg, wu, wd]
    if final:
        in_specs.append(_resident((1, D_MODEL)))
        args.append(final_g.reshape(1, D_MODEL))
    return pl.pallas_call(
        functools.partial(_ffn_body, final=final),
        grid=(n // FFN_TILE,),
        in_specs=in_specs,
        out_specs=tok,
        out_shape=jax.ShapeDtypeStruct((n, D_MODEL), F32),
        scratch_shapes=[pltpu.VMEM((D_MODEL, D_FF), BF16), pltpu.VMEM((D_MODEL, D_FF), BF16),
                        pltpu.VMEM((D_FF, D_MODEL), BF16),
                        pltpu.VMEM((2, D_MODEL, FF_CHUNK), F32), pltpu.VMEM((2, D_MODEL, FF_CHUNK), F32),
                        pltpu.VMEM((2, FF_CHUNK, D_MODEL), F32),
                        pltpu.SemaphoreType.DMA((3, 2))],
        compiler_params=_params(("arbitrary",)),
        name="ffn_final" if final else "ffn",
    )(*args)


_C_CMP = NSA_Q
_C_KV = _C_CMP + 2 * NSA_KV
_W_NSA = _C_KV + 4 * NSA_KV
GATE_PAD = 16
_C_FLOG = NSA_KV_GROUPS * GATE_PAD
N_CHUNKS_PAD = 128
CHUNK_W = CMP_STRIDE * NSA_KV


def _inproj_body(x_ref, g_ref, wn_ref, wf_ref, ws_ref, blk_ref, qn_ref, kc_ref, vc_ref, ks_ref, vs_ref,
                 kw_ref, vw_ref, qf_ref, kf_ref, vf_ref, gate_ref, fl_ref, cmp_ref):
    h = _rms(x_ref[0], g_ref[...]).astype(BF16)
    z = _dot(h, wn_ref[...])
    qn_ref[0] = (z[:, :_C_CMP] * (HEAD_DIM ** -0.5)).astype(BF16)
    rows = z.shape[0] // CMP_STRIDE
    for i, ref in enumerate((kc_ref, vc_ref)):
        cmp_ref[i] = z[:, _C_CMP + i * NSA_KV:_C_CMP + (i + 1) * NSA_KV]
        for r in range(CMP_STRIDE):
            tok = cmp_ref[i, pl.ds(r, rows, stride=CMP_STRIDE), :]
            ref[0, :, r * NSA_KV:(r + 1) * NSA_KV] = tok.astype(BF16)
    ones = jnp.ones((z.shape[0], HEAD_DIM), BF16)
    for i, (ref, extra) in enumerate(((ks_ref, blk_ref[...]), (vs_ref, ones), (kw_ref, None),
                                      (vw_ref, ones))):
        for g in range(NSA_KV_GROUPS):
            c0 = _C_KV + i * NSA_KV + g * HEAD_DIM
            val = z[:, c0:c0 + HEAD_DIM].astype(BF16)
            ref[0, g] = val if extra is None else jnp.concatenate([val, extra], axis=1)
    zf = _dot(h, wf_ref[...])
    for hd in range(FOX_HEADS):
        c0 = hd * HEAD_DIM
        qf_ref[0, hd] = (zf[:, c0:c0 + HEAD_DIM] * (HEAD_DIM ** -0.5)).astype(BF16)
        kf_ref[0, hd] = zf[:, c0 + FOX_W:c0 + FOX_W + HEAD_DIM].astype(BF16)
        v = zf[:, c0 + 2 * FOX_W:c0 + 2 * FOX_W + HEAD_DIM].astype(BF16)
        vf_ref[0, hd] = jnp.concatenate([v, ones], axis=1)
    zs = _dot(h, ws_ref[...])
    for g in range(NSA_KV_GROUPS):
        gate_ref[0, g] = zs[:, g * GATE_PAD:(g + 1) * GATE_PAD]
    fl_ref[0] = zs.T[_C_FLOG:_C_FLOG + FOX_HEADS]


_IN_COLS = np.cumsum((0, NSA_Q, NSA_KV, NSA_KV, NSA_KV, NSA_KV, NSA_KV, NSA_KV, 3 * NSA_HEADS,
                      FOX_W, FOX_W, FOX_W, FOX_HEADS, D_MODEL, D_MODEL)).tolist()
W_PREP_ROWS = 256


def _wprep_body(w_ref, wn_ref, wf_ref, ws_ref, wa_ref, wb_ref):
    wt = w_ref[0]
    c = _IN_COLS
    k = wt.shape[1]
    wn_ref[...] = wt[:c[7]].T.astype(BF16)
    wf_ref[...] = wt[c[8]:c[11]].T.astype(BF16)
    wa_ref[...] = wt[c[12]:c[13]].T.astype(BF16)
    wb_ref[...] = wt[c[13]:c[14]].T.astype(BF16)
    small = []
    for g in range(NSA_KV_GROUPS):
        for br in range(3):
            c0 = c[7] + br * NSA_HEADS + g * NSA_HPG
            small.append(wt[c0:c0 + NSA_HPG])
        small.append(jnp.zeros((GATE_PAD - 3 * NSA_HPG, k), F32))
    small.append(wt[c[11]:c[12]])
    small.append(jnp.zeros((LANES - _C_FLOG - FOX_HEADS, k), F32))
    ws_ref[...] = jnp.concatenate(small, axis=0).T.astype(BF16)


def _pack_w_in(w_in, layer):
    _, k, n = w_in.shape
    c = _IN_COLS
    assert n == c[14]
    widths = (c[7], c[11] - c[8], LANES, D_MODEL, D_MODEL)
    return pl.pallas_call(
        _wprep_body,
        grid=(k // W_PREP_ROWS,),
        in_specs=[pl.BlockSpec((1, n, W_PREP_ROWS), lambda i: (layer, 0, i))],
        out_specs=tuple(pl.BlockSpec((W_PREP_ROWS, w), lambda i: (i, 0)) for w in widths),
        out_shape=tuple(jax.ShapeDtypeStruct((k, w), BF16) for w in widths),
        compiler_params=_params(("parallel",)),
        name="wprep",
    )(jnp.swapaxes(w_in, 1, 2))


def _inproj(x, g, w_nsa, w_fox, w_small):
    b, s, _ = x.shape
    grid = (b, s // TOK_TILE)
    tok = lambda c: pl.BlockSpec((1, TOK_TILE, c), lambda i, j: (i, j, 0))
    heads = lambda nh, w=HEAD_DIM: pl.BlockSpec((1, nh, TOK_TILE, w), lambda i, j: (i, 0, j, 0))
    chunks = pl.BlockSpec((1, TOK_TILE // CMP_STRIDE, CHUNK_W), lambda i, j: (i, j, 0))
    sds = jax.ShapeDtypeStruct
    kv = sds((b, NSA_KV_GROUPS, s, HEAD_DIM), BF16)
    kv2 = sds((b, NSA_KV_GROUPS, s, 2 * HEAD_DIM), BF16)
    fx = sds((b, FOX_HEADS, s, HEAD_DIM), BF16)
    cmp_in = sds((b, s // CMP_STRIDE, CHUNK_W), BF16)
    out_shape = (sds((b, s, NSA_Q), BF16), cmp_in, cmp_in,
                 kv2, kv2, kv, kv2, fx, fx, sds((b, FOX_HEADS, s, 2 * HEAD_DIM), BF16),
                 sds((b, NSA_KV_GROUPS, s, GATE_PAD), F32), sds((b, FOX_HEADS, s), F32))
    out_specs = (tok(NSA_Q), chunks, chunks,
                 heads(NSA_KV_GROUPS, 2 * HEAD_DIM), heads(NSA_KV_GROUPS, 2 * HEAD_DIM),
                 heads(NSA_KV_GROUPS), heads(NSA_KV_GROUPS, 2 * HEAD_DIM),
                 heads(FOX_HEADS), heads(FOX_HEADS), heads(FOX_HEADS, 2 * HEAD_DIM),
                 pl.BlockSpec((1, NSA_KV_GROUPS, TOK_TILE, GATE_PAD), lambda i, j: (i, 0, j, 0)),
                 pl.BlockSpec((1, FOX_HEADS, TOK_TILE), lambda i, j: (i, 0, j)))
    key_blk = np.arange(s) // SEL_BLOCK
    blk_onehot = jnp.asarray(key_blk[:, None] == np.arange(HEAD_DIM)[None, :], BF16)
    return pl.pallas_call(
        _inproj_body,
        grid=grid,
        in_specs=[tok(D_MODEL), _resident((1, D_MODEL)), _resident(w_nsa.shape),
                  _resident(w_fox.shape), _resident(w_small.shape),
                  pl.BlockSpec((TOK_TILE, HEAD_DIM), lambda i, j: (j, 0))],
        out_specs=out_specs,
        out_shape=out_shape,
        scratch_shapes=[pltpu.VMEM((2, TOK_TILE, NSA_KV), F32)],
        compiler_params=_params(("parallel", "parallel")),
        name="inproj",
    )(x, g.reshape(1, D_MODEL), w_nsa, w_fox, w_small, blk_onehot)


CMP_BATCH = 4

def _compress_body(xk_ref, xv_ref, pk_ref, pv_ref, wk1_ref, wv1_ref, wk2_ref, wv2_ref, ok_ref, ov_ref):
    nb = xk_ref.shape[0]
    n = nb * N_CHUNKS_PAD
    for x_ref, p_ref, w1_ref, w2_ref, o_ref in ((xk_ref, pk_ref, wk1_ref, wk2_ref, ok_ref),
                                                (xv_ref, pv_ref, wv1_ref, wv2_ref, ov_ref)):
        x = x_ref[...].reshape(n, CHUNK_W)
        a0 = _dot(x, w1_ref[0])
        a1 = _dot(x, w1_ref[1])
        c = _dot(p_ref[0], w1_ref[0]) + _dot(p_ref[1], w1_ref[1])
        pre = a0 + pltpu.roll(a1, n - 1, axis=0) + c[0:1, :]
        hid = jax.nn.gelu(pre).astype(BF16)
        out = _dot(hid, w2_ref[...]).astype(BF16)
        for i in range(nb):
            for g in range(NSA_KV_GROUPS):
                o_ref[i, g] = out[i * N_CHUNKS_PAD:(i + 1) * N_CHUNKS_PAD, g * HEAD_DIM:(g + 1) * HEAD_DIM]


def _pack_compress(pos, w1, w2):
    r = CMP_BLOCK // CMP_STRIDE
    assert NSA_KV_GROUPS == 2
    w1r = w1.astype(BF16).reshape(r, CMP_STRIDE, HEAD_DIM, CMP_HIDDEN)
    z1 = jnp.zeros_like(w1r)
    w1big = jnp.stack([jnp.concatenate([w1r, z1], axis=-1), jnp.concatenate([z1, w1r], axis=-1)], axis=2)
    w1big = w1big.reshape(r, CHUNK_W, NSA_KV_GROUPS * CMP_HIDDEN)
    z2 = jnp.zeros_like(w2)
    w2big = jnp.concatenate([jnp.concatenate([w2, z2], axis=1), jnp.concatenate([z2, w2], axis=1)], axis=0)
    p = pos.reshape(r, CMP_STRIDE, 1, HEAD_DIM)
    p = jnp.broadcast_to(p, (r, CMP_STRIDE, NSA_KV_GROUPS, HEAD_DIM)).reshape(r, 1, CHUNK_W)
    p = jnp.broadcast_to(p, (r, 16, CHUNK_W))
    return p.astype(BF16), w1big.astype(BF16), w2big.astype(BF16)


def _compress(xk, xv, pos_k, pos_v, k_w1, k_w2, v_w1, v_w2):
    b = xk.shape[0]
    assert xk.shape[1:] == (N_CHUNKS_PAD, CHUNK_W)
    pk, wk1, wk2 = _pack_compress(pos_k, k_w1, k_w2)
    pv, wv1, wv2 = _pack_compress(pos_v, v_w1, v_w2)
    nb = CMP_BATCH if b % CMP_BATCH == 0 else 1
    xs = pl.BlockSpec((nb, N_CHUNKS_PAD, CHUNK_W), lambda i: (i, 0, 0))
    os_ = pl.BlockSpec((nb, NSA_KV_GROUPS, N_CHUNKS_PAD, HEAD_DIM), lambda i: (i, 0, 0, 0))
    osd = jax.ShapeDtypeStruct((b, NSA_KV_GROUPS, N_CHUNKS_PAD, HEAD_DIM), BF16)
    return pl.pallas_call(
        _compress_body,
        grid=(b // nb,),
        in_specs=[xs, xs, _resident(pk.shape), _resident(pv.shape), _resident(wk1.shape),
                  _resident(wv1.shape), _resident(wk2.shape), _resident(wv2.shape)],
        out_specs=(os_, os_),
        out_shape=(osd, osd),
        compiler_params=_params(("parallel",)),
        name="compress",
    )(xk, xv, pk, pv, wk1, wv1, wk2, wv2)


def _bias_lookup(tbl_ref, head, dist):
    acc = jnp.full(dist.shape, tbl_ref[0, head], F32)
    for k, thr in enumerate(BUCKET_THRESHOLDS):
        acc = jnp.where(dist >= thr, tbl_ref[k + 1, head], acc)
    return acc - tbl_ref[NUM_BUCKETS - 1, head]


def _relbias_body(tbl_ref, toep_ref, bc_ref):
    head = pl.program_id(0)
    t = ATT_TILE
    i = lax.broadcasted_iota(jnp.int32, (t, 2 * t), 0)
    u = lax.broadcasted_iota(jnp.int32, (t, 2 * t), 1)
    d = i - u + t
    toep_ref[0] = jnp.where(d >= 0, _bias_lookup(tbl_ref, head, d), NEG_INF)
    n_tiles = bc_ref.shape[1] // t
    per_tile = t // CMP_STRIDE
    off = per_tile * (n_tiles - 1)
    width = 2 * N_CHUNKS_PAD
    assert off + N_CHUNKS_PAD <= width
    i = lax.broadcasted_iota(jnp.int32, (t, width), 0)
    u = lax.broadcasted_iota(jnp.int32, (t, width), 1)
    dc = i - ((u - off) * CMP_STRIDE + CMP_BLOCK - 1)
    strip = jnp.where(dc >= 0, _bias_lookup(tbl_ref, head, dc), NEG_INF)
    for q in range(n_tiles):
        lo = off - q * per_tile
        bc_ref[0, q * t:(q + 1) * t, :] = strip[:, lo:lo + N_CHUNKS_PAD]


def _relbias(tbl, s):
    return pl.pallas_call(
        _relbias_body,
        grid=(NSA_HEADS,),
        in_specs=[pl.BlockSpec(memory_space=pltpu.SMEM)],
        out_specs=(pl.BlockSpec((1, ATT_TILE, 2 * ATT_TILE), lambda h: (h, 0, 0)),
                   pl.BlockSpec((1, s, N_CHUNKS_PAD), lambda h: (h, 0, 0))),
        out_shape=(jax.ShapeDtypeStruct((NSA_HEADS, ATT_TILE, 2 * ATT_TILE), F32),
                   jax.ShapeDtypeStruct((NSA_HEADS, s, N_CHUNKS_PAD), F32)),
        compiler_params=_params(("parallel",)),
        name="relbias",
    )(tbl)


def _lane_fold(x, op):
    return functools.reduce(op, [x[:, i * LANES:(i + 1) * LANES] for i in range(x.shape[1] // LANES)])


def _lane_tile(x, width):
    return jnp.concatenate([x] * (width // LANES), axis=1)


def _row_max_tile(rmax):
    return jnp.broadcast_to(jnp.max(rmax, axis=-1, keepdims=True), rmax.shape)


def _normalize(acc):
    return (acc / pltpu.roll(acc, HEAD_DIM, axis=1))[:, :HEAD_DIM]


def _split3(x):
    hi = x.astype(BF16)
    r = x - hi.astype(F32)
    mid = r.astype(BF16)
    lo = (r - mid.astype(F32)).astype(BF16)
    return hi, mid, lo


def _nsa_body(q_ref, kc_ref, vc_ref, ks_ref, vs_ref, kw_ref, vw_ref, gate_a_ref, gate_b_ref, toep_ref,
              bc_ref, ov_ref, o_ref, s_ref, near_ref, mb_ref, acc_ref, part_ref, *, n_tiles):
    t = ATT_TILE
    j4 = NSA_HPG
    rows = j4 * t
    groups = range(NSA_KV_GROUPS)
    step = pl.program_id(1)
    n_sel = n_tiles * (t // SEL_BLOCK)

    def heads(x):
        return x.reshape(j4, t, x.shape[-1])

    def flat(x):
        return x.reshape(rows, x.shape[-1])

    def key_tile(ref, g, n):
        if isinstance(n, int):
            return ref[0, g, n * t:(n + 1) * t, :]
        return ref[0, g, pl.ds(pl.multiple_of(n * t, t), t), :]

    def gated(gate_ref, g, branch, o):
        gate = jax.nn.sigmoid(gate_ref[0, g])
        return [gate[:, branch * j4 + j:branch * j4 + j + 1] * o[j * t:(j + 1) * t] for j in range(j4)]

    def probs(par, g, s):
        return jnp.exp(s - _lane_tile(mb_ref[par, g], t)).astype(BF16)

    def open_tile(m, par):
        has_prev = m >= 1 if isinstance(m, int) else True
        has_far = m >= 2 if isinstance(m, int) else True
        qs = []
        for g in groups:
            qt = q_ref[0, :, g * j4 * HEAD_DIM:(g + 1) * j4 * HEAD_DIM]
            qs.append(jnp.concatenate([qt[:, j * HEAD_DIM:(j + 1) * HEAD_DIM] for j in range(j4)], axis=0))

        def biases(g):
            toep = toep_ref[g * j4:(g + 1) * j4]
            return toep[:, :, t:], toep[:, :, :t]

        cmp_out = [_nsa_compressed(qs[g], kc_ref[0, g], vc_ref[0, g], bc_ref[g * j4:(g + 1) * j4],
                                   ov_ref[...], m, n_sel) for g in groups]
        part = []
        for g in groups:
            diag_bias, prev_bias = biases(g)
            tiles = [(m, flat(heads(_dot_nt_pair(qs[g], key_tile(kw_ref, g, m))) + diag_bias))]
            if has_prev:
                tiles.append((m - 1, flat(heads(_dot_nt_pair(qs[g], key_tile(kw_ref, g, m - 1))) + prev_bias)))
            if has_far:
                ii = lax.broadcasted_iota(jnp.int32, (t, t), 0)
                jj = lax.broadcasted_iota(jnp.int32, (t, t), 1)
                tri = jnp.where(jj > ii, 0.0, NEG_INF)
                tiles.append((m - 2, flat(heads(_dot_nt_pair(qs[g], key_tile(kw_ref, g, m - 2))) + tri[None])))
            wmax = _lane_tile(_row_max_tile(functools.reduce(
                jnp.maximum, [_lane_fold(w, jnp.maximum) for _, w in tiles])), t)
            acc = sum(_dot_pair(jnp.exp(w - wmax).astype(BF16), key_tile(vw_ref, g, n)) for n, w in tiles)
            part += [a + b for a, b in zip(gated(gate_a_ref, g, 0, cmp_out[g][0]),
                                           gated(gate_a_ref, g, 2, _normalize(acc)))]
        part_ref[par] = jnp.concatenate(part, axis=1)

        q_aug = []
        for g in groups:
            diag_bias, prev_bias = biases(g)
            q_aug.append(jnp.concatenate([qs[g], jnp.concatenate([cmp_out[g][1]] * j4, axis=0)], axis=1))
            s0 = flat(heads(_dot_nt_pair(q_aug[g], key_tile(ks_ref, g, m))) + diag_bias)
            near_ref[par, g, 0] = s0
            rmax = _lane_fold(s0, jnp.maximum)
            if has_prev:
                s1 = flat(heads(_dot_nt_pair(q_aug[g], key_tile(ks_ref, g, m - 1))) + prev_bias)
                near_ref[par, g, 1] = s1
                rmax = jnp.maximum(rmax, _lane_fold(s1, jnp.maximum))
            if has_far:
                s2 = _dot_nt_pair(q_aug[g], key_tile(ks_ref, g, m - 2))
                s_ref[g, m - 2] = s2
                rmax = jnp.maximum(rmax, _lane_fold(s2, jnp.maximum))
            mb_ref[par, g] = rmax
        return q_aug

    def open_far(par, q_aug, n):
        for g in groups:
            s = _dot_nt_pair(q_aug[g], key_tile(ks_ref, g, n))
            s_ref[g, n] = s
            mb_ref[par, g] = jnp.maximum(mb_ref[par, g], _lane_fold(s, jnp.maximum))

    def close_tile(m, par):
        for g in groups:
            mb_ref[par, g] = _row_max_tile(mb_ref[par, g])
            acc = _dot_pair(probs(par, g, near_ref[par, g, 0]), key_tile(vs_ref, g, m))
            if not (isinstance(m, int) and m == 0):
                acc = acc + _dot_pair(probs(par, g, near_ref[par, g, 1]), key_tile(vs_ref, g, m - 1))
            acc_ref[par, g] = acc

    def close_far(par, n):
        for g in groups:
            acc_ref[par, g] += _dot_pair(probs(par, g, s_ref[g, n]), key_tile(vs_ref, g, n))

    def finish(par):
        outs = []
        for g in groups:
            outs += gated(gate_b_ref, g, 1, _normalize(acc_ref[par, g]))
        o_ref[0] = (part_ref[par] + jnp.concatenate(outs, axis=1)).astype(BF16)

    @pl.when(step == 0)
    def _():
        open_tile(0, 0)

    @pl.when(step == 1)
    def _():
        open_tile(1, 1)
        close_tile(0, 0)

    @pl.when(jnp.logical_and(step >= 2, step < n_tiles))
    def _():
        par = step % 2
        finish(par)
        q_aug = open_tile(step, par)
        close_tile(step - 1, 1 - par)

        @pl.loop(0, step - 2)
        def _(n):
            close_far(1 - par, n)
            open_far(par, q_aug, n)

    last = (n_tiles - 1) % 2

    @pl.when(step == n_tiles)
    def _():
        finish(1 - last)
        close_tile(step - 1, last)

        @pl.loop(0, step - 2)
        def _(n):
            close_far(last, n)

    @pl.when(step == n_tiles + 1)
    def _():
        finish(last)


def _nsa_compressed(q, kc, vc, bc, ov, m, n_sel):
    t = ATT_TILE
    j4 = NSA_HPG

    def heads(x):
        return x.reshape(j4, t, x.shape[-1])

    def flat(x):
        return x.reshape(j4 * t, x.shape[-1])

    sc = heads(_dot_nt_pair(q, kc)) + bc
    valid = bc > 0.5 * NEG_INF
    e = jnp.where(valid, jnp.exp(sc - jnp.max(sc, axis=-1, keepdims=True)), 0.0)
    l = jnp.sum(e, axis=-1, keepdims=True)
    pc = jnp.where(l > 0.0, e / l, 0.0)
    o_c = _dot_pair(flat(pc).astype(BF16), vc)
    if isinstance(m, int) and (m + 1) * t <= SEL_TOP_N * SEL_BLOCK:
        return o_c, jnp.zeros((t, HEAD_DIM), BF16)
    psum = pc[0] + pc[1] + pc[2] + pc[3]
    imp = sum(_dot(part, ov) for part in _split3(psum)).T[:n_sel]
    blk = lax.broadcasted_iota(jnp.int32, (n_sel, t), 0)
    tpos = m * t + lax.broadcasted_iota(jnp.int32, (n_sel, t), 1)
    cur = tpos // SEL_BLOCK
    bonus = jnp.where(blk == 0, FORCE_BONUS,
                      jnp.where(blk == cur, FORCE_BONUS, jnp.where(blk == cur - 1, FORCE_BONUS, 0.0)))
    imp = jnp.where(blk * SEL_BLOCK <= tpos, imp + bonus, NEG_INF)
    lanes = 4
    partial = [jnp.zeros((n_sel, t), F32) for _ in range(lanes)]
    for i in range(n_sel):
        row = imp[i:i + 1, :]
        before = jnp.where(blk > i, 1.0, 0.0)
        partial[i % lanes] = partial[i % lanes] + jnp.where(row > imp, 1.0,
                                                            jnp.where(row == imp, before, 0.0))
    rank = (partial[0] + partial[1]) + (partial[2] + partial[3])
    selb = jnp.where(rank < float(SEL_TOP_N), 0.0, NEG_INF)
    selb = jnp.concatenate([selb, jnp.zeros((LANES - n_sel, t), F32)], axis=0).T
    return o_c, selb[:, :HEAD_DIM].astype(BF16)


def _overlap_table(s):
    n_sel = s // SEL_BLOCK
    nc = N_CHUNKS_PAD
    c0 = np.arange(nc)[:, None] * CMP_STRIDE
    s0 = np.arange(n_sel)[None, :] * SEL_BLOCK
    ov = np.clip(np.minimum(c0 + CMP_BLOCK, s0 + SEL_BLOCK) - np.maximum(c0, s0), 0, None) / CMP_STRIDE
    ov[nc - 1] = 0.0
    return jnp.asarray(np.pad(ov, ((0, 0), (0, LANES - n_sel))), BF16)


def _nsa(qn, kc, vc, ks, vs, kw, vw, gates, toep, bc):
    b, s, _ = qn.shape
    t = ATT_TILE
    assert s // SEL_BLOCK <= HEAD_DIM
    ov = _overlap_table(s)
    rows = NSA_HPG * t
    ng = NSA_KV_GROUPS
    nq = s // t
    assert nq >= 2
    seq = lambda n, w=HEAD_DIM: pl.BlockSpec((1, ng, n, w), lambda i, j: (i, 0, 0, 0))
    opened = lambda j: jnp.minimum(j, nq - 1)
    closed = lambda j: jnp.maximum(j - 2, 0)
    return pl.pallas_call(
        functools.partial(_nsa_body, n_tiles=nq),
        grid=(b, nq + 2),
        in_specs=[pl.BlockSpec((1, t, NSA_Q), lambda i, j: (i, opened(j), 0)),
                  seq(N_CHUNKS_PAD), seq(N_CHUNKS_PAD),
                  seq(s, 2 * HEAD_DIM), seq(s, 2 * HEAD_DIM), seq(s), seq(s, 2 * HEAD_DIM),
                  pl.BlockSpec((1, ng, t, GATE_PAD), lambda i, j: (i, 0, opened(j), 0)),
                  pl.BlockSpec((1, ng, t, GATE_PAD), lambda i, j: (i, 0, closed(j), 0)),
                  _resident((NSA_HEADS, t, 2 * t)),
                  pl.BlockSpec((NSA_HEADS, t, N_CHUNKS_PAD), lambda i, j: (0, opened(j), 0)),
                  _resident(ov.shape)],
        out_specs=pl.BlockSpec((1, t, NSA_Q), lambda i, j: (i, closed(j), 0)),
        out_shape=jax.ShapeDtypeStruct((b, s, NSA_Q), BF16),
        scratch_shapes=[pltpu.VMEM((ng, nq - 2, rows, t), F32),
                        pltpu.VMEM((2, ng, 2, rows, t), F32),
                        pltpu.VMEM((2, ng, rows, LANES), F32),
                        pltpu.VMEM((2, ng, rows, 2 * HEAD_DIM), F32),
                        pltpu.VMEM((2, t, NSA_Q), F32)],
        compiler_params=_params(("parallel", "arbitrary")),
        name="nsa",
    )(qn, kc, vc, ks, vs, kw, vw, gates, gates, toep, bc, ov)


def _fcum_body(f_ref, b_ref, o_ref):
    z = f_ref[0] + b_ref[...]
    x = -(jnp.maximum(-z, 0.0) + jnp.log1p(jnp.exp(-jnp.abs(z))))
    n = x.shape[-1]
    lane = lax.broadcasted_iota(jnp.int32, x.shape, 1)
    sh = 1
    while sh < n:
        x = x + jnp.where(lane >= sh, pltpu.roll(x, sh, axis=1), 0.0)
        sh *= 2
    o_ref[0] = x


def _fcum(f_t, b_forget):
    b, h, s = f_t.shape
    blk = pl.BlockSpec((1, h, s), lambda i: (i, 0, 0))
    return pl.pallas_call(
        _fcum_body,
        grid=(b,),
        in_specs=[blk, _resident((h, 1))],
        out_specs=blk,
        out_shape=jax.ShapeDtypeStruct((b, h, s), F32),
        compiler_params=_params(("parallel",)),
        name="fcum",
    )(f_t, b_forget.reshape(h, 1))


FOX_PAIR = 2
FOX_TILE = 512


def _causal_bias(t):
    i = np.arange(t)
    return jnp.asarray(np.where(i[None, :] <= i[:, None], 0.0, NEG_INF), F32)


def _fox_body(q_ref, k_ref, v_ref, c_ref, cm_ref, o_ref, s_ref, mb_ref, acc_ref):
    t = FOX_TILE
    heads = range(FOX_PAIR)
    nq = q_ref.shape[2] // t

    def keys(n):
        return slice(n * t, (n + 1) * t)

    def scores(m, hh, n):
        q = q_ref[0, hh, m * t:(m + 1) * t, :]
        return _dot_nt(q, k_ref[0, hh, keys(n), :]) - c_ref[0, hh, :, keys(n)]

    def scores_diag(m):
        for hh in heads:
            s = scores(m, hh, m) + cm_ref[...]
            s_ref[m % 2, hh, m] = s
            mb_ref[m % 2, hh] = _lane_fold(s, jnp.maximum)

    def scores_far(m, n):
        for hh in heads:
            s = scores(m, hh, n)
            s_ref[m % 2, hh, n] = s
            mb_ref[m % 2, hh] = jnp.maximum(mb_ref[m % 2, hh], _lane_fold(s, jnp.maximum))

    def finish_max(m):
        for hh in heads:
            mb_ref[m % 2, hh] = _row_max_tile(mb_ref[m % 2, hh])

    def weigh(m, n):
        for hh in heads:
            p = jnp.exp(s_ref[m % 2, hh, n] - _lane_tile(mb_ref[m % 2, hh], t)).astype(BF16)
            acc_ref[hh] += _dot(p, v_ref[0, hh, keys(n), :])

    scores_diag(0)
    finish_max(0)
    for m in range(nq):
        ahead = m + 1 < nq
        if ahead:
            scores_diag(m + 1)
        for hh in heads:
            acc_ref[hh] = jnp.zeros((t, 2 * HEAD_DIM), F32)

        for n in range(m + 1):
            weigh(m, n)
            if ahead:
                scores_far(m + 1, n)

        if ahead:
            finish_max(m + 1)
        o_ref[0, m * t:(m + 1) * t, :] = jnp.concatenate(
            [_normalize(acc_ref[hh]) for hh in heads], axis=1).astype(BF16)


def _fox(qf, kf, vf, c):
    b, h, s, _ = qf.shape
    t = FOX_TILE
    seq = lambda w: pl.BlockSpec((1, FOX_PAIR, s, w), lambda i, p: (i, p, 0, 0))
    return pl.pallas_call(
        _fox_body,
        grid=(b, h // FOX_PAIR),
        in_specs=[seq(HEAD_DIM), seq(HEAD_DIM), seq(2 * HEAD_DIM),
                  pl.BlockSpec((1, FOX_PAIR, 1, s), lambda i, p: (i, p, 0, 0)),
                  _resident((t, t))],
        out_specs=pl.BlockSpec((1, s, FOX_PAIR * HEAD_DIM), lambda i, p: (i, 0, p)),
        out_shape=jax.ShapeDtypeStruct((b, s, h * HEAD_DIM), BF16),
        scratch_shapes=[pltpu.VMEM((2, FOX_PAIR, s // t, t, t), F32),
                        pltpu.VMEM((2, FOX_PAIR, t, LANES), F32),
                        pltpu.VMEM((FOX_PAIR, t, 2 * HEAD_DIM), F32)],
        compiler_params=_params(("parallel", "parallel")),
        name="fox",
    )(qf, kf, vf, c.reshape(b, h, 1, s), _causal_bias(t))


def _mixout_body(x_ref, g_ref, on_ref, of_ref, wa_ref, wb_ref, wun_ref, wuf_ref, wo_ref, o_ref):
    x = x_ref[...]
    h = _rms(x, g_ref[...]).astype(BF16)
    y = (jax.nn.sigmoid(_dot(h, wa_ref[...])) * _dot(on_ref[...], wun_ref[...])
         + jax.nn.sigmoid(_dot(h, wb_ref[...])) * _dot(of_ref[...], wuf_ref[...]))
    o_ref[...] = x + _dot(y.astype(BF16), wo_ref[...])


def _mixout(x2d, g, o_nsa, o_fox, w_a, w_b, w_un, w_uf, w_o):
    n = x2d.shape[0]
    tok = lambda c: pl.BlockSpec((TOK_TILE, c), lambda i: (i, 0))
    return pl.pallas_call(
        _mixout_body,
        grid=(n // TOK_TILE,),
        in_specs=[tok(D_MODEL), _resident((1, D_MODEL)), tok(NSA_Q), tok(FOX_W),
                  _resident(w_a.shape), _resident(w_b.shape), _resident(w_un.shape),
                  _resident(w_uf.shape), _resident(w_o.shape)],
        out_specs=tok(D_MODEL),
        out_shape=jax.ShapeDtypeStruct((n, D_MODEL), F32),
        compiler_params=_params(("parallel",)),
        name="mixout",
    )(x2d, g.reshape(1, D_MODEL), o_nsa, o_fox, w_a, w_b, w_un, w_uf, w_o)


MEM_BATCH = 4


def _memkv_body(m_ref, g_ref, w_ref, k_ref, v_ref):
    nb, ml, d = m_ref.shape
    h = _rms(m_ref[...].reshape(nb * ml, d), g_ref[...]).astype(BF16)
    z = _dot(h, w_ref[...]).astype(BF16)
    k_ref[...] = z[:, :D_MODEL].reshape(nb, ml, d)
    v_ref[...] = z[:, D_MODEL:].reshape(nb, ml, d)


def _memkv(mem, g, w_kv):
    b, ml, _ = mem.shape
    nb = MEM_BATCH if b % MEM_BATCH == 0 else 1
    blk = pl.BlockSpec((nb, ml, D_MODEL), lambda i: (i, 0, 0))
    sd = jax.ShapeDtypeStruct((b, ml, D_MODEL), BF16)
    return pl.pallas_call(
        _memkv_body,
        grid=(b // nb,),
        in_specs=[blk, _resident((1, D_MODEL)), _resident(w_kv.shape)],
        out_specs=(blk, blk),
        out_shape=(sd, sd),
        compiler_params=_params(("parallel",)),
        name="memkv",
    )(mem, g.reshape(1, D_MODEL), w_kv)


def _memattn_body(x_ref, g_ref, k_ref, v_ref, wq_ref, wo_ref, o_ref):
    x = x_ref[0]
    h = _rms(x, g_ref[...]).astype(BF16)
    q = (_dot(h, wq_ref[...]) * (MEM_HEAD_DIM ** -0.5)).astype(BF16)
    outs = []
    for hd in range(MEM_HEADS):
        cols = slice(hd * MEM_HEAD_DIM, (hd + 1) * MEM_HEAD_DIM)
        s = _dot_nt(q[:, cols], k_ref[0, :, cols])
        e = jnp.exp(s - jnp.max(s, axis=-1, keepdims=True))
        p = e / jnp.sum(e, axis=-1, keepdims=True)
        outs.append(_dot(p.astype(BF16), v_ref[0, :, cols]))
    o = jnp.concatenate(outs, axis=1).astype(BF16)
    o_ref[0] = x + _dot(o, wo_ref[...])


def _memattn(x, g, k, v, w_q, w_o):
    b, s, _ = x.shape
    ml = k.shape[1]
    tok = pl.BlockSpec((1, TOK_TILE, D_MODEL), lambda i, j: (i, j, 0))
    kvb = pl.BlockSpec((1, ml, D_MODEL), lambda i, j: (i, 0, 0))
    return pl.pallas_call(
        _memattn_body,
        grid=(b, s // TOK_TILE),
        in_specs=[tok, _resident((1, D_MODEL)), kvb, kvb, _resident(w_q.shape), _resident(w_o.shape)],
        out_specs=tok,
        out_shape=jax.ShapeDtypeStruct((b, s, D_MODEL), F32),
        compiler_params=_params(("parallel", "parallel")),
        name="memattn",
    )(x, g.reshape(1, D_MODEL), k, v, w_q, w_o)


def kernel(x, mem, rel_bias_table, ffn1_norm, ffn1_w_gate, ffn1_w_up, ffn1_w_down, mix_norm, mix_w_in, mix_b_forget, cmp_pos_k, cmp_pos_v, cmp_k_w1, cmp_k_w2, cmp_v_w1, cmp_v_w2, w_up_nsa, w_up_fox, mix_w_out, mem_q_norm, mem_kv_norm, mem_w_q, mem_w_kv, mem_w_o, ffn2_norm, ffn2_w_gate, ffn2_w_up, ffn2_w_down, final_norm):
    b, s, d = x.shape
    depth = ffn1_norm.shape[0]
    bf = lambda w: w.astype(BF16)
    toep, bias_c = _relbias(rel_bias_table, s)
    x = x.reshape(b * s, d)
    for l in range(depth):
        last = l == depth - 1
        x = _ffn(x, ffn1_norm[l], ffn1_w_gate[l], ffn1_w_up[l], ffn1_w_down[l])

        w_nsa, w_fox, w_small, w_a, w_b = _pack_w_in(mix_w_in, l)
        qn, kc, vc, ks, vs, kw, vw, qf, kf, vf, gates, flog = _inproj(
            x.reshape(b, s, d), mix_norm[l], w_nsa, w_fox, w_small)
        kcc, vcc = _compress(kc, vc, cmp_pos_k[l], cmp_pos_v[l], cmp_k_w1[l], cmp_k_w2[l],
                             cmp_v_w1[l], cmp_v_w2[l])
        o_nsa = _nsa(qn, kcc, vcc, ks, vs, kw, vw, gates, toep, bias_c)
        c = _fcum(flog, mix_b_forget[l])
        o_fox = _fox(qf, kf, vf, c)
        x = _mixout(x, mix_norm[l], o_nsa.reshape(b * s, NSA_Q), o_fox.reshape(b * s, FOX_W),
                    w_a, w_b, bf(w_up_nsa[l]), bf(w_up_fox[l]), bf(mix_w_out[l]))

        mk, mv = _memkv(mem, mem_kv_norm[l], bf(mem_w_kv[l]))
        x = _memattn(x.reshape(b, s, d), mem_q_norm[l], mk, mv, bf(mem_w_q[l]), bf(mem_w_o[l]))
        x = _ffn(x.reshape(b * s, d), ffn2_norm[l], ffn2_w_gate[l], ffn2_w_up[l],
                 ffn2_w_down[l], final_g=final_norm if last else None)
    return x.reshape(b, s, d)
```

```python
import functools
import math

import numpy as np
import jax
import jax.numpy as jnp
from jax import lax
from jax.experimental import pallas as pl
from jax.experimental.pallas import tpu as pltpu

D_MODEL = 1024
D_FF = 2816
HEAD_DIM = 64
NSA_HEADS = 8
NSA_KV_GROUPS = 2
NSA_HPG = NSA_HEADS // NSA_KV_GROUPS
CMP_BLOCK = 32
CMP_STRIDE = 16
CMP_HIDDEN = 256
SEL_BLOCK = 64
SEL_TOP_N = 16
WINDOW = 512
FOX_HEADS = 8
MEM_HEADS = 4
MEM_HEAD_DIM = D_MODEL // MEM_HEADS
NUM_BUCKETS = 32
MAX_DISTANCE = 128
RMS_EPS = 1e-6
NEG_INF = -1e30
FORCE_BONUS = 1e4

NSA_Q = NSA_HEADS * HEAD_DIM
NSA_KV = NSA_KV_GROUPS * HEAD_DIM
FOX_W = FOX_HEADS * HEAD_DIM

LANES = 128
VMEM_LIMIT = 56 * 1024 * 1024

BF16 = jnp.bfloat16
F32 = jnp.float32

ATT_TILE = 256
TOK_TILE = 1024
FFN_TILE = 1024
FF_CHUNK = 256


def _bucket_thresholds():
    n = np.arange(0, 4 * MAX_DISTANCE)
    exact = NUM_BUCKETS // 2
    large = exact + (np.log(np.maximum(n, 1) / exact) / math.log(MAX_DISTANCE / exact)
                     * (NUM_BUCKETS - exact)).astype(np.int64)
    bucket = np.where(n < exact, n, np.minimum(large, NUM_BUCKETS - 1))
    assert np.all(np.diff(bucket) >= 0)
    return [int(np.argmax(bucket >= k)) for k in range(1, NUM_BUCKETS)]


BUCKET_THRESHOLDS = _bucket_thresholds()


def _dot(a, b):
    return jnp.dot(a, b, preferred_element_type=F32)


def _dot_nt(a, b):
    return lax.dot_general(a, b, (((1,), (1,)), ((), ())), preferred_element_type=F32)


def _row_halves(fn, a, b):
    h = a.shape[0] // 2
    return jnp.concatenate([fn(a[:h], b), fn(a[h:], b)], axis=0)


def _dot_pair(a, b):
    return _row_halves(_dot, a, b)


def _dot_nt_pair(a, b):
    return _row_halves(_dot_nt, a, b)


def _rms(x, g):
    return x * lax.rsqrt(jnp.mean(x * x, axis=-1, keepdims=True) + RMS_EPS) * g


def _resident(shape):
    nd = len(shape)
    return pl.BlockSpec(shape, lambda *_: (0,) * nd, pipeline_mode=pl.Buffered(1))


def _params(sem):
    return pltpu.CompilerParams(dimension_semantics=sem, vmem_limit_bytes=VMEM_LIMIT)


def _ffn_body(x_ref, g_ref, wg_hbm, wu_hbm, wd_hbm, *rest, final):
    if final:
        fg_ref, o_ref, wg_ref, wu_ref, wd_ref, sg_ref, su_ref, sd_ref, sem = rest
    else:
        o_ref, wg_ref, wu_ref, wd_ref, sg_ref, su_ref, sd_ref, sem = rest
    n_chunks = D_FF // FF_CHUNK

    def chunk_copies(c):
        cols = slice(c * FF_CHUNK, (c + 1) * FF_CHUNK)
        slot = c % 2
        return (pltpu.make_async_copy(wg_hbm.at[:, cols], sg_ref.at[slot], sem.at[0, slot]),
                pltpu.make_async_copy(wu_hbm.at[:, cols], su_ref.at[slot], sem.at[1, slot]),
                pltpu.make_async_copy(wd_hbm.at[cols, :], sd_ref.at[slot], sem.at[2, slot]))

    @pl.when(pl.program_id(0) == 0)
    def _():
        for cp in chunk_copies(0):
            cp.start()
        for c in range(n_chunks):
            sl = slice(c * FF_CHUNK, (c + 1) * FF_CHUNK)
            if c + 1 < n_chunks:
                for cp in chunk_copies(c + 1):
                    cp.start()
            for cp in chunk_copies(c):
                cp.wait()
            wg_ref[:, sl] = sg_ref[c % 2].astype(BF16)
            wu_ref[:, sl] = su_ref[c % 2].astype(BF16)
            wd_ref[sl, :] = sd_ref[c % 2].astype(BF16)

    x = x_ref[...]
    h = _rms(x, g_ref[...]).astype(BF16)
    acc = jnp.zeros(x.shape, F32)
    for c in range(n_chunks):
        sl = slice(c * FF_CHUNK, (c + 1) * FF_CHUNK)
        a = _dot(h, wg_ref[:, sl])
        b = _dot(h, wu_ref[:, sl])
        t = (a * jax.nn.sigmoid(a)) * b
        acc = acc + _dot(t.astype(BF16), wd_ref[sl, :])
    y = x + 0.5 * acc
    if final:
        y = _rms(y, fg_ref[...])
    o_ref[...] = y


def _ffn(x2d, g, wg, wu, wd, final_g=None):
    n = x2d.shape[0]
    final = final_g is not None
    tok = pl.BlockSpec((FFN_TILE, D_MODEL), lambda i: (i, 0))
    hbm = pl.BlockSpec(memory_space=pl.ANY)
    in_specs = [tok, _resident((1, D_MODEL)), hbm, hbm, hbm]
    args = [x2d, g.reshape(1, D_MODEL), wg, wu, wd]
    if final:
        in_specs.append(_resident((1, D_MODEL)))
        args.append(final_g.reshape(1, D_MODEL))
    return pl.pallas_call(
        functools.partial(_ffn_body, final=final),
        grid=(n // FFN_TILE,),
        in_specs=in_specs,
        out_specs=tok,
        out_shape=jax.ShapeDtypeStruct((n, D_MODEL), F32),
        scratch_shapes=[pltpu.VMEM((D_MODEL, D_FF), BF16), pltpu.VMEM((D_MODEL, D_FF), BF16),
                        pltpu.VMEM((D_FF, D_MODEL), BF16),
                        pltpu.VMEM((2, D_MODEL, FF_CHUNK), F32), pltpu.VMEM((2, D_MODEL, FF_CHUNK), F32),
                        pltpu.VMEM((2, FF_CHUNK, D_MODEL), F32),
                        pltpu.SemaphoreType.DMA((3, 2))],
        compiler_params=_params(("arbitrary",)),
        name="ffn_final" if final else "ffn",
    )(*args)


_C_CMP = NSA_Q
_C_KV = _C_CMP + 2 * NSA_KV
_W_NSA = _C_KV + 4 * NSA_KV
GATE_PAD = 16
_C_FLOG = NSA_KV_GROUPS * GATE_PAD
N_CHUNKS_PAD = 128
CHUNK_W = CMP_STRIDE * NSA_KV


def _inproj_body(x_ref, g_ref, wn_ref, wf_ref, ws_ref, blk_ref, qn_ref, kc_ref, vc_ref, ks_ref, vs_ref,
                 kw_ref, vw_ref, qf_ref, kf_ref, vf_ref, gate_ref, fl_ref, cmp_ref):
    h = _rms(x_ref[0], g_ref[...]).astype(BF16)
    z = _dot(h, wn_ref[...])
    qn_ref[0] = (z[:, :_C_CMP] * (HEAD_DIM ** -0.5)).astype(BF16)
    rows = z.shape[0] // CMP_STRIDE
    for i, ref in enumerate((kc_ref, vc_ref)):
        cmp_ref[i] = z[:, _C_CMP + i * NSA_KV:_C_CMP + (i + 1) * NSA_KV]
        for r in range(CMP_STRIDE):
            tok = cmp_ref[i, pl.ds(r, rows, stride=CMP_STRIDE), :]
            ref[0, :, r * NSA_KV:(r + 1) * NSA_KV] = tok.astype(BF16)
    ones = jnp.ones((z.shape[0], HEAD_DIM), BF16)
    for i, (ref, extra) in enumerate(((ks_ref, blk_ref[...]), (vs_ref, ones), (kw_ref, None),
                                      (vw_ref, ones))):
        for g in range(NSA_KV_GROUPS):
            c0 = _C_KV + i * NSA_KV + g * HEAD_DIM
            val = z[:, c0:c0 + HEAD_DIM].astype(BF16)
            ref[0, g] = val if extra is None else jnp.concatenate([val, extra], axis=1)
    zf = _dot(h, wf_ref[...])
    for hd in range(FOX_HEADS):
        c0 = hd * HEAD_DIM
        qf_ref[0, hd] = (zf[:, c0:c0 + HEAD_DIM] * (HEAD_DIM ** -0.5)).astype(BF16)
        kf_ref[0, hd] = zf[:, c0 + FOX_W:c0 + FOX_W + HEAD_DIM].astype(BF16)
        v = zf[:, c0 + 2 * FOX_W:c0 + 2 * FOX_W + HEAD_DIM].astype(BF16)
        vf_ref[0, hd] = jnp.concatenate([v, ones], axis=1)
    zs = _dot(h, ws_ref[...])
    for g in range(NSA_KV_GROUPS):
        gate_ref[0, g] = zs[:, g * GATE_PAD:(g + 1) * GATE_PAD]
    fl_ref[0] = zs.T[_C_FLOG:_C_FLOG + FOX_HEADS]


_IN_COLS = np.cumsum((0, NSA_Q, NSA_KV, NSA_KV, NSA_KV, NSA_KV, NSA_KV, NSA_KV, 3 * NSA_HEADS,
                      FOX_W, FOX_W, FOX_W, FOX_HEADS, D_MODEL, D_MODEL)).tolist()
W_PREP_ROWS = 256


def _wprep_body(w_ref, wn_ref, wf_ref, ws_ref, wa_ref, wb_ref):
    wt = w_ref[0]
    c = _IN_COLS
    k = wt.shape[1]
    wn_ref[...] = wt[:c[7]].T.astype(BF16)
    wf_ref[...] = wt[c[8]:c[11]].T.astype(BF16)
    wa_ref[...] = wt[c[12]:c[13]].T.astype(BF16)
    wb_ref[...] = wt[c[13]:c[14]].T.astype(BF16)
    small = []
    for g in range(NSA_KV_GROUPS):
        for br in range(3):
            c0 = c[7] + br * NSA_HEADS + g * NSA_HPG
            small.append(wt[c0:c0 + NSA_HPG])
        small.append(jnp.zeros((GATE_PAD - 3 * NSA_HPG, k), F32))
    small.append(wt[c[11]:c[12]])
    small.append(jnp.zeros((LANES - _C_FLOG - FOX_HEADS, k), F32))
    ws_ref[...] = jnp.concatenate(small, axis=0).T.astype(BF16)


def _pack_w_in(w_in, layer):
    _, k, n = w_in.shape
    c = _IN_COLS
    assert n == c[14]
    widths = (c[7], c[11] - c[8], LANES, D_MODEL, D_MODEL)
    return pl.pallas_call(
        _wprep_body,
        grid=(k // W_PREP_ROWS,),
        in_specs=[pl.BlockSpec((1, n, W_PREP_ROWS), lambda i: (layer, 0, i))],
        out_specs=tuple(pl.BlockSpec((W_PREP_ROWS, w), lambda i: (i, 0)) for w in widths),
        out_shape=tuple(jax.ShapeDtypeStruct((k, w), BF16) for w in widths),
        compiler_params=_params(("parallel",)),
        name="wprep",
    )(jnp.swapaxes(w_in, 1, 2))


def _inproj(x, g, w_nsa, w_fox, w_small):
    b, s, _ = x.shape
    grid = (b, s // TOK_TILE)
    tok = lambda c: pl.BlockSpec((1, TOK_TILE, c), lambda i, j: (i, j, 0))
    heads = lambda nh, w=HEAD_DIM: pl.BlockSpec((1, nh, TOK_TILE, w), lambda i, j: (i, 0, j, 0))
    chunks = pl.BlockSpec((1, TOK_TILE // CMP_STRIDE, CHUNK_W), lambda i, j: (i, j, 0))
    sds = jax.ShapeDtypeStruct
    kv = sds((b, NSA_KV_GROUPS, s, HEAD_DIM), BF16)
    kv2 = sds((b, NSA_KV_GROUPS, s, 2 * HEAD_DIM), BF16)
    fx = sds((b, FOX_HEADS, s, HEAD_DIM), BF16)
    cmp_in = sds((b, s // CMP_STRIDE, CHUNK_W), BF16)
    out_shape = (sds((b, s, NSA_Q), BF16), cmp_in, cmp_in,
                 kv2, kv2, kv, kv2, fx, fx, sds((b, FOX_HEADS, s, 2 * HEAD_DIM), BF16),
                 sds((b, NSA_KV_GROUPS, s, GATE_PAD), F32), sds((b, FOX_HEADS, s), F32))
    out_specs = (tok(NSA_Q), chunks, chunks,
                 heads(NSA_KV_GROUPS, 2 * HEAD_DIM), heads(NSA_KV_GROUPS, 2 * HEAD_DIM),
                 heads(NSA_KV_GROUPS), heads(NSA_KV_GROUPS, 2 * HEAD_DIM),
                 heads(FOX_HEADS), heads(FOX_HEADS), heads(FOX_HEADS, 2 * HEAD_DIM),
                 pl.BlockSpec((1, NSA_KV_GROUPS, TOK_TILE, GATE_PAD), lambda i, j: (i, 0, j, 0)),
                 pl.BlockSpec((1, FOX_HEADS, TOK_TILE), lambda i, j: (i, 0, j)))
    key_blk = np.arange(s) // SEL_BLOCK
    blk_onehot = jnp.asarray(key_blk[:, None] == np.arange(HEAD_DIM)[None, :], BF16)
    return pl.pallas_call(
        _inproj_body,
        grid=grid,
        in_specs=[tok(D_MODEL), _resident((1, D_MODEL)), _resident(w_nsa.shape),
                  _resident(w_fox.shape), _resident(w_small.shape),
                  pl.BlockSpec((TOK_TILE, HEAD_DIM), lambda i, j: (j, 0))],
        out_specs=out_specs,
        out_shape=out_shape,
        scratch_shapes=[pltpu.VMEM((2, TOK_TILE, NSA_KV), F32)],
        compiler_params=_params(("parallel", "parallel")),
        name="inproj",
    )(x, g.reshape(1, D_MODEL), w_nsa, w_fox, w_small, blk_onehot)


CMP_BATCH = 4

def _compress_body(xk_ref, xv_ref, pk_ref, pv_ref, wk1_ref, wv1_ref, wk2_ref, wv2_ref, ok_ref, ov_ref):
    nb = xk_ref.shape[0]
    n = nb * N_CHUNKS_PAD
    for x_ref, p_ref, w1_ref, w2_ref, o_ref in ((xk_ref, pk_ref, wk1_ref, wk2_ref, ok_ref),
                                                (xv_ref, pv_ref, wv1_ref, wv2_ref, ov_ref)):
        x = x_ref[...].reshape(n, CHUNK_W)
        a0 = _dot(x, w1_ref[0])
        a1 = _dot(x, w1_ref[1])
        c = _dot(p_ref[0], w1_ref[0]) + _dot(p_ref[1], w1_ref[1])
        pre = a0 + pltpu.roll(a1, n - 1, axis=0) + c[0:1, :]
        hid = jax.nn.gelu(pre).astype(BF16)
        out = _dot(hid, w2_ref[...]).astype(BF16)
        for i in range(nb):
            for g in range(NSA_KV_GROUPS):
                o_ref[i, g] = out[i * N_CHUNKS_PAD:(i + 1) * N_CHUNKS_PAD, g * HEAD_DIM:(g + 1) * HEAD_DIM]


def _pack_compress(pos, w1, w2):
    r = CMP_BLOCK // CMP_STRIDE
    assert NSA_KV_GROUPS == 2
    w1r = w1.astype(BF16).reshape(r, CMP_STRIDE, HEAD_DIM, CMP_HIDDEN)
    z1 = jnp.zeros_like(w1r)
    w1big = jnp.stack([jnp.concatenate([w1r, z1], axis=-1), jnp.concatenate([z1, w1r], axis=-1)], axis=2)
    w1big = w1big.reshape(r, CHUNK_W, NSA_KV_GROUPS * CMP_HIDDEN)
    z2 = jnp.zeros_like(w2)
    w2big = jnp.concatenate([jnp.concatenate([w2, z2], axis=1), jnp.concatenate([z2, w2], axis=1)], axis=0)
    p = pos.reshape(r, CMP_STRIDE, 1, HEAD_DIM)
    p = jnp.broadcast_to(p, (r, CMP_STRIDE, NSA_KV_GROUPS, HEAD_DIM)).reshape(r, 1, CHUNK_W)
    p = jnp.broadcast_to(p, (r, 16, CHUNK_W))
    return p.astype(BF16), w1big.astype(BF16), w2big.astype(BF16)


def _compress(xk, xv, pos_k, pos_v, k_w1, k_w2, v_w1, v_w2):
    b = xk.shape[0]
    assert xk.shape[1:] == (N_CHUNKS_PAD, CHUNK_W)
    pk, wk1, wk2 = _pack_compress(pos_k, k_w1, k_w2)
    pv, wv1, wv2 = _pack_compress(pos_v, v_w1, v_w2)
    nb = CMP_BATCH if b % CMP_BATCH == 0 else 1
    xs = pl.BlockSpec((nb, N_CHUNKS_PAD, CHUNK_W), lambda i: (i, 0, 0))
    os_ = pl.BlockSpec((nb, NSA_KV_GROUPS, N_CHUNKS_PAD, HEAD_DIM), lambda i: (i, 0, 0, 0))
    osd = jax.ShapeDtypeStruct((b, NSA_KV_GROUPS, N_CHUNKS_PAD, HEAD_DIM), BF16)
    return pl.pallas_call(
        _compress_body,
        grid=(b // nb,),
        in_specs=[xs, xs, _resident(pk.shape), _resident(pv.shape), _resident(wk1.shape),
                  _resident(wv1.shape), _resident(wk2.shape), _resident(wv2.shape)],
        out_specs=(os_, os_),
        out_shape=(osd, osd),
        compiler_params=_params(("parallel",)),
        name="compress",
    )(xk, xv, pk, pv, wk1, wv1, wk2, wv2)


def _bias_lookup(tbl_ref, head, dist):
    acc = jnp.full(dist.shape, tbl_ref[0, head], F32)
    for k, thr in enumerate(BUCKET_THRESHOLDS):
        acc = jnp.where(dist >= thr, tbl_ref[k + 1, head], acc)
    return acc - tbl_ref[NUM_BUCKETS - 1, head]


def _relbias_body(tbl_ref, toep_ref, bc_ref):
    head = pl.program_id(0)
    t = ATT_TILE
    i = lax.broadcasted_iota(jnp.int32, (t, 2 * t), 0)
    u = lax.broadcasted_iota(jnp.int32, (t, 2 * t), 1)
    d = i - u + t
    toep_ref[0] = jnp.where(d >= 0, _bias_lookup(tbl_ref, head, d), NEG_INF)
    n_tiles = bc_ref.shape[1] // t
    per_tile = t // CMP_STRIDE
    off = per_tile * (n_tiles - 1)
    width = 2 * N_CHUNKS_PAD
    assert off + N_CHUNKS_PAD <= width
    i = lax.broadcasted_iota(jnp.int32, (t, width), 0)
    u = lax.broadcasted_iota(jnp.int32, (t, width), 1)
    dc = i - ((u - off) * CMP_STRIDE + CMP_BLOCK - 1)
    strip = jnp.where(dc >= 0, _bias_lookup(tbl_ref, head, dc), NEG_INF)
    for q in range(n_tiles):
        lo = off - q * per_tile
        bc_ref[0, q * t:(q + 1) * t, :] = strip[:, lo:lo + N_CHUNKS_PAD]


def _relbias(tbl, s):
    return pl.pallas_call(
        _relbias_body,
        grid=(NSA_HEADS,),
        in_specs=[pl.BlockSpec(memory_space=pltpu.SMEM)],
        out_specs=(pl.BlockSpec((1, ATT_TILE, 2 * ATT_TILE), lambda h: (h, 0, 0)),
                   pl.BlockSpec((1, s, N_CHUNKS_PAD), lambda h: (h, 0, 0))),
        out_shape=(jax.ShapeDtypeStruct((NSA_HEADS, ATT_TILE, 2 * ATT_TILE), F32),
                   jax.ShapeDtypeStruct((NSA_HEADS, s, N_CHUNKS_PAD), F32)),
        compiler_params=_params(("parallel",)),
        name="relbias",
    )(tbl)


def _lane_fold(x, op):
    return functools.reduce(op, [x[:, i * LANES:(i + 1) * LANES] for i in range(x.shape[1] // LANES)])


def _lane_tile(x, width):
    return jnp.concatenate([x] * (width // LANES), axis=1)


def _row_max_tile(rmax):
    return jnp.broadcast_to(jnp.max(rmax, axis=-1, keepdims=True), rmax.shape)


def _normalize(acc):
    return (acc / pltpu.roll(acc, HEAD_DIM, axis=1))[:, :HEAD_DIM]


def _split3(x):
    hi = x.astype(BF16)
    r = x - hi.astype(F32)
    mid = r.astype(BF16)
    lo = (r - mid.astype(F32)).astype(BF16)
    return hi, mid, lo


def _nsa_body(q_ref, kc_ref, vc_ref, ks_ref, vs_ref, kw_ref, vw_ref, gate_a_ref, gate_b_ref, toep_ref,
              bc_ref, ov_ref, o_ref, s_ref, near_ref, mb_ref, acc_ref, part_ref, *, n_tiles):
    t = ATT_TILE
    j4 = NSA_HPG
    rows = j4 * t
    groups = range(NSA_KV_GROUPS)
    step = pl.program_id(1)
    n_sel = n_tiles * (t // SEL_BLOCK)

    def heads(x):
        return x.reshape(j4, t, x.shape[-1])

    def flat(x):
        return x.reshape(rows, x.shape[-1])

    def key_tile(ref, g, n):
        if isinstance(n, int):
            return ref[0, g, n * t:(n + 1) * t, :]
        return ref[0, g, pl.ds(pl.multiple_of(n * t, t), t), :]

    def gated(gate_ref, g, branch, o):
        gate = jax.nn.sigmoid(gate_ref[0, g])
        return [gate[:, branch * j4 + j:branch * j4 + j + 1] * o[j * t:(j + 1) * t] for j in range(j4)]

    def probs(par, g, s):
        return jnp.exp(s - _lane_tile(mb_ref[par, g], t)).astype(BF16)

    def open_tile(m, par):
        has_prev = m >= 1 if isinstance(m, int) else True
        has_far = m >= 2 if isinstance(m, int) else True
        qs = []
        for g in groups:
            qt = q_ref[0, :, g * j4 * HEAD_DIM:(g + 1) * j4 * HEAD_DIM]
            qs.append(jnp.concatenate([qt[:, j * HEAD_DIM:(j + 1) * HEAD_DIM] for j in range(j4)], axis=0))

        def biases(g):
            toep = toep_ref[g * j4:(g + 1) * j4]
            return toep[:, :, t:], toep[:, :, :t]

        cmp_out = [_nsa_compressed(qs[g], kc_ref[0, g], vc_ref[0, g], bc_ref[g * j4:(g + 1) * j4],
                                   ov_ref[...], m, n_sel) for g in groups]
        part = []
        for g in groups:
            diag_bias, prev_bias = biases(g)
            tiles = [(m, flat(heads(_dot_nt_pair(qs[g], key_tile(kw_ref, g, m))) + diag_bias))]
            if has_prev:
                tiles.append((m - 1, flat(heads(_dot_nt_pair(qs[g], key_tile(kw_ref, g, m - 1))) + prev_bias)))
            if has_far:
                ii = lax.broadcasted_iota(jnp.int32, (t, t), 0)
                jj = lax.broadcasted_iota(jnp.int32, (t, t), 1)
                tri = jnp.where(jj > ii, 0.0, NEG_INF)
                tiles.append((m - 2, flat(heads(_dot_nt_pair(qs[g], key_tile(kw_ref, g, m - 2))) + tri[None])))
            wmax = _lane_tile(_row_max_tile(functools.reduce(
                jnp.maximum, [_lane_fold(w, jnp.maximum) for _, w in tiles])), t)
            acc = sum(_dot_pair(jnp.exp(w - wmax).astype(BF16), key_tile(vw_ref, g, n)) for n, w in tiles)
            part += [a + b for a, b in zip(gated(gate_a_ref, g, 0, cmp_out[g][0]),
                                           gated(gate_a_ref, g, 2, _normalize(acc)))]
        part_ref[par] = jnp.concatenate(part, axis=1)

        q_aug = []
        for g in groups:
            diag_bias, prev_bias = biases(g)
            q_aug.append(jnp.concatenate([qs[g], jnp.concatenate([cmp_out[g][1]] * j4, axis=0)], axis=1))
            s0 = flat(heads(_dot_nt_pair(q_aug[g], key_tile(ks_ref, g, m))) + diag_bias)
            near_ref[par, g, 0] = s0
            rmax = _lane_fold(s0, jnp.maximum)
            if has_prev:
                s1 = flat(heads(_dot_nt_pair(q_aug[g], key_tile(ks_ref, g, m - 1))) + prev_bias)
                near_ref[par, g, 1] = s1
                rmax = jnp.maximum(rmax, _lane_fold(s1, jnp.maximum))
            if has_far:
                s2 = _dot_nt_pair(q_aug[g], key_tile(ks_ref, g, m - 2))
                s_ref[g, m - 2] = s2
                rmax = jnp.maximum(rmax, _lane_fold(s2, jnp.maximum))
            mb_ref[par, g] = rmax
        return q_aug

    def open_far(par, q_aug, n):
        for g in groups:
            s = _dot_nt_pair(q_aug[g], key_tile(ks_ref, g, n))
            s_ref[g, n] = s
            mb_ref[par, g] = jnp.maximum(mb_ref[par, g], _lane_fold(s, jnp.maximum))

    def close_tile(m, par):
        for g in groups:
            mb_ref[par, g] = _row_max_tile(mb_ref[par, g])
            acc = _dot_pair(probs(par, g, near_ref[par, g, 0]), key_tile(vs_ref, g, m))
            if not (isinstance(m, int) and m == 0):
                acc = acc + _dot_pair(probs(par, g, near_ref[par, g, 1]), key_tile(vs_ref, g, m - 1))
            acc_ref[par, g] = acc

    def close_far(par, n):
        for g in groups:
            acc_ref[par, g] += _dot_pair(probs(par, g, s_ref[g, n]), key_tile(vs_ref, g, n))

    def finish(par):
        outs = []
        for g in groups:
            outs += gated(gate_b_ref, g, 1, _normalize(acc_ref[par, g]))
        o_ref[0] = (part_ref[par] + jnp.concatenate(outs, axis=1)).astype(BF16)

    @pl.when(step == 0)
    def _():
        open_tile(0, 0)

    @pl.when(step == 1)
    def _():
        open_tile(1, 1)
        close_tile(0, 0)

    @pl.when(jnp.logical_and(step >= 2, step < n_tiles))
    def _():
        par = step % 2
        finish(par)
        q_aug = open_tile(step, par)
        close_tile(step - 1, 1 - par)

        @pl.loop(0, step - 2)
        def _(n):
            close_far(1 - par, n)
            open_far(par, q_aug, n)

    last = (n_tiles - 1) % 2

    @pl.when(step == n_tiles)
    def _():
        finish(1 - last)
        close_tile(step - 1, last)

        @pl.loop(0, step - 2)
        def _(n):
            close_far(last, n)

    @pl.when(step == n_tiles + 1)
    def _():
        finish(last)


def _nsa_compressed(q, kc, vc, bc, ov, m, n_sel):
    t = ATT_TILE
    j4 = NSA_HPG

    def heads(x):
        return x.reshape(j4, t, x.shape[-1])

    def flat(x):
        return x.reshape(j4 * t, x.shape[-1])

    sc = heads(_dot_nt_pair(q, kc)) + bc
    valid = bc > 0.5 * NEG_INF
    e = jnp.where(valid, jnp.exp(sc - jnp.max(sc, axis=-1, keepdims=True)), 0.0)
    l = jnp.sum(e, axis=-1, keepdims=True)
    pc = jnp.where(l > 0.0, e / l, 0.0)
    o_c = _dot_pair(flat(pc).astype(BF16), vc)
    if isinstance(m, int) and (m + 1) * t <= SEL_TOP_N * SEL_BLOCK:
        return o_c, jnp.zeros((t, HEAD_DIM), BF16)
    psum = pc[0] + pc[1] + pc[2] + pc[3]
    imp = sum(_dot(part, ov) for part in _split3(psum)).T[:n_sel]
    blk = lax.broadcasted_iota(jnp.int32, (n_sel, t), 0)
    tpos = m * t + lax.broadcasted_iota(jnp.int32, (n_sel, t), 1)
    cur = tpos // SEL_BLOCK
    bonus = jnp.where(blk == 0, FORCE_BONUS,
                      jnp.where(blk == cur, FORCE_BONUS, jnp.where(blk == cur - 1, FORCE_BONUS, 0.0)))
    imp = jnp.where(blk * SEL_BLOCK <= tpos, imp + bonus, NEG_INF)
    lanes = 4
    partial = [jnp.zeros((n_sel, t), F32) for _ in range(lanes)]
    for i in range(n_sel):
        row = imp[i:i + 1, :]
        before = jnp.where(blk > i, 1.0, 0.0)
        partial[i % lanes] = partial[i % lanes] + jnp.where(row > imp, 1.0,
                                                            jnp.where(row == imp, before, 0.0))
    rank = (partial[0] + partial[1]) + (partial[2] + partial[3])
    selb = jnp.where(rank < float(SEL_TOP_N), 0.0, NEG_INF)
    selb = jnp.concatenate([selb, jnp.zeros((LANES - n_sel, t), F32)], axis=0).T
    return o_c, selb[:, :HEAD_DIM].astype(BF16)


def _overlap_table(s):
    n_sel = s // SEL_BLOCK
    nc = N_CHUNKS_PAD
    c0 = np.arange(nc)[:, None] * CMP_STRIDE
    s0 = np.arange(n_sel)[None, :] * SEL_BLOCK
    ov = np.clip(np.minimum(c0 + CMP_BLOCK, s0 + SEL_BLOCK) - np.maximum(c0, s0), 0, None) / CMP_STRIDE
    ov[nc - 1] = 0.0
    return jnp.asarray(np.pad(ov, ((0, 0), (0, LANES - n_sel))), BF16)


def _nsa(qn, kc, vc, ks, vs, kw, vw, gates, toep, bc):
    b, s, _ = qn.shape
    t = ATT_TILE
    assert s // SEL_BLOCK <= HEAD_DIM
    ov = _overlap_table(s)
    rows = NSA_HPG * t
    ng = NSA_KV_GROUPS
    nq = s // t
    assert nq >= 2
    seq = lambda n, w=HEAD_DIM: pl.BlockSpec((1, ng, n, w), lambda i, j: (i, 0, 0, 0))
    opened = lambda j: jnp.minimum(j, nq - 1)
    closed = lambda j: jnp.maximum(j - 2, 0)
    return pl.pallas_call(
        functools.partial(_nsa_body, n_tiles=nq),
        grid=(b, nq + 2),
        in_specs=[pl.BlockSpec((1, t, NSA_Q), lambda i, j: (i, opened(j), 0)),
                  seq(N_CHUNKS_PAD), seq(N_CHUNKS_PAD),
                  seq(s, 2 * HEAD_DIM), seq(s, 2 * HEAD_DIM), seq(s), seq(s, 2 * HEAD_DIM),
                  pl.BlockSpec((1, ng, t, GATE_PAD), lambda i, j: (i, 0, opened(j), 0)),
                  pl.BlockSpec((1, ng, t, GATE_PAD), lambda i, j: (i, 0, closed(j), 0)),
                  _resident((NSA_HEADS, t, 2 * t)),
                  pl.BlockSpec((NSA_HEADS, t, N_CHUNKS_PAD), lambda i, j: (0, opened(j), 0)),
                  _resident(ov.shape)],
        out_specs=pl.BlockSpec((1, t, NSA_Q), lambda i, j: (i, closed(j), 0)),
        out_shape=jax.ShapeDtypeStruct((b, s, NSA_Q), BF16),
        scratch_shapes=[pltpu.VMEM((ng, nq - 2, rows, t), F32),
                        pltpu.VMEM((2, ng, 2, rows, t), F32),
                        pltpu.VMEM((2, ng, rows, LANES), F32),
                        pltpu.VMEM((2, ng, rows, 2 * HEAD_DIM), F32),
                        pltpu.VMEM((2, t, NSA_Q), F32)],
        compiler_params=_params(("parallel", "arbitrary")),
        name="nsa",
    )(qn, kc, vc, ks, vs, kw, vw, gates, gates, toep, bc, ov)


def _fcum_body(f_ref, b_ref, o_ref):
    z = f_ref[0] + b_ref[...]
    x = -(jnp.maximum(-z, 0.0) + jnp.log1p(jnp.exp(-jnp.abs(z))))
    n = x.shape[-1]
    lane = lax.broadcasted_iota(jnp.int32, x.shape, 1)
    sh = 1
    while sh < n:
        x = x + jnp.where(lane >= sh, pltpu.roll(x, sh, axis=1), 0.0)
        sh *= 2
    o_ref[0] = x


def _fcum(f_t, b_forget):
    b, h, s = f_t.shape
    blk = pl.BlockSpec((1, h, s), lambda i: (i, 0, 0))
    return pl.pallas_call(
        _fcum_body,
        grid=(b,),
        in_specs=[blk, _resident((h, 1))],
        out_specs=blk,
        out_shape=jax.ShapeDtypeStruct((b, h, s), F32),
        compiler_params=_params(("parallel",)),
        name="fcum",
    )(f_t, b_forget.reshape(h, 1))


FOX_PAIR = 4
FOX_TILE = 512


def _causal_bias(t):
    i = np.arange(t)
    return jnp.asarray(np.where(i[None, :] <= i[:, None], 0.0, NEG_INF), F32)


def _fox_body(q_ref, k_ref, v_ref, c_ref, cm_ref, o_ref, s_ref, mb_ref, acc_ref):
    t = FOX_TILE
    heads = range(FOX_PAIR)
    nq = q_ref.shape[2] // t

    def keys(n):
        return slice(n * t, (n + 1) * t)

    def scores(m, hh, n):
        q = q_ref[0, hh, m * t:(m + 1) * t, :]
        return _dot_nt(q, k_ref[0, hh, keys(n), :]) - c_ref[0, hh, :, keys(n)]

    def scores_diag(m):
        for hh in heads:
            s = scores(m, hh, m) + cm_ref[...]
            s_ref[hh, m] = s
            mb_ref[m % 2, hh] = _lane_fold(s, jnp.maximum)

    def scores_far(m, n):
        for hh in heads:
            s = scores(m, hh, n)
            s_ref[hh, n] = s
            mb_ref[m % 2, hh] = jnp.maximum(mb_ref[m % 2, hh], _lane_fold(s, jnp.maximum))

    def finish_max(m):
        for hh in heads:
            mb_ref[m % 2, hh] = _row_max_tile(mb_ref[m % 2, hh])

    def weigh(m, n):
        for hh in heads:
            p = jnp.exp(s_ref[hh, n] - _lane_tile(mb_ref[m % 2, hh], t)).astype(BF16)
            acc_ref[hh] += _dot(p, v_ref[0, hh, keys(n), :])

    scores_diag(0)
    finish_max(0)
    for m in range(nq):
        ahead = m + 1 < nq
        if ahead:
            scores_diag(m + 1)
        for hh in heads:
            acc_ref[hh] = jnp.zeros((t, 2 * HEAD_DIM), F32)

        for n in range(m + 1):
            weigh(m, n)
            if ahead:
                scores_far(m + 1, n)

        if ahead:
            finish_max(m + 1)
        o_ref[0, m * t:(m + 1) * t, :] = jnp.concatenate(
            [_normalize(acc_ref[hh]) for hh in heads], axis=1).astype(BF16)


def _fox(qf, kf, vf, c):
    b, h, s, _ = qf.shape
    t = FOX_TILE
    seq = lambda w: pl.BlockSpec((1, FOX_PAIR, s, w), lambda i, p: (i, p, 0, 0))
    return pl.pallas_call(
        _fox_body,
        grid=(b, h // FOX_PAIR),
        in_specs=[seq(HEAD_DIM), seq(HEAD_DIM), seq(2 * HEAD_DIM),
                  pl.BlockSpec((1, FOX_PAIR, 1, s), lambda i, p: (i, p, 0, 0)),
                  _resident((t, t))],
        out_specs=pl.BlockSpec((1, s, FOX_PAIR * HEAD_DIM), lambda i, p: (i, 0, p)),
        out_shape=jax.ShapeDtypeStruct((b, s, h * HEAD_DIM), BF16),
        scratch_shapes=[pltpu.VMEM((FOX_PAIR, s // t, t, t), F32),
                        pltpu.VMEM((2, FOX_PAIR, t, LANES), F32),
                        pltpu.VMEM((FOX_PAIR, t, 2 * HEAD_DIM), F32)],
        compiler_params=_params(("parallel", "parallel")),
        name="fox",
    )(qf, kf, vf, c.reshape(b, h, 1, s), _causal_bias(t))


def _mixout_body(x_ref, g_ref, on_ref, of_ref, wa_ref, wb_ref, wun_ref, wuf_ref, wo_ref, o_ref):
    x = x_ref[...]
    h = _rms(x, g_ref[...]).astype(BF16)
    y = (jax.nn.sigmoid(_dot(h, wa_ref[...])) * _dot(on_ref[...], wun_ref[...])
         + jax.nn.sigmoid(_dot(h, wb_ref[...])) * _dot(of_ref[...], wuf_ref[...]))
    o_ref[...] = x + _dot(y.astype(BF16), wo_ref[...])


def _mixout(x2d, g, o_nsa, o_fox, w_a, w_b, w_un, w_uf, w_o):
    n = x2d.shape[0]
    tok = lambda c: pl.BlockSpec((TOK_TILE, c), lambda i: (i, 0))
    return pl.pallas_call(
        _mixout_body,
        grid=(n // TOK_TILE,),
        in_specs=[tok(D_MODEL), _resident((1, D_MODEL)), tok(NSA_Q), tok(FOX_W),
                  _resident(w_a.shape), _resident(w_b.shape), _resident(w_un.shape),
                  _resident(w_uf.shape), _resident(w_o.shape)],
        out_specs=tok(D_MODEL),
        out_shape=jax.ShapeDtypeStruct((n, D_MODEL), F32),
        compiler_params=_params(("parallel",)),
        name="mixout",
    )(x2d, g.reshape(1, D_MODEL), o_nsa, o_fox, w_a, w_b, w_un, w_uf, w_o)


MEM_BATCH = 4


def _memkv_body(m_ref, g_ref, w_ref, k_ref, v_ref):
    nb, ml, d = m_ref.shape
    h = _rms(m_ref[...].reshape(nb * ml, d), g_ref[...]).astype(BF16)
    z = _dot(h, w_ref[...]).astype(BF16)
    k_ref[...] = z[:, :D_MODEL].reshape(nb, ml, d)
    v_ref[...] = z[:, D_MODEL:].reshape(nb, ml, d)


def _memkv(mem, g, w_kv):
    b, ml, _ = mem.shape
    nb = MEM_BATCH if b % MEM_BATCH == 0 else 1
    blk = pl.BlockSpec((nb, ml, D_MODEL), lambda i: (i, 0, 0))
    sd = jax.ShapeDtypeStruct((b, ml, D_MODEL), BF16)
    return pl.pallas_call(
        _memkv_body,
        grid=(b // nb,),
        in_specs=[blk, _resident((1, D_MODEL)), _resident(w_kv.shape)],
        out_specs=(blk, blk),
        out_shape=(sd, sd),
        compiler_params=_params(("parallel",)),
        name="memkv",
    )(mem, g.reshape(1, D_MODEL), w_kv)


def _memattn_body(x_ref, g_ref, k_ref, v_ref, wq_ref, wo_ref, o_ref):
    x = x_ref[0]
    h = _rms(x, g_ref[...]).astype(BF16)
    q = (_dot(h, wq_ref[...]) * (MEM_HEAD_DIM ** -0.5)).astype(BF16)
    outs = []
    for hd in range(MEM_HEADS):
        cols = slice(hd * MEM_HEAD_DIM, (hd + 1) * MEM_HEAD_DIM)
        s = _dot_nt(q[:, cols], k_ref[0, :, cols])
        e = jnp.exp(s - jnp.max(s, axis=-1, keepdims=True))
        p = e / jnp.sum(e, axis=-1, keepdims=True)
        outs.append(_dot(p.astype(BF16), v_ref[0, :, cols]))
    o = jnp.concatenate(outs, axis=1).astype(BF16)
    o_ref[0] = x + _dot(o, wo_ref[...])


def _memattn(x, g, k, v, w_q, w_o):
    b, s, _ = x.shape
    ml = k.shape[1]
    tok = pl.BlockSpec((1, TOK_TILE, D_MODEL), lambda i, j: (i, j, 0))
    kvb = pl.BlockSpec((1, ml, D_MODEL), lambda i, j: (i, 0, 0))
    return pl.pallas_call(
        _memattn_body,
        grid=(b, s // TOK_TILE),
        in_specs=[tok, _resident((1, D_MODEL)), kvb, kvb, _resident(w_q.shape), _resident(w_o.shape)],
        out_specs=tok,
        out_shape=jax.ShapeDtypeStruct((b, s, D_MODEL), F32),
        compiler_params=_params(("parallel", "parallel")),
        name="memattn",
    )(x, g.reshape(1, D_MODEL), k, v, w_q, w_o)


def kernel(x, mem, rel_bias_table, ffn1_norm, ffn1_w_gate, ffn1_w_up, ffn1_w_down, mix_norm, mix_w_in, mix_b_forget, cmp_pos_k, cmp_pos_v, cmp_k_w1, cmp_k_w2, cmp_v_w1, cmp_v_w2, w_up_nsa, w_up_fox, mix_w_out, mem_q_norm, mem_kv_norm, mem_w_q, mem_w_kv, mem_w_o, ffn2_norm, ffn2_w_gate, ffn2_w_up, ffn2_w_down, final_norm):
    b, s, d = x.shape
    depth = ffn1_norm.shape[0]
    bf = lambda w: w.astype(BF16)
    toep, bias_c = _relbias(rel_bias_table, s)
    x = x.reshape(b * s, d)
    for l in range(depth):
        last = l == depth - 1
        x = _ffn(x, ffn1_norm[l], ffn1_w_gate[l], ffn1_w_up[l], ffn1_w_down[l])

        w_nsa, w_fox, w_small, w_a, w_b = _pack_w_in(mix_w_in, l)
        qn, kc, vc, ks, vs, kw, vw, qf, kf, vf, gates, flog = _inproj(
            x.reshape(b, s, d), mix_norm[l], w_nsa, w_fox, w_small)
        kcc, vcc = _compress(kc, vc, cmp_pos_k[l], cmp_pos_v[l], cmp_k_w1[l], cmp_k_w2[l],
                             cmp_v_w1[l], cmp_v_w2[l])
        o_nsa = _nsa(qn, kcc, vcc, ks, vs, kw, vw, gates, toep, bias_c)
        c = _fcum(flog, mix_b_forget[l])
        o_fox = _fox(qf, kf, vf, c)
        x = _mixout(x, mix_norm[l], o_nsa.reshape(b * s, NSA_Q), o_fox.reshape(b * s, FOX_W),
                    w_a, w_b, bf(w_up_nsa[l]), bf(w_up_fox[l]), bf(mix_w_out[l]))

        mk, mv = _memkv(mem, mem_kv_norm[l], bf(mem_w_kv[l]))
        x = _memattn(x.reshape(b, s, d), mem_q_norm[l], mk, mv, bf(mem_w_q[l]), bf(mem_w_o[l]))
        x = _ffn(x.reshape(b * s, d), ffn2_norm[l], ffn2_w_gate[l], ffn2_w_up[l],
                 ffn2_w_down[l], final_g=final_norm if last else None)
    return x.reshape(b, s, d)
```

```python
import functools
import math

import numpy as np
import jax
import jax.numpy as jnp
from jax import lax
from jax.experimental import pallas as pl
from jax.experimental.pallas import tpu as pltpu

D_MODEL = 1024
D_FF = 2816
HEAD_DIM = 64
NSA_HEADS = 8
NSA_KV_GROUPS = 2
NSA_HPG = NSA_HEADS // NSA_KV_GROUPS
CMP_BLOCK = 32
CMP_STRIDE = 16
CMP_HIDDEN = 256
SEL_BLOCK = 64
SEL_TOP_N = 16
WINDOW = 512
FOX_HEADS = 8
MEM_HEADS = 4
MEM_HEAD_DIM = D_MODEL // MEM_HEADS
NUM_BUCKETS = 32
MAX_DISTANCE = 128
RMS_EPS = 1e-6
NEG_INF = -1e30
FORCE_BONUS = 1e4

NSA_Q = NSA_HEADS * HEAD_DIM
NSA_KV = NSA_KV_GROUPS * HEAD_DIM
FOX_W = FOX_HEADS * HEAD_DIM

LANES = 128
VMEM_LIMIT = 56 * 1024 * 1024

BF16 = jnp.bfloat16
F32 = jnp.float32

ATT_TILE = 256
TOK_TILE = 1024
FFN_TILE = 1024
FF_CHUNK = 256


def _bucket_thresholds():
    n = np.arange(0, 4 * MAX_DISTANCE)
    exact = NUM_BUCKETS // 2
    large = exact + (np.log(np.maximum(n, 1) / exact) / math.log(MAX_DISTANCE / exact)
                     * (NUM_BUCKETS - exact)).astype(np.int64)
    bucket = np.where(n < exact, n, np.minimum(large, NUM_BUCKETS - 1))
    assert np.all(np.diff(bucket) >= 0)
    return [int(np.argmax(bucket >= k)) for k in range(1, NUM_BUCKETS)]


BUCKET_THRESHOLDS = _bucket_thresholds()


def _dot(a, b):
    return jnp.dot(a, b, preferred_element_type=F32)


def _dot_nt(a, b):
    return lax.dot_general(a, b, (((1,), (1,)), ((), ())), preferred_element_type=F32)


def _row_halves(fn, a, b):
    h = a.shape[0] // 2
    return jnp.concatenate([fn(a[:h], b), fn(a[h:], b)], axis=0)


def _dot_pair(a, b):
    return _row_halves(_dot, a, b)


def _dot_nt_pair(a, b):
    return _row_halves(_dot_nt, a, b)


def _rms(x, g):
    return x * lax.rsqrt(jnp.mean(x * x, axis=-1, keepdims=True) + RMS_EPS) * g


def _resident(shape):
    nd = len(shape)
    return pl.BlockSpec(shape, lambda *_: (0,) * nd, pipeline_mode=pl.Buffered(1))


def _params(sem):
    return pltpu.CompilerParams(dimension_semantics=sem, vmem_limit_bytes=VMEM_LIMIT)


def _ffn_body(x_ref, g_ref, wg_hbm, wu_hbm, wd_hbm, *rest, final):
    if final:
        fg_ref, o_ref, wg_ref, wu_ref, wd_ref, sg_ref, su_ref, sd_ref, sem = rest
    else:
        o_ref, wg_ref, wu_ref, wd_ref, sg_ref, su_ref, sd_ref, sem = rest
    n_chunks = D_FF // FF_CHUNK

    def chunk_copies(c):
        cols = slice(c * FF_CHUNK, (c + 1) * FF_CHUNK)
        slot = c % 2
        return (pltpu.make_async_copy(wg_hbm.at[:, cols], sg_ref.at[slot], sem.at[0, slot]),
                pltpu.make_async_copy(wu_hbm.at[:, cols], su_ref.at[slot], sem.at[1, slot]),
                pltpu.make_async_copy(wd_hbm.at[cols, :], sd_ref.at[slot], sem.at[2, slot]))

    @pl.when(pl.program_id(0) == 0)
    def _():
        for cp in chunk_copies(0):
            cp.start()
        for c in range(n_chunks):
            sl = slice(c * FF_CHUNK, (c + 1) * FF_CHUNK)
            if c + 1 < n_chunks:
                for cp in chunk_copies(c + 1):
                    cp.start()
            for cp in chunk_copies(c):
                cp.wait()
            wg_ref[:, sl] = sg_ref[c % 2].astype(BF16)
            wu_ref[:, sl] = su_ref[c % 2].astype(BF16)
            wd_ref[sl, :] = sd_ref[c % 2].astype(BF16)

    x = x_ref[...]
    h = _rms(x, g_ref[...]).astype(BF16)
    acc = jnp.zeros(x.shape, F32)
    for c in range(n_chunks):
        sl = slice(c * FF_CHUNK, (c + 1) * FF_CHUNK)
        a = _dot(h, wg_ref[:, sl])
        b = _dot(h, wu_ref[:, sl])
        t = (a * jax.nn.sigmoid(a)) * b
        acc = acc + _dot(t.astype(BF16), wd_ref[sl, :])
    y = x + 0.5 * acc
    if final:
        y = _rms(y, fg_ref[...])
    o_ref[...] = y


def _ffn(x2d, g, wg, wu, wd, final_g=None):
    n = x2d.shape[0]
    final = final_g is not None
    tok = pl.BlockSpec((FFN_TILE, D_MODEL), lambda i: (i, 0))
    hbm = pl.BlockSpec(memory_space=pl.ANY)
    in_specs = [tok, _resident((1, D_MODEL)), hbm, hbm, hbm]
    args = [x2d, g.reshape(1, D_MODEL), wg, wu, wd]
    if final:
        in_specs.append(_resident((1, D_MODEL)))
        args.append(final_g.reshape(1, D_MODEL))
    return pl.pallas_call(
        functools.partial(_ffn_body, final=final),
        grid=(n // FFN_TILE,),
        in_specs=in_specs,
        out_specs=tok,
        out_shape=jax.ShapeDtypeStruct((n, D_MODEL), F32),
        scratch_shapes=[pltpu.VMEM((D_MODEL, D_FF), BF16), pltpu.VMEM((D_MODEL, D_FF), BF16),
                        pltpu.VMEM((D_FF, D_MODEL), BF16),
                        pltpu.VMEM((2, D_MODEL, FF_CHUNK), F32), pltpu.VMEM((2, D_MODEL, FF_CHUNK), F32),
                        pltpu.VMEM((2, FF_CHUNK, D_MODEL), F32),
                        pltpu.SemaphoreType.DMA((3, 2))],
        compiler_params=_params(("arbitrary",)),
        name="ffn_final" if final else "ffn",
    )(*args)


_C_CMP = NSA_Q
_C_KV = _C_CMP + 2 * NSA_KV
_W_NSA = _C_KV + 4 * NSA_KV
GATE_PAD = 16
_C_FLOG = NSA_KV_GROUPS * GATE_PAD
N_CHUNKS_PAD = 128
CHUNK_W = CMP_STRIDE * NSA_KV


def _inproj_body(x_ref, g_ref, wn_ref, wf_ref, ws_ref, blk_ref, qn_ref, kc_ref, vc_ref, ks_ref, vs_ref,
                 kw_ref, vw_ref, qf_ref, kf_ref, vf_ref, gate_ref, fl_ref, cmp_ref):
    h = _rms(x_ref[0], g_ref[...]).astype(BF16)
    z = _dot(h, wn_ref[...])
    qn_ref[0] = (z[:, :_C_CMP] * (HEAD_DIM ** -0.5)).astype(BF16)
    rows = z.shape[0] // CMP_STRIDE
    for i, ref in enumerate((kc_ref, vc_ref)):
        cmp_ref[i] = z[:, _C_CMP + i * NSA_KV:_C_CMP + (i + 1) * NSA_KV]
        for r in range(CMP_STRIDE):
            tok = cmp_ref[i, pl.ds(r, rows, stride=CMP_STRIDE), :]
            ref[0, :, r * NSA_KV:(r + 1) * NSA_KV] = tok.astype(BF16)
    ones = jnp.ones((z.shape[0], HEAD_DIM), BF16)
    for i, (ref, extra) in enumerate(((ks_ref, blk_ref[...]), (vs_ref, ones), (kw_ref, None),
                                      (vw_ref, ones))):
        for g in range(NSA_KV_GROUPS):
            c0 = _C_KV + i * NSA_KV + g * HEAD_DIM
            val = z[:, c0:c0 + HEAD_DIM].astype(BF16)
            ref[0, g] = val if extra is None else jnp.concatenate([val, extra], axis=1)
    zf = _dot(h, wf_ref[...])
    for hd in range(FOX_HEADS):
        c0 = hd * HEAD_DIM
        qf_ref[0, hd] = (zf[:, c0:c0 + HEAD_DIM] * (HEAD_DIM ** -0.5)).astype(BF16)
        kf_ref[0, hd] = zf[:, c0 + FOX_W:c0 + FOX_W + HEAD_DIM].astype(BF16)
        v = zf[:, c0 + 2 * FOX_W:c0 + 2 * FOX_W + HEAD_DIM].astype(BF16)
        vf_ref[0, hd] = jnp.concatenate([v, ones], axis=1)
    zs = _dot(h, ws_ref[...])
    for g in range(NSA_KV_GROUPS):
        gate_ref[0, g] = zs[:, g * GATE_PAD:(g + 1) * GATE_PAD]
    fl_ref[0] = zs.T[_C_FLOG:_C_FLOG + FOX_HEADS]


_IN_COLS = np.cumsum((0, NSA_Q, NSA_KV, NSA_KV, NSA_KV, NSA_KV, NSA_KV, NSA_KV, 3 * NSA_HEADS,
                      FOX_W, FOX_W, FOX_W, FOX_HEADS, D_MODEL, D_MODEL)).tolist()
W_PREP_ROWS = 256


def _wprep_body(w_ref, wn_ref, wf_ref, ws_ref, wa_ref, wb_ref):
    wt = w_ref[0]
    c = _IN_COLS
    k = wt.shape[1]
    wn_ref[...] = wt[:c[7]].T.astype(BF16)
    wf_ref[...] = wt[c[8]:c[11]].T.astype(BF16)
    wa_ref[...] = wt[c[12]:c[13]].T.astype(BF16)
    wb_ref[...] = wt[c[13]:c[14]].T.astype(BF16)
    small = []
    for g in range(NSA_KV_GROUPS):
        for br in range(3):
            c0 = c[7] + br * NSA_HEADS + g * NSA_HPG
            small.append(wt[c0:c0 + NSA_HPG])
        small.append(jnp.zeros((GATE_PAD - 3 * NSA_HPG, k), F32))
    small.append(wt[c[11]:c[12]])
    small.append(jnp.zeros((LANES - _C_FLOG - FOX_HEADS, k), F32))
    ws_ref[...] = jnp.concatenate(small, axis=0).T.astype(BF16)


def _pack_w_in(w_in, layer):
    _, k, n = w_in.shape
    c = _IN_COLS
    assert n == c[14]
    widths = (c[7], c[11] - c[8], LANES, D_MODEL, D_MODEL)
    return pl.pallas_call(
        _wprep_body,
        grid=(k // W_PREP_ROWS,),
        in_specs=[pl.BlockSpec((1, n, W_PREP_ROWS), lambda i: (layer, 0, i))],
        out_specs=tuple(pl.BlockSpec((W_PREP_ROWS, w), lambda i: (i, 0)) for w in widths),
        out_shape=tuple(jax.ShapeDtypeStruct((k, w), BF16) for w in widths),
        compiler_params=_params(("parallel",)),
        name="wprep",
    )(jnp.swapaxes(w_in, 1, 2))


def _inproj(x, g, w_nsa, w_fox, w_small):
    b, s, _ = x.shape
    grid = (b, s // TOK_TILE)
    tok = lambda c: pl.BlockSpec((1, TOK_TILE, c), lambda i, j: (i, j, 0))
    heads = lambda nh, w=HEAD_DIM: pl.BlockSpec((1, nh, TOK_TILE, w), lambda i, j: (i, 0, j, 0))
    chunks = pl.BlockSpec((1, TOK_TILE // CMP_STRIDE, CHUNK_W), lambda i, j: (i, j, 0))
    sds = jax.ShapeDtypeStruct
    kv = sds((b, NSA_KV_GROUPS, s, HEAD_DIM), BF16)
    kv2 = sds((b, NSA_KV_GROUPS, s, 2 * HEAD_DIM), BF16)
    fx = sds((b, FOX_HEADS, s, HEAD_DIM), BF16)
    cmp_in = sds((b, s // CMP_STRIDE, CHUNK_W), BF16)
    out_shape = (sds((b, s, NSA_Q), BF16), cmp_in, cmp_in,
                 kv2, kv2, kv, kv2, fx, fx, sds((b, FOX_HEADS, s, 2 * HEAD_DIM), BF16),
                 sds((b, NSA_KV_GROUPS, s, GATE_PAD), F32), sds((b, FOX_HEADS, s), F32))
    out_specs = (tok(NSA_Q), chunks, chunks,
                 heads(NSA_KV_GROUPS, 2 * HEAD_DIM), heads(NSA_KV_GROUPS, 2 * HEAD_DIM),
                 heads(NSA_KV_GROUPS), heads(NSA_KV_GROUPS, 2 * HEAD_DIM),
                 heads(FOX_HEADS), heads(FOX_HEADS), heads(FOX_HEADS, 2 * HEAD_DIM),
                 pl.BlockSpec((1, NSA_KV_GROUPS, TOK_TILE, GATE_PAD), lambda i, j: (i, 0, j, 0)),
                 pl.BlockSpec((1, FOX_HEADS, TOK_TILE), lambda i, j: (i, 0, j)))
    key_blk = np.arange(s) // SEL_BLOCK
    blk_onehot = jnp.asarray(key_blk[:, None] == np.arange(HEAD_DIM)[None, :], BF16)
    return pl.pallas_call(
        _inproj_body,
        grid=grid,
        in_specs=[tok(D_MODEL), _resident((1, D_MODEL)), _resident(w_nsa.shape),
                  _resident(w_fox.shape), _resident(w_small.shape),
                  pl.BlockSpec((TOK_TILE, HEAD_DIM), lambda i, j: (j, 0))],
        out_specs=out_specs,
        out_shape=out_shape,
        scratch_shapes=[pltpu.VMEM((2, TOK_TILE, NSA_KV), F32)],
        compiler_params=_params(("parallel", "parallel")),
        name="inproj",
    )(x, g.reshape(1, D_MODEL), w_nsa, w_fox, w_small, blk_onehot)


CMP_BATCH = 4

def _compress_body(xk_ref, xv_ref, pk_ref, pv_ref, wk1_ref, wv1_ref, wk2_ref, wv2_ref, ok_ref, ov_ref):
    nb = xk_ref.shape[0]
    n = nb * N_CHUNKS_PAD
    for x_ref, p_ref, w1_ref, w2_ref, o_ref in ((xk_ref, pk_ref, wk1_ref, wk2_ref, ok_ref),
                                                (xv_ref, pv_ref, wv1_ref, wv2_ref, ov_ref)):
        x = x_ref[...].reshape(n, CHUNK_W)
        a0 = _dot(x, w1_ref[0])
        a1 = _dot(x, w1_ref[1])
        c = _dot(p_ref[0], w1_ref[0]) + _dot(p_ref[1], w1_ref[1])
        pre = a0 + pltpu.roll(a1, n - 1, axis=0) + c[0:1, :]
        hid = jax.nn.gelu(pre).astype(BF16)
        out = _dot(hid, w2_ref[...]).astype(BF16)
        for i in range(nb):
            for g in range(NSA_KV_GROUPS):
                o_ref[i, g] = out[i * N_CHUNKS_PAD:(i + 1) * N_CHUNKS_PAD, g * HEAD_DIM:(g + 1) * HEAD_DIM]


def _pack_compress(pos, w1, w2):
    r = CMP_BLOCK // CMP_STRIDE
    assert NSA_KV_GROUPS == 2
    w1r = w1.astype(BF16).reshape(r, CMP_STRIDE, HEAD_DIM, CMP_HIDDEN)
    z1 = jnp.zeros_like(w1r)
    w1big = jnp.stack([jnp.concatenate([w1r, z1], axis=-1), jnp.concatenate([z1, w1r], axis=-1)], axis=2)
    w1big = w1big.reshape(r, CHUNK_W, NSA_KV_GROUPS * CMP_HIDDEN)
    z2 = jnp.zeros_like(w2)
    w2big = jnp.concatenate([jnp.concatenate([w2, z2], axis=1), jnp.concatenate([z2, w2], axis=1)], axis=0)
    p = pos.reshape(r, CMP_STRIDE, 1, HEAD_DIM)
    p = jnp.broadcast_to(p, (r, CMP_STRIDE, NSA_KV_GROUPS, HEAD_DIM)).reshape(r, 1, CHUNK_W)
    p = jnp.broadcast_to(p, (r, 16, CHUNK_W))
    return p.astype(BF16), w1big.astype(BF16), w2big.astype(BF16)


def _compress(xk, xv, pos_k, pos_v, k_w1, k_w2, v_w1, v_w2):
    b = xk.shape[0]
    assert xk.shape[1:] == (N_CHUNKS_PAD, CHUNK_W)
    pk, wk1, wk2 = _pack_compress(pos_k, k_w1, k_w2)
    pv, wv1, wv2 = _pack_compress(pos_v, v_w1, v_w2)
    nb = CMP_BATCH if b % CMP_BATCH == 0 else 1
    xs = pl.BlockSpec((nb, N_CHUNKS_PAD, CHUNK_W), lambda i: (i, 0, 0))
    os_ = pl.BlockSpec((nb, NSA_KV_GROUPS, N_CHUNKS_PAD, HEAD_DIM), lambda i: (i, 0, 0, 0))
    osd = jax.ShapeDtypeStruct((b, NSA_KV_GROUPS, N_CHUNKS_PAD, HEAD_DIM), BF16)
    return pl.pallas_call(
        _compress_body,
        grid=(b // nb,),
        in_specs=[xs, xs, _resident(pk.shape), _resident(pv.shape), _resident(wk1.shape),
                  _resident(wv1.shape), _resident(wk2.shape), _resident(wv2.shape)],
        out_specs=(os_, os_),
        out_shape=(osd, osd),
        compiler_params=_params(("parallel",)),
        name="compress",
    )(xk, xv, pk, pv, wk1, wv1, wk2, wv2)


def _bias_lookup(tbl_ref, head, dist):
    acc = jnp.full(dist.shape, tbl_ref[0, head], F32)
    for k, thr in enumerate(BUCKET_THRESHOLDS):
        acc = jnp.where(dist >= thr, tbl_ref[k + 1, head], acc)
    return acc - tbl_ref[NUM_BUCKETS - 1, head]


def _relbias_body(tbl_ref, toep_ref, bc_ref):
    head = pl.program_id(0)
    t = ATT_TILE
    i = lax.broadcasted_iota(jnp.int32, (t, 2 * t), 0)
    u = lax.broadcasted_iota(jnp.int32, (t, 2 * t), 1)
    d = i - u + t
    toep_ref[0] = jnp.where(d >= 0, _bias_lookup(tbl_ref, head, d), NEG_INF)
    n_tiles = bc_ref.shape[1] // t
    per_tile = t // CMP_STRIDE
    off = per_tile * (n_tiles - 1)
    width = 2 * N_CHUNKS_PAD
    assert off + N_CHUNKS_PAD <= width
    i = lax.broadcasted_iota(jnp.int32, (t, width), 0)
    u = lax.broadcasted_iota(jnp.int32, (t, width), 1)
    dc = i - ((u - off) * CMP_STRIDE + CMP_BLOCK - 1)
    strip = jnp.where(dc >= 0, _bias_lookup(tbl_ref, head, dc), NEG_INF)
    for q in range(n_tiles):
        lo = off - q * per_tile
        bc_ref[0, q * t:(q + 1) * t, :] = strip[:, lo:lo + N_CHUNKS_PAD]


def _relbias(tbl, s):
    return pl.pallas_call(
        _relbias_body,
        grid=(NSA_HEADS,),
        in_specs=[pl.BlockSpec(memory_space=pltpu.SMEM)],
        out_specs=(pl.BlockSpec((1, ATT_TILE, 2 * ATT_TILE), lambda h: (h, 0, 0)),
                   pl.BlockSpec((1, s, N_CHUNKS_PAD), lambda h: (h, 0, 0))),
        out_shape=(jax.ShapeDtypeStruct((NSA_HEADS, ATT_TILE, 2 * ATT_TILE), F32),
                   jax.ShapeDtypeStruct((NSA_HEADS, s, N_CHUNKS_PAD), F32)),
        compiler_params=_params(("parallel",)),
        name="relbias",
    )(tbl)


def _lane_fold(x, op):
    return functools.reduce(op, [x[:, i * LANES:(i + 1) * LANES] for i in range(x.shape[1] // LANES)])


def _lane_tile(x, width):
    return jnp.concatenate([x] * (width // LANES), axis=1)


def _row_max_tile(rmax):
    return jnp.broadcast_to(jnp.max(rmax, axis=-1, keepdims=True), rmax.shape)


def _normalize(acc):
    return (acc / pltpu.roll(acc, HEAD_DIM, axis=1))[:, :HEAD_DIM]


def _split3(x):
    hi = x.astype(BF16)
    r = x - hi.astype(F32)
    mid = r.astype(BF16)
    lo = (r - mid.astype(F32)).astype(BF16)
    return hi, mid, lo


def _nsa_body(q_ref, kc_ref, vc_ref, ks_ref, vs_ref, kw_ref, vw_ref, gate_a_ref, gate_b_ref, toep_ref,
              bc_ref, ov_ref, o_ref, s_ref, near_ref, mb_ref, acc_ref, part_ref, *, n_tiles):
    t = ATT_TILE
    j4 = NSA_HPG
    rows = j4 * t
    groups = range(NSA_KV_GROUPS)
    step = pl.program_id(1)
    n_sel = n_tiles * (t // SEL_BLOCK)

    def heads(x):
        return x.reshape(j4, t, x.shape[-1])

    def flat(x):
        return x.reshape(rows, x.shape[-1])

    def key_tile(ref, g, n):
        if isinstance(n, int):
            return ref[0, g, n * t:(n + 1) * t, :]
        return ref[0, g, pl.ds(pl.multiple_of(n * t, t), t), :]

    def gated(gate_ref, g, branch, o):
        gate = jax.nn.sigmoid(gate_ref[0, g])
        return [gate[:, branch * j4 + j:branch * j4 + j + 1] * o[j * t:(j + 1) * t] for j in range(j4)]

    def probs(par, g, s):
        return jnp.exp(s - _lane_tile(mb_ref[par, g], t)).astype(BF16)

    def open_tile(m, par):
        has_prev = m >= 1 if isinstance(m, int) else True
        has_far = m >= 2 if isinstance(m, int) else True
        qs = []
        for g in groups:
            qt = q_ref[0, :, g * j4 * HEAD_DIM:(g + 1) * j4 * HEAD_DIM]
            qs.append(jnp.concatenate([qt[:, j * HEAD_DIM:(j + 1) * HEAD_DIM] for j in range(j4)], axis=0))

        def biases(g):
            toep = toep_ref[g * j4:(g + 1) * j4]
            return toep[:, :, t:], toep[:, :, :t]

        cmp_out = [_nsa_compressed(qs[g], kc_ref[0, g], vc_ref[0, g], bc_ref[g * j4:(g + 1) * j4],
                                   ov_ref[...], m, n_sel) for g in groups]
        part = []
        for g in groups:
            diag_bias, prev_bias = biases(g)
            tiles = [(m, flat(heads(_dot_nt_pair(qs[g], key_tile(kw_ref, g, m))) + diag_bias))]
            if has_prev:
                tiles.append((m - 1, flat(heads(_dot_nt_pair(qs[g], key_tile(kw_ref, g, m - 1))) + prev_bias)))
            if has_far:
                ii = lax.broadcasted_iota(jnp.int32, (t, t), 0)
                jj = lax.broadcasted_iota(jnp.int32, (t, t), 1)
                tri = jnp.where(jj > ii, 0.0, NEG_INF)
                tiles.append((m - 2, flat(heads(_dot_nt_pair(qs[g], key_tile(kw_ref, g, m - 2))) + tri[None])))
            wmax = _lane_tile(_row_max_tile(functools.reduce(
                jnp.maximum, [_lane_fold(w, jnp.maximum) for _, w in tiles])), t)
            acc = sum(_dot_pair(jnp.exp(w - wmax).astype(BF16), key_tile(vw_ref, g, n)) for n, w in tiles)
            part += [a + b for a, b in zip(gated(gate_a_ref, g, 0, cmp_out[g][0]),
                                           gated(gate_a_ref, g, 2, _normalize(acc)))]
        part_ref[par] = jnp.concatenate(part, axis=1)

        q_aug = []
        for g in groups:
            diag_bias, prev_bias = biases(g)
            q_aug.append(jnp.concatenate([qs[g], jnp.concatenate([cmp_out[g][1]] * j4, axis=0)], axis=1))
            s0 = flat(heads(_dot_nt_pair(q_aug[g], key_tile(ks_ref, g, m))) + diag_bias)
            near_ref[par, g, 0] = s0
            rmax = _lane_fold(s0, jnp.maximum)
            if has_prev:
                s1 = flat(heads(_dot_nt_pair(q_aug[g], key_tile(ks_ref, g, m - 1))) + prev_bias)
                near_ref[par, g, 1] = s1
                rmax = jnp.maximum(rmax, _lane_fold(s1, jnp.maximum))
            if has_far:
                s2 = _dot_nt_pair(q_aug[g], key_tile(ks_ref, g, m - 2))
                s_ref[g, m - 2] = s2
                rmax = jnp.maximum(rmax, _lane_fold(s2, jnp.maximum))
            mb_ref[par, g] = rmax
        return q_aug

    def open_far(par, q_aug, n):
        for g in groups:
            s = _dot_nt_pair(q_aug[g], key_tile(ks_ref, g, n))
            s_ref[g, n] = s
            mb_ref[par, g] = jnp.maximum(mb_ref[par, g], _lane_fold(s, jnp.maximum))

    def close_tile(m, par):
        for g in groups:
            mb_ref[par, g] = _row_max_tile(mb_ref[par, g])
            acc = _dot_pair(probs(par, g, near_ref[par, g, 0]), key_tile(vs_ref, g, m))
            if not (isinstance(m, int) and m == 0):
                acc = acc + _dot_pair(probs(par, g, near_ref[par, g, 1]), key_tile(vs_ref, g, m - 1))
            acc_ref[par, g] = acc

    def close_far(par, n):
        for g in groups:
            acc_ref[par, g] += _dot_pair(probs(par, g, s_ref[g, n]), key_tile(vs_ref, g, n))

    def finish(par):
        outs = []
        for g in groups:
            outs += gated(gate_b_ref, g, 1, _normalize(acc_ref[par, g]))
        o_ref[0] = (part_ref[par] + jnp.concatenate(outs, axis=1)).astype(BF16)

    @pl.when(step == 0)
    def _():
        open_tile(0, 0)

    @pl.when(step == 1)
    def _():
        open_tile(1, 1)
        close_tile(0, 0)

    @pl.when(jnp.logical_and(step >= 2, step < n_tiles))
    def _():
        par = step % 2
        finish(par)
        q_aug = open_tile(step, par)
        close_tile(step - 1, 1 - par)

        @pl.loop(0, step - 2)
        def _(n):
            close_far(1 - par, n)
            open_far(par, q_aug, n)

    last = (n_tiles - 1) % 2

    @pl.when(step == n_tiles)
    def _():
        finish(1 - last)
        close_tile(step - 1, last)

        @pl.loop(0, step - 2)
        def _(n):
            close_far(last, n)

    @pl.when(step == n_tiles + 1)
    def _():
        finish(last)


def _nsa_compressed(q, kc, vc, bc, ov, m, n_sel):
    t = ATT_TILE
    j4 = NSA_HPG

    def heads(x):
        return x.reshape(j4, t, x.shape[-1])

    def flat(x):
        return x.reshape(j4 * t, x.shape[-1])

    sc = heads(_dot_nt_pair(q, kc)) + bc
    valid = bc > 0.5 * NEG_INF
    e = jnp.where(valid, jnp.exp(sc - jnp.max(sc, axis=-1, keepdims=True)), 0.0)
    l = jnp.sum(e, axis=-1, keepdims=True)
    pc = jnp.where(l > 0.0, e / l, 0.0)
    o_c = _dot_pair(flat(pc).astype(BF16), vc)
    if isinstance(m, int) and (m + 1) * t <= SEL_TOP_N * SEL_BLOCK:
        return o_c, jnp.zeros((t, HEAD_DIM), BF16)
    psum = pc[0] + pc[1] + pc[2] + pc[3]
    imp = sum(_dot(part, ov) for part in _split3(psum)).T[:n_sel]
    blk = lax.broadcasted_iota(jnp.int32, (n_sel, t), 0)
    tpos = m * t + lax.broadcasted_iota(jnp.int32, (n_sel, t), 1)
    cur = tpos // SEL_BLOCK
    bonus = jnp.where(blk == 0, FORCE_BONUS,
                      jnp.where(blk == cur, FORCE_BONUS, jnp.where(blk == cur - 1, FORCE_BONUS, 0.0)))
    imp = jnp.where(blk * SEL_BLOCK <= tpos, imp + bonus, NEG_INF)
    lanes = 4
    partial = [jnp.zeros((n_sel, t), F32) for _ in range(lanes)]
    for i in range(n_sel):
        row = imp[i:i + 1, :]
        before = jnp.where(blk > i, 1.0, 0.0)
        partial[i % lanes] = partial[i % lanes] + jnp.where(row > imp, 1.0,
                                                            jnp.where(row == imp, before, 0.0))
    rank = (partial[0] + partial[1]) + (partial[2] + partial[3])
    selb = jnp.where(rank < float(SEL_TOP_N), 0.0, NEG_INF)
    selb = jnp.concatenate([selb, jnp.zeros((LANES - n_sel, t), F32)], axis=0).T
    return o_c, selb[:, :HEAD_DIM].astype(BF16)


def _overlap_table(s):
    n_sel = s // SEL_BLOCK
    nc = N_CHUNKS_PAD
    c0 = np.arange(nc)[:, None] * CMP_STRIDE
    s0 = np.arange(n_sel)[None, :] * SEL_BLOCK
    ov = np.clip(np.minimum(c0 + CMP_BLOCK, s0 + SEL_BLOCK) - np.maximum(c0, s0), 0, None) / CMP_STRIDE
    ov[nc - 1] = 0.0
    return jnp.asarray(np.pad(ov, ((0, 0), (0, LANES - n_sel))), BF16)


def _nsa(qn, kc, vc, ks, vs, kw, vw, gates, toep, bc):
    b, s, _ = qn.shape
    t = ATT_TILE
    assert s // SEL_BLOCK <= HEAD_DIM
    ov = _overlap_table(s)
    rows = NSA_HPG * t
    ng = NSA_KV_GROUPS
    nq = s // t
    assert nq >= 2
    seq = lambda n, w=HEAD_DIM: pl.BlockSpec((1, ng, n, w), lambda i, j: (i, 0, 0, 0))
    opened = lambda j: jnp.minimum(j, nq - 1)
    closed = lambda j: jnp.maximum(j - 2, 0)
    return pl.pallas_call(
        functools.partial(_nsa_body, n_tiles=nq),
        grid=(b, nq + 2),
        in_specs=[pl.BlockSpec((1, t, NSA_Q), lambda i, j: (i, opened(j), 0)),
                  seq(N_CHUNKS_PAD), seq(N_CHUNKS_PAD),
                  seq(s, 2 * HEAD_DIM), seq(s, 2 * HEAD_DIM), seq(s), seq(s, 2 * HEAD_DIM),
                  pl.BlockSpec((1, ng, t, GATE_PAD), lambda i, j: (i, 0, opened(j), 0)),
                  pl.BlockSpec((1, ng, t, GATE_PAD), lambda i, j: (i, 0, closed(j), 0)),
                  _resident((NSA_HEADS, t, 2 * t)),
                  pl.BlockSpec((NSA_HEADS, t, N_CHUNKS_PAD), lambda i, j: (0, opened(j), 0)),
                  _resident(ov.shape)],
        out_specs=pl.BlockSpec((1, t, NSA_Q), lambda i, j: (i, closed(j), 0)),
        out_shape=jax.ShapeDtypeStruct((b, s, NSA_Q), BF16),
        scratch_shapes=[pltpu.VMEM((ng, nq - 2, rows, t), F32),
                        pltpu.VMEM((2, ng, 2, rows, t), F32),
                        pltpu.VMEM((2, ng, rows, LANES), F32),
                        pltpu.VMEM((2, ng, rows, 2 * HEAD_DIM), F32),
                        pltpu.VMEM((2, t, NSA_Q), F32)],
        compiler_params=_params(("parallel", "arbitrary")),
        name="nsa",
    )(qn, kc, vc, ks, vs, kw, vw, gates, gates, toep, bc, ov)


def _fcum_body(f_ref, b_ref, o_ref):
    nb = f_ref.shape[0] // b_ref.shape[0]
    z = f_ref[...] + jnp.concatenate([b_ref[...]] * nb, axis=0)
    x = -(jnp.maximum(-z, 0.0) + jnp.log1p(jnp.exp(-jnp.abs(z))))
    n = x.shape[-1]
    lane = lax.broadcasted_iota(jnp.int32, x.shape, 1)
    sh = 1
    while sh < n:
        x = x + jnp.where(lane >= sh, pltpu.roll(x, sh, axis=1), 0.0)
        sh *= 2
    o_ref[...] = x


def _fcum(f_t, b_forget):
    b, h, s = f_t.shape
    out = pl.pallas_call(
        _fcum_body,
        grid=(1,),
        in_specs=[_resident((b * h, s)), _resident((h, 1))],
        out_specs=pl.BlockSpec((b * h, s), lambda i: (0, 0)),
        out_shape=jax.ShapeDtypeStruct((b * h, s), F32),
        compiler_params=_params(("parallel",)),
        name="fcum",
    )(f_t.reshape(b * h, s), b_forget.reshape(h, 1))
    return out.reshape(b, h, s)


FOX_PAIR = 4
FOX_TILE = 512


def _causal_bias(t):
    i = np.arange(t)
    return jnp.asarray(np.where(i[None, :] <= i[:, None], 0.0, NEG_INF), F32)


def _fox_body(q_ref, k_ref, v_ref, c_ref, cm_ref, o_ref, s_ref, mb_ref, acc_ref):
    t = FOX_TILE
    heads = range(FOX_PAIR)
    nq = q_ref.shape[2] // t

    def keys(n):
        return slice(n * t, (n + 1) * t)

    def scores(m, hh, n):
        q = q_ref[0, hh, m * t:(m + 1) * t, :]
        return _dot_nt(q, k_ref[0, hh, keys(n), :]) - c_ref[0, hh, :, keys(n)]

    def scores_diag(m):
        for hh in heads:
            s = scores(m, hh, m) + cm_ref[...]
            s_ref[hh, m] = s
            mb_ref[m % 2, hh] = _lane_fold(s, jnp.maximum)

    def scores_far(m, n):
        for hh in heads:
            s = scores(m, hh, n)
            s_ref[hh, n] = s
            mb_ref[m % 2, hh] = jnp.maximum(mb_ref[m % 2, hh], _lane_fold(s, jnp.maximum))

    def finish_max(m):
        for hh in heads:
            mb_ref[m % 2, hh] = _row_max_tile(mb_ref[m % 2, hh])

    def weigh(m, n):
        for hh in heads:
            p = jnp.exp(s_ref[hh, n] - _lane_tile(mb_ref[m % 2, hh], t)).astype(BF16)
            acc_ref[hh] += _dot(p, v_ref[0, hh, keys(n), :])

    scores_diag(0)
    finish_max(0)
    for m in range(nq):
        ahead = m + 1 < nq
        if ahead:
            scores_diag(m + 1)
        for hh in heads:
            acc_ref[hh] = jnp.zeros((t, 2 * HEAD_DIM), F32)

        for n in range(m + 1):
            weigh(m, n)
            if ahead:
                scores_far(m + 1, n)

        if ahead:
            finish_max(m + 1)
        o_ref[0, m * t:(m + 1) * t, :] = jnp.concatenate(
            [_normalize(acc_ref[hh]) for hh in heads], axis=1).astype(BF16)


def _fox(qf, kf, vf, c):
    b, h, s, _ = qf.shape
    t = FOX_TILE
    seq = lambda w: pl.BlockSpec((1, FOX_PAIR, s, w), lambda i, p: (i, p, 0, 0))
    return pl.pallas_call(
        _fox_body,
        grid=(b, h // FOX_PAIR),
        in_specs=[seq(HEAD_DIM), seq(HEAD_DIM), seq(2 * HEAD_DIM),
                  pl.BlockSpec((1, FOX_PAIR, 1, s), lambda i, p: (i, p, 0, 0)),
                  _resident((t, t))],
        out_specs=pl.BlockSpec((1, s, FOX_PAIR * HEAD_DIM), lambda i, p: (i, 0, p)),
        out_shape=jax.ShapeDtypeStruct((b, s, h * HEAD_DIM), BF16),
        scratch_shapes=[pltpu.VMEM((FOX_PAIR, s // t, t, t), F32),
                        pltpu.VMEM((2, FOX_PAIR, t, LANES), F32),
                        pltpu.VMEM((FOX_PAIR, t, 2 * HEAD_DIM), F32)],
        compiler_params=_params(("parallel", "parallel")),
        name="fox",
    )(qf, kf, vf, c.reshape(b, h, 1, s), _causal_bias(t))


def _mixout_body(x_ref, g_ref, on_ref, of_ref, wa_ref, wb_ref, wun_ref, wuf_ref, wo_ref, o_ref):
    x = x_ref[...]
    h = _rms(x, g_ref[...]).astype(BF16)
    y = (jax.nn.sigmoid(_dot(h, wa_ref[...])) * _dot(on_ref[...], wun_ref[...])
         + jax.nn.sigmoid(_dot(h, wb_ref[...])) * _dot(of_ref[...], wuf_ref[...]))
    o_ref[...] = x + _dot(y.astype(BF16), wo_ref[...])


def _mixout(x2d, g, o_nsa, o_fox, w_a, w_b, w_un, w_uf, w_o):
    n = x2d.shape[0]
    tok = lambda c: pl.BlockSpec((TOK_TILE, c), lambda i: (i, 0))
    return pl.pallas_call(
        _mixout_body,
        grid=(n // TOK_TILE,),
        in_specs=[tok(D_MODEL), _resident((1, D_MODEL)), tok(NSA_Q), tok(FOX_W),
                  _resident(w_a.shape), _resident(w_b.shape), _resident(w_un.shape),
                  _resident(w_uf.shape), _resident(w_o.shape)],
        out_specs=tok(D_MODEL),
        out_shape=jax.ShapeDtypeStruct((n, D_MODEL), F32),
        compiler_params=_params(("parallel",)),
        name="mixout",
    )(x2d, g.reshape(1, D_MODEL), o_nsa, o_fox, w_a, w_b, w_un, w_uf, w_o)


MEM_BATCH = 4


def _memkv_body(m_ref, g_ref, w_ref, k_ref, v_ref):
    nb, ml, d = m_ref.shape
    h = _rms(m_ref[...].reshape(nb * ml, d), g_ref[...]).astype(BF16)
    z = _dot(h, w_ref[...]).astype(BF16)
    k_ref[...] = z[:, :D_MODEL].reshape(nb, ml, d)
    v_ref[...] = z[:, D_MODEL:].reshape(nb, ml, d)


def _memkv(mem, g, w_kv):
    b, ml, _ = mem.shape
    nb = MEM_BATCH if b % MEM_BATCH == 0 else 1
    blk = pl.BlockSpec((nb, ml, D_MODEL), lambda i: (i, 0, 0))
    sd = jax.ShapeDtypeStruct((b, ml, D_MODEL), BF16)
    return pl.pallas_call(
        _memkv_body,
        grid=(b // nb,),
        in_specs=[blk, _resident((1, D_MODEL)), _resident(w_kv.shape)],
        out_specs=(blk, blk),
        out_shape=(sd, sd),
        compiler_params=_params(("parallel",)),
        name="memkv",
    )(mem, g.reshape(1, D_MODEL), w_kv)


def _memattn_body(x_ref, g_ref, k_ref, v_ref, wq_ref, wo_ref, o_ref):
    x = x_ref[0]
    h = _rms(x, g_ref[...]).astype(BF16)
    q = (_dot(h, wq_ref[...]) * (MEM_HEAD_DIM ** -0.5)).astype(BF16)
    outs = []
    for hd in range(MEM_HEADS):
        cols = slice(hd * MEM_HEAD_DIM, (hd + 1) * MEM_HEAD_DIM)
        s = _dot_nt(q[:, cols], k_ref[0, :, cols])
        e = jnp.exp(s - jnp.max(s, axis=-1, keepdims=True))
        p = e / jnp.sum(e, axis=-1, keepdims=True)
        outs.append(_dot(p.astype(BF16), v_ref[0, :, cols]))
    o = jnp.concatenate(outs, axis=1).astype(BF16)
    o_ref[0] = x + _dot(o, wo_ref[...])


def _memattn(x, g, k, v, w_q, w_o):
    b, s, _ = x.shape
    ml = k.shape[1]
    tok = pl.BlockSpec((1, TOK_TILE, D_MODEL), lambda i, j: (i, j, 0))
    kvb = pl.BlockSpec((1, ml, D_MODEL), lambda i, j: (i, 0, 0))
    return pl.pallas_call(
        _memattn_body,
        grid=(b, s // TOK_TILE),
        in_specs=[tok, _resident((1, D_MODEL)), kvb, kvb, _resident(w_q.shape), _resident(w_o.shape)],
        out_specs=tok,
        out_shape=jax.ShapeDtypeStruct((b, s, D_MODEL), F32),
        compiler_params=_params(("parallel", "parallel")),
        name="memattn",
    )(x, g.reshape(1, D_MODEL), k, v, w_q, w_o)


def kernel(x, mem, rel_bias_table, ffn1_norm, ffn1_w_gate, ffn1_w_up, ffn1_w_down, mix_norm, mix_w_in, mix_b_forget, cmp_pos_k, cmp_pos_v, cmp_k_w1, cmp_k_w2, cmp_v_w1, cmp_v_w2, w_up_nsa, w_up_fox, mix_w_out, mem_q_norm, mem_kv_norm, mem_w_q, mem_w_kv, mem_w_o, ffn2_norm, ffn2_w_gate, ffn2_w_up, ffn2_w_down, final_norm):
    b, s, d = x.shape
    depth = ffn1_norm.shape[0]
    bf = lambda w: w.astype(BF16)
    toep, bias_c = _relbias(rel_bias_table, s)
    x = x.reshape(b * s, d)
    for l in range(depth):
        last = l == depth - 1
        x = _ffn(x, ffn1_norm[l], ffn1_w_gate[l], ffn1_w_up[l], ffn1_w_down[l])

        w_nsa, w_fox, w_small, w_a, w_b = _pack_w_in(mix_w_in, l)
        qn, kc, vc, ks, vs, kw, vw, qf, kf, vf, gates, flog = _inproj(
            x.reshape(b, s, d), mix_norm[l], w_nsa, w_fox, w_small)
        kcc, vcc = _compress(kc, vc, cmp_pos_k[l], cmp_pos_v[l], cmp_k_w1[l], cmp_k_w2[l],
                             cmp_v_w1[l], cmp_v_w2[l])
        o_nsa = _nsa(qn, kcc, vcc, ks, vs, kw, vw, gates, toep, bias_c)
        c = _fcum(flog, mix_b_forget[l])
        o_fox = _fox(qf, kf, vf, c)
        x = _mixout(x, mix_norm[l], o_nsa.reshape(b * s, NSA_Q), o_fox.reshape(b * s, FOX_W),
                    w_a, w_b, bf(w_up_nsa[l]), bf(w_up_fox[l]), bf(mix_w_out[l]))

        mk, mv = _memkv(mem, mem_kv_norm[l], bf(mem_w_kv[l]))
        x = _memattn(x.reshape(b, s, d), mem_q_norm[l], mk, mv, bf(mem_w_q[l]), bf(mem_w_o[l]))
        x = _ffn(x.reshape(b * s, d), ffn2_norm[l], ffn2_w_gate[l], ffn2_w_up[l],
                 ffn2_w_down[l], final_g=final_norm if last else None)
    return x.reshape(b, s, d)
```

```python
import functools
import math

import numpy as np
import jax
import jax.numpy as jnp
from jax import lax
from jax.experimental import pallas as pl
from jax.experimental.pallas import tpu as pltpu

D_MODEL = 1024
D_FF = 2816
HEAD_DIM = 64
NSA_HEADS = 8
NSA_KV_GROUPS = 2
NSA_HPG = NSA_HEADS // NSA_KV_GROUPS
CMP_BLOCK = 32
CMP_STRIDE = 16
CMP_HIDDEN = 256
SEL_BLOCK = 64
SEL_TOP_N = 16
WINDOW = 512
FOX_HEADS = 8
MEM_HEADS = 4
MEM_HEAD_DIM = D_MODEL // MEM_HEADS
NUM_BUCKETS = 32
MAX_DISTANCE = 128
RMS_EPS = 1e-6
NEG_INF = -1e30
FORCE_BONUS = 1e4

NSA_Q = NSA_HEADS * HEAD_DIM
NSA_KV = NSA_KV_GROUPS * HEAD_DIM
FOX_W = FOX_HEADS * HEAD_DIM

LANES = 128
VMEM_LIMIT = 56 * 1024 * 1024

BF16 = jnp.bfloat16
F32 = jnp.float32

ATT_TILE = 256
TOK_TILE = 1024
FFN_TILE = 1024
FF_CHUNK = 256


def _bucket_thresholds():
    n = np.arange(0, 4 * MAX_DISTANCE)
    exact = NUM_BUCKETS // 2
    large = exact + (np.log(np.maximum(n, 1) / exact) / math.log(MAX_DISTANCE / exact)
                     * (NUM_BUCKETS - exact)).astype(np.int64)
    bucket = np.where(n < exact, n, np.minimum(large, NUM_BUCKETS - 1))
    assert np.all(np.diff(bucket) >= 0)
    return [int(np.argmax(bucket >= k)) for k in range(1, NUM_BUCKETS)]


BUCKET_THRESHOLDS = _bucket_thresholds()


def _dot(a, b):
    return jnp.dot(a, b, preferred_element_type=F32)


def _dot_nt(a, b):
    return lax.dot_general(a, b, (((1,), (1,)), ((), ())), preferred_element_type=F32)


def _row_halves(fn, a, b):
    h = a.shape[0] // 2
    return jnp.concatenate([fn(a[:h], b), fn(a[h:], b)], axis=0)


def _dot_pair(a, b):
    return _row_halves(_dot, a, b)


def _dot_nt_pair(a, b):
    return _row_halves(_dot_nt, a, b)


def _rms(x, g):
    return x * lax.rsqrt(jnp.mean(x * x, axis=-1, keepdims=True) + RMS_EPS) * g


def _resident(shape):
    nd = len(shape)
    return pl.BlockSpec(shape, lambda *_: (0,) * nd, pipeline_mode=pl.Buffered(1))


def _params(sem):
    return pltpu.CompilerParams(dimension_semantics=sem, vmem_limit_bytes=VMEM_LIMIT)


def _ffn_body(x_ref, g_ref, wg_hbm, wu_hbm, wd_hbm, *rest, final):
    if final:
        fg_ref, o_ref, wg_ref, wu_ref, wd_ref, sg_ref, su_ref, sd_ref, sem = rest
    else:
        o_ref, wg_ref, wu_ref, wd_ref, sg_ref, su_ref, sd_ref, sem = rest
    n_chunks = D_FF // FF_CHUNK

    def chunk_copies(c):
        cols = slice(c * FF_CHUNK, (c + 1) * FF_CHUNK)
        slot = c % 2
        return (pltpu.make_async_copy(wg_hbm.at[:, cols], sg_ref.at[slot], sem.at[0, slot]),
                pltpu.make_async_copy(wu_hbm.at[:, cols], su_ref.at[slot], sem.at[1, slot]),
                pltpu.make_async_copy(wd_hbm.at[cols, :], sd_ref.at[slot], sem.at[2, slot]))

    @pl.when(pl.program_id(0) == 0)
    def _():
        for cp in chunk_copies(0):
            cp.start()
        for c in range(n_chunks):
            sl = slice(c * FF_CHUNK, (c + 1) * FF_CHUNK)
            if c + 1 < n_chunks:
                for cp in chunk_copies(c + 1):
                    cp.start()
            for cp in chunk_copies(c):
                cp.wait()
            wg_ref[:, sl] = sg_ref[c % 2].astype(BF16)
            wu_ref[:, sl] = su_ref[c % 2].astype(BF16)
            wd_ref[sl, :] = sd_ref[c % 2].astype(BF16)

    x = x_ref[...]
    h = _rms(x, g_ref[...]).astype(BF16)
    acc = jnp.zeros(x.shape, F32)
    for c in range(n_chunks):
        sl = slice(c * FF_CHUNK, (c + 1) * FF_CHUNK)
        a = _dot(h, wg_ref[:, sl])
        b = _dot(h, wu_ref[:, sl])
        t = (a * jax.nn.sigmoid(a)) * b
        acc = acc + _dot(t.astype(BF16), wd_ref[sl, :])
    y = x + 0.5 * acc
    if final:
        y = _rms(y, fg_ref[...])
    o_ref[...] = y


def _ffn(x2d, g, wg, wu, wd, final_g=None):
    n = x2d.shape[0]
    final = final_g is not None
    tok = pl.BlockSpec((FFN_TILE, D_MODEL), lambda i: (i, 0))
    hbm = pl.BlockSpec(memory_space=pl.ANY)
    in_specs = [tok, _resident((1, D_MODEL)), hbm, hbm, hbm]
    args = [x2d, g.reshape(1, D_MODEL), wg, wu, wd]
    if final:
        in_specs.append(_resident((1, D_MODEL)))
        args.append(final_g.reshape(1, D_MODEL))
    return pl.pallas_call(
        functools.partial(_ffn_body, final=final),
        grid=(n // FFN_TILE,),
        in_specs=in_specs,
        out_specs=tok,
        out_shape=jax.ShapeDtypeStruct((n, D_MODEL), F32),
        scratch_shapes=[pltpu.VMEM((D_MODEL, D_FF), BF16), pltpu.VMEM((D_MODEL, D_FF), BF16),
                        pltpu.VMEM((D_FF, D_MODEL), BF16),
                        pltpu.VMEM((2, D_MODEL, FF_CHUNK), F32), pltpu.VMEM((2, D_MODEL, FF_CHUNK), F32),
                        pltpu.VMEM((2, FF_CHUNK, D_MODEL), F32),
                        pltpu.SemaphoreType.DMA((3, 2))],
        compiler_params=_params(("arbitrary",)),
        name="ffn_final" if final else "ffn",
    )(*args)


_C_CMP = NSA_Q
_C_KV = _C_CMP + 2 * NSA_KV
_W_NSA = _C_KV + 4 * NSA_KV
GATE_PAD = 16
_C_FLOG = NSA_KV_GROUPS * GATE_PAD
N_CHUNKS_PAD = 128
CHUNK_W = CMP_STRIDE * NSA_KV


def _inproj_body(x_ref, g_ref, wn_ref, wf_ref, ws_ref, blk_ref, qn_ref, kc_ref, vc_ref, ks_ref, vs_ref,
                 kw_ref, vw_ref, qf_ref, kf_ref, vf_ref, gate_ref, fl_ref, cmp_ref):
    h = _rms(x_ref[0], g_ref[...]).astype(BF16)
    z = _dot(h, wn_ref[...])
    qn_ref[0] = (z[:, :_C_CMP] * (HEAD_DIM ** -0.5)).astype(BF16)
    rows = z.shape[0] // CMP_STRIDE
    for i, ref in enumerate((kc_ref, vc_ref)):
        cmp_ref[i] = z[:, _C_CMP + i * NSA_KV:_C_CMP + (i + 1) * NSA_KV]
        for r in range(CMP_STRIDE):
            tok = cmp_ref[i, pl.ds(r, rows, stride=CMP_STRIDE), :]
            ref[0, :, r * NSA_KV:(r + 1) * NSA_KV] = tok.astype(BF16)
    ones = jnp.ones((z.shape[0], HEAD_DIM), BF16)
    for i, (ref, extra) in enumerate(((ks_ref, blk_ref[...]), (vs_ref, ones), (kw_ref, None),
                                      (vw_ref, ones))):
        for g in range(NSA_KV_GROUPS):
            c0 = _C_KV + i * NSA_KV + g * HEAD_DIM
            val = z[:, c0:c0 + HEAD_DIM].astype(BF16)
            ref[0, g] = val if extra is None else jnp.concatenate([val, extra], axis=1)
    zf = _dot(h, wf_ref[...])
    for hd in range(FOX_HEADS):
        c0 = hd * HEAD_DIM
        qf_ref[0, hd] = (zf[:, c0:c0 + HEAD_DIM] * (HEAD_DIM ** -0.5)).astype(BF16)
        kf_ref[0, hd] = zf[:, c0 + FOX_W:c0 + FOX_W + HEAD_DIM].astype(BF16)
        v = zf[:, c0 + 2 * FOX_W:c0 + 2 * FOX_W + HEAD_DIM].astype(BF16)
        vf_ref[0, hd] = jnp.concatenate([v, ones], axis=1)
    zs = _dot(h, ws_ref[...])
    for g in range(NSA_KV_GROUPS):
        gate_ref[0, g] = zs[:, g * GATE_PAD:(g + 1) * GATE_PAD]
    fl_ref[0] = zs.T[_C_FLOG:_C_FLOG + FOX_HEADS]


_IN_COLS = np.cumsum((0, NSA_Q, NSA_KV, NSA_KV, NSA_KV, NSA_KV, NSA_KV, NSA_KV, 3 * NSA_HEADS,
                      FOX_W, FOX_W, FOX_W, FOX_HEADS, D_MODEL, D_MODEL)).tolist()
W_PREP_ROWS = 256


def _wprep_body(w_ref, wn_ref, wf_ref, ws_ref, wa_ref, wb_ref):
    wt = w_ref[0]
    c = _IN_COLS
    k = wt.shape[1]
    wn_ref[...] = wt[:c[7]].T.astype(BF16)
    wf_ref[...] = wt[c[8]:c[11]].T.astype(BF16)
    wa_ref[...] = wt[c[12]:c[13]].T.astype(BF16)
    wb_ref[...] = wt[c[13]:c[14]].T.astype(BF16)
    small = []
    for g in range(NSA_KV_GROUPS):
        for br in range(3):
            c0 = c[7] + br * NSA_HEADS + g * NSA_HPG
            small.append(wt[c0:c0 + NSA_HPG])
        small.append(jnp.zeros((GATE_PAD - 3 * NSA_HPG, k), F32))
    small.append(wt[c[11]:c[12]])
    small.append(jnp.zeros((LANES - _C_FLOG - FOX_HEADS, k), F32))
    ws_ref[...] = jnp.concatenate(small, axis=0).T.astype(BF16)


def _pack_w_in(w_in, layer):
    _, k, n = w_in.shape
    c = _IN_COLS
    assert n == c[14]
    widths = (c[7], c[11] - c[8], LANES, D_MODEL, D_MODEL)
    return pl.pallas_call(
        _wprep_body,
        grid=(k // W_PREP_ROWS,),
        in_specs=[pl.BlockSpec((1, n, W_PREP_ROWS), lambda i: (layer, 0, i))],
        out_specs=tuple(pl.BlockSpec((W_PREP_ROWS, w), lambda i: (i, 0)) for w in widths),
        out_shape=tuple(jax.ShapeDtypeStruct((k, w), BF16) for w in widths),
        compiler_params=_params(("parallel",)),
        name="wprep",
    )(jnp.swapaxes(w_in, 1, 2))


def _inproj(x, g, w_nsa, w_fox, w_small):
    b, s, _ = x.shape
    grid = (b, s // TOK_TILE)
    tok = lambda c: pl.BlockSpec((1, TOK_TILE, c), lambda i, j: (i, j, 0))
    heads = lambda nh, w=HEAD_DIM: pl.BlockSpec((1, nh, TOK_TILE, w), lambda i, j: (i, 0, j, 0))
    chunks = pl.BlockSpec((1, TOK_TILE // CMP_STRIDE, CHUNK_W), lambda i, j: (i, j, 0))
    sds = jax.ShapeDtypeStruct
    kv = sds((b, NSA_KV_GROUPS, s, HEAD_DIM), BF16)
    kv2 = sds((b, NSA_KV_GROUPS, s, 2 * HEAD_DIM), BF16)
    fx = sds((b, FOX_HEADS, s, HEAD_DIM), BF16)
    cmp_in = sds((b, s // CMP_STRIDE, CHUNK_W), BF16)
    out_shape = (sds((b, s, NSA_Q), BF16), cmp_in, cmp_in,
                 kv2, kv2, kv, kv2, fx, fx, sds((b, FOX_HEADS, s, 2 * HEAD_DIM), BF16),
                 sds((b, NSA_KV_GROUPS, s, GATE_PAD), F32), sds((b, FOX_HEADS, s), F32))
    out_specs = (tok(NSA_Q), chunks, chunks,
                 heads(NSA_KV_GROUPS, 2 * HEAD_DIM), heads(NSA_KV_GROUPS, 2 * HEAD_DIM),
                 heads(NSA_KV_GROUPS), heads(NSA_KV_GROUPS, 2 * HEAD_DIM),
                 heads(FOX_HEADS), heads(FOX_HEADS), heads(FOX_HEADS, 2 * HEAD_DIM),
                 pl.BlockSpec((1, NSA_KV_GROUPS, TOK_TILE, GATE_PAD), lambda i, j: (i, 0, j, 0)),
                 pl.BlockSpec((1, FOX_HEADS, TOK_TILE), lambda i, j: (i, 0, j)))
    key_blk = np.arange(s) // SEL_BLOCK
    blk_onehot = jnp.asarray(key_blk[:, None] == np.arange(HEAD_DIM)[None, :], BF16)
    return pl.pallas_call(
        _inproj_body,
        grid=grid,
        in_specs=[tok(D_MODEL), _resident((1, D_MODEL)), _resident(w_nsa.shape),
                  _resident(w_fox.shape), _resident(w_small.shape),
                  pl.BlockSpec((TOK_TILE, HEAD_DIM), lambda i, j: (j, 0))],
        out_specs=out_specs,
        out_shape=out_shape,
        scratch_shapes=[pltpu.VMEM((2, TOK_TILE, NSA_KV), F32)],
        compiler_params=_params(("parallel", "parallel")),
        name="inproj",
    )(x, g.reshape(1, D_MODEL), w_nsa, w_fox, w_small, blk_onehot)


CMP_BATCH = 4

def _compress_body(xk_ref, xv_ref, pk_ref, pv_ref, wk1_ref, wv1_ref, wk2_ref, wv2_ref, ok_ref, ov_ref):
    nb = xk_ref.shape[0]
    n = nb * N_CHUNKS_PAD
    for x_ref, p_ref, w1_ref, w2_ref, o_ref in ((xk_ref, pk_ref, wk1_ref, wk2_ref, ok_ref),
                                                (xv_ref, pv_ref, wv1_ref, wv2_ref, ov_ref)):
        x = x_ref[...].reshape(n, CHUNK_W)
        a0 = _dot(x, w1_ref[0])
        a1 = _dot(x, w1_ref[1])
        c = _dot(p_ref[0], w1_ref[0]) + _dot(p_ref[1], w1_ref[1])
        pre = a0 + pltpu.roll(a1, n - 1, axis=0) + c[0:1, :]
        hid = jax.nn.gelu(pre).astype(BF16)
        out = _dot(hid, w2_ref[...]).astype(BF16)
        for i in range(nb):
            for g in range(NSA_KV_GROUPS):
                o_ref[i, g] = out[i * N_CHUNKS_PAD:(i + 1) * N_CHUNKS_PAD, g * HEAD_DIM:(g + 1) * HEAD_DIM]


def _pack_compress(pos, w1, w2):
    r = CMP_BLOCK // CMP_STRIDE
    assert NSA_KV_GROUPS == 2
    w1r = w1.astype(BF16).reshape(r, CMP_STRIDE, HEAD_DIM, CMP_HIDDEN)
    z1 = jnp.zeros_like(w1r)
    w1big = jnp.stack([jnp.concatenate([w1r, z1], axis=-1), jnp.concatenate([z1, w1r], axis=-1)], axis=2)
    w1big = w1big.reshape(r, CHUNK_W, NSA_KV_GROUPS * CMP_HIDDEN)
    z2 = jnp.zeros_like(w2)
    w2big = jnp.concatenate([jnp.concatenate([w2, z2], axis=1), jnp.concatenate([z2, w2], axis=1)], axis=0)
    p = pos.reshape(r, CMP_STRIDE, 1, HEAD_DIM)
    p = jnp.broadcast_to(p, (r, CMP_STRIDE, NSA_KV_GROUPS, HEAD_DIM)).reshape(r, 1, CHUNK_W)
    p = jnp.broadcast_to(p, (r, 16, CHUNK_W))
    return p.astype(BF16), w1big.astype(BF16), w2big.astype(BF16)


def _compress(xk, xv, pos_k, pos_v, k_w1, k_w2, v_w1, v_w2):
    b = xk.shape[0]
    assert xk.shape[1:] == (N_CHUNKS_PAD, CHUNK_W)
    pk, wk1, wk2 = _pack_compress(pos_k, k_w1, k_w2)
    pv, wv1, wv2 = _pack_compress(pos_v, v_w1, v_w2)
    nb = CMP_BATCH if b % CMP_BATCH == 0 else 1
    xs = pl.BlockSpec((nb, N_CHUNKS_PAD, CHUNK_W), lambda i: (i, 0, 0))
    os_ = pl.BlockSpec((nb, NSA_KV_GROUPS, N_CHUNKS_PAD, HEAD_DIM), lambda i: (i, 0, 0, 0))
    osd = jax.ShapeDtypeStruct((b, NSA_KV_GROUPS, N_CHUNKS_PAD, HEAD_DIM), BF16)
    return pl.pallas_call(
        _compress_body,
        grid=(b // nb,),
        in_specs=[xs, xs, _resident(pk.shape), _resident(pv.shape), _resident(wk1.shape),
                  _resident(wv1.shape), _resident(wk2.shape), _resident(wv2.shape)],
        out_specs=(os_, os_),
        out_shape=(osd, osd),
        compiler_params=_params(("parallel",)),
        name="compress",
    )(xk, xv, pk, pv, wk1, wv1, wk2, wv2)


def _bias_lookup(tbl_ref, head, dist):
    acc = jnp.full(dist.shape, tbl_ref[0, head], F32)
    for k, thr in enumerate(BUCKET_THRESHOLDS):
        acc = jnp.where(dist >= thr, tbl_ref[k + 1, head], acc)
    return acc - tbl_ref[NUM_BUCKETS - 1, head]


def _relbias_body(tbl_ref, toep_ref, bc_ref):
    head = pl.program_id(0)
    t = ATT_TILE
    i = lax.broadcasted_iota(jnp.int32, (t, 2 * t), 0)
    u = lax.broadcasted_iota(jnp.int32, (t, 2 * t), 1)
    d = i - u + t
    toep_ref[0] = jnp.where(d >= 0, _bias_lookup(tbl_ref, head, d), NEG_INF)
    n_tiles = bc_ref.shape[1] // t
    per_tile = t // CMP_STRIDE
    off = per_tile * (n_tiles - 1)
    width = 2 * N_CHUNKS_PAD
    assert off + N_CHUNKS_PAD <= width
    i = lax.broadcasted_iota(jnp.int32, (t, width), 0)
    u = lax.broadcasted_iota(jnp.int32, (t, width), 1)
    dc = i - ((u - off) * CMP_STRIDE + CMP_BLOCK - 1)
    strip = jnp.where(dc >= 0, _bias_lookup(tbl_ref, head, dc), NEG_INF)
    for q in range(n_tiles):
        lo = off - q * per_tile
        bc_ref[0, q * t:(q + 1) * t, :] = strip[:, lo:lo + N_CHUNKS_PAD]


def _relbias(tbl, s):
    return pl.pallas_call(
        _relbias_body,
        grid=(NSA_HEADS,),
        in_specs=[pl.BlockSpec(memory_space=pltpu.SMEM)],
        out_specs=(pl.BlockSpec((1, ATT_TILE, 2 * ATT_TILE), lambda h: (h, 0, 0)),
                   pl.BlockSpec((1, s, N_CHUNKS_PAD), lambda h: (h, 0, 0))),
        out_shape=(jax.ShapeDtypeStruct((NSA_HEADS, ATT_TILE, 2 * ATT_TILE), F32),
                   jax.ShapeDtypeStruct((NSA_HEADS, s, N_CHUNKS_PAD), F32)),
        compiler_params=_params(("parallel",)),
        name="relbias",
    )(tbl)


def _lane_fold(x, op):
    return functools.reduce(op, [x[:, i * LANES:(i + 1) * LANES] for i in range(x.shape[1] // LANES)])


def _lane_tile(x, width):
    return jnp.concatenate([x] * (width // LANES), axis=1)


def _row_max_tile(rmax):
    return jnp.broadcast_to(jnp.max(rmax, axis=-1, keepdims=True), rmax.shape)


def _normalize(acc):
    return (acc / pltpu.roll(acc, HEAD_DIM, axis=1))[:, :HEAD_DIM]


def _split3(x):
    hi = x.astype(BF16)
    r = x - hi.astype(F32)
    mid = r.astype(BF16)
    lo = (r - mid.astype(F32)).astype(BF16)
    return hi, mid, lo


def _nsa_body(q_ref, kc_ref, vc_ref, ks_ref, vs_ref, kw_ref, vw_ref, gate_a_ref, gate_b_ref, toep_ref,
              bc_ref, ov_ref, o_ref, s_ref, near_ref, mb_ref, acc_ref, part_ref, *, n_tiles):
    t = ATT_TILE
    j4 = NSA_HPG
    rows = j4 * t
    groups = range(NSA_KV_GROUPS)
    step = pl.program_id(1)
    n_sel = n_tiles * (t // SEL_BLOCK)

    def heads(x):
        return x.reshape(j4, t, x.shape[-1])

    def flat(x):
        return x.reshape(rows, x.shape[-1])

    def key_tile(ref, g, n):
        if isinstance(n, int):
            return ref[0, g, n * t:(n + 1) * t, :]
        return ref[0, g, pl.ds(pl.multiple_of(n * t, t), t), :]

    def gated(gate_ref, g, branch, o):
        gate = jax.nn.sigmoid(gate_ref[0, g])
        return [gate[:, branch * j4 + j:branch * j4 + j + 1] * o[j * t:(j + 1) * t] for j in range(j4)]

    def probs(par, g, s):
        return jnp.exp(s - _lane_tile(mb_ref[par, g], t)).astype(BF16)

    def open_tile(m, par):
        has_prev = m >= 1 if isinstance(m, int) else True
        has_far = m >= 2 if isinstance(m, int) else True
        qs = []
        for g in groups:
            qt = q_ref[0, :, g * j4 * HEAD_DIM:(g + 1) * j4 * HEAD_DIM]
            qs.append(jnp.concatenate([qt[:, j * HEAD_DIM:(j + 1) * HEAD_DIM] for j in range(j4)], axis=0))

        def biases(g):
            toep = toep_ref[g * j4:(g + 1) * j4]
            return toep[:, :, t:], toep[:, :, :t]

        cmp_out = [_nsa_compressed(qs[g], kc_ref[0, g], vc_ref[0, g], bc_ref[g * j4:(g + 1) * j4],
                                   ov_ref[...], m, n_sel) for g in groups]
        part = []
        for g in groups:
            diag_bias, prev_bias = biases(g)
            tiles = [(m, flat(heads(_dot_nt_pair(qs[g], key_tile(kw_ref, g, m))) + diag_bias))]
            if has_prev:
                tiles.append((m - 1, flat(heads(_dot_nt_pair(qs[g], key_tile(kw_ref, g, m - 1))) + prev_bias)))
            if has_far:
                ii = lax.broadcasted_iota(jnp.int32, (t, t), 0)
                jj = lax.broadcasted_iota(jnp.int32, (t, t), 1)
                tri = jnp.where(jj > ii, 0.0, NEG_INF)
                tiles.append((m - 2, flat(heads(_dot_nt_pair(qs[g], key_tile(kw_ref, g, m - 2))) + tri[None])))
            wmax = _lane_tile(_row_max_tile(functools.reduce(
                jnp.maximum, [_lane_fold(w, jnp.maximum) for _, w in tiles])), t)
            acc = sum(_dot_pair(jnp.exp(w - wmax).astype(BF16), key_tile(vw_ref, g, n)) for n, w in tiles)
            part += [a + b for a, b in zip(gated(gate_a_ref, g, 0, cmp_out[g][0]),
                                           gated(gate_a_ref, g, 2, _normalize(acc)))]
        part_ref[par] = jnp.concatenate(part, axis=1)

        q_aug = []
        for g in groups:
            diag_bias, prev_bias = biases(g)
            q_aug.append(jnp.concatenate([qs[g], jnp.concatenate([cmp_out[g][1]] * j4, axis=0)], axis=1))
            s0 = flat(heads(_dot_nt_pair(q_aug[g], key_tile(ks_ref, g, m))) + diag_bias)
            near_ref[par, g, 0] = s0
            rmax = _lane_fold(s0, jnp.maximum)
            if has_prev:
                s1 = flat(heads(_dot_nt_pair(q_aug[g], key_tile(ks_ref, g, m - 1))) + prev_bias)
                near_ref[par, g, 1] = s1
                rmax = jnp.maximum(rmax, _lane_fold(s1, jnp.maximum))
            if has_far:
                s2 = _dot_nt_pair(q_aug[g], key_tile(ks_ref, g, m - 2))
                s_ref[g, m - 2] = s2
                rmax = jnp.maximum(rmax, _lane_fold(s2, jnp.maximum))
            mb_ref[par, g] = rmax
        return q_aug

    def open_far(par, q_aug, n):
        for g in groups:
            s = _dot_nt_pair(q_aug[g], key_tile(ks_ref, g, n))
            s_ref[g, n] = s
            mb_ref[par, g] = jnp.maximum(mb_ref[par, g], _lane_fold(s, jnp.maximum))

    def close_tile(m, par):
        for g in groups:
            mb_ref[par, g] = _row_max_tile(mb_ref[par, g])
            acc = _dot_pair(probs(par, g, near_ref[par, g, 0]), key_tile(vs_ref, g, m))
            if not (isinstance(m, int) and m == 0):
                acc = acc + _dot_pair(probs(par, g, near_ref[par, g, 1]), key_tile(vs_ref, g, m - 1))
            acc_ref[par, g] = acc

    def close_far(par, n):
        for g in groups:
            acc_ref[par, g] += _dot_pair(probs(par, g, s_ref[g, n]), key_tile(vs_ref, g, n))

    def finish(par):
        outs = []
        for g in groups:
            outs += gated(gate_b_ref, g, 1, _normalize(acc_ref[par, g]))
        o_ref[0] = (part_ref[par] + jnp.concatenate(outs, axis=1)).astype(BF16)

    @pl.when(step == 0)
    def _():
        open_tile(0, 0)

    @pl.when(step == 1)
    def _():
        open_tile(1, 1)
        close_tile(0, 0)

    @pl.when(jnp.logical_and(step >= 2, step < n_tiles))
    def _():
        par = step % 2
        finish(par)
        q_aug = open_tile(step, par)
        close_tile(step - 1, 1 - par)

        @pl.loop(0, step - 2)
        def _(n):
            close_far(1 - par, n)
            open_far(par, q_aug, n)

    last = (n_tiles - 1) % 2

    @pl.when(step == n_tiles)
    def _():
        finish(1 - last)
        close_tile(step - 1, last)

        @pl.loop(0, step - 2)
        def _(n):
            close_far(last, n)

    @pl.when(step == n_tiles + 1)
    def _():
        finish(last)


def _nsa_compressed(q, kc, vc, bc, ov, m, n_sel):
    t = ATT_TILE
    j4 = NSA_HPG

    def heads(x):
        return x.reshape(j4, t, x.shape[-1])

    def flat(x):
        return x.reshape(j4 * t, x.shape[-1])

    sc = heads(_dot_nt_pair(q, kc)) + bc
    valid = bc > 0.5 * NEG_INF
    e = jnp.where(valid, jnp.exp(sc - jnp.max(sc, axis=-1, keepdims=True)), 0.0)
    l = jnp.sum(e, axis=-1, keepdims=True)
    pc = jnp.where(l > 0.0, e / l, 0.0)
    o_c = _dot_pair(flat(pc).astype(BF16), vc)
    if isinstance(m, int) and (m + 1) * t <= SEL_TOP_N * SEL_BLOCK:
        return o_c, jnp.zeros((t, HEAD_DIM), BF16)
    psum = pc[0] + pc[1] + pc[2] + pc[3]
    imp = sum(_dot(part, ov) for part in _split3(psum)).T[:n_sel]
    blk = lax.broadcasted_iota(jnp.int32, (n_sel, t), 0)
    tpos = m * t + lax.broadcasted_iota(jnp.int32, (n_sel, t), 1)
    cur = tpos // SEL_BLOCK
    bonus = jnp.where(blk == 0, FORCE_BONUS,
                      jnp.where(blk == cur, FORCE_BONUS, jnp.where(blk == cur - 1, FORCE_BONUS, 0.0)))
    imp = jnp.where(blk * SEL_BLOCK <= tpos, imp + bonus, NEG_INF)
    lanes = 4
    partial = [jnp.zeros((n_sel, t), F32) for _ in range(lanes)]
    for i in range(n_sel):
        row = imp[i:i + 1, :]
        before = jnp.where(blk > i, 1.0, 0.0)
        partial[i % lanes] = partial[i % lanes] + jnp.where(row > imp, 1.0,
                                                            jnp.where(row == imp, before, 0.0))
    rank = (partial[0] + partial[1]) + (partial[2] + partial[3])
    selb = jnp.where(rank < float(SEL_TOP_N), 0.0, NEG_INF)
    selb = jnp.concatenate([selb, jnp.zeros((LANES - n_sel, t), F32)], axis=0).T
    return o_c, selb[:, :HEAD_DIM].astype(BF16)


def _overlap_table(s):
    n_sel = s // SEL_BLOCK
    nc = N_CHUNKS_PAD
    c0 = np.arange(nc)[:, None] * CMP_STRIDE
    s0 = np.arange(n_sel)[None, :] * SEL_BLOCK
    ov = np.clip(np.minimum(c0 + CMP_BLOCK, s0 + SEL_BLOCK) - np.maximum(c0, s0), 0, None) / CMP_STRIDE
    ov[nc - 1] = 0.0
    return jnp.asarray(np.pad(ov, ((0, 0), (0, LANES - n_sel))), BF16)


def _nsa(qn, kc, vc, ks, vs, kw, vw, gates, toep, bc):
    b, s, _ = qn.shape
    t = ATT_TILE
    assert s // SEL_BLOCK <= HEAD_DIM
    ov = _overlap_table(s)
    rows = NSA_HPG * t
    ng = NSA_KV_GROUPS
    nq = s // t
    assert nq >= 2
    seq = lambda n, w=HEAD_DIM: pl.BlockSpec((1, ng, n, w), lambda i, j: (i, 0, 0, 0))
    opened = lambda j: jnp.minimum(j, nq - 1)
    closed = lambda j: jnp.maximum(j - 2, 0)
    return pl.pallas_call(
        functools.partial(_nsa_body, n_tiles=nq),
        grid=(b, nq + 2),
        in_specs=[pl.BlockSpec((1, t, NSA_Q), lambda i, j: (i, opened(j), 0)),
                  seq(N_CHUNKS_PAD), seq(N_CHUNKS_PAD),
                  seq(s, 2 * HEAD_DIM), seq(s, 2 * HEAD_DIM), seq(s), seq(s, 2 * HEAD_DIM),
                  pl.BlockSpec((1, ng, t, GATE_PAD), lambda i, j: (i, 0, opened(j), 0)),
                  pl.BlockSpec((1, ng, t, GATE_PAD), lambda i, j: (i, 0, closed(j), 0)),
                  _resident((NSA_HEADS, t, 2 * t)),
                  pl.BlockSpec((NSA_HEADS, t, N_CHUNKS_PAD), lambda i, j: (0, opened(j), 0)),
                  _resident(ov.shape)],
        out_specs=pl.BlockSpec((1, t, NSA_Q), lambda i, j: (i, closed(j), 0)),
        out_shape=jax.ShapeDtypeStruct((b, s, NSA_Q), BF16),
        scratch_shapes=[pltpu.VMEM((ng, nq - 2, rows, t), F32),
                        pltpu.VMEM((2, ng, 2, rows, t), F32),
                        pltpu.VMEM((2, ng, rows, LANES), F32),
                        pltpu.VMEM((2, ng, rows, 2 * HEAD_DIM), F32),
                        pltpu.VMEM((2, t, NSA_Q), F32)],
        compiler_params=_params(("parallel", "arbitrary")),
        name="nsa",
    )(qn, kc, vc, ks, vs, kw, vw, gates, gates, toep, bc, ov)


def _fcum_body(f_ref, b_ref, o_ref):
    nb = f_ref.shape[0] // b_ref.shape[0]
    z = f_ref[...] + jnp.concatenate([b_ref[...]] * nb, axis=0)
    x = -(jnp.maximum(-z, 0.0) + jnp.log1p(jnp.exp(-jnp.abs(z))))
    n = x.shape[-1]
    lane = lax.broadcasted_iota(jnp.int32, x.shape, 1)
    sh = 1
    while sh < n:
        x = x + jnp.where(lane >= sh, pltpu.roll(x, sh, axis=1), 0.0)
        sh *= 2
    o_ref[...] = x


def _fcum(f_t, b_forget):
    b, h, s = f_t.shape
    out = pl.pallas_call(
        _fcum_body,
        grid=(1,),
        in_specs=[_resident((b * h, s)), _resident((h, 1))],
        out_specs=pl.BlockSpec((b * h, s), lambda i: (0, 0)),
        out_shape=jax.ShapeDtypeStruct((b * h, s), F32),
        compiler_params=_params(("parallel",)),
        name="fcum",
    )(f_t.reshape(b * h, s), b_forget.reshape(h, 1))
    return out.reshape(b, h, s)


FOX_PAIR = 4
assert FOX_HEADS % FOX_PAIR == 0 and (FOX_PAIR * HEAD_DIM) % LANES == 0
FOX_TILE = 512


def _causal_bias(t):
    i = np.arange(t)
    return jnp.asarray(np.where(i[None, :] <= i[:, None], 0.0, NEG_INF), F32)


def _fox_body(q_ref, k_ref, v_ref, c_ref, cm_ref, o_ref, s_ref, mb_ref, acc_ref):
    t = FOX_TILE
    heads = range(FOX_PAIR)
    nq = q_ref.shape[2] // t

    def keys(n):
        return slice(n * t, (n + 1) * t)

    def scores(m, hh, n):
        q = q_ref[0, hh, m * t:(m + 1) * t, :]
        return _dot_nt(q, k_ref[0, hh, keys(n), :]) - c_ref[0, hh, :, keys(n)]

    def scores_diag(m):
        for hh in heads:
            s = scores(m, hh, m) + cm_ref[...]
            s_ref[hh, m] = s
            mb_ref[m % 2, hh] = _lane_fold(s, jnp.maximum)

    def scores_far(m, n):
        for hh in heads:
            s = scores(m, hh, n)
            s_ref[hh, n] = s
            mb_ref[m % 2, hh] = jnp.maximum(mb_ref[m % 2, hh], _lane_fold(s, jnp.maximum))

    def finish_max(m):
        for hh in heads:
            mb_ref[m % 2, hh] = _row_max_tile(mb_ref[m % 2, hh])

    def weigh(m, n):
        for hh in heads:
            p = jnp.exp(s_ref[hh, n] - _lane_tile(mb_ref[m % 2, hh], t)).astype(BF16)
            acc_ref[hh] += _dot(p, v_ref[0, hh, keys(n), :])

    scores_diag(0)
    finish_max(0)
    for m in range(nq):
        ahead = m + 1 < nq
        if ahead:
            scores_diag(m + 1)
        for hh in heads:
            acc_ref[hh] = jnp.zeros((t, 2 * HEAD_DIM), F32)

        for n in range(m + 1):
            weigh(m, n)
            if ahead:
                scores_far(m + 1, n)

        if ahead:
            finish_max(m + 1)
        o_ref[0, m * t:(m + 1) * t, :] = jnp.concatenate(
            [_normalize(acc_ref[hh]) for hh in heads], axis=1).astype(BF16)


def _fox(qf, kf, vf, c):
    b, h, s, _ = qf.shape
    t = FOX_TILE
    seq = lambda w: pl.BlockSpec((1, FOX_PAIR, s, w), lambda i, p: (i, p, 0, 0))
    return pl.pallas_call(
        _fox_body,
        grid=(b, h // FOX_PAIR),
        in_specs=[seq(HEAD_DIM), seq(HEAD_DIM), seq(2 * HEAD_DIM),
                  pl.BlockSpec((1, FOX_PAIR, 1, s), lambda i, p: (i, p, 0, 0)),
                  _resident((t, t))],
        out_specs=pl.BlockSpec((1, s, FOX_PAIR * HEAD_DIM), lambda i, p: (i, 0, p)),
        out_shape=jax.ShapeDtypeStruct((b, s, h * HEAD_DIM), BF16),
        scratch_shapes=[pltpu.VMEM((FOX_PAIR, s // t, t, t), F32),
                        pltpu.VMEM((2, FOX_PAIR, t, LANES), F32),
                        pltpu.VMEM((FOX_PAIR, t, 2 * HEAD_DIM), F32)],
        compiler_params=_params(("parallel", "parallel")),
        name="fox",
    )(qf, kf, vf, c.reshape(b, h, 1, s), _causal_bias(t))


def _mixout_body(x_ref, g_ref, on_ref, of_ref, wa_ref, wb_ref, wun_ref, wuf_ref, wo_ref, o_ref):
    x = x_ref[...]
    h = _rms(x, g_ref[...]).astype(BF16)
    y = (jax.nn.sigmoid(_dot(h, wa_ref[...])) * _dot(on_ref[...], wun_ref[...])
         + jax.nn.sigmoid(_dot(h, wb_ref[...])) * _dot(of_ref[...], wuf_ref[...]))
    o_ref[...] = x + _dot(y.astype(BF16), wo_ref[...])


def _mixout(x2d, g, o_nsa, o_fox, w_a, w_b, w_un, w_uf, w_o):
    n = x2d.shape[0]
    tok = lambda c: pl.BlockSpec((TOK_TILE, c), lambda i: (i, 0))
    return pl.pallas_call(
        _mixout_body,
        grid=(n // TOK_TILE,),
        in_specs=[tok(D_MODEL), _resident((1, D_MODEL)), tok(NSA_Q), tok(FOX_W),
                  _resident(w_a.shape), _resident(w_b.shape), _resident(w_un.shape),
                  _resident(w_uf.shape), _resident(w_o.shape)],
        out_specs=tok(D_MODEL),
        out_shape=jax.ShapeDtypeStruct((n, D_MODEL), F32),
        compiler_params=_params(("parallel",)),
        name="mixout",
    )(x2d, g.reshape(1, D_MODEL), o_nsa, o_fox, w_a, w_b, w_un, w_uf, w_o)


MEM_BATCH = 4


def _memkv_body(m_ref, g_ref, w_ref, k_ref, v_ref):
    nb, ml, d = m_ref.shape
    h = _rms(m_ref[...].reshape(nb * ml, d), g_ref[...]).astype(BF16)
    z = _dot(h, w_ref[...]).astype(BF16)
    k_ref[...] = z[:, :D_MODEL].reshape(nb, ml, d)
    v_ref[...] = z[:, D_MODEL:].reshape(nb, ml, d)


def _memkv(mem, g, w_kv):
    b, ml, _ = mem.shape
    nb = MEM_BATCH if b % MEM_BATCH == 0 else 1
    blk = pl.BlockSpec((nb, ml, D_MODEL), lambda i: (i, 0, 0))
    sd = jax.ShapeDtypeStruct((b, ml, D_MODEL), BF16)
    return pl.pallas_call(
        _memkv_body,
        grid=(b // nb,),
        in_specs=[blk, _resident((1, D_MODEL)), _resident(w_kv.shape)],
        out_specs=(blk, blk),
        out_shape=(sd, sd),
        compiler_params=_params(("parallel",)),
        name="memkv",
    )(mem, g.reshape(1, D_MODEL), w_kv)


def _memattn_body(x_ref, g_ref, k_ref, v_ref, wq_ref, wo_ref, o_ref):
    x = x_ref[0]
    h = _rms(x, g_ref[...]).astype(BF16)
    q = (_dot(h, wq_ref[...]) * (MEM_HEAD_DIM ** -0.5)).astype(BF16)
    outs = []
    for hd in range(MEM_HEADS):
        cols = slice(hd * MEM_HEAD_DIM, (hd + 1) * MEM_HEAD_DIM)
        s = _dot_nt(q[:, cols], k_ref[0, :, cols])
        e = jnp.exp(s - jnp.max(s, axis=-1, keepdims=True))
        p = e / jnp.sum(e, axis=-1, keepdims=True)
        outs.append(_dot(p.astype(BF16), v_ref[0, :, cols]))
    o = jnp.concatenate(outs, axis=1).astype(BF16)
    o_ref[0] = x + _dot(o, wo_ref[...])


def _memattn(x, g, k, v, w_q, w_o):
    b, s, _ = x.shape
    ml = k.shape[1]
    tok = pl.BlockSpec((1, TOK_TILE, D_MODEL), lambda i, j: (i, j, 0))
    kvb = pl.BlockSpec((1, ml, D_MODEL), lambda i, j: (i, 0, 0))
    return pl.pallas_call(
        _memattn_body,
        grid=(b, s // TOK_TILE),
        in_specs=[tok, _resident((1, D_MODEL)), kvb, kvb, _resident(w_q.shape), _resident(w_o.shape)],
        out_specs=tok,
        out_shape=jax.ShapeDtypeStruct((b, s, D_MODEL), F32),
        compiler_params=_params(("parallel", "parallel")),
        name="memattn",
    )(x, g.reshape(1, D_MODEL), k, v, w_q, w_o)


def kernel(x, mem, rel_bias_table, ffn1_norm, ffn1_w_gate, ffn1_w_up, ffn1_w_down, mix_norm, mix_w_in, mix_b_forget, cmp_pos_k, cmp_pos_v, cmp_k_w1, cmp_k_w2, cmp_v_w1, cmp_v_w2, w_up_nsa, w_up_fox, mix_w_out, mem_q_norm, mem_kv_norm, mem_w_q, mem_w_kv, mem_w_o, ffn2_norm, ffn2_w_gate, ffn2_w_up, ffn2_w_down, final_norm):
    b, s, d = x.shape
    depth = ffn1_norm.shape[0]
    bf = lambda w: w.astype(BF16)
    toep, bias_c = _relbias(rel_bias_table, s)
    x = x.reshape(b * s, d)
    for l in range(depth):
        last = l == depth - 1
        x = _ffn(x, ffn1_norm[l], ffn1_w_gate[l], ffn1_w_up[l], ffn1_w_down[l])

        w_nsa, w_fox, w_small, w_a, w_b = _pack_w_in(mix_w_in, l)
        qn, kc, vc, ks, vs, kw, vw, qf, kf, vf, gates, flog = _inproj(
            x.reshape(b, s, d), mix_norm[l], w_nsa, w_fox, w_small)
        kcc, vcc = _compress(kc, vc, cmp_pos_k[l], cmp_pos_v[l], cmp_k_w1[l], cmp_k_w2[l],
                             cmp_v_w1[l], cmp_v_w2[l])
        o_nsa = _nsa(qn, kcc, vcc, ks, vs, kw, vw, gates, toep, bias_c)
        c = _fcum(flog, mix_b_forget[l])
        o_fox = _fox(qf, kf, vf, c)
        x = _mixout(x, mix_norm[l], o_nsa.reshape(b * s, NSA_Q), o_fox.reshape(b * s, FOX_W),
                    w_a, w_b, bf(w_up_nsa[l]), bf(w_up_fox[l]), bf(mix_w_out[l]))

        mk, mv = _memkv(mem, mem_kv_norm[l], bf(mem_w_kv[l]))
        x = _memattn(x.reshape(b, s, d), mem_q_norm[l], mk, mv, bf(mem_w_q[l]), bf(mem_w_o[l]))
        x = _ffn(x.reshape(b * s, d), ffn2_norm[l], ffn2_w_gate[l], ffn2_w_up[l],
                 ffn2_w_down[l], final_g=final_norm if last else None)
    return x.reshape(b, s, d)
```

```python
import functools
import math

import numpy as np
import jax
import jax.numpy as jnp
from jax import lax
from jax.experimental import pallas as pl
from jax.experimental.pallas import tpu as pltpu

D_MODEL = 1024
D_FF = 2816
HEAD_DIM = 64
NSA_HEADS = 8
NSA_KV_GROUPS = 2
NSA_HPG = NSA_HEADS // NSA_KV_GROUPS
CMP_BLOCK = 32
CMP_STRIDE = 16
CMP_HIDDEN = 256
SEL_BLOCK = 64
SEL_TOP_N = 16
WINDOW = 512
FOX_HEADS = 8
MEM_HEADS = 4
MEM_HEAD_DIM = D_MODEL // MEM_HEADS
NUM_BUCKETS = 32
MAX_DISTANCE = 128
RMS_EPS = 1e-6
NEG_INF = -1e30
FORCE_BONUS = 1e4

NSA_Q = NSA_HEADS * HEAD_DIM
NSA_KV = NSA_KV_GROUPS * HEAD_DIM
FOX_W = FOX_HEADS * HEAD_DIM

LANES = 128
VMEM_LIMIT = 56 * 1024 * 1024

BF16 = jnp.bfloat16
F32 = jnp.float32

ATT_TILE = 256
TOK_TILE = 1024
FFN_TILE = 1024
FF_CHUNK = 256


def _bucket_thresholds():
    n = np.arange(0, 4 * MAX_DISTANCE)
    exact = NUM_BUCKETS // 2
    large = exact + (np.log(np.maximum(n, 1) / exact) / math.log(MAX_DISTANCE / exact)
                     * (NUM_BUCKETS - exact)).astype(np.int64)
    bucket = np.where(n < exact, n, np.minimum(large, NUM_BUCKETS - 1))
    assert np.all(np.diff(bucket) >= 0)
    return [int(np.argmax(bucket >= k)) for k in range(1, NUM_BUCKETS)]


BUCKET_THRESHOLDS = _bucket_thresholds()


def _dot(a, b):
    return jnp.dot(a, b, preferred_element_type=F32)


def _dot_nt(a, b):
    return lax.dot_general(a, b, (((1,), (1,)), ((), ())), preferred_element_type=F32)


def _row_halves(fn, a, b):
    h = a.shape[0] // 2
    return jnp.concatenate([fn(a[:h], b), fn(a[h:], b)], axis=0)


def _dot_pair(a, b):
    return _row_halves(_dot, a, b)


def _dot_nt_pair(a, b):
    return _row_halves(_dot_nt, a, b)


def _rms(x, g):
    return x * lax.rsqrt(jnp.mean(x * x, axis=-1, keepdims=True) + RMS_EPS) * g


def _resident(shape):
    nd = len(shape)
    return pl.BlockSpec(shape, lambda *_: (0,) * nd, pipeline_mode=pl.Buffered(1))


def _params(sem):
    return pltpu.CompilerParams(dimension_semantics=sem, vmem_limit_bytes=VMEM_LIMIT)


def _ffn_body(x_ref, g_ref, wg_hbm, wu_hbm, wd_hbm, *rest, final):
    if final:
        fg_ref, o_ref, wg_ref, wu_ref, wd_ref, sg_ref, su_ref, sd_ref, sem = rest
    else:
        o_ref, wg_ref, wu_ref, wd_ref, sg_ref, su_ref, sd_ref, sem = rest
    n_chunks = D_FF // FF_CHUNK

    def chunk_copies(c):
        cols = slice(c * FF_CHUNK, (c + 1) * FF_CHUNK)
        slot = c % 2
        return (pltpu.make_async_copy(wg_hbm.at[:, cols], sg_ref.at[slot], sem.at[0, slot]),
                pltpu.make_async_copy(wu_hbm.at[:, cols], su_ref.at[slot], sem.at[1, slot]),
                pltpu.make_async_copy(wd_hbm.at[cols, :], sd_ref.at[slot], sem.at[2, slot]))

    @pl.when(pl.program_id(0) == 0)
    def _():
        for cp in chunk_copies(0):
            cp.start()
        for c in range(n_chunks):
            sl = slice(c * FF_CHUNK, (c + 1) * FF_CHUNK)
            if c + 1 < n_chunks:
                for cp in chunk_copies(c + 1):
                    cp.start()
            for cp in chunk_copies(c):
                cp.wait()
            wg_ref[:, sl] = sg_ref[c % 2].astype(BF16)
            wu_ref[:, sl] = su_ref[c % 2].astype(BF16)
            wd_ref[sl, :] = sd_ref[c % 2].astype(BF16)

    x = x_ref[...]
    h = _rms(x, g_ref[...]).astype(BF16)
    acc = jnp.zeros(x.shape, F32)
    for c in range(n_chunks):
        sl = slice(c * FF_CHUNK, (c + 1) * FF_CHUNK)
        a = _dot(h, wg_ref[:, sl])
        b = _dot(h, wu_ref[:, sl])
        t = (a * jax.nn.sigmoid(a)) * b
        acc = acc + _dot(t.astype(BF16), wd_ref[sl, :])
    y = x + 0.5 * acc
    if final:
        y = _rms(y, fg_ref[...])
    o_ref[...] = y


def _ffn(x2d, g, wg, wu, wd, final_g=None):
    n = x2d.shape[0]
    final = final_g is not None
    tok = pl.BlockSpec((FFN_TILE, D_MODEL), lambda i: (i, 0))
    hbm = pl.BlockSpec(memory_space=pl.ANY)
    in_specs = [tok, _resident((1, D_MODEL)), hbm, hbm, hbm]
    args = [x2d, g.reshape(1, D_MODEL), wg, wu, wd]
    if final:
        in_specs.append(_resident((1, D_MODEL)))
        args.append(final_g.reshape(1, D_MODEL))
    return pl.pallas_call(
        functools.partial(_ffn_body, final=final),
        grid=(n // FFN_TILE,),
        in_specs=in_specs,
        out_specs=tok,
        out_shape=jax.ShapeDtypeStruct((n, D_MODEL), F32),
        scratch_shapes=[pltpu.VMEM((D_MODEL, D_FF), BF16), pltpu.VMEM((D_MODEL, D_FF), BF16),
                        pltpu.VMEM((D_FF, D_MODEL), BF16),
                        pltpu.VMEM((2, D_MODEL, FF_CHUNK), F32), pltpu.VMEM((2, D_MODEL, FF_CHUNK), F32),
                        pltpu.VMEM((2, FF_CHUNK, D_MODEL), F32),
                        pltpu.SemaphoreType.DMA((3, 2))],
        compiler_params=_params(("arbitrary",)),
        name="ffn_final" if final else "ffn",
    )(*args)


_C_CMP = NSA_Q
_C_KV = _C_CMP + 2 * NSA_KV
_W_NSA = _C_KV + 4 * NSA_KV
GATE_PAD = 16
_C_FLOG = NSA_KV_GROUPS * GATE_PAD
N_CHUNKS_PAD = 128
CHUNK_W = CMP_STRIDE * NSA_KV


def _inproj_body(x_ref, g_ref, wn_ref, wf_ref, ws_ref, blk_ref, qn_ref, kc_ref, vc_ref, ks_ref, vs_ref,
                 kw_ref, vw_ref, qf_ref, kf_ref, vf_ref, gate_ref, fl_ref, cmp_ref):
    h = _rms(x_ref[0], g_ref[...]).astype(BF16)
    z = _dot(h, wn_ref[...])
    qn_ref[0] = (z[:, :_C_CMP] * (HEAD_DIM ** -0.5)).astype(BF16)
    rows = z.shape[0] // CMP_STRIDE
    for i, ref in enumerate((kc_ref, vc_ref)):
        cmp_ref[i] = z[:, _C_CMP + i * NSA_KV:_C_CMP + (i + 1) * NSA_KV]
        for r in range(CMP_STRIDE):
            tok = cmp_ref[i, pl.ds(r, rows, stride=CMP_STRIDE), :]
            ref[0, :, r * NSA_KV:(r + 1) * NSA_KV] = tok.astype(BF16)
    ones = jnp.ones((z.shape[0], HEAD_DIM), BF16)
    for i, (ref, extra) in enumerate(((ks_ref, blk_ref[...]), (vs_ref, ones), (kw_ref, None),
                                      (vw_ref, ones))):
        for g in range(NSA_KV_GROUPS):
            c0 = _C_KV + i * NSA_KV + g * HEAD_DIM
            val = z[:, c0:c0 + HEAD_DIM].astype(BF16)
            ref[0, g] = val if extra is None else jnp.concatenate([val, extra], axis=1)
    zf = _dot(h, wf_ref[...])
    for hd in range(FOX_HEADS):
        c0 = hd * HEAD_DIM
        qf_ref[0, hd] = (zf[:, c0:c0 + HEAD_DIM] * (HEAD_DIM ** -0.5)).astype(BF16)
        kf_ref[0, hd] = zf[:, c0 + FOX_W:c0 + FOX_W + HEAD_DIM].astype(BF16)
        v = zf[:, c0 + 2 * FOX_W:c0 + 2 * FOX_W + HEAD_DIM].astype(BF16)
        vf_ref[0, hd] = jnp.concatenate([v, ones], axis=1)
    zs = _dot(h, ws_ref[...])
    for g in range(NSA_KV_GROUPS):
        gate_ref[0, g] = zs[:, g * GATE_PAD:(g + 1) * GATE_PAD]
    fl_ref[0] = zs.T[_C_FLOG:_C_FLOG + FOX_HEADS]


_IN_COLS = np.cumsum((0, NSA_Q, NSA_KV, NSA_KV, NSA_KV, NSA_KV, NSA_KV, NSA_KV, 3 * NSA_HEADS,
                      FOX_W, FOX_W, FOX_W, FOX_HEADS, D_MODEL, D_MODEL)).tolist()
W_PREP_ROWS = 256


def _wprep_body(w_ref, wn_ref, wf_ref, ws_ref, wa_ref, wb_ref):
    wt = w_ref[0]
    c = _IN_COLS
    k = wt.shape[1]
    wn_ref[...] = wt[:c[7]].T.astype(BF16)
    wf_ref[...] = wt[c[8]:c[11]].T.astype(BF16)
    wa_ref[...] = wt[c[12]:c[13]].T.astype(BF16)
    wb_ref[...] = wt[c[13]:c[14]].T.astype(BF16)
    small = []
    for g in range(NSA_KV_GROUPS):
        for br in range(3):
            c0 = c[7] + br * NSA_HEADS + g * NSA_HPG
            small.append(wt[c0:c0 + NSA_HPG])
        small.append(jnp.zeros((GATE_PAD - 3 * NSA_HPG, k), F32))
    small.append(wt[c[11]:c[12]])
    small.append(jnp.zeros((LANES - _C_FLOG - FOX_HEADS, k), F32))
    ws_ref[...] = jnp.concatenate(small, axis=0).T.astype(BF16)


def _pack_w_in(w_in, layer):
    _, k, n = w_in.shape
    c = _IN_COLS
    assert n == c[14]
    widths = (c[7], c[11] - c[8], LANES, D_MODEL, D_MODEL)
    return pl.pallas_call(
        _wprep_body,
        grid=(k // W_PREP_ROWS,),
        in_specs=[pl.BlockSpec((1, n, W_PREP_ROWS), lambda i: (layer, 0, i))],
        out_specs=tuple(pl.BlockSpec((W_PREP_ROWS, w), lambda i: (i, 0)) for w in widths),
        out_shape=tuple(jax.ShapeDtypeStruct((k, w), BF16) for w in widths),
        compiler_params=_params(("parallel",)),
        name="wprep",
    )(jnp.swapaxes(w_in, 1, 2))


def _inproj(x, g, w_nsa, w_fox, w_small):
    b, s, _ = x.shape
    grid = (b, s // TOK_TILE)
    tok = lambda c: pl.BlockSpec((1, TOK_TILE, c), lambda i, j: (i, j, 0))
    heads = lambda nh, w=HEAD_DIM: pl.BlockSpec((1, nh, TOK_TILE, w), lambda i, j: (i, 0, j, 0))
    chunks = pl.BlockSpec((1, TOK_TILE // CMP_STRIDE, CHUNK_W), lambda i, j: (i, j, 0))
    sds = jax.ShapeDtypeStruct
    kv = sds((b, NSA_KV_GROUPS, s, HEAD_DIM), BF16)
    kv2 = sds((b, NSA_KV_GROUPS, s, 2 * HEAD_DIM), BF16)
    fx = sds((b, FOX_HEADS, s, HEAD_DIM), BF16)
    cmp_in = sds((b, s // CMP_STRIDE, CHUNK_W), BF16)
    out_shape = (sds((b, s, NSA_Q), BF16), cmp_in, cmp_in,
                 kv2, kv2, kv, kv2, fx, fx, sds((b, FOX_HEADS, s, 2 * HEAD_DIM), BF16),
                 sds((b, NSA_KV_GROUPS, s, GATE_PAD), F32), sds((b, FOX_HEADS, s), F32))
    out_specs = (tok(NSA_Q), chunks, chunks,
                 heads(NSA_KV_GROUPS, 2 * HEAD_DIM), heads(NSA_KV_GROUPS, 2 * HEAD_DIM),
                 heads(NSA_KV_GROUPS), heads(NSA_KV_GROUPS, 2 * HEAD_DIM),
                 heads(FOX_HEADS), heads(FOX_HEADS), heads(FOX_HEADS, 2 * HEAD_DIM),
                 pl.BlockSpec((1, NSA_KV_GROUPS, TOK_TILE, GATE_PAD), lambda i, j: (i, 0, j, 0)),
                 pl.BlockSpec((1, FOX_HEADS, TOK_TILE), lambda i, j: (i, 0, j)))
    key_blk = np.arange(s) // SEL_BLOCK
    blk_onehot = jnp.asarray(key_blk[:, None] == np.arange(HEAD_DIM)[None, :], BF16)
    return pl.pallas_call(
        _inproj_body,
        grid=grid,
        in_specs=[tok(D_MODEL), _resident((1, D_MODEL)), _resident(w_nsa.shape),
                  _resident(w_fox.shape), _resident(w_small.shape),
                  pl.BlockSpec((TOK_TILE, HEAD_DIM), lambda i, j: (j, 0))],
        out_specs=out_specs,
        out_shape=out_shape,
        scratch_shapes=[pltpu.VMEM((2, TOK_TILE, NSA_KV), F32)],
        compiler_params=_params(("parallel", "parallel")),
        name="inproj",
    )(x, g.reshape(1, D_MODEL), w_nsa, w_fox, w_small, blk_onehot)


CMP_BATCH = 4

def _compress_body(xk_ref, xv_ref, pk_ref, pv_ref, wk1_ref, wv1_ref, wk2_ref, wv2_ref, ok_ref, ov_ref):
    nb = xk_ref.shape[0]
    n = nb * N_CHUNKS_PAD
    for x_ref, p_ref, w1_ref, w2_ref, o_ref in ((xk_ref, pk_ref, wk1_ref, wk2_ref, ok_ref),
                                                (xv_ref, pv_ref, wv1_ref, wv2_ref, ov_ref)):
        x = x_ref[...].reshape(n, CHUNK_W)
        a0 = _dot(x, w1_ref[0])
        a1 = _dot(x, w1_ref[1])
        c = _dot(p_ref[0], w1_ref[0]) + _dot(p_ref[1], w1_ref[1])
        pre = a0 + pltpu.roll(a1, n - 1, axis=0) + c[0:1, :]
        hid = jax.nn.gelu(pre).astype(BF16)
        out = _dot(hid, w2_ref[...]).astype(BF16)
        for i in range(nb):
            for g in range(NSA_KV_GROUPS):
                o_ref[i, g] = out[i * N_CHUNKS_PAD:(i + 1) * N_CHUNKS_PAD, g * HEAD_DIM:(g + 1) * HEAD_DIM]


def _pack_compress(pos, w1, w2):
    r = CMP_BLOCK // CMP_STRIDE
    assert NSA_KV_GROUPS == 2
    w1r = w1.astype(BF16).reshape(r, CMP_STRIDE, HEAD_DIM, CMP_HIDDEN)
    z1 = jnp.zeros_like(w1r)
    w1big = jnp.stack([jnp.concatenate([w1r, z1], axis=-1), jnp.concatenate([z1, w1r], axis=-1)], axis=2)
    w1big = w1big.reshape(r, CHUNK_W, NSA_KV_GROUPS * CMP_HIDDEN)
    z2 = jnp.zeros_like(w2)
    w2big = jnp.concatenate([jnp.concatenate([w2, z2], axis=1), jnp.concatenate([z2, w2], axis=1)], axis=0)
    p = pos.reshape(r, CMP_STRIDE, 1, HEAD_DIM)
    p = jnp.broadcast_to(p, (r, CMP_STRIDE, NSA_KV_GROUPS, HEAD_DIM)).reshape(r, 1, CHUNK_W)
    p = jnp.broadcast_to(p, (r, 16, CHUNK_W))
    return p.astype(BF16), w1big.astype(BF16), w2big.astype(BF16)


def _compress(xk, xv, pos_k, pos_v, k_w1, k_w2, v_w1, v_w2):
    b = xk.shape[0]
    assert xk.shape[1:] == (N_CHUNKS_PAD, CHUNK_W)
    pk, wk1, wk2 = _pack_compress(pos_k, k_w1, k_w2)
    pv, wv1, wv2 = _pack_compress(pos_v, v_w1, v_w2)
    nb = CMP_BATCH if b % CMP_BATCH == 0 else 1
    xs = pl.BlockSpec((nb, N_CHUNKS_PAD, CHUNK_W), lambda i: (i, 0, 0))
    os_ = pl.BlockSpec((nb, NSA_KV_GROUPS, N_CHUNKS_PAD, HEAD_DIM), lambda i: (i, 0, 0, 0))
    osd = jax.ShapeDtypeStruct((b, NSA_KV_GROUPS, N_CHUNKS_PAD, HEAD_DIM), BF16)
    return pl.pallas_call(
        _compress_body,
        grid=(b // nb,),
        in_specs=[xs, xs, _resident(pk.shape), _resident(pv.shape), _resident(wk1.shape),
                  _resident(wv1.shape), _resident(wk2.shape), _resident(wv2.shape)],
        out_specs=(os_, os_),
        out_shape=(osd, osd),
        compiler_params=_params(("parallel",)),
        name="compress",
    )(xk, xv, pk, pv, wk1, wv1, wk2, wv2)


def _bias_lookup(tbl_ref, head, dist):
    acc = jnp.full(dist.shape, tbl_ref[0, head], F32)
    for k, thr in enumerate(BUCKET_THRESHOLDS):
        acc = jnp.where(dist >= thr, tbl_ref[k + 1, head], acc)
    return acc - tbl_ref[NUM_BUCKETS - 1, head]


def _relbias_body(tbl_ref, toep_ref, bc_ref):
    head = pl.program_id(0)
    t = ATT_TILE
    i = lax.broadcasted_iota(jnp.int32, (t, 2 * t), 0)
    u = lax.broadcasted_iota(jnp.int32, (t, 2 * t), 1)
    d = i - u + t
    toep_ref[0] = jnp.where(d >= 0, _bias_lookup(tbl_ref, head, d), NEG_INF)
    n_tiles = bc_ref.shape[1] // t
    per_tile = t // CMP_STRIDE
    off = per_tile * (n_tiles - 1)
    width = 2 * N_CHUNKS_PAD
    assert off + N_CHUNKS_PAD <= width
    i = lax.broadcasted_iota(jnp.int32, (t, width), 0)
    u = lax.broadcasted_iota(jnp.int32, (t, width), 1)
    dc = i - ((u - off) * CMP_STRIDE + CMP_BLOCK - 1)
    strip = jnp.where(dc >= 0, _bias_lookup(tbl_ref, head, dc), NEG_INF)
    for q in range(n_tiles):
        lo = off - q * per_tile
        bc_ref[0, q * t:(q + 1) * t, :] = strip[:, lo:lo + N_CHUNKS_PAD]


def _relbias(tbl, s):
    return pl.pallas_call(
        _relbias_body,
        grid=(NSA_HEADS,),
        in_specs=[pl.BlockSpec(memory_space=pltpu.SMEM)],
        out_specs=(pl.BlockSpec((1, ATT_TILE, 2 * ATT_TILE), lambda h: (h, 0, 0)),
                   pl.BlockSpec((1, s, N_CHUNKS_PAD), lambda h: (h, 0, 0))),
        out_shape=(jax.ShapeDtypeStruct((NSA_HEADS, ATT_TILE, 2 * ATT_TILE), F32),
                   jax.ShapeDtypeStruct((NSA_HEADS, s, N_CHUNKS_PAD), F32)),
        compiler_params=_params(("parallel",)),
        name="relbias",
    )(tbl)


def _lane_fold(x, op):
    return functools.reduce(op, [x[:, i * LANES:(i + 1) * LANES] for i in range(x.shape[1] // LANES)])


def _lane_tile(x, width):
    return jnp.concatenate([x] * (width // LANES), axis=1)


def _row_max_tile(rmax):
    return jnp.broadcast_to(jnp.max(rmax, axis=-1, keepdims=True), rmax.shape)


def _normalize(acc):
    return (acc / pltpu.roll(acc, HEAD_DIM, axis=1))[:, :HEAD_DIM]


def _split3(x):
    hi = x.astype(BF16)
    r = x - hi.astype(F32)
    mid = r.astype(BF16)
    lo = (r - mid.astype(F32)).astype(BF16)
    return hi, mid, lo


def _nsa_body(q_ref, kc_ref, vc_ref, ks_ref, vs_ref, kw_ref, vw_ref, gate_a_ref, gate_b_ref, toep_ref,
              bc_ref, ov_ref, o_ref, s_ref, near_ref, mb_ref, acc_ref, part_ref, *, n_tiles):
    t = ATT_TILE
    j4 = NSA_HPG
    rows = j4 * t
    groups = range(NSA_KV_GROUPS)
    step = pl.program_id(1)
    n_sel = n_tiles * (t // SEL_BLOCK)

    def heads(x):
        return x.reshape(j4, t, x.shape[-1])

    def flat(x):
        return x.reshape(rows, x.shape[-1])

    def key_tile(ref, g, n):
        if isinstance(n, int):
            return ref[0, g, n * t:(n + 1) * t, :]
        return ref[0, g, pl.ds(pl.multiple_of(n * t, t), t), :]

    def gated(gate_ref, g, branch, o):
        gate = jax.nn.sigmoid(gate_ref[0, g])
        return [gate[:, branch * j4 + j:branch * j4 + j + 1] * o[j * t:(j + 1) * t] for j in range(j4)]

    def probs(par, g, s):
        return jnp.exp(s - _lane_tile(mb_ref[par, g], t)).astype(BF16)

    def open_tile(m, par):
        has_prev = m >= 1 if isinstance(m, int) else True
        has_far = m >= 2 if isinstance(m, int) else True
        qs = []
        for g in groups:
            qt = q_ref[0, :, g * j4 * HEAD_DIM:(g + 1) * j4 * HEAD_DIM]
            qs.append(jnp.concatenate([qt[:, j * HEAD_DIM:(j + 1) * HEAD_DIM] for j in range(j4)], axis=0))

        def biases(g):
            toep = toep_ref[g * j4:(g + 1) * j4]
            return toep[:, :, t:], toep[:, :, :t]

        cmp_out = [_nsa_compressed(qs[g], kc_ref[0, g], vc_ref[0, g], bc_ref[g * j4:(g + 1) * j4],
                                   ov_ref[...], m, n_sel) for g in groups]
        part = []
        for g in groups:
            diag_bias, prev_bias = biases(g)
            tiles = [(m, flat(heads(_dot_nt_pair(qs[g], key_tile(kw_ref, g, m))) + diag_bias))]
            if has_prev:
                tiles.append((m - 1, flat(heads(_dot_nt_pair(qs[g], key_tile(kw_ref, g, m - 1))) + prev_bias)))
            if has_far:
                ii = lax.broadcasted_iota(jnp.int32, (t, t), 0)
                jj = lax.broadcasted_iota(jnp.int32, (t, t), 1)
                tri = jnp.where(jj > ii, 0.0, NEG_INF)
                tiles.append((m - 2, flat(heads(_dot_nt_pair(qs[g], key_tile(kw_ref, g, m - 2))) + tri[None])))
            wmax = _lane_tile(_row_max_tile(functools.reduce(
                jnp.maximum, [_lane_fold(w, jnp.maximum) for _, w in tiles])), t)
            acc = sum(_dot_pair(jnp.exp(w - wmax).astype(BF16), key_tile(vw_ref, g, n)) for n, w in tiles)
            part += [a + b for a, b in zip(gated(gate_a_ref, g, 0, cmp_out[g][0]),
                                           gated(gate_a_ref, g, 2, _normalize(acc)))]
        part_ref[par] = jnp.concatenate(part, axis=1)

        q_aug = []
        for g in groups:
            diag_bias, prev_bias = biases(g)
            q_aug.append(jnp.concatenate([qs[g], jnp.concatenate([cmp_out[g][1]] * j4, axis=0)], axis=1))
            s0 = flat(heads(_dot_nt_pair(q_aug[g], key_tile(ks_ref, g, m))) + diag_bias)
            near_ref[par, g, 0] = s0
            rmax = _lane_fold(s0, jnp.maximum)
            if has_prev:
                s1 = flat(heads(_dot_nt_pair(q_aug[g], key_tile(ks_ref, g, m - 1))) + prev_bias)
                near_ref[par, g, 1] = s1
                rmax = jnp.maximum(rmax, _lane_fold(s1, jnp.maximum))
            if has_far:
                s2 = _dot_nt_pair(q_aug[g], key_tile(ks_ref, g, m - 2))
                s_ref[g, m - 2] = s2
                rmax = jnp.maximum(rmax, _lane_fold(s2, jnp.maximum))
            mb_ref[par, g] = rmax
        return q_aug

    def open_far(par, q_aug, n):
        for g in groups:
            s = _dot_nt_pair(q_aug[g], key_tile(ks_ref, g, n))
            s_ref[g, n] = s
            mb_ref[par, g] = jnp.maximum(mb_ref[par, g], _lane_fold(s, jnp.maximum))

    def close_tile(m, par):
        for g in groups:
            mb_ref[par, g] = _row_max_tile(mb_ref[par, g])
            acc = _dot_pair(probs(par, g, near_ref[par, g, 0]), key_tile(vs_ref, g, m))
            if not (isinstance(m, int) and m == 0):
                acc = acc + _dot_pair(probs(par, g, near_ref[par, g, 1]), key_tile(vs_ref, g, m - 1))
            acc_ref[par, g] = acc

    def close_far(par, n):
        for g in groups:
            acc_ref[par, g] += _dot_pair(probs(par, g, s_ref[g, n]), key_tile(vs_ref, g, n))

    def finish(par):
        outs = []
        for g in groups:
            outs += gated(gate_b_ref, g, 1, _normalize(acc_ref[par, g]))
        o_ref[0] = (part_ref[par] + jnp.concatenate(outs, axis=1)).astype(BF16)

    @pl.when(step == 0)
    def _():
        open_tile(0, 0)

    @pl.when(step == 1)
    def _():
        open_tile(1, 1)
        close_tile(0, 0)

    @pl.when(jnp.logical_and(step >= 2, step < n_tiles))
    def _():
        par = step % 2
        finish(par)
        q_aug = open_tile(step, par)
        close_tile(step - 1, 1 - par)

        @pl.loop(0, step - 2)
        def _(n):
            close_far(1 - par, n)
            open_far(par, q_aug, n)

    last = (n_tiles - 1) % 2

    @pl.when(step == n_tiles)
    def _():
        finish(1 - last)
        close_tile(step - 1, last)

        @pl.loop(0, step - 2)
        def _(n):
            close_far(last, n)

    @pl.when(step == n_tiles + 1)
    def _():
        finish(last)


def _nsa_compressed(q, kc, vc, bc, ov, m, n_sel):
    t = ATT_TILE
    j4 = NSA_HPG

    def heads(x):
        return x.reshape(j4, t, x.shape[-1])

    def flat(x):
        return x.reshape(j4 * t, x.shape[-1])

    sc = heads(_dot_nt_pair(q, kc)) + bc
    valid = bc > 0.5 * NEG_INF
    e = jnp.where(valid, jnp.exp(sc - jnp.max(sc, axis=-1, keepdims=True)), 0.0)
    l = jnp.sum(e, axis=-1, keepdims=True)
    pc = jnp.where(l > 0.0, e / l, 0.0)
    o_c = _dot_pair(flat(pc).astype(BF16), vc)
    if isinstance(m, int) and (m + 1) * t <= SEL_TOP_N * SEL_BLOCK:
        return o_c, jnp.zeros((t, HEAD_DIM), BF16)
    psum = pc[0] + pc[1] + pc[2] + pc[3]
    imp = sum(_dot(part, ov) for part in _split3(psum)).T[:n_sel]
    blk = lax.broadcasted_iota(jnp.int32, (n_sel, t), 0)
    tpos = m * t + lax.broadcasted_iota(jnp.int32, (n_sel, t), 1)
    cur = tpos // SEL_BLOCK
    bonus = jnp.where(blk == 0, FORCE_BONUS,
                      jnp.where(blk == cur, FORCE_BONUS, jnp.where(blk == cur - 1, FORCE_BONUS, 0.0)))
    imp = jnp.where(blk * SEL_BLOCK <= tpos, imp + bonus, NEG_INF)
    lanes = 4
    partial = [jnp.zeros((n_sel, t), F32) for _ in range(lanes)]
    for i in range(n_sel):
        row = imp[i:i + 1, :]
        before = jnp.where(blk > i, 1.0, 0.0)
        partial[i % lanes] = partial[i % lanes] + jnp.where(row > imp, 1.0,
                                                            jnp.where(row == imp, before, 0.0))
    rank = (partial[0] + partial[1]) + (partial[2] + partial[3])
    selb = jnp.where(rank < float(SEL_TOP_N), 0.0, NEG_INF)
    selb = jnp.concatenate([selb, jnp.zeros((LANES - n_sel, t), F32)], axis=0).T
    return o_c, selb[:, :HEAD_DIM].astype(BF16)


def _overlap_table(s):
    n_sel = s // SEL_BLOCK
    nc = N_CHUNKS_PAD
    c0 = np.arange(nc)[:, None] * CMP_STRIDE
    s0 = np.arange(n_sel)[None, :] * SEL_BLOCK
    ov = np.clip(np.minimum(c0 + CMP_BLOCK, s0 + SEL_BLOCK) - np.maximum(c0, s0), 0, None) / CMP_STRIDE
    ov[nc - 1] = 0.0
    return jnp.asarray(np.pad(ov, ((0, 0), (0, LANES - n_sel))), BF16)


def _nsa(qn, kc, vc, ks, vs, kw, vw, gates, toep, bc):
    b, s, _ = qn.shape
    t = ATT_TILE
    assert s // SEL_BLOCK <= HEAD_DIM
    ov = _overlap_table(s)
    rows = NSA_HPG * t
    ng = NSA_KV_GROUPS
    nq = s // t
    assert nq >= 2
    seq = lambda n, w=HEAD_DIM: pl.BlockSpec((1, ng, n, w), lambda i, j: (i, 0, 0, 0))
    opened = lambda j: jnp.minimum(j, nq - 1)
    closed = lambda j: jnp.maximum(j - 2, 0)
    return pl.pallas_call(
        functools.partial(_nsa_body, n_tiles=nq),
        grid=(b, nq + 2),
        in_specs=[pl.BlockSpec((1, t, NSA_Q), lambda i, j: (i, opened(j), 0)),
                  seq(N_CHUNKS_PAD), seq(N_CHUNKS_PAD),
                  seq(s, 2 * HEAD_DIM), seq(s, 2 * HEAD_DIM), seq(s), seq(s, 2 * HEAD_DIM),
                  pl.BlockSpec((1, ng, t, GATE_PAD), lambda i, j: (i, 0, opened(j), 0)),
                  pl.BlockSpec((1, ng, t, GATE_PAD), lambda i, j: (i, 0, closed(j), 0)),
                  _resident((NSA_HEADS, t, 2 * t)),
                  pl.BlockSpec((NSA_HEADS, t, N_CHUNKS_PAD), lambda i, j: (0, opened(j), 0)),
                  _resident(ov.shape)],
        out_specs=pl.BlockSpec((1, t, NSA_Q), lambda i, j: (i, closed(j), 0)),
        out_shape=jax.ShapeDtypeStruct((b, s, NSA_Q), BF16),
        scratch_shapes=[pltpu.VMEM((ng, nq - 2, rows, t), F32),
                        pltpu.VMEM((2, ng, 2, rows, t), F32),
                        pltpu.VMEM((2, ng, rows, LANES), F32),
                        pltpu.VMEM((2, ng, rows, 2 * HEAD_DIM), F32),
                        pltpu.VMEM((2, t, NSA_Q), F32)],
        compiler_params=_params(("parallel", "arbitrary")),
        name="nsa",
    )(qn, kc, vc, ks, vs, kw, vw, gates, gates, toep, bc, ov)


def _fcum_body(f_ref, b_ref, o_ref):
    nb = f_ref.shape[0] // b_ref.shape[0]
    z = f_ref[...] + jnp.concatenate([b_ref[...]] * nb, axis=0)
    x = -(jnp.maximum(-z, 0.0) + jnp.log1p(jnp.exp(-jnp.abs(z))))
    n = x.shape[-1]
    lane = lax.broadcasted_iota(jnp.int32, x.shape, 1)
    sh = 1
    while sh < n:
        x = x + jnp.where(lane >= sh, pltpu.roll(x, sh, axis=1), 0.0)
        sh *= 2
    o_ref[...] = x


def _fcum(f_t, b_forget):
    b, h, s = f_t.shape
    out = pl.pallas_call(
        _fcum_body,
        grid=(1,),
        in_specs=[_resident((b * h, s)), _resident((h, 1))],
        out_specs=pl.BlockSpec((b * h, s), lambda i: (0, 0)),
        out_shape=jax.ShapeDtypeStruct((b * h, s), F32),
        compiler_params=_params(("parallel",)),
        name="fcum",
    )(f_t.reshape(b * h, s), b_forget.reshape(h, 1))
    return out.reshape(b, h, s)


FOX_PAIR = 4
assert FOX_HEADS % FOX_PAIR == 0 and (FOX_PAIR * HEAD_DIM) % LANES == 0
FOX_TILE = 512


def _causal_bias(t):
    i = np.arange(t)
    return jnp.asarray(np.where(i[None, :] <= i[:, None], 0.0, NEG_INF), F32)


def _fox_body(q_ref, k_ref, v_ref, c_ref, cm_ref, o_ref, s_ref, mb_ref, acc_ref):
    t = FOX_TILE
    heads = range(FOX_PAIR)
    nq = q_ref.shape[2] // t

    def keys(n):
        return slice(n * t, (n + 1) * t)

    def scores(m, hh, n):
        q = q_ref[0, hh, m * t:(m + 1) * t, :]
        return _dot_nt(q, k_ref[0, hh, keys(n), :]) - c_ref[0, hh, :, keys(n)]

    def scores_diag(m):
        for hh in heads:
            s = scores(m, hh, m) + cm_ref[...]
            s_ref[hh, m] = s
            mb_ref[m % 2, hh] = _lane_fold(s, jnp.maximum)

    def scores_far(m, n):
        for hh in heads:
            s = scores(m, hh, n)
            s_ref[hh, n] = s
            mb_ref[m % 2, hh] = jnp.maximum(mb_ref[m % 2, hh], _lane_fold(s, jnp.maximum))

    def finish_max(m):
        for hh in heads:
            mb_ref[m % 2, hh] = _row_max_tile(mb_ref[m % 2, hh])

    def weigh(m, n):
        for hh in heads:
            p = jnp.exp(s_ref[hh, n] - _lane_tile(mb_ref[m % 2, hh], t)).astype(BF16)
            acc_ref[hh] += _dot(p, v_ref[0, hh, keys(n), :])

    scores_diag(0)
    finish_max(0)
    for m in range(nq):
        ahead = m + 1 < nq
        if ahead:
            scores_diag(m + 1)
        for hh in heads:
            acc_ref[hh] = jnp.zeros((t, 2 * HEAD_DIM), F32)

        for n in range(m + 1):
            weigh(m, n)
            if ahead:
                scores_far(m + 1, n)

        if ahead:
            finish_max(m + 1)
        o_ref[0, m * t:(m + 1) * t, :] = jnp.concatenate(
            [_normalize(acc_ref[hh]) for hh in heads], axis=1).astype(BF16)


def _fox(qf, kf, vf, c):
    b, h, s, _ = qf.shape
    t = FOX_TILE
    seq = lambda w: pl.BlockSpec((1, FOX_PAIR, s, w), lambda i, p: (i, p, 0, 0))
    return pl.pallas_call(
        _fox_body,
        grid=(b, h // FOX_PAIR),
        in_specs=[seq(HEAD_DIM), seq(HEAD_DIM), seq(2 * HEAD_DIM),
                  pl.BlockSpec((1, FOX_PAIR, 1, s), lambda i, p: (i, p, 0, 0)),
                  _resident((t, t))],
        out_specs=pl.BlockSpec((1, s, FOX_PAIR * HEAD_DIM), lambda i, p: (i, 0, p)),
        out_shape=jax.ShapeDtypeStruct((b, s, h * HEAD_DIM), BF16),
        scratch_shapes=[pltpu.VMEM((FOX_PAIR, s // t, t, t), F32),
                        pltpu.VMEM((2, FOX_PAIR, t, LANES), F32),
                        pltpu.VMEM((FOX_PAIR, t, 2 * HEAD_DIM), F32)],
        compiler_params=_params(("parallel", "parallel")),
        name="fox",
    )(qf, kf, vf, c.reshape(b, h, 1, s), _causal_bias(t))


MIX_CHUNK = 256


def _mixout_body(x_ref, g_ref, on_ref, of_ref, wa_ref, wb_ref, wun_ref, wuf_ref, wo_ref, o_ref):
    x = x_ref[...]
    h = _rms(x, g_ref[...]).astype(BF16)
    on = on_ref[...]
    of = of_ref[...]
    acc = x
    for c in range(D_MODEL // MIX_CHUNK):
        sl = slice(c * MIX_CHUNK, (c + 1) * MIX_CHUNK)
        y = (jax.nn.sigmoid(_dot(h, wa_ref[:, sl])) * _dot(on, wun_ref[:, sl])
             + jax.nn.sigmoid(_dot(h, wb_ref[:, sl])) * _dot(of, wuf_ref[:, sl]))
        acc = acc + _dot(y.astype(BF16), wo_ref[sl, :])
    o_ref[...] = acc


def _mixout(x2d, g, o_nsa, o_fox, w_a, w_b, w_un, w_uf, w_o):
    n = x2d.shape[0]
    tok = lambda c: pl.BlockSpec((TOK_TILE, c), lambda i: (i, 0))
    return pl.pallas_call(
        _mixout_body,
        grid=(n // TOK_TILE,),
        in_specs=[tok(D_MODEL), _resident((1, D_MODEL)), tok(NSA_Q), tok(FOX_W),
                  _resident(w_a.shape), _resident(w_b.shape), _resident(w_un.shape),
                  _resident(w_uf.shape), _resident(w_o.shape)],
        out_specs=tok(D_MODEL),
        out_shape=jax.ShapeDtypeStruct((n, D_MODEL), F32),
        compiler_params=_params(("parallel",)),
        name="mixout",
    )(x2d, g.reshape(1, D_MODEL), o_nsa, o_fox, w_a, w_b, w_un, w_uf, w_o)


MEM_BATCH = 4


def _memkv_body(m_ref, g_ref, w_ref, k_ref, v_ref):
    nb, ml, d = m_ref.shape
    h = _rms(m_ref[...].reshape(nb * ml, d), g_ref[...]).astype(BF16)
    z = _dot(h, w_ref[...]).astype(BF16)
    k_ref[...] = z[:, :D_MODEL].reshape(nb, ml, d)
    v_ref[...] = z[:, D_MODEL:].reshape(nb, ml, d)


def _memkv(mem, g, w_kv):
    b, ml, _ = mem.shape
    nb = MEM_BATCH if b % MEM_BATCH == 0 else 1
    blk = pl.BlockSpec((nb, ml, D_MODEL), lambda i: (i, 0, 0))
    sd = jax.ShapeDtypeStruct((b, ml, D_MODEL), BF16)
    return pl.pallas_call(
        _memkv_body,
        grid=(b // nb,),
        in_specs=[blk, _resident((1, D_MODEL)), _resident(w_kv.shape)],
        out_specs=(blk, blk),
        out_shape=(sd, sd),
        compiler_params=_params(("parallel",)),
        name="memkv",
    )(mem, g.reshape(1, D_MODEL), w_kv)


def _memattn_body(x_ref, g_ref, k_ref, v_ref, wq_ref, wo_ref, o_ref):
    x = x_ref[0]
    h = _rms(x, g_ref[...]).astype(BF16)
    q = (_dot(h, wq_ref[...]) * (MEM_HEAD_DIM ** -0.5)).astype(BF16)
    outs = []
    for hd in range(MEM_HEADS):
        cols = slice(hd * MEM_HEAD_DIM, (hd + 1) * MEM_HEAD_DIM)
        s = _dot_nt(q[:, cols], k_ref[0, :, cols])
        e = jnp.exp(s - jnp.max(s, axis=-1, keepdims=True))
        p = e / jnp.sum(e, axis=-1, keepdims=True)
        outs.append(_dot(p.astype(BF16), v_ref[0, :, cols]))
    o = jnp.concatenate(outs, axis=1).astype(BF16)
    o_ref[0] = x + _dot(o, wo_ref[...])


def _memattn(x, g, k, v, w_q, w_o):
    b, s, _ = x.shape
    ml = k.shape[1]
    tok = pl.BlockSpec((1, TOK_TILE, D_MODEL), lambda i, j: (i, j, 0))
    kvb = pl.BlockSpec((1, ml, D_MODEL), lambda i, j: (i, 0, 0))
    return pl.pallas_call(
        _memattn_body,
        grid=(b, s // TOK_TILE),
        in_specs=[tok, _resident((1, D_MODEL)), kvb, kvb, _resident(w_q.shape), _resident(w_o.shape)],
        out_specs=tok,
        out_shape=jax.ShapeDtypeStruct((b, s, D_MODEL), F32),
        compiler_params=_params(("parallel", "parallel")),
        name="memattn",
    )(x, g.reshape(1, D_MODEL), k, v, w_q, w_o)


def kernel(x, mem, rel_bias_table, ffn1_norm, ffn1_w_gate, ffn1_w_up, ffn1_w_down, mix_norm, mix_w_in, mix_b_forget, cmp_pos_k, cmp_pos_v, cmp_k_w1, cmp_k_w2, cmp_v_w1, cmp_v_w2, w_up_nsa, w_up_fox, mix_w_out, mem_q_norm, mem_kv_norm, mem_w_q, mem_w_kv, mem_w_o, ffn2_norm, ffn2_w_gate, ffn2_w_up, ffn2_w_down, final_norm):
    b, s, d = x.shape
    depth = ffn1_norm.shape[0]
    bf = lambda w: w.astype(BF16)
    toep, bias_c = _relbias(rel_bias_table, s)
    x = x.reshape(b * s, d)
    for l in range(depth):
        last = l == depth - 1
        x = _ffn(x, ffn1_norm[l], ffn1_w_gate[l], ffn1_w_up[l], ffn1_w_down[l])

        w_nsa, w_fox, w_small, w_a, w_b = _pack_w_in(mix_w_in, l)
        qn, kc, vc, ks, vs, kw, vw, qf, kf, vf, gates, flog = _inproj(
            x.reshape(b, s, d), mix_norm[l], w_nsa, w_fox, w_small)
        kcc, vcc = _compress(kc, vc, cmp_pos_k[l], cmp_pos_v[l], cmp_k_w1[l], cmp_k_w2[l],
                             cmp_v_w1[l], cmp_v_w2[l])
        o_nsa = _nsa(qn, kcc, vcc, ks, vs, kw, vw, gates, toep, bias_c)
        c = _fcum(flog, mix_b_forget[l])
        o_fox = _fox(qf, kf, vf, c)
        x = _mixout(x, mix_norm[l], o_nsa.reshape(b * s, NSA_Q), o_fox.reshape(b * s, FOX_W),
                    w_a, w_b, bf(w_up_nsa[l]), bf(w_up_fox[l]), bf(mix_w_out[l]))

        mk, mv = _memkv(mem, mem_kv_norm[l], bf(mem_w_kv[l]))
        x = _memattn(x.reshape(b, s, d), mem_q_norm[l], mk, mv, bf(mem_w_q[l]), bf(mem_w_o[l]))
        x = _ffn(x.reshape(b * s, d), ffn2_norm[l], ffn2_w_gate[l], ffn2_w_up[l],
                 ffn2_w_down[l], final_g=final_norm if last else None)
    return x.reshape(b, s, d)
```
